```python
import math
import jax, jax.numpy as jnp
from jax import lax
import numpy as np

D_MODEL = 1024
BATCH = 8
SEQ = 2048
DEPTH = 1

MEM_LEN = 256
HEAD_DIM = 64
NSA_HEADS = 16
NSA_KV_GROUPS = 4
NSA_HPG = NSA_HEADS // NSA_KV_GROUPS
CMP_LEN = 32
CMP_STRIDE = 16
CMP_HIDDEN = 256
SEL_BLOCK = 64
SEL_TOPK = 8
WINDOW = 512
Q_BLOCK = 128
ROPE_THETA = 500000.0
ROPE_DIM = HEAD_DIM // 4
RNN_WIDTH = D_MODEL
RNN_BLOCKS = 16
RNN_BLOCK_DIM = RNN_WIDTH // RNN_BLOCKS
CONV_WIDTH = 4
RGLRU_C = 8.0
XATTN_HEADS = 4
XATTN_WIDTH = XATTN_HEADS * HEAD_DIM
D_FF = 4 * D_MODEL
N_BRANCHES = 3
RMS_EPS = 1e-6

NSA_Q = NSA_HEADS * HEAD_DIM
NSA_KV = NSA_KV_GROUPS * HEAD_DIM
NSA_GATES = NSA_HEADS * 3
IN_WIDTHS = (NSA_Q, NSA_KV, NSA_KV, NSA_KV, NSA_KV, NSA_KV, NSA_KV, NSA_GATES,
             RNN_WIDTH, RNN_WIDTH, XATTN_WIDTH, N_BRANCHES * D_MODEL)
D_IN = sum(IN_WIDTHS)

kernel_name = "hybrid_nsa_rglru_gated_block"


def rms_norm(x, g):
    xf = x.astype(jnp.float32)
    y = xf * lax.rsqrt(jnp.mean(xf * xf, axis=-1, keepdims=True) + RMS_EPS)
    return (y * g.astype(jnp.float32)).astype(x.dtype)


def masked_softmax(s, mask):
    s = jnp.where(mask, s.astype(jnp.float32), -jnp.inf)
    m = jnp.max(s, axis=-1, keepdims=True)
    m = jnp.where(jnp.isfinite(m), m, 0.0)
    e = jnp.exp(s - m)
    return e / jnp.maximum(jnp.sum(e, axis=-1, keepdims=True), 1e-30)


def rope_tables(pos):
    inv = 1.0 / (ROPE_THETA ** (jnp.arange(0, ROPE_DIM, 2, dtype=jnp.float32) / ROPE_DIM))
    ang = pos.astype(jnp.float32)[:, None] * inv[None, :]
    return jnp.cos(ang), jnp.sin(ang)


def apply_partial_rope(x, cos, sin):
    half = ROPE_DIM // 2
    xf = x.astype(jnp.float32)
    x1 = xf[..., :half]
    x2 = xf[..., half:ROPE_DIM]
    out = jnp.concatenate([x1 * cos - x2 * sin, x2 * cos + x1 * sin, xf[..., ROPE_DIM:]], axis=-1)
    return out.astype(x.dtype)


def cmp_to_sel_weights(n_cmp, n_sel):
    c0 = np.arange(n_cmp)[:, None] * CMP_STRIDE
    s0 = np.arange(n_sel)[None, :] * SEL_BLOCK
    ov = np.clip(np.minimum(c0 + CMP_LEN, s0 + SEL_BLOCK) - np.maximum(c0, s0), 0, None)
    return (ov / CMP_LEN).astype(np.float32)


def compress_blocks(kv, pos_emb, w1, w2):
    B, G, S, hd = kv.shape
    n_cmp = (S - CMP_LEN) // CMP_STRIDE + 1
    idx = jnp.arange(n_cmp)[:, None] * CMP_STRIDE + jnp.arange(CMP_LEN)[None, :]
    blocks = kv[:, :, idx, :] + pos_emb
    flat = blocks.reshape(B, G, n_cmp, CMP_LEN * hd)
    return jax.nn.gelu(flat @ w1) @ w2


def nsa_mixer(q, k_c, v_c, k_s, v_s, k_w, v_w, gates):
    B, G, Hg, S, hd = q.shape
    n_cmp = k_c.shape[2]
    n_sel = S // SEL_BLOCK
    n_top = min(SEL_TOPK, n_sel)
    nq = S // Q_BLOCK
    scale = HEAD_DIM ** -0.5
    c2s = jnp.asarray(cmp_to_sel_weights(n_cmp, n_sel))
    cmp_end = jnp.arange(n_cmp) * CMP_STRIDE + (CMP_LEN - 1)
    ks_blocks = k_s.reshape(B, G, n_sel, SEL_BLOCK, hd)
    vs_blocks = v_s.reshape(B, G, n_sel, SEL_BLOCK, hd)
    kw_pad = jnp.pad(k_w, ((0, 0), (0, 0), (WINDOW, 0), (0, 0)))
    vw_pad = jnp.pad(v_w, ((0, 0), (0, 0), (WINDOW, 0), (0, 0)))
    bi = jnp.arange(B)[:, None, None, None]
    gi = jnp.arange(G)[None, :, None, None]
    blk_ids = jnp.arange(n_sel)

    def block(args):
        i, q_b, g_b = args
        t = i * Q_BLOCK + jnp.arange(Q_BLOCK)
        s_c = jnp.einsum('bghqd,bgnd->bghqn', q_b, k_c) * scale
        p_c = masked_softmax(s_c, cmp_end[None, :] <= t[:, None])
        o_c = jnp.einsum('bghqn,bgnd->bghqd', p_c.astype(q_b.dtype), v_c)
        imp = jnp.einsum('bghqn,ns->bgqs', p_c, c2s)
        cur = (t // SEL_BLOCK)[:, None]
        forced = (blk_ids == 0) | (blk_ids == cur) | (blk_ids == cur - 1)
        future = blk_ids * SEL_BLOCK > t[:, None]
        imp = jnp.where(forced, jnp.inf, jnp.where(future, -jnp.inf, imp))
        _, idx = lax.top_k(imp, n_top)
        k_g = ks_blocks[bi, gi, idx]
        v_g = vs_blocks[bi, gi, idx]
        key_pos = (idx[..., None] * SEL_BLOCK + jnp.arange(SEL_BLOCK)).reshape(B, G, 1, Q_BLOCK, n_top * SEL_BLOCK)
        s_s = jnp.einsum('bghqd,bgqkld->bghqkl', q_b, k_g).reshape(B, G, Hg, Q_BLOCK, n_top * SEL_BLOCK) * scale
        p_s = masked_softmax(s_s, key_pos <= t[:, None])
        o_s = jnp.einsum('bghqkl,bgqkld->bghqd',
                         p_s.astype(q_b.dtype).reshape(B, G, Hg, Q_BLOCK, n_top, SEL_BLOCK), v_g)
        start = i * Q_BLOCK
        k_b = lax.dynamic_slice_in_dim(kw_pad, start, Q_BLOCK + WINDOW, axis=2)
        v_b = lax.dynamic_slice_in_dim(vw_pad, start, Q_BLOCK + WINDOW, axis=2)
        w_pos = start - WINDOW + jnp.arange(Q_BLOCK + WINDOW)
        mask_w = (w_pos[None, :] <= t[:, None]) & (w_pos[None, :] > t[:, None] - WINDOW)
        s_w = jnp.einsum('bghqd,bgkd->bghqk', q_b, k_b) * scale
        p_w = masked_softmax(s_w, mask_w)
        o_w = jnp.einsum('bghqk,bgkd->bghqd', p_w.astype(q_b.dtype), v_b)
        return g_b[..., 0:1] * o_c + g_b[..., 1:2] * o_s + g_b[..., 2:3] * o_w

    q_blocks = jnp.moveaxis(q.reshape(B, G, Hg, nq, Q_BLOCK, hd), 3, 0)
    g_blocks = jnp.moveaxis(gates.reshape(B, G, Hg, nq, Q_BLOCK, 3), 3, 0)
    out = lax.map(block, (jnp.arange(nq), q_blocks, g_blocks))
    out = jnp.moveaxis(out, 0, 3).reshape(B, G, Hg, S, hd)
    return out.transpose(0, 3, 1, 2, 4).reshape(B, S, G * Hg * hd)


def causal_depthwise_conv(x, w, b):
    S = x.shape[1]
    xp = jnp.pad(x, ((0, 0), (CONV_WIDTH - 1, 0), (0, 0)))
    y = b
    for k in range(CONV_WIDTH):
        y = y + xp[:, k:k + S, :] * w[k]
    return y


def rg_lru(xr, w_a, b_a, w_i, b_i, lam):
    B, S, C = xr.shape
    xb = xr.reshape(B, S, RNN_BLOCKS, RNN_BLOCK_DIM)
    r = jax.nn.sigmoid(jnp.einsum('bsnk,nkj->bsnj', xb, w_a).reshape(B, S, C) + b_a)
    gi = jax.nn.sigmoid(jnp.einsum('bsnk,nkj->bsnj', xb, w_i).reshape(B, S, C) + b_i)
    log_a = -RGLRU_C * r.astype(jnp.float32) * jax.nn.softplus(-lam.astype(jnp.float32))
    a = jnp.exp(log_a)
    mult = jnp.sqrt(-jnp.expm1(2.0 * log_a))
    u = mult * (gi * xr).astype(jnp.float32)

    def combine(c1, c2):
        a1, b1 = c1
        a2, b2 = c2
        return a1 * a2, a2 * b1 + b2

    _, h = lax.associative_scan(combine, (a, u), axis=1)
    return h.astype(xr.dtype)


def cross_attention(q_x, mem, g_mem, w_mem_kv, w_xo):
    B, S, _ = q_x.shape
    M = mem.shape[1]
    q = q_x.reshape(B, S, XATTN_HEADS, HEAD_DIM)
    kv = (rms_norm(mem, g_mem) @ w_mem_kv).reshape(B, M, 2, XATTN_HEADS, HEAD_DIM)
    s = jnp.einsum('bshd,bmhd->bhsm', q, kv[:, :, 0]) * (HEAD_DIM ** -0.5)
    p = jax.nn.softmax(s.astype(jnp.float32), axis=-1).astype(q.dtype)
    o = jnp.einsum('bhsm,bmhd->bshd', p, kv[:, :, 1]).reshape(B, S, XATTN_WIDTH)
    return o @ w_xo


def setup_inputs(seed: int = 0) -> dict:
    key = jax.random.key(seed)
    ks = jax.random.split(key, 32)
    L = DEPTH

    def nrm(k, shape, fan_in):
        return jax.random.normal(k, shape, jnp.float32) * (fan_in ** -0.5)

    def gain(k, shape):
        return 1.0 + 0.02 * jax.random.normal(k, shape, jnp.float32)

    def small(k, shape):
        return 0.02 * jax.random.normal(k, shape, jnp.float32)

    u = jax.random.uniform(ks[20], (L, RNN_WIDTH), jnp.float32, 0.9, 0.999) ** (1.0 / RGLRU_C)
    rg_lambda = jnp.log(u) - jnp.log1p(-u)
    return {
        "x": jax.random.normal(ks[0], (BATCH, SEQ, D_MODEL), jnp.float32),
        "mem": jax.random.normal(ks[1], (BATCH, MEM_LEN, D_MODEL), jnp.float32),
        "g_mix": gain(ks[2], (L, D_MODEL)),
        "w_in": nrm(ks[3], (L, D_MODEL, D_IN), D_MODEL),
        "cmp_pos_k": small(ks[4], (L, CMP_LEN, HEAD_DIM)),
        "cmp_pos_v": small(ks[5], (L, CMP_LEN, HEAD_DIM)),
        "w_cmp_k1": nrm(ks[6], (L, CMP_LEN * HEAD_DIM, CMP_HIDDEN), CMP_LEN * HEAD_DIM),
        "w_cmp_k2": nrm(ks[7], (L, CMP_HIDDEN, HEAD_DIM), CMP_HIDDEN),
        "w_cmp_v1": nrm(ks[8], (L, CMP_LEN * HEAD_DIM, CMP_HIDDEN), CMP_LEN * HEAD_DIM),
        "w_cmp_v2": nrm(ks[9], (L, CMP_HIDDEN, HEAD_DIM), CMP_HIDDEN),
        "conv_w": nrm(ks[10], (L, CONV_WIDTH, RNN_WIDTH), CONV_WIDTH),
        "conv_b": small(ks[11], (L, RNN_WIDTH)),
        "w_rg_a": nrm(ks[12], (L, RNN_BLOCKS, RNN_BLOCK_DIM, RNN_BLOCK_DIM), RNN_BLOCK_DIM),
        "b_rg_a": small(ks[13], (L, RNN_WIDTH)),
        "w_rg_i": nrm(ks[14], (L, RNN_BLOCKS, RNN_BLOCK_DIM, RNN_BLOCK_DIM), RNN_BLOCK_DIM),
        "b_rg_i": small(ks[15], (L, RNN_WIDTH)),
        "rg_lambda": rg_lambda,
        "g_mem": gain(ks[16], (L, D_MODEL)),
        "w_mem_kv": nrm(ks[17], (L, D_MODEL, 2 * XATTN_WIDTH), D_MODEL),
        "w_xo": nrm(ks[18], (L, XATTN_WIDTH, D_MODEL), XATTN_WIDTH),
        "w_o": nrm(ks[19], (L, D_MODEL, D_MODEL), D_MODEL),
        "g_mlp": gain(ks[21], (L, D_MODEL)),
        "w_up": nrm(ks[22], (L, D_MODEL, D_FF), D_MODEL),
        "w_down": nrm(ks[23], (L, D_FF, D_MODEL), D_FF),
        "g_final": gain(ks[24], (D_MODEL,)),
    }


def reference(x, mem, g_mix, w_in, cmp_pos_k, cmp_pos_v, w_cmp_k1, w_cmp_k2, w_cmp_v1, w_cmp_v2,
              conv_w, conv_b, w_rg_a, b_rg_a, w_rg_i, b_rg_i, rg_lambda, g_mem, w_mem_kv, w_xo,
              w_o, g_mlp, w_up, w_down, g_final):
    B, S, D = x.shape
    G, Hg, hd = NSA_KV_GROUPS, NSA_HPG, HEAD_DIM
    offsets = np.cumsum(IN_WIDTHS)[:-1].tolist()
    cos, sin = rope_tables(jnp.arange(S))
    n_cmp = (S - CMP_LEN) // CMP_STRIDE + 1
    cos_c, sin_c = rope_tables(jnp.arange(n_cmp) * CMP_STRIDE + (CMP_LEN - 1))

    def to_groups(t):
        return t.reshape(B, S, G, hd).transpose(0, 2, 1, 3)

    h = x
    for l in range(DEPTH):
        u = rms_norm(h, g_mix[l])
        z = u @ w_in[l]
        (q_n, kc_raw, vc_raw, k_s, v_s, k_w, v_w, nsa_g,
         x_rnn, g_rnn, q_x, merge_g) = jnp.split(z, offsets, axis=-1)

        q = q_n.reshape(B, S, G, Hg, hd).transpose(0, 2, 3, 1, 4)
        q = apply_partial_rope(q, cos, sin)
        k_s = apply_partial_rope(to_groups(k_s), cos, sin)
        k_w = apply_partial_rope(to_groups(k_w), cos, sin)
        k_c = compress_blocks(to_groups(kc_raw), cmp_pos_k[l], w_cmp_k1[l], w_cmp_k2[l])
        k_c = apply_partial_rope(k_c, cos_c, sin_c)
        v_c = compress_blocks(to_groups(vc_raw), cmp_pos_v[l], w_cmp_v1[l], w_cmp_v2[l])
        nsa_gates = jax.nn.sigmoid(nsa_g).reshape(B, S, G, Hg, 3).transpose(0, 2, 3, 1, 4)
        y_nsa = nsa_mixer(q, k_c, v_c, k_s, to_groups(v_s), k_w, to_groups(v_w), nsa_gates)

        xc = causal_depthwise_conv(x_rnn, conv_w[l], conv_b[l])
        hr = rg_lru(xc, w_rg_a[l], b_rg_a[l], w_rg_i[l], b_rg_i[l], rg_lambda[l])
        y_rnn = jax.nn.gelu(g_rnn) * hr

        y_x = cross_attention(q_x, mem, g_mem[l], w_mem_kv[l], w_xo[l])

        gm = jax.nn.sigmoid(merge_g).reshape(B, S, N_BRANCHES, D)
        y = gm[:, :, 0] * y_nsa + gm[:, :, 1] * y_rnn + gm[:, :, 2] * y_x
        h = h + y @ w_o[l]

        v = rms_norm(h, g_mlp[l])
        h = h + jnp.square(jax.nn.relu(v @ w_up[l])) @ w_down[l]

    return rms_norm(h, g_final)
```

```python
import functools

import numpy as np
import jax
import jax.numpy as jnp
from jax import lax
from jax.experimental import pallas as pl
from jax.experimental.pallas import tpu as pltpu

F32 = jnp.float32
BF16 = jnp.bfloat16

D_MODEL = 1024
HEAD_DIM = 64
NSA_HEADS = 16
NSA_GROUPS = 4
NSA_HPG = NSA_HEADS // NSA_GROUPS
CMP_LEN = 32
CMP_STRIDE = 16
CMP_HIDDEN = 256
SEL_BLOCK = 64
SEL_TOPK = 8
WINDOW = 512
Q_BLOCK = 128
ROPE_THETA = 500000.0
ROPE_DIM = HEAD_DIM // 4
ROPE_HALF = ROPE_DIM // 2
RNN_BLOCKS = 16
RNN_BLOCK_DIM = D_MODEL // RNN_BLOCKS
CONV_WIDTH = 4
RGLRU_C = 8.0
XATTN_HEADS = 4
XATTN_WIDTH = XATTN_HEADS * HEAD_DIM
D_FF = 4 * D_MODEL
RMS_EPS = 1e-6
SCALE = HEAD_DIM ** -0.5
NEG = -1e30

LANES = 128
SUBLANES = 8
VMEM_LIMIT = 48 * 1024 * 1024

OFF_MERGE = 0
OFF_XRNN = 3072
OFF_GRNN = 4096
OFF_Q = 5120
OFF_KS = 6144
OFF_KW = 6400
OFF_KC = 6656
OFF_VC = 6912
OFF_VS = 7168
OFF_VW = 7424
OFF_QX = 7680
OFF_GATE = 7936
Z_WIDTH = 8704
INPROJ_TN = 512
ROPE_TILE_LO = OFF_Q // INPROJ_TN
ROPE_TILE_HI = OFF_KC // INPROJ_TN

NT_DIMS = (((1,), (1,)), ((), ()))


def _sigmoid(x):
    return 1.0 / (1.0 + jnp.exp(-x))


def _gelu_tanh(x):
    return 0.5 * x * (1.0 + jnp.tanh(0.7978845608028654 * (x + 0.044715 * (x * x * x))))


def _rope_swap(z):
    n = z.shape[-1]
    lane = lax.broadcasted_iota(jnp.int32, z.shape, z.ndim - 1)
    first_half = (lane & (HEAD_DIM - 1)) < ROPE_HALF
    return jnp.where(first_half, pltpu.roll(z, n - ROPE_HALF, z.ndim - 1), pltpu.roll(z, ROPE_HALF, z.ndim - 1))


def _norm_proj_kernel(x_ref, g_ref, w_ref, cos_ref, sin_ref, z_ref, u_scr, *, rope_lo, rope_hi):
    j = pl.program_id(1)

    @pl.when(j == 0)
    def _():
        x = x_ref[...]
        ms = jnp.mean(x * x, axis=-1, keepdims=True)
        u_scr[...] = (x * lax.rsqrt(ms + RMS_EPS) * g_ref[...]).astype(BF16)

    z = jnp.dot(u_scr[...], w_ref[...], preferred_element_type=F32)
    is_rope = jnp.logical_and(j >= rope_lo, j < rope_hi)

    @pl.when(is_rope)
    def _():
        z_ref[...] = z * cos_ref[...] + _rope_swap(z) * sin_ref[...]

    @pl.when(jnp.logical_not(is_rope))
    def _():
        z_ref[...] = z


def _norm_proj(x, g, w, cos_t, sin_t, *, tm, tn, rope_lo, rope_hi, name):
    t, d = x.shape
    n = w.shape[1]
    s_tiles = cos_t.shape[0] // tm
    return pl.pallas_call(
        functools.partial(_norm_proj_kernel, rope_lo=rope_lo, rope_hi=rope_hi),
        grid=(t // tm, n // tn),
        in_specs=[
            pl.BlockSpec((tm, d), lambda i, j: (i, 0)),
            pl.BlockSpec((1, d), lambda i, j: (0, 0)),
            pl.BlockSpec((d, tn), lambda i, j: (0, j)),
            pl.BlockSpec((tm, tn), lambda i, j: (i % s_tiles, 0)),
            pl.BlockSpec((tm, tn), lambda i, j: (i % s_tiles, 0)),
        ],
        out_specs=pl.BlockSpec((tm, tn), lambda i, j: (i, j)),
        out_shape=jax.ShapeDtypeStruct((t, n), F32),
        scratch_shapes=[pltpu.VMEM((tm, d), BF16)],
        compiler_params=pltpu.CompilerParams(
            dimension_semantics=("parallel", "arbitrary"), vmem_limit_bytes=VMEM_LIMIT),
        name=name,
    )(x, g, w, cos_t, sin_t)


def _compress_kernel(rk_ref, rv_ref, pka_ref, pkb_ref, pva_ref, pvb_ref,
                     wk1a_ref, wk1b_ref, wv1a_ref, wv1b_ref, w2_ref, cos_ref, sin_ref, o_ref):
    m = rk_ref.shape[0]

    def hidden(r_ref, pa_ref, pb_ref, w1a_ref, w1b_ref):
        r = r_ref[...]
        pa = jnp.dot((r + pa_ref[...]).astype(BF16), w1a_ref[...], preferred_element_type=F32)
        pb = jnp.dot((r + pb_ref[...]).astype(BF16), w1b_ref[...], preferred_element_type=F32)
        return _gelu_tanh(pa + pltpu.roll(pb, m - 1, 0)).astype(BF16)

    hk = hidden(rk_ref, pka_ref, pkb_ref, wk1a_ref, wk1b_ref)
    hv = hidden(rv_ref, pva_ref, pvb_ref, wv1a_ref, wv1b_ref)
    hcat = jnp.concatenate([hk, hv], axis=1)
    kv = jnp.dot(hcat, w2_ref[...], preferred_element_type=F32)
    o_ref[...] = kv * cos_ref[...] + _rope_swap(kv) * sin_ref[...]


def _compress(rk, rv, pka, pkb, pva, pvb, wk1a, wk1b, wv1a, wv1b, w2, cos_t, sin_t, *, tm):
    m, k = rk.shape
    full = lambda shape: pl.BlockSpec(shape, lambda i: (0, 0))
    return pl.pallas_call(
        _compress_kernel,
        grid=(m // tm,),
        in_specs=[
            pl.BlockSpec((tm, k), lambda i: (i, 0)),
            pl.BlockSpec((tm, k), lambda i: (i, 0)),
            full((1, k)), full((1, k)), full((1, k)), full((1, k)),
            full((k, CMP_HIDDEN)), full((k, CMP_HIDDEN)), full((k, CMP_HIDDEN)), full((k, CMP_HIDDEN)),
            full((2 * CMP_HIDDEN, 2 * HEAD_DIM)),
            full((tm, 2 * HEAD_DIM)), full((tm, 2 * HEAD_DIM)),
        ],
        out_specs=pl.BlockSpec((tm, 2 * HEAD_DIM), lambda i: (i, 0)),
        out_shape=jax.ShapeDtypeStruct((m, 2 * HEAD_DIM), F32),
        compiler_params=pltpu.CompilerParams(
            dimension_semantics=("parallel",), vmem_limit_bytes=VMEM_LIMIT),
        name="compress",
    )(rk, rv, pka, pkb, pva, pvb, wk1a, wk1b, wv1a, wv1b, w2, cos_t, sin_t)


def _nsa_kernel(q_ref, g_ref, ks_ref, vs_ref, kw_ref, vw_ref, kvc_ref, c2st_ref, e3_ref, o_ref,
                m_scr, l_scr, acc_scr, *, n_sel):
    i = pl.program_id(2)
    rows = NSA_HPG * Q_BLOCK
    q0 = i * Q_BLOCK

    qf = q_ref[...] * SCALE
    qs = jnp.concatenate([qf[:, h * HEAD_DIM:(h + 1) * HEAD_DIM] for h in range(NSA_HPG)],
                         axis=0).astype(BF16)
    row = lax.broadcasted_iota(jnp.int32, (rows, 1), 0)
    tq = q0 + (row & (Q_BLOCK - 1))
    col = lax.broadcasted_iota(jnp.int32, (1, LANES), 1)

    kvc = kvc_ref[...]
    kc = kvc[:, :HEAD_DIM].astype(BF16)
    vc = kvc[:, HEAD_DIM:].astype(BF16)
    s = lax.dot_general(qs, kc, NT_DIMS, preferred_element_type=F32)
    ok = (col * CMP_STRIDE + (CMP_LEN - 1)) <= tq
    s = jnp.where(ok, s, NEG)
    mx = jnp.max(s, axis=-1, keepdims=True)
    e = jnp.where(ok, jnp.exp(s - mx), 0.0)
    p_c = e / jnp.maximum(jnp.sum(e, axis=-1, keepdims=True), 1e-30)
    o_c = jnp.dot(p_c.astype(BF16), vc, preferred_element_type=F32)

    p_sum = p_c[0:Q_BLOCK]
    for h in range(1, NSA_HPG):
        p_sum = p_sum + p_c[h * Q_BLOCK:(h + 1) * Q_BLOCK]
    imp_t = lax.dot_general(c2st_ref[...], p_sum, NT_DIMS, preferred_element_type=F32,
                            precision=lax.Precision.HIGHEST)
    blk = lax.broadcasted_iota(jnp.int32, (n_sel, Q_BLOCK), 0)
    tq_l = q0 + lax.broadcasted_iota(jnp.int32, (n_sel, Q_BLOCK), 1)
    cur = tq_l // SEL_BLOCK
    forced = (blk == 0) | (blk == cur) | (blk == cur - 1)
    future = blk > cur
    val = jnp.where(forced, jnp.inf, jnp.where(future, -jnp.inf, imp_t))
    cnt = jnp.zeros((n_sel, Q_BLOCK), F32)
    for sp in range(n_sel):
        r = val[sp:sp + 1, :]
        ahead = (r > val) | ((r == val) & (blk > sp))
        cnt = cnt + jnp.where(ahead, 1.0, 0.0)
    sel_t = jnp.where(cnt < float(SEL_TOPK), 1.0, 0.0)
    sel_t = jnp.concatenate([sel_t, jnp.zeros((LANES - n_sel, Q_BLOCK), F32)], axis=0)
    sel = jnp.transpose(sel_t).astype(BF16)

    def attend(k_ref, v_ref, lo, hi, allowed_fn, m_init, l_init):
        m_scr[...] = m_init
        l_scr[...] = l_init
        acc_scr[...] = jnp.zeros(acc_scr.shape, F32)

        def body(kt, carry):
            k0 = pl.multiple_of(kt * LANES, LANES)
            k = k_ref[0, 0, pl.ds(k0, LANES), :]
            v = v_ref[0, 0, pl.ds(k0, LANES), :]
            sc = lax.dot_general(qs, k, NT_DIMS, preferred_element_type=F32)
            allowed = allowed_fn(kt, k0 + col)
            sc = jnp.where(allowed, sc, NEG)
            m_old = m_scr[...]
            m_new = jnp.maximum(m_old, jnp.max(sc, axis=-1, keepdims=True))
            alpha = jnp.exp(m_old - m_new)
            ex = jnp.where(allowed, jnp.exp(sc - m_new), 0.0)
            l_scr[...] = alpha * l_scr[...] + jnp.sum(ex, axis=-1, keepdims=True)
            acc_scr[...] = alpha * acc_scr[...] + jnp.dot(ex.astype(BF16), v, preferred_element_type=F32)
            m_scr[...] = m_new
            return carry

        lax.fori_loop(lo, hi, body, 0)
        return acc_scr[...] / jnp.maximum(l_scr[...], 1e-30)

    def sel_allowed(kt, kpos):
        mk = jnp.dot(sel, e3_ref[kt], preferred_element_type=F32)
        mk4 = jnp.concatenate([mk] * NSA_HPG, axis=0)
        return (mk4 > 0.5) & (kpos <= tq)

    m_empty = jnp.full((rows, 1), NEG, F32)
    l_empty = jnp.zeros((rows, 1), F32)
    o_s = attend(ks_ref, vs_ref, 0, i + 1, sel_allowed, m_empty, l_empty)

    def win_allowed(kt, kpos):
        return (kpos <= tq) & (kpos > tq - WINDOW)

    n_pad = jnp.maximum(WINDOW - 1 - tq, 0).astype(F32)
    m_pad = jnp.where(n_pad > 0.0, 0.0, NEG)
    o_w = attend(kw_ref, vw_ref, jnp.maximum(i - WINDOW // Q_BLOCK, 0), i + 1, win_allowed, m_pad, n_pad)

    gate = _sigmoid(g_ref[...])
    outs = []
    for h in range(NSA_HPG):
        sl = slice(h * Q_BLOCK, (h + 1) * Q_BLOCK)
        outs.append(gate[:, 3 * h:3 * h + 1] * o_c[sl]
                    + gate[:, 3 * h + 1:3 * h + 2] * o_s[sl]
                    + gate[:, 3 * h + 2:3 * h + 3] * o_w[sl])
    o_ref[...] = jnp.concatenate(outs, axis=1)


def _nsa(z, ks, vs, kw, vw, kvc, c2st, e3, *, batch, seq):
    nq = seq // Q_BLOCK
    n_sel = seq // SEL_BLOCK
    gw = NSA_HPG * HEAD_DIM
    kv_spec = pl.BlockSpec((1, 1, seq, HEAD_DIM), lambda b, g, i: (b, g, 0, 0))
    rows = NSA_HPG * Q_BLOCK
    return pl.pallas_call(
        functools.partial(_nsa_kernel, n_sel=n_sel),
        grid=(batch, NSA_GROUPS, nq),
        in_specs=[
            pl.BlockSpec((Q_BLOCK, gw), lambda b, g, i: (b * nq + i, OFF_Q // gw + g)),
            pl.BlockSpec((Q_BLOCK, LANES), lambda b, g, i: (b * nq + i, OFF_GATE // LANES + g)),
            kv_spec, kv_spec, kv_spec, kv_spec,
            pl.BlockSpec((LANES, 2 * HEAD_DIM), lambda b, g, i: (b * NSA_GROUPS + g, 0)),
            pl.BlockSpec((n_sel, LANES), lambda b, g, i: (0, 0)),
            pl.BlockSpec((nq, LANES, LANES), lambda b, g, i: (0, 0, 0)),
        ],
        out_specs=pl.BlockSpec((Q_BLOCK, gw), lambda b, g, i: (b * nq + i, g)),
        out_shape=jax.ShapeDtypeStruct((batch * seq, NSA_HEADS * HEAD_DIM), F32),
        scratch_shapes=[pltpu.VMEM((rows, 1), F32), pltpu.VMEM((rows, 1), F32),
                        pltpu.VMEM((rows, HEAD_DIM), F32)],
        compiler_params=pltpu.CompilerParams(
            dimension_semantics=("parallel", "parallel", "arbitrary"), vmem_limit_bytes=VMEM_LIMIT),
        name="nsa",
    )(z, z, ks, vs, kw, vw, kvc, c2st, e3)


def _rglru_kernel(x_ref, gr_ref, cw_ref, cb_ref, wa_ref, ba_ref, wi_ref, bi_ref, lam_ref, y_ref,
                  xprev_scr, h_scr):
    ts, c = x_ref.shape

    @pl.when(pl.program_id(1) == 0)
    def _():
        xprev_scr[...] = jnp.zeros(xprev_scr.shape, F32)
        h_scr[...] = jnp.zeros(h_scr.shape, F32)

    x = x_ref[...]
    xx = jnp.concatenate([xprev_scr[...], x], axis=0)
    xprev_scr[...] = x[ts - SUBLANES:, :]
    xc = cb_ref[...] + cw_ref[CONV_WIDTH - 1:CONV_WIDTH, :] * x
    for k in range(CONV_WIDTH - 1):
        off = SUBLANES - (CONV_WIDTH - 1) + k
        xc = xc + cw_ref[k:k + 1, :] * xx[off:off + ts, :]

    ra, ri = [], []
    for mblk in range(c // LANES):
        xb = xc[:, mblk * LANES:(mblk + 1) * LANES].astype(BF16)
        ra.append(jnp.dot(xb, wa_ref[mblk], preferred_element_type=F32))
        ri.append(jnp.dot(xb, wi_ref[mblk], preferred_element_type=F32))
    r = _sigmoid(jnp.concatenate(ra, axis=1) + ba_ref[...])
    gi = _sigmoid(jnp.concatenate(ri, axis=1) + bi_ref[...])

    nl = -lam_ref[...]
    softplus = jnp.maximum(nl, 0.0) + jnp.log1p(jnp.exp(-jnp.abs(nl)))
    log_a = (-RGLRU_C) * r * softplus
    a = jnp.exp(log_a)
    th = jnp.tanh(log_a)
    b = jnp.sqrt(-2.0 * th / (1.0 - th)) * (gi * xc)

    ridx = lax.broadcasted_iota(jnp.int32, (ts, c), 0)
    d = 1
    while d < ts:
        keep = ridx >= d
        a_sh = jnp.where(keep, pltpu.roll(a, d, 0), 1.0)
        b_sh = jnp.where(keep, pltpu.roll(b, d, 0), 0.0)
        b = a * b_sh + b
        a = a * a_sh
        d *= 2
    h = b + a * h_scr[...]
    h_scr[...] = h[ts - 1:ts, :]
    y_ref[...] = _gelu_tanh(gr_ref[...]) * h


def _rglru(z, cw, cb, wa2, ba, wi2, bi, lam, *, batch, seq, ts):
    c = D_MODEL
    nt = seq // ts
    vec = pl.BlockSpec((1, c), lambda b, t: (0, 0))
    return pl.pallas_call(
        _rglru_kernel,
        grid=(batch, nt),
        in_specs=[
            pl.BlockSpec((ts, c), lambda b, t: (b * nt + t, OFF_XRNN // c)),
            pl.BlockSpec((ts, c), lambda b, t: (b * nt + t, OFF_GRNN // c)),
            pl.BlockSpec((CONV_WIDTH, c), lambda b, t: (0, 0)),
            vec,
            pl.BlockSpec((c // LANES, LANES, LANES), lambda b, t: (0, 0, 0)),
            vec,
            pl.BlockSpec((c // LANES, LANES, LANES), lambda b, t: (0, 0, 0)),
            vec, vec,
        ],
        out_specs=pl.BlockSpec((ts, c), lambda b, t: (b * nt + t, 0)),
        out_shape=jax.ShapeDtypeStruct((batch * seq, c), F32),
        scratch_shapes=[pltpu.VMEM((SUBLANES, c), F32), pltpu.VMEM((1, c), F32)],
        compiler_params=pltpu.CompilerParams(
            dimension_semantics=("parallel", "arbitrary"), vmem_limit_bytes=VMEM_LIMIT),
        name="rglru",
    )(z, z, cw, cb, wa2, ba, wi2, bi, lam)


def _merge_kernel(x_ref, m0_ref, m1_ref, m2_ref, ynsa_ref, yrnn_ref, qx_ref, kvm_ref, wxo_ref, wo_ref, h_ref):
    qx = qx_ref[...] * SCALE
    kvm = kvm_ref[...]
    yx = None
    for h in range(XATTN_HEADS):
        lo = h * HEAD_DIM
        qh = qx[:, lo:lo + HEAD_DIM].astype(BF16)
        kh = kvm[:, lo:lo + HEAD_DIM].astype(BF16)
        vh = kvm[:, XATTN_WIDTH + lo:XATTN_WIDTH + lo + HEAD_DIM].astype(BF16)
        s = lax.dot_general(qh, kh, NT_DIMS, preferred_element_type=F32)
        e = jnp.exp(s - jnp.max(s, axis=-1, keepdims=True))
        p = e / jnp.sum(e, axis=-1, keepdims=True)
        oh = jnp.dot(p.astype(BF16), vh, preferred_element_type=F32)
        part = jnp.dot(oh.astype(BF16), wxo_ref[lo:lo + HEAD_DIM, :], preferred_element_type=F32)
        yx = part if yx is None else yx + part
    y = (_sigmoid(m0_ref[...]) * ynsa_ref[...] + _sigmoid(m1_ref[...]) * yrnn_ref[...]
         + _sigmoid(m2_ref[...]) * yx)
    h_ref[...] = x_ref[...] + jnp.dot(y.astype(BF16), wo_ref[...], preferred_element_type=F32)


def _merge(x, z, ynsa, yrnn, kvm, wxo, wo, *, seq, mem_len, tq):
    t, d = x.shape
    nt = seq // tq
    row = lambda cb: pl.BlockSpec((tq, d), lambda i, cb=cb: (i, cb))
    return pl.pallas_call(
        _merge_kernel,
        grid=(t // tq,),
        in_specs=[
            row(0),
            row(OFF_MERGE // d), row(OFF_MERGE // d + 1), row(OFF_MERGE // d + 2),
            row(0), row(0),
            pl.BlockSpec((tq, XATTN_WIDTH), lambda i: (i, OFF_QX // XATTN_WIDTH)),
            pl.BlockSpec((mem_len, 2 * XATTN_WIDTH), lambda i: (i // nt, 0)),
            pl.BlockSpec((XATTN_WIDTH, d), lambda i: (0, 0)),
            pl.BlockSpec((d, d), lambda i: (0, 0)),
        ],
        out_specs=row(0),
        out_shape=jax.ShapeDtypeStruct((t, d), F32),
        compiler_params=pltpu.CompilerParams(
            dimension_semantics=("parallel",), vmem_limit_bytes=VMEM_LIMIT),
        name="merge",
    )(x, z, z, z, ynsa, yrnn, z, kvm, wxo, wo)


def _mlp_kernel(h_ref, g_ref, wup_ref, wdn_ref, gf_ref, o_ref, v_scr, acc_scr):
    f = pl.program_id(1)

    @pl.when(f == 0)
    def _():
        h = h_ref[...]
        ms = jnp.mean(h * h, axis=-1, keepdims=True)
        v_scr[...] = (h * lax.rsqrt(ms + RMS_EPS) * g_ref[...]).astype(BF16)
        acc_scr[...] = jnp.zeros(acc_scr.shape, F32)

    up = jnp.dot(v_scr[...], wup_ref[...], preferred_element_type=F32)
    act = jnp.square(jnp.maximum(up, 0.0)).astype(BF16)
    acc_scr[...] += jnp.dot(act, wdn_ref[...], preferred_element_type=F32)

    @pl.when(f == pl.num_programs(1) - 1)
    def _():
        h2 = h_ref[...] + acc_scr[...]
        ms = jnp.mean(h2 * h2, axis=-1, keepdims=True)
        o_ref[...] = h2 * lax.rsqrt(ms + RMS_EPS) * gf_ref[...]


def _mlp(h, g, wup, wdn, gf, *, tm, tf):
    t, d = h.shape
    dff = wup.shape[1]
    return pl.pallas_call(
        _mlp_kernel,
        grid=(t // tm, dff // tf),
        in_specs=[
            pl.BlockSpec((tm, d), lambda i, f: (i, 0)),
            pl.BlockSpec((1, d), lambda i, f: (0, 0)),
            pl.BlockSpec((d, tf), lambda i, f: (0, f)),
            pl.BlockSpec((tf, d), lambda i, f: (f, 0)),
            pl.BlockSpec((1, d), lambda i, f: (0, 0)),
        ],
        out_specs=pl.BlockSpec((tm, d), lambda i, f: (i, 0)),
        out_shape=jax.ShapeDtypeStruct((t, d), F32),
        scratch_shapes=[pltpu.VMEM((tm, d), BF16), pltpu.VMEM((tm, d), F32)],
        compiler_params=pltpu.CompilerParams(
            dimension_semantics=("parallel", "arbitrary"), vmem_limit_bytes=VMEM_LIMIT),
        name="mlp",
    )(h, g, wup, wdn, gf)


def _rope_tables(pos, width):
    inv = 1.0 / (ROPE_THETA ** (jnp.arange(0, ROPE_DIM, 2, dtype=F32) / ROPE_DIM))
    ang = pos.astype(F32)[:, None] * inv[None, :]
    cos, sin = jnp.cos(ang), jnp.sin(ang)
    n = pos.shape[0]
    pad = HEAD_DIM - ROPE_DIM
    cos_h = jnp.concatenate([cos, cos, jnp.ones((n, pad), F32)], axis=1)
    sin_h = jnp.concatenate([-sin, sin, jnp.zeros((n, pad), F32)], axis=1)
    reps = width // HEAD_DIM
    return jnp.tile(cos_h, (1, reps)), jnp.tile(sin_h, (1, reps))


def _pack_w_in(w_in):
    d = w_in.shape[0]
    nq = NSA_HEADS * HEAD_DIM
    nkv = NSA_GROUPS * HEAD_DIM
    ng = NSA_HEADS * 3
    widths = (nq, nkv, nkv, nkv, nkv, nkv, nkv, ng, D_MODEL, D_MODEL, XATTN_WIDTH, 3 * D_MODEL)
    offs = np.concatenate([[0], np.cumsum(widths)])
    (q, kc, vc, ks, vs, kw, vw, gates, xr, gr, qx, mg) = [w_in[:, offs[k]:offs[k + 1]] for k in range(12)]
    per_group = ng // NSA_GROUPS
    gates = gates.reshape(d, NSA_GROUPS, per_group)
    gates = jnp.pad(gates, ((0, 0), (0, 0), (0, LANES - per_group))).reshape(d, NSA_GROUPS * LANES)
    packed = jnp.concatenate([mg, xr, gr, q, ks, kw, kc, vc, vs, vw, qx, gates], axis=1)
    packed = jnp.pad(packed, ((0, 0), (0, Z_WIDTH - packed.shape[1])))
    return packed.astype(BF16)


def _block_diag_pairs(w):
    nb, k, _ = w.shape
    w = w.reshape(nb // 2, 2, k, k)
    zero = jnp.zeros((nb // 2, k, k), w.dtype)
    top = jnp.concatenate([w[:, 0], zero], axis=2)
    bot = jnp.concatenate([zero, w[:, 1]], axis=2)
    return jnp.concatenate([top, bot], axis=1).astype(BF16)


def _cmp_to_sel_t(n_cmp, n_cmp_pad, n_sel):
    c0 = np.arange(n_cmp_pad)[None, :] * CMP_STRIDE
    s0 = np.arange(n_sel)[:, None] * SEL_BLOCK
    ov = np.clip(np.minimum(c0 + CMP_LEN, s0 + SEL_BLOCK) - np.maximum(c0, s0), 0, None)
    ov = np.where(np.arange(n_cmp_pad)[None, :] < n_cmp, ov, 0)
    return (ov / CMP_LEN).astype(np.float32)


def _block_expand(nq):
    kt = np.arange(nq)[:, None, None]
    s = np.arange(LANES)[None, :, None]
    c = np.arange(LANES)[None, None, :]
    return (s == (kt * LANES + c) // SEL_BLOCK).astype(np.float32)


def kernel(x, mem, g_mix, w_in, cmp_pos_k, cmp_pos_v, w_cmp_k1, w_cmp_k2, w_cmp_v1, w_cmp_v2, conv_w, conv_b, w_rg_a, b_rg_a, w_rg_i, b_rg_i, rg_lambda, g_mem, w_mem_kv, w_xo, w_o, g_mlp, w_up, w_down, g_final):
    batch, seq, d = x.shape
    mem_len = mem.shape[1]
    t = batch * seq
    depth = g_mix.shape[0]
    nq = seq // Q_BLOCK
    n_sel = seq // SEL_BLOCK
    n_cmp_pad = seq // CMP_STRIDE
    assert n_cmp_pad == LANES and n_sel <= LANES and d == D_MODEL

    cos_s, sin_s = _rope_tables(jnp.arange(seq), INPROJ_TN)
    cos_m = jnp.ones((mem_len, 2 * XATTN_WIDTH), F32)
    sin_m = jnp.zeros((mem_len, 2 * XATTN_WIDTH), F32)
    cmp_pos = jnp.arange(n_cmp_pad) * CMP_STRIDE + (CMP_LEN - 1)
    cos_c, sin_c = _rope_tables(cmp_pos, HEAD_DIM)
    cos_c = jnp.concatenate([cos_c, jnp.ones_like(cos_c)], axis=1)
    sin_c = jnp.concatenate([sin_c, jnp.zeros_like(sin_c)], axis=1)
    cmp_tm = 8 * n_cmp_pad
    cos_c = jnp.tile(cos_c, (cmp_tm // n_cmp_pad, 1))
    sin_c = jnp.tile(sin_c, (cmp_tm // n_cmp_pad, 1))
    c2st = jnp.asarray(_cmp_to_sel_t((seq - CMP_LEN) // CMP_STRIDE + 1, n_cmp_pad, n_sel))
    e3 = jnp.asarray(_block_expand(nq), dtype=BF16)

    h = x.reshape(t, d)
    for l in range(depth):
        z = _norm_proj(h, g_mix[l][None, :], _pack_w_in(w_in[l]), cos_s, sin_s,
                       tm=1024, tn=INPROJ_TN, rope_lo=ROPE_TILE_LO, rope_hi=ROPE_TILE_HI, name="inproj")

        def groups(off, dtype):
            a = z[:, off:off + NSA_GROUPS * HEAD_DIM].reshape(batch, seq, NSA_GROUPS, HEAD_DIM)
            return a.transpose(0, 2, 1, 3).astype(dtype)

        ks, vs = groups(OFF_KS, BF16), groups(OFF_VS, BF16)
        kw, vw = groups(OFF_KW, BF16), groups(OFF_VW, BF16)
        half = CMP_STRIDE * HEAD_DIM
        rk = groups(OFF_KC, F32).reshape(batch * NSA_GROUPS * n_cmp_pad, half)
        rv = groups(OFF_VC, F32).reshape(batch * NSA_GROUPS * n_cmp_pad, half)
        pk = cmp_pos_k[l].reshape(2, half)
        pv = cmp_pos_v[l].reshape(2, half)
        w2 = jnp.zeros((2 * CMP_HIDDEN, 2 * HEAD_DIM), F32)
        w2 = w2.at[:CMP_HIDDEN, :HEAD_DIM].set(w_cmp_k2[l]).at[CMP_HIDDEN:, HEAD_DIM:].set(w_cmp_v2[l])
        kvc = _compress(rk, rv, pk[0:1], pk[1:2], pv[0:1], pv[1:2],
                        w_cmp_k1[l][:half].astype(BF16), w_cmp_k1[l][half:].astype(BF16),
                        w_cmp_v1[l][:half].astype(BF16), w_cmp_v1[l][half:].astype(BF16),
                        w2.astype(BF16), cos_c, sin_c, tm=cmp_tm)
        y_nsa = _nsa(z, ks, vs, kw, vw, kvc, c2st, e3, batch=batch, seq=seq)

        y_rnn = _rglru(z, conv_w[l], conv_b[l][None, :], _block_diag_pairs(w_rg_a[l]), b_rg_a[l][None, :],
                       _block_diag_pairs(w_rg_i[l]), b_rg_i[l][None, :], rg_lambda[l][None, :],
                       batch=batch, seq=seq, ts=256)

        kvm = _norm_proj(mem.reshape(batch * mem_len, d), g_mem[l][None, :], w_mem_kv[l].astype(BF16),
                         cos_m, sin_m, tm=mem_len, tn=2 * XATTN_WIDTH, rope_lo=0, rope_hi=0, name="memkv")

        h1 = _merge(h, z, y_nsa, y_rnn, kvm, w_xo[l].astype(BF16), w_o[l].astype(BF16),
                    seq=seq, mem_len=mem_len, tq=256)
        last = l == depth - 1
        gf = g_final if last else jnp.ones_like(g_final)
        h = _mlp(h1, g_mlp[l][None, :], w_up[l].astype(BF16), w_down[l].astype(BF16), gf[None, :],
                 tm=1024, tf=1024)
        assert last, "final norm is fused into the last layer's MLP kernel"
    return h.reshape(batch, seq, d)
```

```python
import functools
import math

import numpy as np
import jax
import jax.numpy as jnp
from jax import lax
from jax.experimental import pallas as pl
from jax.experimental.pallas import tpu as pltpu

F32 = jnp.float32
BF16 = jnp.bfloat16

D_MODEL = 1024
HEAD_DIM = 64
NSA_HEADS = 16
NSA_GROUPS = 4
NSA_HPG = NSA_HEADS // NSA_GROUPS
CMP_LEN = 32
CMP_STRIDE = 16
CMP_HIDDEN = 256
SEL_BLOCK = 64
SEL_TOPK = 8
WINDOW = 512
Q_BLOCK = 128
ROPE_THETA = 500000.0
ROPE_DIM = HEAD_DIM // 4
ROPE_HALF = ROPE_DIM // 2
RNN_BLOCKS = 16
RNN_BLOCK_DIM = D_MODEL // RNN_BLOCKS
CONV_WIDTH = 4
RGLRU_C = 8.0
XATTN_HEADS = 4
XATTN_WIDTH = XATTN_HEADS * HEAD_DIM
D_FF = 4 * D_MODEL
RMS_EPS = 1e-6
SCALE = HEAD_DIM ** -0.5
LOG2E = math.log2(math.e)
NEG = -1e30

LANES = 128
SUBLANES = 8
VMEM_LIMIT = 48 * 1024 * 1024

OFF_MERGE = 0
OFF_XRNN = 3072
OFF_GRNN = 4096
OFF_Q = 5120
OFF_KS = 6144
OFF_KW = 6400
OFF_KC = 6656
OFF_VC = 6912
OFF_VS = 7168
OFF_VW = 7424
OFF_QX = 7680
OFF_GATE = 7936
Z_WIDTH = 8704
INPROJ_TN = 512
ROPE_TILE_LO = OFF_Q // INPROJ_TN
ROPE_TILE_HI = OFF_KC // INPROJ_TN

NSA_LANES = NSA_HPG * Q_BLOCK
SEL_TILE = 2 * Q_BLOCK
SEL_TILE_BLOCKS = SEL_TILE // SEL_BLOCK
WIN_TILES = WINDOW // Q_BLOCK + 1


def _sigmoid(x):
    return 1.0 / (1.0 + jnp.exp(-x))


def _gelu_tanh(x):
    return 0.5 * x * (1.0 + jnp.tanh(0.7978845608028654 * (x + 0.044715 * (x * x * x))))


def _rope_swap(z):
    n = z.shape[-1]
    lane = lax.broadcasted_iota(jnp.int32, z.shape, z.ndim - 1)
    first_half = (lane & (HEAD_DIM - 1)) < ROPE_HALF
    return jnp.where(first_half, pltpu.roll(z, n - ROPE_HALF, z.ndim - 1), pltpu.roll(z, ROPE_HALF, z.ndim - 1))


def _fold_rows(x, op):
    parts = [x[r * SUBLANES:(r + 1) * SUBLANES] for r in range(x.shape[0] // SUBLANES)]
    return functools.reduce(op, parts)


def _norm_proj_kernel(x_ref, g_ref, w_ref, cos_ref, sin_ref, z_ref, u_scr, *, rope_lo, rope_hi):
    j = pl.program_id(1)

    @pl.when(j == 0)
    def _():
        x = x_ref[...]
        ms = jnp.mean(x * x, axis=-1, keepdims=True)
        u_scr[...] = (x * lax.rsqrt(ms + RMS_EPS) * g_ref[...]).astype(BF16)

    z = jnp.dot(u_scr[...], w_ref[...], preferred_element_type=F32)
    is_rope = jnp.logical_and(j >= rope_lo, j < rope_hi)

    @pl.when(is_rope)
    def _():
        z_ref[...] = z * cos_ref[...] + _rope_swap(z) * sin_ref[...]

    @pl.when(jnp.logical_not(is_rope))
    def _():
        z_ref[...] = z


def _norm_proj(x, g, w, cos_t, sin_t, *, tm, tn, rope_lo, rope_hi, name):
    t, d = x.shape
    n = w.shape[1]
    s_tiles = cos_t.shape[0] // tm
    return pl.pallas_call(
        functools.partial(_norm_proj_kernel, rope_lo=rope_lo, rope_hi=rope_hi),
        grid=(t // tm, n // tn),
        in_specs=[
            pl.BlockSpec((tm, d), lambda i, j: (i, 0)),
            pl.BlockSpec((1, d), lambda i, j: (0, 0)),
            pl.BlockSpec((d, tn), lambda i, j: (0, j)),
            pl.BlockSpec((tm, tn), lambda i, j: (i % s_tiles, 0)),
            pl.BlockSpec((tm, tn), lambda i, j: (i % s_tiles, 0)),
        ],
        out_specs=pl.BlockSpec((tm, tn), lambda i, j: (i, j)),
        out_shape=jax.ShapeDtypeStruct((t, n), F32),
        scratch_shapes=[pltpu.VMEM((tm, d), BF16)],
        compiler_params=pltpu.CompilerParams(
            dimension_semantics=("parallel", "arbitrary"), vmem_limit_bytes=VMEM_LIMIT),
        name=name,
    )(x, g, w, cos_t, sin_t)


def _compress_kernel(rk_ref, rv_ref, pka_ref, pkb_ref, pva_ref, pvb_ref,
                     wk1a_ref, wk1b_ref, wv1a_ref, wv1b_ref, w2_ref, cos_ref, sin_ref, o_ref):
    m = rk_ref.shape[0]

    def hidden(r_ref, pa_ref, pb_ref, w1a_ref, w1b_ref):
        r = r_ref[...]
        pa = jnp.dot((r + pa_ref[...]).astype(BF16), w1a_ref[...], preferred_element_type=F32)
        pb = jnp.dot((r + pb_ref[...]).astype(BF16), w1b_ref[...], preferred_element_type=F32)
        return _gelu_tanh(pa + pltpu.roll(pb, m - 1, 0)).astype(BF16)

    hk = hidden(rk_ref, pka_ref, pkb_ref, wk1a_ref, wk1b_ref)
    hv = hidden(rv_ref, pva_ref, pvb_ref, wv1a_ref, wv1b_ref)
    hcat = jnp.concatenate([hk, hv], axis=1)
    kv = jnp.dot(hcat, w2_ref[...], preferred_element_type=F32)
    o_ref[...] = kv * cos_ref[...] + _rope_swap(kv) * sin_ref[...]


def _compress(rk, rv, pka, pkb, pva, pvb, wk1a, wk1b, wv1a, wv1b, w2, cos_t, sin_t, *, tm):
    m, k = rk.shape
    full = lambda shape: pl.BlockSpec(shape, lambda i: (0, 0))
    return pl.pallas_call(
        _compress_kernel,
        grid=(m // tm,),
        in_specs=[
            pl.BlockSpec((tm, k), lambda i: (i, 0)),
            pl.BlockSpec((tm, k), lambda i: (i, 0)),
            full((1, k)), full((1, k)), full((1, k)), full((1, k)),
            full((k, CMP_HIDDEN)), full((k, CMP_HIDDEN)), full((k, CMP_HIDDEN)), full((k, CMP_HIDDEN)),
            full((2 * CMP_HIDDEN, 2 * HEAD_DIM)),
            full((tm, 2 * HEAD_DIM)), full((tm, 2 * HEAD_DIM)),
        ],
        out_specs=pl.BlockSpec((tm, 2 * HEAD_DIM), lambda i: (i, 0)),
        out_shape=jax.ShapeDtypeStruct((m, 2 * HEAD_DIM), F32),
        compiler_params=pltpu.CompilerParams(
            dimension_semantics=("parallel",), vmem_limit_bytes=VMEM_LIMIT),
        name="compress",
    )(rk, rv, pka, pkb, pva, pvb, wk1a, wk1b, wv1a, wv1b, w2, cos_t, sin_t)


def _nsa_kernel(q_ref, g_ref, ks_ref, vst_ref, kw_ref, vwt_ref, kvc_ref, c2st_ref, o_ref,
                s_scr, bias_scr, *, n_sel):
    i = pl.program_id(2)
    q0 = i * Q_BLOCK

    def heads_on_lanes(x):
        xt = jnp.transpose(x)
        return jnp.concatenate([xt[h * HEAD_DIM:(h + 1) * HEAD_DIM, :] for h in range(NSA_HPG)], axis=1)

    def tile_heads(x):
        return jnp.concatenate([x] * NSA_HPG, axis=1)

    qt = heads_on_lanes(q_ref[...] * (SCALE * LOG2E)).astype(BF16)
    tq = q0 + (lax.broadcasted_iota(jnp.int32, (1, NSA_LANES), 1) & (Q_BLOCK - 1))

    kvc = kvc_ref[...]
    kc = kvc[:, :HEAD_DIM].astype(BF16)
    vct = jnp.transpose(kvc)[HEAD_DIM:, :].astype(BF16)
    sc = jnp.dot(kc, qt, preferred_element_type=F32)
    n_idx = lax.broadcasted_iota(jnp.int32, (LANES, 1), 0)
    ok = (n_idx * CMP_STRIDE + (CMP_LEN - 1)) <= tq
    sc = jnp.where(ok, sc, NEG)
    m_c = jnp.max(sc, axis=0, keepdims=True)
    e_c = jnp.where(ok, jnp.exp2(sc - m_c), 0.0)
    p_c = e_c * (1.0 / jnp.maximum(jnp.sum(e_c, axis=0, keepdims=True), 1e-30))
    o_c = jnp.dot(vct, p_c.astype(BF16), preferred_element_type=F32)

    p_sum = p_c[:, 0:Q_BLOCK]
    for h in range(1, NSA_HPG):
        p_sum = p_sum + p_c[:, h * Q_BLOCK:(h + 1) * Q_BLOCK]
    imp = jnp.dot(c2st_ref[...], p_sum, preferred_element_type=F32,
                  precision=lax.Precision.HIGHEST)
    blk = lax.broadcasted_iota(jnp.int32, (n_sel, Q_BLOCK), 0)
    cur = (q0 + lax.broadcasted_iota(jnp.int32, (n_sel, Q_BLOCK), 1)) // SEL_BLOCK
    forced = (blk == 0) | (blk == cur) | (blk == cur - 1)
    val = jnp.where(forced, jnp.inf, jnp.where(blk > cur, -jnp.inf, imp))
    cnt = jnp.zeros((n_sel, Q_BLOCK), F32)
    for sp in range(n_sel):
        r = val[sp:sp + 1, :]
        ahead = (r > val) | ((r == val) & (blk > sp))
        cnt = cnt + jnp.where(ahead, 1.0, 0.0)
    bias_scr[...] = tile_heads(jnp.where(cnt < float(SEL_TOPK), 0.0, NEG))

    def block_rows(s, b):
        return s[b * SEL_BLOCK:(b + 1) * SEL_BLOCK]

    def scores_pass(j, m8, causal):
        k0 = pl.multiple_of(j * SEL_TILE, SEL_TILE)
        s = jnp.dot(ks_ref[0, 0, pl.ds(k0, SEL_TILE), :], qt, preferred_element_type=F32)
        if causal:
            kpos = k0 + lax.broadcasted_iota(jnp.int32, (SEL_TILE, 1), 0)
            s = s + jnp.where(kpos <= tq, 0.0, NEG)
        s_scr[pl.ds(k0, SEL_TILE), :] = s
        for b in range(SEL_TILE_BLOCKS):
            brow = bias_scr[pl.ds(j * SEL_TILE_BLOCKS + b, 1), :]
            m8 = jnp.maximum(m8, _fold_rows(block_rows(s, b), jnp.maximum) + brow)
        return m8

    def probs_pass(j, carry, m_row):
        l8, acc = carry
        k0 = pl.multiple_of(j * SEL_TILE, SEL_TILE)
        s = s_scr[pl.ds(k0, SEL_TILE), :]
        ps = []
        for b in range(SEL_TILE_BLOCKS):
            brow = bias_scr[pl.ds(j * SEL_TILE_BLOCKS + b, 1), :] - m_row
            p = jnp.exp2(block_rows(s, b) + brow)
            l8 = l8 + _fold_rows(p, jnp.add)
            ps.append(p.astype(BF16))
        acc = acc + jnp.dot(vst_ref[0, 0, j], jnp.concatenate(ps, axis=0), preferred_element_type=F32)
        return l8, acc

    n_full = i // 2
    m8 = lax.fori_loop(0, n_full, lambda j, m: scores_pass(j, m, False),
                       jnp.full((SUBLANES, NSA_LANES), NEG, F32))
    m8 = scores_pass(n_full, m8, True)
    m_s = jnp.max(m8, axis=0, keepdims=True)
    carry = (jnp.zeros((SUBLANES, NSA_LANES), F32), jnp.zeros((HEAD_DIM, NSA_LANES), F32))
    carry = lax.fori_loop(0, n_full, lambda j, c: probs_pass(j, c, m_s), carry)
    l8, acc = probs_pass(n_full, carry, m_s)
    o_s = acc * (1.0 / jnp.maximum(jnp.sum(l8, axis=0, keepdims=True), 1e-30))

    c_idx = lax.broadcasted_iota(jnp.int32, (Q_BLOCK, 1), 0)
    r_idx = lax.broadcasted_iota(jnp.int32, (1, Q_BLOCK), 1)
    tri_diag = jnp.where(c_idx <= r_idx, 0.0, NEG)
    tri_old = jnp.where(c_idx > r_idx, 0.0, NEG)
    tiles = []
    for d in range(1 - WIN_TILES, 1):
        td = i + d
        tc = jnp.maximum(td, 0)
        k0 = pl.multiple_of(tc * Q_BLOCK, Q_BLOCK)
        s = jnp.dot(kw_ref[0, 0, pl.ds(k0, Q_BLOCK), :], qt, preferred_element_type=F32)
        if d == 0:
            s = s + tile_heads(tri_diag)
        else:
            before_start = jnp.where(td >= 0, 0.0, NEG)
            s = s + (tile_heads(tri_old + before_start) if d == 1 - WIN_TILES else before_start)
        tiles.append((tc, s))
    m_w = jnp.max(functools.reduce(jnp.maximum, [_fold_rows(s, jnp.maximum) for _, s in tiles]),
                  axis=0, keepdims=True)
    n_pad = jnp.maximum(WINDOW - 1 - tq, 0).astype(F32)
    m_w = jnp.where(n_pad > 0.0, jnp.maximum(m_w, 0.0), m_w)
    l8 = jnp.zeros((SUBLANES, NSA_LANES), F32)
    acc = jnp.zeros((HEAD_DIM, NSA_LANES), F32)
    for tc, s in tiles:
        p = jnp.exp2(s - m_w)
        l8 = l8 + _fold_rows(p, jnp.add)
        acc = acc + jnp.dot(vwt_ref[0, 0, tc], p.astype(BF16), preferred_element_type=F32)
    l_w = jnp.sum(l8, axis=0, keepdims=True) + n_pad * jnp.exp2(-m_w)
    o_w = acc * (1.0 / jnp.maximum(l_w, 1e-30))

    gt = jnp.transpose(_sigmoid(g_ref[...]))
    outs = []
    for h in range(NSA_HPG):
        sl = slice(h * Q_BLOCK, (h + 1) * Q_BLOCK)
        outs.append(gt[3 * h:3 * h + 1, :] * o_c[:, sl] + gt[3 * h + 1:3 * h + 2, :] * o_s[:, sl]
                    + gt[3 * h + 2:3 * h + 3, :] * o_w[:, sl])
    o_ref[...] = jnp.transpose(jnp.concatenate(outs, axis=0))


def _nsa(z, ks, vst, kw, vwt, kvc, c2st, *, batch, seq):
    nq = seq // Q_BLOCK
    n_sel = seq // SEL_BLOCK
    gw = NSA_HPG * HEAD_DIM
    k_spec = pl.BlockSpec((1, 1, seq, HEAD_DIM), lambda b, g, i: (b, g, 0, 0))
    vt_spec = lambda tile: pl.BlockSpec((1, 1, seq // tile, HEAD_DIM, tile), lambda b, g, i: (b, g, 0, 0, 0))
    return pl.pallas_call(
        functools.partial(_nsa_kernel, n_sel=n_sel),
        grid=(batch, NSA_GROUPS, nq),
        in_specs=[
            pl.BlockSpec((Q_BLOCK, gw), lambda b, g, i: (b * nq + i, OFF_Q // gw + g)),
            pl.BlockSpec((Q_BLOCK, LANES), lambda b, g, i: (b * nq + i, OFF_GATE // LANES + g)),
            k_spec, vt_spec(SEL_TILE), k_spec, vt_spec(Q_BLOCK),
            pl.BlockSpec((LANES, 2 * HEAD_DIM), lambda b, g, i: (b * NSA_GROUPS + g, 0)),
            pl.BlockSpec((n_sel, LANES), lambda b, g, i: (0, 0)),
        ],
        out_specs=pl.BlockSpec((Q_BLOCK, gw), lambda b, g, i: (b * nq + i, g)),
        out_shape=jax.ShapeDtypeStruct((batch * seq, NSA_HEADS * HEAD_DIM), F32),
        scratch_shapes=[pltpu.VMEM((seq, NSA_LANES), F32), pltpu.VMEM((n_sel, NSA_LANES), F32)],
        compiler_params=pltpu.CompilerParams(
            dimension_semantics=("parallel", "parallel", "arbitrary"), vmem_limit_bytes=VMEM_LIMIT),
        name="nsa",
    )(z, z, ks, vst, kw, vwt, kvc, c2st)


def _rglru_kernel(x_ref, gr_ref, cw_ref, cb_ref, wa_ref, ba_ref, wi_ref, bi_ref, lam_ref, y_ref,
                  xprev_scr, h_scr):
    ts, c = x_ref.shape

    @pl.when(pl.program_id(1) == 0)
    def _():
        xprev_scr[...] = jnp.zeros(xprev_scr.shape, F32)
        h_scr[...] = jnp.zeros(h_scr.shape, F32)

    x = x_ref[...]
    xx = jnp.concatenate([xprev_scr[...], x], axis=0)
    xprev_scr[...] = x[ts - SUBLANES:, :]
    xc = cb_ref[...] + cw_ref[CONV_WIDTH - 1:CONV_WIDTH, :] * x
    for k in range(CONV_WIDTH - 1):
        off = SUBLANES - (CONV_WIDTH - 1) + k
        xc = xc + cw_ref[k:k + 1, :] * xx[off:off + ts, :]

    ra, ri = [], []
    for mblk in range(c // LANES):
        xb = xc[:, mblk * LANES:(mblk + 1) * LANES].astype(BF16)
        ra.append(jnp.dot(xb, wa_ref[mblk], preferred_element_type=F32))
        ri.append(jnp.dot(xb, wi_ref[mblk], preferred_element_type=F32))
    r = _sigmoid(jnp.concatenate(ra, axis=1) + ba_ref[...])
    gi = _sigmoid(jnp.concatenate(ri, axis=1) + bi_ref[...])

    nl = -lam_ref[...]
    softplus = jnp.maximum(nl, 0.0) + jnp.log1p(jnp.exp(-jnp.abs(nl)))
    log_a = (-RGLRU_C) * r * softplus
    a = jnp.exp(log_a)
    th = jnp.tanh(log_a)
    b = jnp.sqrt(-2.0 * th / (1.0 - th)) * (gi * xc)

    ridx = lax.broadcasted_iota(jnp.int32, (ts, c), 0)
    d = 1
    while d < ts:
        keep = ridx >= d
        a_sh = jnp.where(keep, pltpu.roll(a, d, 0), 1.0)
        b_sh = jnp.where(keep, pltpu.roll(b, d, 0), 0.0)
        b = a * b_sh + b
        a = a * a_sh
        d *= 2
    h = b + a * h_scr[...]
    h_scr[...] = h[ts - 1:ts, :]
    y_ref[...] = _gelu_tanh(gr_ref[...]) * h


def _rglru(z, cw, cb, wa2, ba, wi2, bi, lam, *, batch, seq, ts):
    c = D_MODEL
    nt = seq // ts
    vec = pl.BlockSpec((1, c), lambda b, t: (0, 0))
    return pl.pallas_call(
        _rglru_kernel,
        grid=(batch, nt),
        in_specs=[
            pl.BlockSpec((ts, c), lambda b, t: (b * nt + t, OFF_XRNN // c)),
            pl.BlockSpec((ts, c), lambda b, t: (b * nt + t, OFF_GRNN // c)),
            pl.BlockSpec((CONV_WIDTH, c), lambda b, t: (0, 0)),
            vec,
            pl.BlockSpec((c // LANES, LANES, LANES), lambda b, t: (0, 0, 0)),
            vec,
            pl.BlockSpec((c // LANES, LANES, LANES), lambda b, t: (0, 0, 0)),
            vec, vec,
        ],
        out_specs=pl.BlockSpec((ts, c), lambda b, t: (b * nt + t, 0)),
        out_shape=jax.ShapeDtypeStruct((batch * seq, c), F32),
        scratch_shapes=[pltpu.VMEM((SUBLANES, c), F32), pltpu.VMEM((1, c), F32)],
        compiler_params=pltpu.CompilerParams(
            dimension_semantics=("parallel", "arbitrary"), vmem_limit_bytes=VMEM_LIMIT),
        name="rglru",
    )(z, z, cw, cb, wa2, ba, wi2, bi, lam)


NT_DIMS = (((1,), (1,)), ((), ()))


def _merge_kernel(x_ref, m0_ref, m1_ref, m2_ref, ynsa_ref, yrnn_ref, qx_ref, kvm_ref, wxo_ref, wo_ref, h_ref):
    qx = qx_ref[...] * SCALE
    kvm = kvm_ref[...]
    yx = None
    for h in range(XATTN_HEADS):
        lo = h * HEAD_DIM
        qh = qx[:, lo:lo + HEAD_DIM].astype(BF16)
        kh = kvm[:, lo:lo + HEAD_DIM].astype(BF16)
        vh = kvm[:, XATTN_WIDTH + lo:XATTN_WIDTH + lo + HEAD_DIM].astype(BF16)
        s = lax.dot_general(qh, kh, NT_DIMS, preferred_element_type=F32)
        e = jnp.exp(s - jnp.max(s, axis=-1, keepdims=True))
        p = e / jnp.sum(e, axis=-1, keepdims=True)
        oh = jnp.dot(p.astype(BF16), vh, preferred_element_type=F32)
        part = jnp.dot(oh.astype(BF16), wxo_ref[lo:lo + HEAD_DIM, :], preferred_element_type=F32)
        yx = part if yx is None else yx + part
    y = (_sigmoid(m0_ref[...]) * ynsa_ref[...] + _sigmoid(m1_ref[...]) * yrnn_ref[...]
         + _sigmoid(m2_ref[...]) * yx)
    h_ref[...] = x_ref[...] + jnp.dot(y.astype(BF16), wo_ref[...], preferred_element_type=F32)


def _merge(x, z, ynsa, yrnn, kvm, wxo, wo, *, seq, mem_len, tq):
    t, d = x.shape
    nt = seq // tq
    row = lambda cb: pl.BlockSpec((tq, d), lambda i, cb=cb: (i, cb))
    return pl.pallas_call(
        _merge_kernel,
        grid=(t // tq,),
        in_specs=[
            row(0),
            row(OFF_MERGE // d), row(OFF_MERGE // d + 1), row(OFF_MERGE // d + 2),
            row(0), row(0),
            pl.BlockSpec((tq, XATTN_WIDTH), lambda i: (i, OFF_QX // XATTN_WIDTH)),
            pl.BlockSpec((mem_len, 2 * XATTN_WIDTH), lambda i: (i // nt, 0)),
            pl.BlockSpec((XATTN_WIDTH, d), lambda i: (0, 0)),
            pl.BlockSpec((d, d), lambda i: (0, 0)),
        ],
        out_specs=row(0),
        out_shape=jax.ShapeDtypeStruct((t, d), F32),
        compiler_params=pltpu.CompilerParams(
            dimension_semantics=("parallel",), vmem_limit_bytes=VMEM_LIMIT),
        name="merge",
    )(x, z, z, z, ynsa, yrnn, z, kvm, wxo, wo)


def _mlp_kernel(h_ref, g_ref, wup_ref, wdn_ref, gf_ref, o_ref, v_scr, acc_scr):
    f = pl.program_id(1)

    @pl.when(f == 0)
    def _():
        h = h_ref[...]
        ms = jnp.mean(h * h, axis=-1, keepdims=True)
        v_scr[...] = (h * lax.rsqrt(ms + RMS_EPS) * g_ref[...]).astype(BF16)
        acc_scr[...] = jnp.zeros(acc_scr.shape, F32)

    up = jnp.dot(v_scr[...], wup_ref[...], preferred_element_type=F32)
    act = jnp.square(jnp.maximum(up, 0.0)).astype(BF16)
    acc_scr[...] += jnp.dot(act, wdn_ref[...], preferred_element_type=F32)

    @pl.when(f == pl.num_programs(1) - 1)
    def _():
        h2 = h_ref[...] + acc_scr[...]
        ms = jnp.mean(h2 * h2, axis=-1, keepdims=True)
        o_ref[...] = h2 * lax.rsqrt(ms + RMS_EPS) * gf_ref[...]


def _mlp(h, g, wup, wdn, gf, *, tm, tf):
    t, d = h.shape
    dff = wup.shape[1]
    return pl.pallas_call(
        _mlp_kernel,
        grid=(t // tm, dff // tf),
        in_specs=[
            pl.BlockSpec((tm, d), lambda i, f: (i, 0)),
            pl.BlockSpec((1, d), lambda i, f: (0, 0)),
            pl.BlockSpec((d, tf), lambda i, f: (0, f)),
            pl.BlockSpec((tf, d), lambda i, f: (f, 0)),
            pl.BlockSpec((1, d), lambda i, f: (0, 0)),
        ],
        out_specs=pl.BlockSpec((tm, d), lambda i, f: (i, 0)),
        out_shape=jax.ShapeDtypeStruct((t, d), F32),
        scratch_shapes=[pltpu.VMEM((tm, d), BF16), pltpu.VMEM((tm, d), F32)],
        compiler_params=pltpu.CompilerParams(
            dimension_semantics=("parallel", "arbitrary"), vmem_limit_bytes=VMEM_LIMIT),
        name="mlp",
    )(h, g, wup, wdn, gf)


def _rope_tables(pos, width):
    inv = 1.0 / (ROPE_THETA ** (jnp.arange(0, ROPE_DIM, 2, dtype=F32) / ROPE_DIM))
    ang = pos.astype(F32)[:, None] * inv[None, :]
    cos, sin = jnp.cos(ang), jnp.sin(ang)
    n = pos.shape[0]
    pad = HEAD_DIM - ROPE_DIM
    cos_h = jnp.concatenate([cos, cos, jnp.ones((n, pad), F32)], axis=1)
    sin_h = jnp.concatenate([-sin, sin, jnp.zeros((n, pad), F32)], axis=1)
    reps = width // HEAD_DIM
    return jnp.tile(cos_h, (1, reps)), jnp.tile(sin_h, (1, reps))


def _pack_w_in(w_in):
    d = w_in.shape[0]
    nq = NSA_HEADS * HEAD_DIM
    nkv = NSA_GROUPS * HEAD_DIM
    ng = NSA_HEADS * 3
    widths = (nq, nkv, nkv, nkv, nkv, nkv, nkv, ng, D_MODEL, D_MODEL, XATTN_WIDTH, 3 * D_MODEL)
    offs = np.concatenate([[0], np.cumsum(widths)])
    (q, kc, vc, ks, vs, kw, vw, gates, xr, gr, qx, mg) = [w_in[:, offs[k]:offs[k + 1]] for k in range(12)]
    per_group = ng // NSA_GROUPS
    gates = gates.reshape(d, NSA_GROUPS, per_group)
    gates = jnp.pad(gates, ((0, 0), (0, 0), (0, LANES - per_group))).reshape(d, NSA_GROUPS * LANES)
    packed = jnp.concatenate([mg, xr, gr, q, ks, kw, kc, vc, vs, vw, qx, gates], axis=1)
    packed = jnp.pad(packed, ((0, 0), (0, Z_WIDTH - packed.shape[1])))
    return packed.astype(BF16)


def _block_diag_pairs(w):
    nb, k, _ = w.shape
    w = w.reshape(nb // 2, 2, k, k)
    zero = jnp.zeros((nb // 2, k, k), w.dtype)
    top = jnp.concatenate([w[:, 0], zero], axis=2)
    bot = jnp.concatenate([zero, w[:, 1]], axis=2)
    return jnp.concatenate([top, bot], axis=1).astype(BF16)


def _cmp_to_sel_t(n_cmp, n_cmp_pad, n_sel):
    c0 = np.arange(n_cmp_pad)[None, :] * CMP_STRIDE
    s0 = np.arange(n_sel)[:, None] * SEL_BLOCK
    ov = np.clip(np.minimum(c0 + CMP_LEN, s0 + SEL_BLOCK) - np.maximum(c0, s0), 0, None)
    ov = np.where(np.arange(n_cmp_pad)[None, :] < n_cmp, ov, 0)
    return (ov / CMP_LEN).astype(np.float32)


def kernel(x, mem, g_mix, w_in, cmp_pos_k, cmp_pos_v, w_cmp_k1, w_cmp_k2, w_cmp_v1, w_cmp_v2, conv_w, conv_b, w_rg_a, b_rg_a, w_rg_i, b_rg_i, rg_lambda, g_mem, w_mem_kv, w_xo, w_o, g_mlp, w_up, w_down, g_final):
    batch, seq, d = x.shape
    mem_len = mem.shape[1]
    t = batch * seq
    depth = g_mix.shape[0]
    n_sel = seq // SEL_BLOCK
    n_cmp_pad = seq // CMP_STRIDE
    assert n_cmp_pad == LANES and n_sel <= LANES and d == D_MODEL

    cos_s, sin_s = _rope_tables(jnp.arange(seq), INPROJ_TN)
    cos_m = jnp.ones((mem_len, 2 * XATTN_WIDTH), F32)
    sin_m = jnp.zeros((mem_len, 2 * XATTN_WIDTH), F32)
    cmp_pos = jnp.arange(n_cmp_pad) * CMP_STRIDE + (CMP_LEN - 1)
    cos_c, sin_c = _rope_tables(cmp_pos, HEAD_DIM)
    cos_c = jnp.concatenate([cos_c, jnp.ones_like(cos_c)], axis=1)
    sin_c = jnp.concatenate([sin_c, jnp.zeros_like(sin_c)], axis=1)
    cmp_tm = 8 * n_cmp_pad
    cos_c = jnp.tile(cos_c, (cmp_tm // n_cmp_pad, 1))
    sin_c = jnp.tile(sin_c, (cmp_tm // n_cmp_pad, 1))
    c2st = jnp.asarray(_cmp_to_sel_t((seq - CMP_LEN) // CMP_STRIDE + 1, n_cmp_pad, n_sel))

    h = x.reshape(t, d)
    for l in range(depth):
        z = _norm_proj(h, g_mix[l][None, :], _pack_w_in(w_in[l]), cos_s, sin_s,
                       tm=1024, tn=INPROJ_TN, rope_lo=ROPE_TILE_LO, rope_hi=ROPE_TILE_HI, name="inproj")

        def groups(off, dtype):
            a = z[:, off:off + NSA_GROUPS * HEAD_DIM].reshape(batch, seq, NSA_GROUPS, HEAD_DIM)
            return a.transpose(0, 2, 1, 3).astype(dtype)

        def groups_t(off, tile):
            a = z[:, off:off + NSA_GROUPS * HEAD_DIM].reshape(batch, seq // tile, tile, NSA_GROUPS, HEAD_DIM)
            return a.transpose(0, 3, 1, 4, 2).astype(BF16)

        ks, vst = groups(OFF_KS, BF16), groups_t(OFF_VS, SEL_TILE)
        kw, vwt = groups(OFF_KW, BF16), groups_t(OFF_VW, Q_BLOCK)
        half = CMP_STRIDE * HEAD_DIM
        rk = groups(OFF_KC, F32).reshape(batch * NSA_GROUPS * n_cmp_pad, half)
        rv = groups(OFF_VC, F32).reshape(batch * NSA_GROUPS * n_cmp_pad, half)
        pk = cmp_pos_k[l].reshape(2, half)
        pv = cmp_pos_v[l].reshape(2, half)
        w2 = jnp.zeros((2 * CMP_HIDDEN, 2 * HEAD_DIM), F32)
        w2 = w2.at[:CMP_HIDDEN, :HEAD_DIM].set(w_cmp_k2[l]).at[CMP_HIDDEN:, HEAD_DIM:].set(w_cmp_v2[l])
        kvc = _compress(rk, rv, pk[0:1], pk[1:2], pv[0:1], pv[1:2],
                        w_cmp_k1[l][:half].astype(BF16), w_cmp_k1[l][half:].astype(BF16),
                        w_cmp_v1[l][:half].astype(BF16), w_cmp_v1[l][half:].astype(BF16),
                        w2.astype(BF16), cos_c, sin_c, tm=cmp_tm)
        y_nsa = _nsa(z, ks, vst, kw, vwt, kvc, c2st, batch=batch, seq=seq)

        y_rnn = _rglru(z, conv_w[l], conv_b[l][None, :], _block_diag_pairs(w_rg_a[l]), b_rg_a[l][None, :],
                       _block_diag_pairs(w_rg_i[l]), b_rg_i[l][None, :], rg_lambda[l][None, :],
                       batch=batch, seq=seq, ts=256)

        kvm = _norm_proj(mem.reshape(batch * mem_len, d), g_mem[l][None, :], w_mem_kv[l].astype(BF16),
                         cos_m, sin_m, tm=mem_len, tn=2 * XATTN_WIDTH, rope_lo=0, rope_hi=0, name="memkv")

        h1 = _merge(h, z, y_nsa, y_rnn, kvm, w_xo[l].astype(BF16), w_o[l].astype(BF16),
                    seq=seq, mem_len=mem_len, tq=256)
        last = l == depth - 1
        gf = g_final if last else jnp.ones_like(g_final)
        h = _mlp(h1, g_mlp[l][None, :], w_up[l].astype(BF16), w_down[l].astype(BF16), gf[None, :],
                 tm=1024, tf=1024)
        assert last, "final norm is fused into the last layer's MLP kernel"
    return h.reshape(batch, seq, d)
```

```python
import functools
import math

import numpy as np
import jax
import jax.numpy as jnp
from jax import lax
from jax.experimental import pallas as pl
from jax.experimental.pallas import tpu as pltpu

F32 = jnp.float32
BF16 = jnp.bfloat16

D_MODEL = 1024
HEAD_DIM = 64
NSA_HEADS = 16
NSA_GROUPS = 4
NSA_HPG = NSA_HEADS // NSA_GROUPS
CMP_LEN = 32
CMP_STRIDE = 16
CMP_HIDDEN = 256
SEL_BLOCK = 64
SEL_TOPK = 8
WINDOW = 512
Q_BLOCK = 128
ROPE_THETA = 500000.0
ROPE_DIM = HEAD_DIM // 4
ROPE_HALF = ROPE_DIM // 2
RNN_BLOCKS = 16
RNN_BLOCK_DIM = D_MODEL // RNN_BLOCKS
CONV_WIDTH = 4
RGLRU_C = 8.0
XATTN_HEADS = 4
XATTN_WIDTH = XATTN_HEADS * HEAD_DIM
D_FF = 4 * D_MODEL
RMS_EPS = 1e-6
SCALE = HEAD_DIM ** -0.5
LOG2E = math.log2(math.e)
NEG = -1e30

LANES = 128
SUBLANES = 8
VMEM_LIMIT = 48 * 1024 * 1024

OFF_MERGE = 0
OFF_XRNN = 3072
OFF_GRNN = 4096
OFF_Q = 5120
OFF_KS = 6144
OFF_KW = 6400
OFF_KC = 6656
OFF_VC = 6912
OFF_VS = 7168
OFF_VW = 7424
OFF_QX = 7680
OFF_GATE = 7936
Z_WIDTH = 8704
INPROJ_TN = 512
ROPE_TILE_LO = OFF_Q // INPROJ_TN
ROPE_TILE_HI = OFF_KC // INPROJ_TN

NSA_LANES = NSA_HPG * Q_BLOCK
SEL_CHUNK = 4 * Q_BLOCK
WIN_TILES = WINDOW // Q_BLOCK + 1


def _sigmoid(x):
    return 1.0 / (1.0 + jnp.exp(-x))


def _gelu_tanh(x):
    return 0.5 * x * (1.0 + jnp.tanh(0.7978845608028654 * (x + 0.044715 * (x * x * x))))


def _rope_swap(z):
    n = z.shape[-1]
    lane = lax.broadcasted_iota(jnp.int32, z.shape, z.ndim - 1)
    first_half = (lane & (HEAD_DIM - 1)) < ROPE_HALF
    return jnp.where(first_half, pltpu.roll(z, n - ROPE_HALF, z.ndim - 1), pltpu.roll(z, ROPE_HALF, z.ndim - 1))


def _fold_rows(x, op):
    parts = [x[r * SUBLANES:(r + 1) * SUBLANES] for r in range(x.shape[0] // SUBLANES)]
    return functools.reduce(op, parts)


def _norm_proj_kernel(x_ref, g_ref, w_ref, cos_ref, sin_ref, z_ref, u_scr, *, rope_lo, rope_hi):
    j = pl.program_id(1)

    @pl.when(j == 0)
    def _():
        x = x_ref[...]
        ms = jnp.mean(x * x, axis=-1, keepdims=True)
        u_scr[...] = (x * lax.rsqrt(ms + RMS_EPS) * g_ref[...]).astype(BF16)

    z = jnp.dot(u_scr[...], w_ref[...], preferred_element_type=F32)
    is_rope = jnp.logical_and(j >= rope_lo, j < rope_hi)

    @pl.when(is_rope)
    def _():
        z_ref[...] = z * cos_ref[...] + _rope_swap(z) * sin_ref[...]

    @pl.when(jnp.logical_not(is_rope))
    def _():
        z_ref[...] = z


def _norm_proj(x, g, w, cos_t, sin_t, *, tm, tn, rope_lo, rope_hi, name):
    t, d = x.shape
    n = w.shape[1]
    s_tiles = cos_t.shape[0] // tm
    return pl.pallas_call(
        functools.partial(_norm_proj_kernel, rope_lo=rope_lo, rope_hi=rope_hi),
        grid=(t // tm, n // tn),
        in_specs=[
            pl.BlockSpec((tm, d), lambda i, j: (i, 0)),
            pl.BlockSpec((1, d), lambda i, j: (0, 0)),
            pl.BlockSpec((d, tn), lambda i, j: (0, j)),
            pl.BlockSpec((tm, tn), lambda i, j: (i % s_tiles, 0)),
            pl.BlockSpec((tm, tn), lambda i, j: (i % s_tiles, 0)),
        ],
        out_specs=pl.BlockSpec((tm, tn), lambda i, j: (i, j)),
        out_shape=jax.ShapeDtypeStruct((t, n), F32),
        scratch_shapes=[pltpu.VMEM((tm, d), BF16)],
        compiler_params=pltpu.CompilerParams(
            dimension_semantics=("parallel", "arbitrary"), vmem_limit_bytes=VMEM_LIMIT),
        name=name,
    )(x, g, w, cos_t, sin_t)


def _compress_kernel(rk_ref, rv_ref, pka_ref, pkb_ref, pva_ref, pvb_ref,
                     wk1a_ref, wk1b_ref, wv1a_ref, wv1b_ref, w2_ref, cos_ref, sin_ref, o_ref):
    m = rk_ref.shape[0]

    def hidden(r_ref, pa_ref, pb_ref, w1a_ref, w1b_ref):
        r = r_ref[...]
        pa = jnp.dot((r + pa_ref[...]).astype(BF16), w1a_ref[...], preferred_element_type=F32)
        pb = jnp.dot((r + pb_ref[...]).astype(BF16), w1b_ref[...], preferred_element_type=F32)
        return _gelu_tanh(pa + pltpu.roll(pb, m - 1, 0)).astype(BF16)

    hk = hidden(rk_ref, pka_ref, pkb_ref, wk1a_ref, wk1b_ref)
    hv = hidden(rv_ref, pva_ref, pvb_ref, wv1a_ref, wv1b_ref)
    hcat = jnp.concatenate([hk, hv], axis=1)
    kv = jnp.dot(hcat, w2_ref[...], preferred_element_type=F32)
    o_ref[...] = kv * cos_ref[...] + _rope_swap(kv) * sin_ref[...]


def _compress(rk, rv, pka, pkb, pva, pvb, wk1a, wk1b, wv1a, wv1b, w2, cos_t, sin_t, *, tm):
    m, k = rk.shape
    full = lambda shape: pl.BlockSpec(shape, lambda i: (0, 0))
    return pl.pallas_call(
        _compress_kernel,
        grid=(m // tm,),
        in_specs=[
            pl.BlockSpec((tm, k), lambda i: (i, 0)),
            pl.BlockSpec((tm, k), lambda i: (i, 0)),
            full((1, k)), full((1, k)), full((1, k)), full((1, k)),
            full((k, CMP_HIDDEN)), full((k, CMP_HIDDEN)), full((k, CMP_HIDDEN)), full((k, CMP_HIDDEN)),
            full((2 * CMP_HIDDEN, 2 * HEAD_DIM)),
            full((tm, 2 * HEAD_DIM)), full((tm, 2 * HEAD_DIM)),
        ],
        out_specs=pl.BlockSpec((tm, 2 * HEAD_DIM), lambda i: (i, 0)),
        out_shape=jax.ShapeDtypeStruct((m, 2 * HEAD_DIM), F32),
        compiler_params=pltpu.CompilerParams(
            dimension_semantics=("parallel",), vmem_limit_bytes=VMEM_LIMIT),
        name="compress",
    )(rk, rv, pka, pkb, pva, pvb, wk1a, wk1b, wv1a, wv1b, w2, cos_t, sin_t)


def _nsa_kernel(q_ref, g_ref, ks_ref, vst_ref, kw_ref, vwt_ref, kvc_ref, c2st_ref, o_ref,
                s_scr, os_scr, *, n_sel):
    i = pl.program_id(2)
    q0 = i * Q_BLOCK

    def heads_on_lanes(x):
        xt = jnp.transpose(x)
        return jnp.concatenate([xt[h * HEAD_DIM:(h + 1) * HEAD_DIM, :] for h in range(NSA_HPG)], axis=1)

    def tile_heads(x):
        return jnp.concatenate([x] * NSA_HPG, axis=1)

    qt = heads_on_lanes(q_ref[...] * (SCALE * LOG2E)).astype(BF16)
    tq = q0 + (lax.broadcasted_iota(jnp.int32, (1, NSA_LANES), 1) & (Q_BLOCK - 1))
    c_idx = lax.broadcasted_iota(jnp.int32, (Q_BLOCK, 1), 0)
    r_idx = lax.broadcasted_iota(jnp.int32, (1, Q_BLOCK), 1)

    kvc = kvc_ref[...]
    kc = kvc[:, :HEAD_DIM].astype(BF16)
    vct = jnp.transpose(kvc)[HEAD_DIM:, :].astype(BF16)
    sc = jnp.dot(kc, qt, preferred_element_type=F32)
    n_idx = lax.broadcasted_iota(jnp.int32, (LANES, 1), 0)
    ok = (n_idx * CMP_STRIDE + (CMP_LEN - 1)) <= tq
    sc = jnp.where(ok, sc, NEG)
    m_c = jnp.max(sc, axis=0, keepdims=True)
    e_c = jnp.exp2(sc - m_c)
    inv_c = 1.0 / jnp.maximum(jnp.sum(e_c, axis=0, keepdims=True), 1e-30)
    p_c = e_c * jnp.where(m_c > 0.5 * NEG, inv_c, 0.0)
    o_c = jnp.dot(vct, p_c.astype(BF16), preferred_element_type=F32)

    p_sum = p_c[:, 0:Q_BLOCK]
    for h in range(1, NSA_HPG):
        p_sum = p_sum + p_c[:, h * Q_BLOCK:(h + 1) * Q_BLOCK]
    imp = jnp.dot(c2st_ref[...], p_sum, preferred_element_type=F32,
                  precision=lax.Precision.HIGHEST)
    blk = lax.broadcasted_iota(jnp.int32, (n_sel, Q_BLOCK), 0)
    cur = (q0 + lax.broadcasted_iota(jnp.int32, (n_sel, Q_BLOCK), 1)) // SEL_BLOCK
    forced = (blk == 0) | (blk == cur) | (blk == cur - 1)
    val = jnp.where(forced, jnp.inf, jnp.where(blk > cur, -jnp.inf, imp))
    n_grp = n_sel // SUBLANES
    grp = [val[v * SUBLANES:(v + 1) * SUBLANES] for v in range(n_grp)]
    sub = lax.broadcasted_iota(jnp.int32, (SUBLANES, Q_BLOCK), 0)
    cnt = [jnp.zeros((SUBLANES, Q_BLOCK), F32) for _ in range(n_grp)]
    for sp in range(n_sel):
        r = val[sp:sp + 1, :]
        for v in range(n_grp):
            if v * SUBLANES > sp:
                cnt[v] = cnt[v] + jnp.where(r >= grp[v], 1.0, 0.0)
            elif v * SUBLANES + SUBLANES - 1 <= sp:
                cnt[v] = cnt[v] + jnp.where(r > grp[v], 1.0, 0.0)
            else:
                later = jnp.where(sub > sp - v * SUBLANES, 1.0, 0.0)
                cnt[v] = cnt[v] + jnp.where(r > grp[v], 1.0, jnp.where(r == grp[v], later, 0.0))
    block_bias = tile_heads(jnp.concatenate(
        [jnp.where(c < float(SEL_TOPK), 0.0, NEG) for c in cnt], axis=0))

    tri_diag = jnp.where(c_idx <= r_idx, 0.0, NEG)
    tri_old = jnp.where(c_idx > r_idx, 0.0, NEG)
    tiles = []
    for d in range(1 - WIN_TILES, 1):
        td = i + d
        tc = jnp.maximum(td, 0)
        k0 = pl.multiple_of(tc * Q_BLOCK, Q_BLOCK)
        s = jnp.dot(kw_ref[0, 0, pl.ds(k0, Q_BLOCK), :], qt, preferred_element_type=F32)
        if d == 0:
            s = s + tile_heads(tri_diag)
        else:
            before_start = jnp.where(td >= 0, 0.0, NEG)
            s = s + (tile_heads(tri_old + before_start) if d == 1 - WIN_TILES else before_start)
        tiles.append((tc, s))
    m_w = jnp.max(functools.reduce(jnp.maximum, [_fold_rows(s, jnp.maximum) for _, s in tiles]),
                  axis=0, keepdims=True)
    n_pad = jnp.maximum(WINDOW - 1 - tq, 0).astype(F32)
    m_w = jnp.where(n_pad > 0.0, jnp.maximum(m_w, 0.0), m_w)
    l8 = jnp.zeros((SUBLANES, NSA_LANES), F32)
    acc = jnp.zeros((HEAD_DIM, NSA_LANES), F32)
    for tc, s in tiles:
        p = jnp.exp2(s - m_w)
        l8 = l8 + _fold_rows(p, jnp.add)
        acc = acc + jnp.dot(vwt_ref[0, 0, tc], p.astype(BF16), preferred_element_type=F32)
    l_w = jnp.sum(l8, axis=0, keepdims=True) + n_pad * jnp.exp2(-m_w)
    o_w = acc * (1.0 / jnp.maximum(l_w, 1e-30))

    blocks_per_chunk = SEL_CHUNK // SEL_BLOCK
    q_per_chunk = SEL_CHUNK // Q_BLOCK

    def selected(k):
        rel = i - k * q_per_chunk
        m8 = jnp.full((SUBLANES, NSA_LANES), NEG, F32)
        for c in range(k + 1):
            s = jnp.dot(ks_ref[0, 0, c * SEL_CHUNK:(c + 1) * SEL_CHUNK, :], qt,
                        preferred_element_type=F32)
            if c == k:
                causal = [jnp.where(rel > u, 0.0, jnp.where(rel == u, tri_diag, NEG)) for u in range(q_per_chunk)]
                s = s + tile_heads(jnp.concatenate(causal, axis=0))
            s_scr[c * SEL_CHUNK:(c + 1) * SEL_CHUNK, :] = s
            for b in range(blocks_per_chunk):
                blk_id = c * blocks_per_chunk + b
                bmax = _fold_rows(s[b * SEL_BLOCK:(b + 1) * SEL_BLOCK], jnp.maximum)
                m8 = jnp.maximum(m8, bmax + block_bias[blk_id:blk_id + 1, :])
        m_s = jnp.max(m8, axis=0, keepdims=True)
        shift = block_bias - m_s
        l8 = jnp.zeros((SUBLANES, NSA_LANES), F32)
        acc = jnp.zeros((HEAD_DIM, NSA_LANES), F32)
        for c in range(k + 1):
            ps = []
            for b in range(blocks_per_chunk):
                blk_id = c * blocks_per_chunk + b
                lo = blk_id * SEL_BLOCK
                p = jnp.exp2(s_scr[lo:lo + SEL_BLOCK, :] + shift[blk_id:blk_id + 1, :])
                l8 = l8 + _fold_rows(p, jnp.add)
                ps.append(p.astype(BF16))
            acc = acc + jnp.dot(vst_ref[0, 0, c], jnp.concatenate(ps, axis=0), preferred_element_type=F32)
        os_scr[...] = acc * (1.0 / jnp.maximum(jnp.sum(l8, axis=0, keepdims=True), 1e-30))

    for k in range(n_sel * SEL_BLOCK // SEL_CHUNK):
        pl.when(i // q_per_chunk == k)(functools.partial(selected, k))
    o_s = os_scr[...]

    gt = jnp.transpose(_sigmoid(g_ref[...]))
    outs = []
    for h in range(NSA_HPG):
        sl = slice(h * Q_BLOCK, (h + 1) * Q_BLOCK)
        outs.append(gt[3 * h:3 * h + 1, :] * o_c[:, sl] + gt[3 * h + 1:3 * h + 2, :] * o_s[:, sl]
                    + gt[3 * h + 2:3 * h + 3, :] * o_w[:, sl])
    o_ref[...] = jnp.transpose(jnp.concatenate(outs, axis=0))


def _nsa(z, ks, vst, kw, vwt, kvc, c2st, *, batch, seq):
    nq = seq // Q_BLOCK
    n_sel = seq // SEL_BLOCK
    gw = NSA_HPG * HEAD_DIM
    k_spec = pl.BlockSpec((1, 1, seq, HEAD_DIM), lambda b, g, i: (b, g, 0, 0))
    vt_spec = lambda tile: pl.BlockSpec((1, 1, seq // tile, HEAD_DIM, tile), lambda b, g, i: (b, g, 0, 0, 0))
    return pl.pallas_call(
        functools.partial(_nsa_kernel, n_sel=n_sel),
        grid=(batch, NSA_GROUPS, nq),
        in_specs=[
            pl.BlockSpec((Q_BLOCK, gw), lambda b, g, i: (b * nq + i, OFF_Q // gw + g)),
            pl.BlockSpec((Q_BLOCK, LANES), lambda b, g, i: (b * nq + i, OFF_GATE // LANES + g)),
            k_spec, vt_spec(SEL_CHUNK), k_spec, vt_spec(Q_BLOCK),
            pl.BlockSpec((LANES, 2 * HEAD_DIM), lambda b, g, i: (b * NSA_GROUPS + g, 0)),
            pl.BlockSpec((n_sel, LANES), lambda b, g, i: (0, 0)),
        ],
        out_specs=pl.BlockSpec((Q_BLOCK, gw), lambda b, g, i: (b * nq + i, g)),
        out_shape=jax.ShapeDtypeStruct((batch * seq, NSA_HEADS * HEAD_DIM), F32),
        scratch_shapes=[pltpu.VMEM((seq, NSA_LANES), F32), pltpu.VMEM((HEAD_DIM, NSA_LANES), F32)],
        compiler_params=pltpu.CompilerParams(
            dimension_semantics=("parallel", "parallel", "arbitrary"), vmem_limit_bytes=VMEM_LIMIT),
        name="nsa",
    )(z, z, ks, vst, kw, vwt, kvc, c2st)


def _rglru_kernel(x_ref, gr_ref, cw_ref, cb_ref, wa_ref, ba_ref, wi_ref, bi_ref, lam_ref, y_ref,
                  xprev_scr, h_scr):
    ts, c = x_ref.shape

    @pl.when(pl.program_id(1) == 0)
    def _():
        xprev_scr[...] = jnp.zeros(xprev_scr.shape, F32)
        h_scr[...] = jnp.zeros(h_scr.shape, F32)

    x = x_ref[...]
    xx = jnp.concatenate([xprev_scr[...], x], axis=0)
    xprev_scr[...] = x[ts - SUBLANES:, :]
    xc = cb_ref[...] + cw_ref[CONV_WIDTH - 1:CONV_WIDTH, :] * x
    for k in range(CONV_WIDTH - 1):
        off = SUBLANES - (CONV_WIDTH - 1) + k
        xc = xc + cw_ref[k:k + 1, :] * xx[off:off + ts, :]

    ra, ri = [], []
    for mblk in range(c // LANES):
        xb = xc[:, mblk * LANES:(mblk + 1) * LANES].astype(BF16)
        ra.append(jnp.dot(xb, wa_ref[mblk], preferred_element_type=F32))
        ri.append(jnp.dot(xb, wi_ref[mblk], preferred_element_type=F32))
    r = _sigmoid(jnp.concatenate(ra, axis=1) + ba_ref[...])
    gi = _sigmoid(jnp.concatenate(ri, axis=1) + bi_ref[...])

    nl = -lam_ref[...]
    softplus = jnp.maximum(nl, 0.0) + jnp.log1p(jnp.exp(-jnp.abs(nl)))
    log_a = (-RGLRU_C) * r * softplus
    a = jnp.exp(log_a)
    th = jnp.tanh(log_a)
    b = jnp.sqrt(-2.0 * th / (1.0 - th)) * (gi * xc)

    ridx = lax.broadcasted_iota(jnp.int32, (ts, c), 0)
    d = 1
    while d < ts:
        keep = ridx >= d
        a_sh = jnp.where(keep, pltpu.roll(a, d, 0), 1.0)
        b_sh = jnp.where(keep, pltpu.roll(b, d, 0), 0.0)
        b = a * b_sh + b
        a = a * a_sh
        d *= 2
    h = b + a * h_scr[...]
    h_scr[...] = h[ts - 1:ts, :]
    y_ref[...] = _gelu_tanh(gr_ref[...]) * h


def _rglru(z, cw, cb, wa2, ba, wi2, bi, lam, *, batch, seq, ts):
    c = D_MODEL
    nt = seq // ts
    vec = pl.BlockSpec((1, c), lambda b, t: (0, 0))
    return pl.pallas_call(
        _rglru_kernel,
        grid=(batch, nt),
        in_specs=[
            pl.BlockSpec((ts, c), lambda b, t: (b * nt + t, OFF_XRNN // c)),
            pl.BlockSpec((ts, c), lambda b, t: (b * nt + t, OFF_GRNN // c)),
            pl.BlockSpec((CONV_WIDTH, c), lambda b, t: (0, 0)),
            vec,
            pl.BlockSpec((c // LANES, LANES, LANES), lambda b, t: (0, 0, 0)),
            vec,
            pl.BlockSpec((c // LANES, LANES, LANES), lambda b, t: (0, 0, 0)),
            vec, vec,
        ],
        out_specs=pl.BlockSpec((ts, c), lambda b, t: (b * nt + t, 0)),
        out_shape=jax.ShapeDtypeStruct((batch * seq, c), F32),
        scratch_shapes=[pltpu.VMEM((SUBLANES, c), F32), pltpu.VMEM((1, c), F32)],
        compiler_params=pltpu.CompilerParams(
            dimension_semantics=("parallel", "arbitrary"), vmem_limit_bytes=VMEM_LIMIT),
        name="rglru",
    )(z, z, cw, cb, wa2, ba, wi2, bi, lam)


NT_DIMS = (((1,), (1,)), ((), ()))


def _merge_kernel(x_ref, m0_ref, m1_ref, m2_ref, ynsa_ref, yrnn_ref, qx_ref, kvm_ref, wxo_ref, wo_ref, h_ref):
    qx = qx_ref[...] * SCALE
    kvm = kvm_ref[...]
    yx = None
    for h in range(XATTN_HEADS):
        lo = h * HEAD_DIM
        qh = qx[:, lo:lo + HEAD_DIM].astype(BF16)
        kh = kvm[:, lo:lo + HEAD_DIM].astype(BF16)
        vh = kvm[:, XATTN_WIDTH + lo:XATTN_WIDTH + lo + HEAD_DIM].astype(BF16)
        s = lax.dot_general(qh, kh, NT_DIMS, preferred_element_type=F32)
        e = jnp.exp(s - jnp.max(s, axis=-1, keepdims=True))
        p = e / jnp.sum(e, axis=-1, keepdims=True)
        oh = jnp.dot(p.astype(BF16), vh, preferred_element_type=F32)
        part = jnp.dot(oh.astype(BF16), wxo_ref[lo:lo + HEAD_DIM, :], preferred_element_type=F32)
        yx = part if yx is None else yx + part
    y = (_sigmoid(m0_ref[...]) * ynsa_ref[...] + _sigmoid(m1_ref[...]) * yrnn_ref[...]
         + _sigmoid(m2_ref[...]) * yx)
    h_ref[...] = x_ref[...] + jnp.dot(y.astype(BF16), wo_ref[...], preferred_element_type=F32)


def _merge(x, z, ynsa, yrnn, kvm, wxo, wo, *, seq, mem_len, tq):
    t, d = x.shape
    nt = seq // tq
    row = lambda cb: pl.BlockSpec((tq, d), lambda i, cb=cb: (i, cb))
    return pl.pallas_call(
        _merge_kernel,
        grid=(t // tq,),
        in_specs=[
            row(0),
            row(OFF_MERGE // d), row(OFF_MERGE // d + 1), row(OFF_MERGE // d + 2),
            row(0), row(0),
            pl.BlockSpec((tq, XATTN_WIDTH), lambda i: (i, OFF_QX // XATTN_WIDTH)),
            pl.BlockSpec((mem_len, 2 * XATTN_WIDTH), lambda i: (i // nt, 0)),
            pl.BlockSpec((XATTN_WIDTH, d), lambda i: (0, 0)),
            pl.BlockSpec((d, d), lambda i: (0, 0)),
        ],
        out_specs=row(0),
        out_shape=jax.ShapeDtypeStruct((t, d), F32),
        compiler_params=pltpu.CompilerParams(
            dimension_semantics=("parallel",), vmem_limit_bytes=VMEM_LIMIT),
        name="merge",
    )(x, z, z, z, ynsa, yrnn, z, kvm, wxo, wo)


def _mlp_kernel(h_ref, g_ref, wup_ref, wdn_ref, gf_ref, o_ref, v_scr, acc_scr):
    f = pl.program_id(1)

    @pl.when(f == 0)
    def _():
        h = h_ref[...]
        ms = jnp.mean(h * h, axis=-1, keepdims=True)
        v_scr[...] = (h * lax.rsqrt(ms + RMS_EPS) * g_ref[...]).astype(BF16)
        acc_scr[...] = jnp.zeros(acc_scr.shape, F32)

    up = jnp.dot(v_scr[...], wup_ref[...], preferred_element_type=F32)
    act = jnp.square(jnp.maximum(up, 0.0)).astype(BF16)
    acc_scr[...] += jnp.dot(act, wdn_ref[...], preferred_element_type=F32)

    @pl.when(f == pl.num_programs(1) - 1)
    def _():
        h2 = h_ref[...] + acc_scr[...]
        ms = jnp.mean(h2 * h2, axis=-1, keepdims=True)
        o_ref[...] = h2 * lax.rsqrt(ms + RMS_EPS) * gf_ref[...]


def _mlp(h, g, wup, wdn, gf, *, tm, tf):
    t, d = h.shape
    dff = wup.shape[1]
    return pl.pallas_call(
        _mlp_kernel,
        grid=(t // tm, dff // tf),
        in_specs=[
            pl.BlockSpec((tm, d), lambda i, f: (i, 0)),
            pl.BlockSpec((1, d), lambda i, f: (0, 0)),
            pl.BlockSpec((d, tf), lambda i, f: (0, f)),
            pl.BlockSpec((tf, d), lambda i, f: (f, 0)),
            pl.BlockSpec((1, d), lambda i, f: (0, 0)),
        ],
        out_specs=pl.BlockSpec((tm, d), lambda i, f: (i, 0)),
        out_shape=jax.ShapeDtypeStruct((t, d), F32),
        scratch_shapes=[pltpu.VMEM((tm, d), BF16), pltpu.VMEM((tm, d), F32)],
        compiler_params=pltpu.CompilerParams(
            dimension_semantics=("parallel", "arbitrary"), vmem_limit_bytes=VMEM_LIMIT),
        name="mlp",
    )(h, g, wup, wdn, gf)


def _rope_tables(pos, width):
    inv = 1.0 / (ROPE_THETA ** (jnp.arange(0, ROPE_DIM, 2, dtype=F32) / ROPE_DIM))
    ang = pos.astype(F32)[:, None] * inv[None, :]
    cos, sin = jnp.cos(ang), jnp.sin(ang)
    n = pos.shape[0]
    pad = HEAD_DIM - ROPE_DIM
    cos_h = jnp.concatenate([cos, cos, jnp.ones((n, pad), F32)], axis=1)
    sin_h = jnp.concatenate([-sin, sin, jnp.zeros((n, pad), F32)], axis=1)
    reps = width // HEAD_DIM
    return jnp.tile(cos_h, (1, reps)), jnp.tile(sin_h, (1, reps))


def _pack_w_in(w_in):
    d = w_in.shape[0]
    nq = NSA_HEADS * HEAD_DIM
    nkv = NSA_GROUPS * HEAD_DIM
    ng = NSA_HEADS * 3
    widths = (nq, nkv, nkv, nkv, nkv, nkv, nkv, ng, D_MODEL, D_MODEL, XATTN_WIDTH, 3 * D_MODEL)
    offs = np.concatenate([[0], np.cumsum(widths)])
    (q, kc, vc, ks, vs, kw, vw, gates, xr, gr, qx, mg) = [w_in[:, offs[k]:offs[k + 1]] for k in range(12)]
    per_group = ng // NSA_GROUPS
    gates = gates.reshape(d, NSA_GROUPS, per_group)
    gates = jnp.pad(gates, ((0, 0), (0, 0), (0, LANES - per_group))).reshape(d, NSA_GROUPS * LANES)
    packed = jnp.concatenate([mg, xr, gr, q, ks, kw, kc, vc, vs, vw, qx, gates], axis=1)
    packed = jnp.pad(packed, ((0, 0), (0, Z_WIDTH - packed.shape[1])))
    return packed.astype(BF16)


def _block_diag_pairs(w):
    nb, k, _ = w.shape
    w = w.reshape(nb // 2, 2, k, k)
    zero = jnp.zeros((nb // 2, k, k), w.dtype)
    top = jnp.concatenate([w[:, 0], zero], axis=2)
    bot = jnp.concatenate([zero, w[:, 1]], axis=2)
    return jnp.concatenate([top, bot], axis=1).astype(BF16)


def _cmp_to_sel_t(n_cmp, n_cmp_pad, n_sel):
    c0 = np.arange(n_cmp_pad)[None, :] * CMP_STRIDE
    s0 = np.arange(n_sel)[:, None] * SEL_BLOCK
    ov = np.clip(np.minimum(c0 + CMP_LEN, s0 + SEL_BLOCK) - np.maximum(c0, s0), 0, None)
    ov = np.where(np.arange(n_cmp_pad)[None, :] < n_cmp, ov, 0)
    return (ov / CMP_LEN).astype(np.float32)


def kernel(x, mem, g_mix, w_in, cmp_pos_k, cmp_pos_v, w_cmp_k1, w_cmp_k2, w_cmp_v1, w_cmp_v2, conv_w, conv_b, w_rg_a, b_rg_a, w_rg_i, b_rg_i, rg_lambda, g_mem, w_mem_kv, w_xo, w_o, g_mlp, w_up, w_down, g_final):
    batch, seq, d = x.shape
    mem_len = mem.shape[1]
    t = batch * seq
    depth = g_mix.shape[0]
    n_sel = seq // SEL_BLOCK
    n_cmp_pad = seq // CMP_STRIDE
    assert n_cmp_pad == LANES and n_sel <= LANES and d == D_MODEL

    cos_s, sin_s = _rope_tables(jnp.arange(seq), INPROJ_TN)
    cos_m = jnp.ones((mem_len, 2 * XATTN_WIDTH), F32)
    sin_m = jnp.zeros((mem_len, 2 * XATTN_WIDTH), F32)
    cmp_pos = jnp.arange(n_cmp_pad) * CMP_STRIDE + (CMP_LEN - 1)
    cos_c, sin_c = _rope_tables(cmp_pos, HEAD_DIM)
    cos_c = jnp.concatenate([cos_c, jnp.ones_like(cos_c)], axis=1)
    sin_c = jnp.concatenate([sin_c, jnp.zeros_like(sin_c)], axis=1)
    cmp_tm = 8 * n_cmp_pad
    cos_c = jnp.tile(cos_c, (cmp_tm // n_cmp_pad, 1))
    sin_c = jnp.tile(sin_c, (cmp_tm // n_cmp_pad, 1))
    c2st = jnp.asarray(_cmp_to_sel_t((seq - CMP_LEN) // CMP_STRIDE + 1, n_cmp_pad, n_sel))

    h = x.reshape(t, d)
    for l in range(depth):
        z = _norm_proj(h, g_mix[l][None, :], _pack_w_in(w_in[l]), cos_s, sin_s,
                       tm=1024, tn=INPROJ_TN, rope_lo=ROPE_TILE_LO, rope_hi=ROPE_TILE_HI, name="inproj")

        def groups(off, dtype):
            a = z[:, off:off + NSA_GROUPS * HEAD_DIM].reshape(batch, seq, NSA_GROUPS, HEAD_DIM)
            return a.transpose(0, 2, 1, 3).astype(dtype)

        def groups_t(off, tile):
            a = z[:, off:off + NSA_GROUPS * HEAD_DIM].reshape(batch, seq // tile, tile, NSA_GROUPS, HEAD_DIM)
            return a.transpose(0, 3, 1, 4, 2).astype(BF16)

        ks, vst = groups(OFF_KS, BF16), groups_t(OFF_VS, SEL_CHUNK)
        kw, vwt = groups(OFF_KW, BF16), groups_t(OFF_VW, Q_BLOCK)
        half = CMP_STRIDE * HEAD_DIM
        rk = groups(OFF_KC, F32).reshape(batch * NSA_GROUPS * n_cmp_pad, half)
        rv = groups(OFF_VC, F32).reshape(batch * NSA_GROUPS * n_cmp_pad, half)
        pk = cmp_pos_k[l].reshape(2, half)
        pv = cmp_pos_v[l].reshape(2, half)
        w2 = jnp.zeros((2 * CMP_HIDDEN, 2 * HEAD_DIM), F32)
        w2 = w2.at[:CMP_HIDDEN, :HEAD_DIM].set(w_cmp_k2[l]).at[CMP_HIDDEN:, HEAD_DIM:].set(w_cmp_v2[l])
        kvc = _compress(rk, rv, pk[0:1], pk[1:2], pv[0:1], pv[1:2],
                        w_cmp_k1[l][:half].astype(BF16), w_cmp_k1[l][half:].astype(BF16),
                        w_cmp_v1[l][:half].astype(BF16), w_cmp_v1[l][half:].astype(BF16),
                        w2.astype(BF16), cos_c, sin_c, tm=cmp_tm)
        y_nsa = _nsa(z, ks, vst, kw, vwt, kvc, c2st, batch=batch, seq=seq)

        y_rnn = _rglru(z, conv_w[l], conv_b[l][None, :], _block_diag_pairs(w_rg_a[l]), b_rg_a[l][None, :],
                       _block_diag_pairs(w_rg_i[l]), b_rg_i[l][None, :], rg_lambda[l][None, :],
                       batch=batch, seq=seq, ts=256)

        kvm = _norm_proj(mem.reshape(batch * mem_len, d), g_mem[l][None, :], w_mem_kv[l].astype(BF16),
                         cos_m, sin_m, tm=mem_len, tn=2 * XATTN_WIDTH, rope_lo=0, rope_hi=0, name="memkv")

        h1 = _merge(h, z, y_nsa, y_rnn, kvm, w_xo[l].astype(BF16), w_o[l].astype(BF16),
                    seq=seq, mem_len=mem_len, tq=256)
        last = l == depth - 1
        gf = g_final if last else jnp.ones_like(g_final)
        h = _mlp(h1, g_mlp[l][None, :], w_up[l].astype(BF16), w_down[l].astype(BF16), gf[None, :],
                 tm=1024, tf=1024)
        assert last, "final norm is fused into the last layer's MLP kernel"
    return h.reshape(batch, seq, d)
```

```python
import functools
import math

import numpy as np
import jax
import jax.numpy as jnp
from jax import lax
from jax.experimental import pallas as pl
from jax.experimental.pallas import tpu as pltpu

F32 = jnp.float32
BF16 = jnp.bfloat16

D_MODEL = 1024
HEAD_DIM = 64
NSA_HEADS = 16
NSA_GROUPS = 4
NSA_HPG = NSA_HEADS // NSA_GROUPS
NSA_BRANCHES = 3
CMP_LEN = 32
CMP_STRIDE = 16
CMP_HIDDEN = 256
SEL_BLOCK = 64
SEL_TOPK = 8
WINDOW = 512
Q_BLOCK = 128
ROPE_THETA = 500000.0
ROPE_DIM = HEAD_DIM // 4
ROPE_HALF = ROPE_DIM // 2
RNN_BLOCKS = 16
RNN_BLOCK_DIM = D_MODEL // RNN_BLOCKS
CONV_WIDTH = 4
RGLRU_C = 8.0
XATTN_HEADS = 4
XATTN_WIDTH = XATTN_HEADS * HEAD_DIM
D_FF = 4 * D_MODEL
RMS_EPS = 1e-6
SCALE = HEAD_DIM ** -0.5
LOG2E = math.log2(math.e)
NEG = -1e30

LANES = 128
SUBLANES = 8
VMEM_LIMIT = 48 * 1024 * 1024

NSA_KV = NSA_GROUPS * HEAD_DIM
OFF_Q = 0
OFF_KC = 1024
OFF_VC = 1280
OFF_KS = 1536
OFF_VS = 1792
OFF_KW = 2048
OFF_VW = 2304
OFF_QX = 2560
OFF_GATE = 2816
GATE_STRIDE = 64
OFF_XRNN = 3072
OFF_GRNN = 4096
OFF_MERGE = 5120
Z_WIDTH = 8192
INPROJ_TN = 512
ROPE_ALL_TILES = (OFF_Q // INPROJ_TN, OFF_Q // INPROJ_TN + 1)
ROPE_HALF_TILES = (OFF_KS // INPROJ_TN, OFF_KW // INPROJ_TN)

NSA_LANES = NSA_HPG * Q_BLOCK
SEL_CHUNK = 4 * Q_BLOCK
WIN_TILES = WINDOW // Q_BLOCK + 1

NT_DIMS = (((1,), (1,)), ((), ()))


def _sigmoid(x):
    return 1.0 / (1.0 + jnp.exp(-x))


def _gelu_tanh(x):
    return 0.5 * x * (1.0 + jnp.tanh(0.7978845608028654 * (x + 0.044715 * (x * x * x))))


def _rope_swap(z):
    n = z.shape[-1]
    lane = lax.broadcasted_iota(jnp.int32, z.shape, z.ndim - 1)
    first_half = (lane & (HEAD_DIM - 1)) < ROPE_HALF
    return jnp.where(first_half, pltpu.roll(z, n - ROPE_HALF, z.ndim - 1), pltpu.roll(z, ROPE_HALF, z.ndim - 1))


def _fold_rows(x, op):
    parts = [x[r * SUBLANES:(r + 1) * SUBLANES] for r in range(x.shape[0] // SUBLANES)]
    return functools.reduce(op, parts)


def _norm_proj_kernel(x_ref, g_ref, w_ref, cos_ref, sin_ref, z_ref, u_scr, *, rope_tiles):
    j = pl.program_id(1)

    @pl.when(j == 0)
    def _():
        x = x_ref[...]
        ms = jnp.mean(x * x, axis=-1, keepdims=True)
        u_scr[...] = (x * lax.rsqrt(ms + RMS_EPS) * g_ref[...]).astype(BF16)

    z = jnp.dot(u_scr[...], w_ref[...], preferred_element_type=F32)
    if not rope_tiles:
        z_ref[...] = z.astype(z_ref.dtype)
        return
    is_rope = functools.reduce(jnp.logical_or, [j == t for t in rope_tiles])

    @pl.when(is_rope)
    def _():
        z_ref[...] = (z * cos_ref[0] + _rope_swap(z) * sin_ref[0]).astype(z_ref.dtype)

    @pl.when(jnp.logical_not(is_rope))
    def _():
        z_ref[...] = z.astype(z_ref.dtype)


def _norm_proj(x, g, w, cos_t, sin_t, *, tm, tn, rope_all, rope_half, out_dtype, name):
    t, d = x.shape
    n = w.shape[1]
    s_tiles = cos_t.shape[1] // tm
    half_lo = min(rope_half) if rope_half else n // tn
    table_spec = pl.BlockSpec((1, tm, tn), lambda i, j: (jnp.where(j >= half_lo, 1, 0), i % s_tiles, 0))
    return pl.pallas_call(
        functools.partial(_norm_proj_kernel, rope_tiles=tuple(rope_all) + tuple(rope_half)),
        grid=(t // tm, n // tn),
        in_specs=[
            pl.BlockSpec((tm, d), lambda i, j: (i, 0)),
            pl.BlockSpec((1, d), lambda i, j: (0, 0)),
            pl.BlockSpec((d, tn), lambda i, j: (0, j)),
            table_spec, table_spec,
        ],
        out_specs=pl.BlockSpec((tm, tn), lambda i, j: (i, j)),
        out_shape=jax.ShapeDtypeStruct((t, n), out_dtype),
        scratch_shapes=[pltpu.VMEM((tm, d), BF16)],
        compiler_params=pltpu.CompilerParams(
            dimension_semantics=("parallel", "arbitrary"), vmem_limit_bytes=VMEM_LIMIT),
        name=name,
    )(x, g, w, cos_t, sin_t)


def _compress_kernel(rk_ref, rv_ref, pka_ref, pkb_ref, pva_ref, pvb_ref,
                     wk1a_ref, wk1b_ref, wv1a_ref, wv1b_ref, w2_ref, cos_ref, sin_ref, o_ref):
    m = rk_ref.shape[0]

    def hidden(r_ref, pa_ref, pb_ref, w1a_ref, w1b_ref):
        r = r_ref[...].astype(F32)
        pa = jnp.dot((r + pa_ref[...]).astype(BF16), w1a_ref[...], preferred_element_type=F32)
        pb = jnp.dot((r + pb_ref[...]).astype(BF16), w1b_ref[...], preferred_element_type=F32)
        return _gelu_tanh(pa + pltpu.roll(pb, m - 1, 0)).astype(BF16)

    hk = hidden(rk_ref, pka_ref, pkb_ref, wk1a_ref, wk1b_ref)
    hv = hidden(rv_ref, pva_ref, pvb_ref, wv1a_ref, wv1b_ref)
    hcat = jnp.concatenate([hk, hv], axis=1)
    kv = jnp.dot(hcat, w2_ref[...], preferred_element_type=F32)
    o_ref[...] = kv * cos_ref[...] + _rope_swap(kv) * sin_ref[...]


def _compress(rk, rv, pka, pkb, pva, pvb, wk1a, wk1b, wv1a, wv1b, w2, cos_t, sin_t, *, tm):
    m, k = rk.shape
    full = lambda shape: pl.BlockSpec(shape, lambda i: (0, 0))
    return pl.pallas_call(
        _compress_kernel,
        grid=(m // tm,),
        in_specs=[
            pl.BlockSpec((tm, k), lambda i: (i, 0)),
            pl.BlockSpec((tm, k), lambda i: (i, 0)),
            full((1, k)), full((1, k)), full((1, k)), full((1, k)),
            full((k, CMP_HIDDEN)), full((k, CMP_HIDDEN)), full((k, CMP_HIDDEN)), full((k, CMP_HIDDEN)),
            full((2 * CMP_HIDDEN, 2 * HEAD_DIM)),
            full((tm, 2 * HEAD_DIM)), full((tm, 2 * HEAD_DIM)),
        ],
        out_specs=pl.BlockSpec((tm, 2 * HEAD_DIM), lambda i: (i, 0)),
        out_shape=jax.ShapeDtypeStruct((m, 2 * HEAD_DIM), F32),
        compiler_params=pltpu.CompilerParams(
            dimension_semantics=("parallel",), vmem_limit_bytes=VMEM_LIMIT),
        name="compress",
    )(rk, rv, pka, pkb, pva, pvb, wk1a, wk1b, wv1a, wv1b, w2, cos_t, sin_t)


def _nsa_kernel(q_ref, g_ref, ks_ref, vst_ref, kw_ref, vwt_ref, kvc_ref, c2st_ref, o_ref,
                s_scr, gt_scr, *, n_sel):
    g = pl.program_id(1)
    i = pl.program_id(2)
    q0 = i * Q_BLOCK

    def heads_on_lanes(x):
        xt = jnp.transpose(x)
        return jnp.concatenate([xt[h * HEAD_DIM:(h + 1) * HEAD_DIM, :] for h in range(NSA_HPG)], axis=1)

    def tile_heads(x):
        return jnp.concatenate([x] * NSA_HPG, axis=1)

    qt = heads_on_lanes(q_ref[...].astype(F32) * (SCALE * LOG2E)).astype(BF16)
    tq = q0 + (lax.broadcasted_iota(jnp.int32, (1, NSA_LANES), 1) & (Q_BLOCK - 1))
    c_idx = lax.broadcasted_iota(jnp.int32, (Q_BLOCK, 1), 0)
    r_idx = lax.broadcasted_iota(jnp.int32, (1, Q_BLOCK), 1)
    tri_diag = jnp.where(c_idx <= r_idx, 0.0, NEG)
    tri_old = jnp.where(c_idx > r_idx, 0.0, NEG)

    gt_scr[...] = jnp.transpose(_sigmoid(g_ref[...].astype(F32)))
    gt = gt_scr[pl.ds(pl.multiple_of(g * GATE_STRIDE, GATE_STRIDE), 2 * SUBLANES), :]

    kvc = kvc_ref[...]
    kc = kvc[:, :HEAD_DIM].astype(BF16)
    vct = jnp.transpose(kvc)[HEAD_DIM:, :].astype(BF16)
    sc = jnp.dot(kc, qt, preferred_element_type=F32)
    n_idx = lax.broadcasted_iota(jnp.int32, (LANES, 1), 0)
    ok = (n_idx * CMP_STRIDE + (CMP_LEN - 1)) <= tq
    sc = jnp.where(ok, sc, NEG)
    m_c = jnp.max(sc, axis=0, keepdims=True)
    e_c = jnp.exp2(sc - m_c)
    inv_c = 1.0 / jnp.maximum(jnp.sum(e_c, axis=0, keepdims=True), 1e-30)
    p_c = e_c * jnp.where(m_c > 0.5 * NEG, inv_c, 0.0)
    o_c = jnp.dot(vct, p_c.astype(BF16), preferred_element_type=F32)

    p_sum = p_c[:, 0:Q_BLOCK]
    for h in range(1, NSA_HPG):
        p_sum = p_sum + p_c[:, h * Q_BLOCK:(h + 1) * Q_BLOCK]
    imp = jnp.dot(c2st_ref[...], p_sum, preferred_element_type=F32,
                  precision=lax.Precision.HIGHEST)
    blk = lax.broadcasted_iota(jnp.int32, (n_sel, Q_BLOCK), 0)
    cur = (q0 + lax.broadcasted_iota(jnp.int32, (n_sel, Q_BLOCK), 1)) // SEL_BLOCK
    forced = (blk == 0) | (blk == cur) | (blk == cur - 1)
    val = jnp.where(forced, jnp.inf, jnp.where(blk > cur, -jnp.inf, imp))
    n_grp = n_sel // SUBLANES
    grp = [val[v * SUBLANES:(v + 1) * SUBLANES] for v in range(n_grp)]
    sub = lax.broadcasted_iota(jnp.int32, (SUBLANES, Q_BLOCK), 0)
    cnt = [jnp.zeros((SUBLANES, Q_BLOCK), F32) for _ in range(n_grp)]
    for sp in range(n_sel):
        r = val[sp:sp + 1, :]
        for v in range(n_grp):
            if v * SUBLANES > sp:
                cnt[v] = cnt[v] + jnp.where(r >= grp[v], 1.0, 0.0)
            elif v * SUBLANES + SUBLANES - 1 <= sp:
                cnt[v] = cnt[v] + jnp.where(r > grp[v], 1.0, 0.0)
            else:
                later = jnp.where(sub > sp - v * SUBLANES, 1.0, 0.0)
                cnt[v] = cnt[v] + jnp.where(r > grp[v], 1.0, jnp.where(r == grp[v], later, 0.0))
    block_bias = tile_heads(jnp.concatenate(
        [jnp.where(c < float(SEL_TOPK), 0.0, NEG) for c in cnt], axis=0))

    def window():
        tiles = []
        for d in range(1 - WIN_TILES, 1):
            td = i + d
            tc = jnp.maximum(td, 0)
            k0 = pl.multiple_of(tc * Q_BLOCK, Q_BLOCK)
            s = jnp.dot(kw_ref[0, 0, pl.ds(k0, Q_BLOCK), :], qt, preferred_element_type=F32)
            if d == 0:
                s = s + tile_heads(tri_diag)
            else:
                before_start = jnp.where(td >= 0, 0.0, NEG)
                s = s + (tile_heads(tri_old + before_start) if d == 1 - WIN_TILES else before_start)
            tiles.append((tc, s))
        m_w = jnp.max(functools.reduce(jnp.maximum, [_fold_rows(s, jnp.maximum) for _, s in tiles]),
                      axis=0, keepdims=True)
        n_pad = jnp.maximum(WINDOW - 1 - tq, 0).astype(F32)
        m_w = jnp.where(n_pad > 0.0, jnp.maximum(m_w, 0.0), m_w)
        l8 = jnp.zeros((SUBLANES, NSA_LANES), F32)
        acc = jnp.zeros((HEAD_DIM, NSA_LANES), F32)
        for tc, s in tiles:
            p = jnp.exp2(s - m_w)
            l8 = l8 + _fold_rows(p, jnp.add)
            acc = acc + jnp.dot(vwt_ref[0, 0, tc], p.astype(BF16), preferred_element_type=F32)
        l_w = jnp.sum(l8, axis=0, keepdims=True) + n_pad * jnp.exp2(-m_w)
        return acc * (1.0 / jnp.maximum(l_w, 1e-30))

    blocks_per_chunk = SEL_CHUNK // SEL_BLOCK
    q_per_chunk = SEL_CHUNK // Q_BLOCK

    def selected(k):
        rel = i - k * q_per_chunk
        m8 = jnp.full((SUBLANES, NSA_LANES), NEG, F32)
        for c in range(k + 1):
            s = jnp.dot(ks_ref[0, 0, c * SEL_CHUNK:(c + 1) * SEL_CHUNK, :], qt,
                        preferred_element_type=F32)
            if c == k:
                causal = [jnp.where(rel > u, 0.0, jnp.where(rel == u, tri_diag, NEG)) for u in range(q_per_chunk)]
                s = s + tile_heads(jnp.concatenate(causal, axis=0))
            s_scr[c * SEL_CHUNK:(c + 1) * SEL_CHUNK, :] = s
            for b in range(blocks_per_chunk):
                blk_id = c * blocks_per_chunk + b
                bmax = _fold_rows(s[b * SEL_BLOCK:(b + 1) * SEL_BLOCK], jnp.maximum)
                m8 = jnp.maximum(m8, bmax + block_bias[blk_id:blk_id + 1, :])
        m_s = jnp.max(m8, axis=0, keepdims=True)
        shift = block_bias - m_s
        l8 = jnp.zeros((SUBLANES, NSA_LANES), F32)
        acc = jnp.zeros((HEAD_DIM, NSA_LANES), F32)
        for c in range(k + 1):
            ps = []
            for b in range(blocks_per_chunk):
                blk_id = c * blocks_per_chunk + b
                lo = blk_id * SEL_BLOCK
                p = jnp.exp2(s_scr[lo:lo + SEL_BLOCK, :] + shift[blk_id:blk_id + 1, :])
                l8 = l8 + _fold_rows(p, jnp.add)
                ps.append(p.astype(BF16))
            acc = acc + jnp.dot(vst_ref[0, 0, c], jnp.concatenate(ps, axis=0), preferred_element_type=F32)
        return acc * (1.0 / jnp.maximum(jnp.sum(l8, axis=0, keepdims=True), 1e-30))

    def finish(k):
        o_w = window()
        o_s = selected(k)
        outs = []
        for h in range(NSA_HPG):
            sl = slice(h * Q_BLOCK, (h + 1) * Q_BLOCK)
            row = NSA_BRANCHES * h
            outs.append(gt[row:row + 1, :] * o_c[:, sl] + gt[row + 1:row + 2, :] * o_s[:, sl]
                        + gt[row + 2:row + 3, :] * o_w[:, sl])
        o_ref[...] = jnp.transpose(jnp.concatenate(outs, axis=0))

    for k in range(n_sel * SEL_BLOCK // SEL_CHUNK):
        pl.when(i // q_per_chunk == k)(functools.partial(finish, k))


def _nsa(z, ks, vst, kw, vwt, kvc, c2st, *, batch, seq):
    nq = seq // Q_BLOCK
    n_sel = seq // SEL_BLOCK
    gw = NSA_HPG * HEAD_DIM
    gates_w = NSA_GROUPS * GATE_STRIDE
    k_spec = pl.BlockSpec((1, 1, seq, HEAD_DIM), lambda b, g, i: (b, g, 0, 0))
    vt_spec = lambda tile: pl.BlockSpec((1, 1, seq // tile, HEAD_DIM, tile), lambda b, g, i: (b, g, 0, 0, 0))
    return pl.pallas_call(
        functools.partial(_nsa_kernel, n_sel=n_sel),
        grid=(batch, NSA_GROUPS, nq),
        in_specs=[
            pl.BlockSpec((Q_BLOCK, gw), lambda b, g, i: (b * nq + i, OFF_Q // gw + g)),
            pl.BlockSpec((Q_BLOCK, gates_w), lambda b, g, i: (b * nq + i, OFF_GATE // gates_w)),
            k_spec, vt_spec(SEL_CHUNK), k_spec, vt_spec(Q_BLOCK),
            pl.BlockSpec((LANES, 2 * HEAD_DIM), lambda b, g, i: (b * NSA_GROUPS + g, 0)),
            pl.BlockSpec((n_sel, LANES), lambda b, g, i: (0, 0)),
        ],
        out_specs=pl.BlockSpec((Q_BLOCK, gw), lambda b, g, i: (b * nq + i, g)),
        out_shape=jax.ShapeDtypeStruct((batch * seq, NSA_HEADS * HEAD_DIM), F32),
        scratch_shapes=[pltpu.VMEM((seq, NSA_LANES), F32), pltpu.VMEM((gates_w, Q_BLOCK), F32)],
        compiler_params=pltpu.CompilerParams(
            dimension_semantics=("parallel", "parallel", "arbitrary"), vmem_limit_bytes=VMEM_LIMIT),
        name="nsa",
    )(z, z, ks, vst, kw, vwt, kvc, c2st)


def _rglru_kernel(x_ref, gr_ref, cw_ref, cb_ref, wa_ref, ba_ref, wi_ref, bi_ref, lam_ref, y_ref,
                  xprev_scr, h_scr):
    ts, c = x_ref.shape

    @pl.when(pl.program_id(1) == 0)
    def _():
        xprev_scr[...] = jnp.zeros(xprev_scr.shape, F32)
        h_scr[...] = jnp.zeros(h_scr.shape, F32)

    x = x_ref[...].astype(F32)
    xx = jnp.concatenate([xprev_scr[...], x], axis=0)
    xprev_scr[...] = x[ts - SUBLANES:, :]
    xc = cb_ref[...] + cw_ref[CONV_WIDTH - 1:CONV_WIDTH, :] * x
    for k in range(CONV_WIDTH - 1):
        off = SUBLANES - (CONV_WIDTH - 1) + k
        xc = xc + cw_ref[k:k + 1, :] * xx[off:off + ts, :]

    ra, ri = [], []
    for mblk in range(c // LANES):
        xb = xc[:, mblk * LANES:(mblk + 1) * LANES].astype(BF16)
        ra.append(jnp.dot(xb, wa_ref[mblk], preferred_element_type=F32))
        ri.append(jnp.dot(xb, wi_ref[mblk], preferred_element_type=F32))
    r = _sigmoid(jnp.concatenate(ra, axis=1) + ba_ref[...])
    gi = _sigmoid(jnp.concatenate(ri, axis=1) + bi_ref[...])

    nl = -lam_ref[...]
    softplus = jnp.maximum(nl, 0.0) + jnp.log1p(jnp.exp(-jnp.abs(nl)))
    log_a = (-RGLRU_C) * r * softplus
    a = jnp.exp(log_a)
    th = jnp.tanh(log_a)
    b = jnp.sqrt(-2.0 * th / (1.0 - th)) * (gi * xc)

    ridx = lax.broadcasted_iota(jnp.int32, (ts, c), 0)
    d = 1
    while d < ts:
        keep = ridx >= d
        a_sh = jnp.where(keep, pltpu.roll(a, d, 0), 1.0)
        b_sh = jnp.where(keep, pltpu.roll(b, d, 0), 0.0)
        b = a * b_sh + b
        a = a * a_sh
        d *= 2
    h = b + a * h_scr[...]
    h_scr[...] = h[ts - 1:ts, :]
    y_ref[...] = _gelu_tanh(gr_ref[...].astype(F32)) * h


def _rglru(z, cw, cb, wa2, ba, wi2, bi, lam, *, batch, seq, ts):
    c = D_MODEL
    nt = seq // ts
    vec = pl.BlockSpec((1, c), lambda b, t: (0, 0))
    return pl.pallas_call(
        _rglru_kernel,
        grid=(batch, nt),
        in_specs=[
            pl.BlockSpec((ts, c), lambda b, t: (b * nt + t, OFF_XRNN // c)),
            pl.BlockSpec((ts, c), lambda b, t: (b * nt + t, OFF_GRNN // c)),
            pl.BlockSpec((CONV_WIDTH, c), lambda b, t: (0, 0)),
            vec,
            pl.BlockSpec((c // LANES, LANES, LANES), lambda b, t: (0, 0, 0)),
            vec,
            pl.BlockSpec((c // LANES, LANES, LANES), lambda b, t: (0, 0, 0)),
            vec, vec,
        ],
        out_specs=pl.BlockSpec((ts, c), lambda b, t: (b * nt + t, 0)),
        out_shape=jax.ShapeDtypeStruct((batch * seq, c), F32),
        scratch_shapes=[pltpu.VMEM((SUBLANES, c), F32), pltpu.VMEM((1, c), F32)],
        compiler_params=pltpu.CompilerParams(
            dimension_semantics=("parallel", "arbitrary"), vmem_limit_bytes=VMEM_LIMIT),
        name="rglru",
    )(z, z, cw, cb, wa2, ba, wi2, bi, lam)


def _merge_kernel(x_ref, m0_ref, m1_ref, m2_ref, ynsa_ref, yrnn_ref, qx_ref, kvm_ref, wxo_ref, wo_ref, h_ref):
    qx = qx_ref[...].astype(F32) * SCALE
    kvm = kvm_ref[...]
    yx = None
    for h in range(XATTN_HEADS):
        lo = h * HEAD_DIM
        qh = qx[:, lo:lo + HEAD_DIM].astype(BF16)
        kh = kvm[:, lo:lo + HEAD_DIM].astype(BF16)
        vh = kvm[:, XATTN_WIDTH + lo:XATTN_WIDTH + lo + HEAD_DIM].astype(BF16)
        s = lax.dot_general(qh, kh, NT_DIMS, preferred_element_type=F32)
        e = jnp.exp(s - jnp.max(s, axis=-1, keepdims=True))
        p = e / jnp.sum(e, axis=-1, keepdims=True)
        oh = jnp.dot(p.astype(BF16), vh, preferred_element_type=F32)
        part = jnp.dot(oh.astype(BF16), wxo_ref[lo:lo + HEAD_DIM, :], preferred_element_type=F32)
        yx = part if yx is None else yx + part
    y = (_sigmoid(m0_ref[...].astype(F32)) * ynsa_ref[...] + _sigmoid(m1_ref[...].astype(F32)) * yrnn_ref[...]
         + _sigmoid(m2_ref[...].astype(F32)) * yx)
    h_ref[...] = x_ref[...] + jnp.dot(y.astype(BF16), wo_ref[...], preferred_element_type=F32)


def _merge(x, z, ynsa, yrnn, kvm, wxo, wo, *, seq, mem_len, tq):
    t, d = x.shape
    nt = seq // tq
    row = lambda cb: pl.BlockSpec((tq, d), lambda i, cb=cb: (i, cb))
    return pl.pallas_call(
        _merge_kernel,
        grid=(t // tq,),
        in_specs=[
            row(0),
            row(OFF_MERGE // d), row(OFF_MERGE // d + 1), row(OFF_MERGE // d + 2),
            row(0), row(0),
            pl.BlockSpec((tq, XATTN_WIDTH), lambda i: (i, OFF_QX // XATTN_WIDTH)),
            pl.BlockSpec((mem_len, 2 * XATTN_WIDTH), lambda i: (i // nt, 0)),
            pl.BlockSpec((XATTN_WIDTH, d), lambda i: (0, 0)),
            pl.BlockSpec((d, d), lambda i: (0, 0)),
        ],
        out_specs=row(0),
        out_shape=jax.ShapeDtypeStruct((t, d), F32),
        compiler_params=pltpu.CompilerParams(
            dimension_semantics=("parallel",), vmem_limit_bytes=VMEM_LIMIT),
        name="merge",
    )(x, z, z, z, ynsa, yrnn, z, kvm, wxo, wo)


def _mlp_kernel(h_ref, g_ref, wup_ref, wdn_ref, gf_ref, o_ref, v_scr, acc_scr):
    f = pl.program_id(1)

    @pl.when(f == 0)
    def _():
        h = h_ref[...]
        ms = jnp.mean(h * h, axis=-1, keepdims=True)
        v_scr[...] = (h * lax.rsqrt(ms + RMS_EPS) * g_ref[...]).astype(BF16)
        acc_scr[...] = jnp.zeros(acc_scr.shape, F32)

    up = jnp.dot(v_scr[...], wup_ref[...], preferred_element_type=F32)
    act = jnp.square(jnp.maximum(up, 0.0)).astype(BF16)
    acc_scr[...] += jnp.dot(act, wdn_ref[...], preferred_element_type=F32)

    @pl.when(f == pl.num_programs(1) - 1)
    def _():
        h2 = h_ref[...] + acc_scr[...]
        ms = jnp.mean(h2 * h2, axis=-1, keepdims=True)
        o_ref[...] = h2 * lax.rsqrt(ms + RMS_EPS) * gf_ref[...]


def _mlp(h, g, wup, wdn, gf, *, tm, tf):
    t, d = h.shape
    dff = wup.shape[1]
    return pl.pallas_call(
        _mlp_kernel,
        grid=(t // tm, dff // tf),
        in_specs=[
            pl.BlockSpec((tm, d), lambda i, f: (i, 0)),
            pl.BlockSpec((1, d), lambda i, f: (0, 0)),
            pl.BlockSpec((d, tf), lambda i, f: (0, f)),
            pl.BlockSpec((tf, d), lambda i, f: (f, 0)),
            pl.BlockSpec((1, d), lambda i, f: (0, 0)),
        ],
        out_specs=pl.BlockSpec((tm, d), lambda i, f: (i, 0)),
        out_shape=jax.ShapeDtypeStruct((t, d), F32),
        scratch_shapes=[pltpu.VMEM((tm, d), BF16), pltpu.VMEM((tm, d), F32)],
        compiler_params=pltpu.CompilerParams(
            dimension_semantics=("parallel", "arbitrary"), vmem_limit_bytes=VMEM_LIMIT),
        name="mlp",
    )(h, g, wup, wdn, gf)


def _rope_tables(pos, width, rope_width=None):
    rope_width = width if rope_width is None else rope_width
    inv = 1.0 / (ROPE_THETA ** (jnp.arange(0, ROPE_DIM, 2, dtype=F32) / ROPE_DIM))
    ang = pos.astype(F32)[:, None] * inv[None, :]
    cos, sin = jnp.cos(ang), jnp.sin(ang)
    n = pos.shape[0]
    pad = HEAD_DIM - ROPE_DIM
    cos_h = jnp.concatenate([cos, cos, jnp.ones((n, pad), F32)], axis=1)
    sin_h = jnp.concatenate([-sin, sin, jnp.zeros((n, pad), F32)], axis=1)
    reps = rope_width // HEAD_DIM
    rest = width - rope_width
    cos_t = jnp.concatenate([jnp.tile(cos_h, (1, reps)), jnp.ones((n, rest), F32)], axis=1)
    sin_t = jnp.concatenate([jnp.tile(sin_h, (1, reps)), jnp.zeros((n, rest), F32)], axis=1)
    return cos_t, sin_t


def _pack_w_in(w_in):
    d = w_in.shape[0]
    w = w_in.astype(BF16)
    n_qkv = NSA_HEADS * HEAD_DIM + 6 * NSA_KV
    n_gates = NSA_HEADS * NSA_BRANCHES
    rnn_lo = n_qkv + n_gates
    qx_lo = rnn_lo + 2 * D_MODEL
    mg_lo = qx_lo + XATTN_WIDTH
    per_group = n_gates // NSA_GROUPS
    gates = w[:, n_qkv:rnn_lo].reshape(d, NSA_GROUPS, per_group)
    gates = jnp.pad(gates, ((0, 0), (0, 0), (0, GATE_STRIDE - per_group))).reshape(d, NSA_GROUPS * GATE_STRIDE)
    return jnp.concatenate([w[:, :n_qkv], w[:, qx_lo:mg_lo], gates, w[:, rnn_lo:qx_lo], w[:, mg_lo:]], axis=1)


def _block_diag_pairs(w):
    nb, k, _ = w.shape
    w = w.reshape(nb // 2, 2, k, k)
    zero = jnp.zeros((nb // 2, k, k), w.dtype)
    top = jnp.concatenate([w[:, 0], zero], axis=2)
    bot = jnp.concatenate([zero, w[:, 1]], axis=2)
    return jnp.concatenate([top, bot], axis=1).astype(BF16)


def _cmp_to_sel_t(n_cmp, n_cmp_pad, n_sel):
    c0 = np.arange(n_cmp_pad)[None, :] * CMP_STRIDE
    s0 = np.arange(n_sel)[:, None] * SEL_BLOCK
    ov = np.clip(np.minimum(c0 + CMP_LEN, s0 + SEL_BLOCK) - np.maximum(c0, s0), 0, None)
    ov = np.where(np.arange(n_cmp_pad)[None, :] < n_cmp, ov, 0)
    return (ov / CMP_LEN).astype(np.float32)


def kernel(x, mem, g_mix, w_in, cmp_pos_k, cmp_pos_v, w_cmp_k1, w_cmp_k2, w_cmp_v1, w_cmp_v2, conv_w, conv_b, w_rg_a, b_rg_a, w_rg_i, b_rg_i, rg_lambda, g_mem, w_mem_kv, w_xo, w_o, g_mlp, w_up, w_down, g_final):
    batch, seq, d = x.shape
    mem_len = mem.shape[1]
    t = batch * seq
    depth = g_mix.shape[0]
    n_sel = seq // SEL_BLOCK
    n_cmp_pad = seq // CMP_STRIDE
    assert n_cmp_pad == LANES and n_sel <= LANES and d == D_MODEL

    pos = jnp.arange(seq)
    cos_s, sin_s = zip(_rope_tables(pos, INPROJ_TN), _rope_tables(pos, INPROJ_TN, NSA_KV))
    cos_s, sin_s = jnp.stack(cos_s), jnp.stack(sin_s)
    no_rope = jnp.zeros((2, mem_len, 2 * XATTN_WIDTH), F32)
    cmp_pos = jnp.arange(n_cmp_pad) * CMP_STRIDE + (CMP_LEN - 1)
    cmp_tm = 8 * n_cmp_pad
    cos_c, sin_c = _rope_tables(cmp_pos, 2 * HEAD_DIM, HEAD_DIM)
    cos_c = jnp.tile(cos_c, (cmp_tm // n_cmp_pad, 1))
    sin_c = jnp.tile(sin_c, (cmp_tm // n_cmp_pad, 1))
    c2st = jnp.asarray(_cmp_to_sel_t((seq - CMP_LEN) // CMP_STRIDE + 1, n_cmp_pad, n_sel))

    h = x.reshape(t, d)
    for l in range(depth):
        z = _norm_proj(h, g_mix[l][None, :], _pack_w_in(w_in[l]), cos_s, sin_s, tm=1024, tn=INPROJ_TN,
                       rope_all=ROPE_ALL_TILES, rope_half=ROPE_HALF_TILES, out_dtype=BF16, name="inproj")

        def groups(off):
            a = z[:, off:off + NSA_KV].reshape(batch, seq, NSA_GROUPS, HEAD_DIM)
            return a.transpose(0, 2, 1, 3)

        def groups_t(off, tile):
            a = z[:, off:off + NSA_KV].reshape(batch, seq // tile, tile, NSA_GROUPS, HEAD_DIM)
            return a.transpose(0, 3, 1, 4, 2)

        ks, vst = groups(OFF_KS), groups_t(OFF_VS, SEL_CHUNK)
        kw, vwt = groups(OFF_KW), groups_t(OFF_VW, Q_BLOCK)
        half = CMP_STRIDE * HEAD_DIM
        rk = groups(OFF_KC).reshape(batch * NSA_GROUPS * n_cmp_pad, half)
        rv = groups(OFF_VC).reshape(batch * NSA_GROUPS * n_cmp_pad, half)
        pk = cmp_pos_k[l].reshape(2, half)
        pv = cmp_pos_v[l].reshape(2, half)
        w2 = jnp.zeros((2 * CMP_HIDDEN, 2 * HEAD_DIM), F32)
        w2 = w2.at[:CMP_HIDDEN, :HEAD_DIM].set(w_cmp_k2[l]).at[CMP_HIDDEN:, HEAD_DIM:].set(w_cmp_v2[l])
        kvc = _compress(rk, rv, pk[0:1], pk[1:2], pv[0:1], pv[1:2],
                        w_cmp_k1[l][:half].astype(BF16), w_cmp_k1[l][half:].astype(BF16),
                        w_cmp_v1[l][:half].astype(BF16), w_cmp_v1[l][half:].astype(BF16),
                        w2.astype(BF16), cos_c, sin_c, tm=cmp_tm)
        y_nsa = _nsa(z, ks, vst, kw, vwt, kvc, c2st, batch=batch, seq=seq)

        y_rnn = _rglru(z, conv_w[l], conv_b[l][None, :], _block_diag_pairs(w_rg_a[l]), b_rg_a[l][None, :],
                       _block_diag_pairs(w_rg_i[l]), b_rg_i[l][None, :], rg_lambda[l][None, :],
                       batch=batch, seq=seq, ts=256)

        kvm = _norm_proj(mem.reshape(batch * mem_len, d), g_mem[l][None, :], w_mem_kv[l].astype(BF16),
                         no_rope, no_rope, tm=mem_len, tn=2 * XATTN_WIDTH, rope_all=(), rope_half=(),
                         out_dtype=F32, name="memkv")

        h1 = _merge(h, z, y_nsa, y_rnn, kvm, w_xo[l].astype(BF16), w_o[l].astype(BF16),
                    seq=seq, mem_len=mem_len, tq=256)
        last = l == depth - 1
        gf = g_final if last else jnp.ones_like(g_final)
        h = _mlp(h1, g_mlp[l][None, :], w_up[l].astype(BF16), w_down[l].astype(BF16), gf[None, :],
                 tm=1024, tf=1024)
        assert last, "final norm is fused into the last layer's MLP kernel"
    return h.reshape(batch, seq, d)
```

```python
import functools
import math

import numpy as np
import jax
import jax.numpy as jnp
from jax import lax
from jax.experimental import pallas as pl
from jax.experimental.pallas import tpu as pltpu

F32 = jnp.float32
BF16 = jnp.bfloat16

D_MODEL = 1024
HEAD_DIM = 64
NSA_HEADS = 16
NSA_GROUPS = 4
NSA_HPG = NSA_HEADS // NSA_GROUPS
NSA_BRANCHES = 3
CMP_LEN = 32
CMP_STRIDE = 16
CMP_HIDDEN = 256
SEL_BLOCK = 64
SEL_TOPK = 8
WINDOW = 512
Q_BLOCK = 128
ROPE_THETA = 500000.0
ROPE_DIM = HEAD_DIM // 4
ROPE_HALF = ROPE_DIM // 2
RNN_BLOCKS = 16
RNN_BLOCK_DIM = D_MODEL // RNN_BLOCKS
CONV_WIDTH = 4
RGLRU_C = 8.0
XATTN_HEADS = 4
XATTN_WIDTH = XATTN_HEADS * HEAD_DIM
D_FF = 4 * D_MODEL
RMS_EPS = 1e-6
SCALE = HEAD_DIM ** -0.5
LOG2E = math.log2(math.e)
NEG = -1e30

LANES = 128
SUBLANES = 8
BF16_ROWS = 16
VMEM_LIMIT = 48 * 1024 * 1024

NSA_KV = NSA_GROUPS * HEAD_DIM
OFF_Q = 0
OFF_KC = 1024
OFF_VC = 1280
OFF_KS = 1536
OFF_VS = 1792
OFF_KW = 2048
OFF_VW = 2304
OFF_QX = 2560
OFF_GATE = 2816
GATE_STRIDE = 64
OFF_XRNN = 3072
OFF_GRNN = 4096
OFF_MERGE = 5120
Z_WIDTH = 8192
INPROJ_TN = 512
ROPE_TILES = (OFF_KS // INPROJ_TN, OFF_KW // INPROJ_TN)

NSA_GPS = 2
NSA_LANES = NSA_HPG * Q_BLOCK
SEL_CHUNK = 4 * Q_BLOCK
WIN_TILES = WINDOW // Q_BLOCK + 1
K_AUG = 2 * HEAD_DIM
V_AUG = HEAD_DIM + BF16_ROWS

NT_DIMS = (((1,), (1,)), ((), ()))


def _sigmoid(x):
    return 1.0 / (1.0 + jnp.exp(-x))


def _gelu_tanh(x):
    return 0.5 * x * (1.0 + jnp.tanh(0.7978845608028654 * (x + 0.044715 * (x * x * x))))


def _rope_swap(z):
    n = z.shape[-1]
    lane = lax.broadcasted_iota(jnp.int32, z.shape, z.ndim - 1)
    first_half = (lane & (HEAD_DIM - 1)) < ROPE_HALF
    return jnp.where(first_half, pltpu.roll(z, n - ROPE_HALF, z.ndim - 1), pltpu.roll(z, ROPE_HALF, z.ndim - 1))


def _fold_rows(x, op):
    parts = [x[r * SUBLANES:(r + 1) * SUBLANES] for r in range(x.shape[0] // SUBLANES)]
    return functools.reduce(op, parts)


def _norm_proj_kernel(x_ref, g_ref, w_ref, cos_ref, sin_ref, z_ref, u_scr, *, rope_tiles):
    j = pl.program_id(1)

    @pl.when(j == 0)
    def _():
        x = x_ref[...]
        ms = jnp.mean(x * x, axis=-1, keepdims=True)
        u_scr[...] = (x * lax.rsqrt(ms + RMS_EPS) * g_ref[...]).astype(BF16)

    z = jnp.dot(u_scr[...], w_ref[...], preferred_element_type=F32)
    if not rope_tiles:
        z_ref[...] = z.astype(z_ref.dtype)
        return
    is_rope = functools.reduce(jnp.logical_or, [j == t for t in rope_tiles])
    rw = cos_ref.shape[1]

    @pl.when(is_rope)
    def _():
        zr = z[:, :rw]
        zr = zr * cos_ref[...] + _rope_swap(zr) * sin_ref[...]
        z_ref[...] = jnp.concatenate([zr, z[:, rw:]], axis=1).astype(z_ref.dtype)

    @pl.when(jnp.logical_not(is_rope))
    def _():
        z_ref[...] = z.astype(z_ref.dtype)


def _norm_proj(x, g, w, cos_t, sin_t, *, tm, tn, rope_tiles, out_dtype, name):
    t, d = x.shape
    n = w.shape[1]
    s_tiles = cos_t.shape[0] // tm
    rw = cos_t.shape[1]
    table_spec = pl.BlockSpec((tm, rw), lambda i, j: (i % s_tiles, 0))
    return pl.pallas_call(
        functools.partial(_norm_proj_kernel, rope_tiles=tuple(rope_tiles)),
        grid=(t // tm, n // tn),
        in_specs=[
            pl.BlockSpec((tm, d), lambda i, j: (i, 0)),
            pl.BlockSpec((1, d), lambda i, j: (0, 0)),
            pl.BlockSpec((d, tn), lambda i, j: (0, j)),
            table_spec, table_spec,
        ],
        out_specs=pl.BlockSpec((tm, tn), lambda i, j: (i, j)),
        out_shape=jax.ShapeDtypeStruct((t, n), out_dtype),
        scratch_shapes=[pltpu.VMEM((tm, d), BF16)],
        compiler_params=pltpu.CompilerParams(
            dimension_semantics=("parallel", "arbitrary"), vmem_limit_bytes=VMEM_LIMIT),
        name=name,
    )(x, g, w, cos_t, sin_t)


def _compress_kernel(rk_ref, rv_ref, pka_ref, pkb_ref, pva_ref, pvb_ref,
                     wk1a_ref, wk1b_ref, wv1a_ref, wv1b_ref, w2_ref, cos_ref, sin_ref, o_ref):
    m = rk_ref.shape[0]

    def hidden(r_ref, pa_ref, pb_ref, w1a_ref, w1b_ref):
        r = r_ref[...].astype(F32)
        pa = jnp.dot((r + pa_ref[...]).astype(BF16), w1a_ref[...], preferred_element_type=F32)
        pb = jnp.dot((r + pb_ref[...]).astype(BF16), w1b_ref[...], preferred_element_type=F32)
        return _gelu_tanh(pa + pltpu.roll(pb, m - 1, 0)).astype(BF16)

    hk = hidden(rk_ref, pka_ref, pkb_ref, wk1a_ref, wk1b_ref)
    hv = hidden(rv_ref, pva_ref, pvb_ref, wv1a_ref, wv1b_ref)
    hcat = jnp.concatenate([hk, hv], axis=1)
    kv = jnp.dot(hcat, w2_ref[...], preferred_element_type=F32)
    o_ref[...] = kv * cos_ref[...] + _rope_swap(kv) * sin_ref[...]


def _compress(rk, rv, pka, pkb, pva, pvb, wk1a, wk1b, wv1a, wv1b, w2, cos_t, sin_t, *, tm):
    m, k = rk.shape
    full = lambda shape: pl.BlockSpec(shape, lambda i: (0, 0))
    return pl.pallas_call(
        _compress_kernel,
        grid=(m // tm,),
        in_specs=[
            pl.BlockSpec((tm, k), lambda i: (i, 0)),
            pl.BlockSpec((tm, k), lambda i: (i, 0)),
            full((1, k)), full((1, k)), full((1, k)), full((1, k)),
            full((k, CMP_HIDDEN)), full((k, CMP_HIDDEN)), full((k, CMP_HIDDEN)), full((k, CMP_HIDDEN)),
            full((2 * CMP_HIDDEN, 2 * HEAD_DIM)),
            full((tm, 2 * HEAD_DIM)), full((tm, 2 * HEAD_DIM)),
        ],
        out_specs=pl.BlockSpec((tm, 2 * HEAD_DIM), lambda i: (i, 0)),
        out_shape=jax.ShapeDtypeStruct((m, 2 * HEAD_DIM), F32),
        compiler_params=pltpu.CompilerParams(
            dimension_semantics=("parallel",), vmem_limit_bytes=VMEM_LIMIT),
        name="compress",
    )(rk, rv, pka, pkb, pva, pvb, wk1a, wk1b, wv1a, wv1b, w2, cos_t, sin_t)


def _nsa_kernel(q_ref, g_ref, cos_ref, sin_ref, ks_ref, vst_ref, kw_ref, vwt_ref, kvc_ref, c2st_ref, o_ref,
                gt_scr, s_scr, *, n_sel):
    g2 = pl.program_id(1)
    i = pl.program_id(2)
    q0 = i * Q_BLOCK
    blocks_per_chunk = SEL_CHUNK // SEL_BLOCK
    q_per_chunk = SEL_CHUNK // Q_BLOCK

    def tile_heads(x):
        return jnp.concatenate([x] * NSA_HPG, axis=1)

    def extra_rows(first_row):
        rid = lax.broadcasted_iota(jnp.int32, (BF16_ROWS, NSA_LANES), 0)
        head = jnp.where(rid == 0, first_row, 0.0).astype(BF16)
        return jnp.concatenate([head, jnp.zeros((K_AUG - HEAD_DIM - BF16_ROWS, NSA_LANES), BF16)], axis=0)

    tq = q0 + (lax.broadcasted_iota(jnp.int32, (1, NSA_LANES), 1) & (Q_BLOCK - 1))
    c_idx = lax.broadcasted_iota(jnp.int32, (Q_BLOCK, 1), 0)
    r_idx = lax.broadcasted_iota(jnp.int32, (1, Q_BLOCK), 1)
    tri_diag = tile_heads(jnp.where(c_idx <= r_idx, 0.0, NEG))
    tri_old = tile_heads(jnp.where(c_idx > r_idx, 0.0, NEG))
    cos_q = tile_heads(cos_ref[...])
    sin_q = tile_heads(sin_ref[...])
    blk = lax.broadcasted_iota(jnp.int32, (n_sel, Q_BLOCK), 0)
    cur = (q0 + lax.broadcasted_iota(jnp.int32, (n_sel, Q_BLOCK), 1)) // SEL_BLOCK
    sub = lax.broadcasted_iota(jnp.int32, (SUBLANES, Q_BLOCK), 0)

    gt_scr[...] = jnp.transpose(_sigmoid(g_ref[...].astype(F32)))

    def prepare(gg):
        qx = jnp.transpose(q_ref[:, gg * NSA_HPG * HEAD_DIM:(gg + 1) * NSA_HPG * HEAD_DIM].astype(F32)
                           * (SCALE * LOG2E))
        qf = jnp.concatenate([qx[h * HEAD_DIM:(h + 1) * HEAD_DIM, :] for h in range(NSA_HPG)], axis=1)
        x1, x2 = qf[0:ROPE_HALF], qf[ROPE_HALF:ROPE_DIM]
        qt = jnp.concatenate([x1 * cos_q - x2 * sin_q, x2 * cos_q + x1 * sin_q, qf[ROPE_DIM:]],
                             axis=0).astype(BF16)

        kvc = kvc_ref[gg * LANES:(gg + 1) * LANES, :]
        kc = kvc[:, :HEAD_DIM].astype(BF16)
        vct = jnp.transpose(kvc)[HEAD_DIM:, :].astype(BF16)
        sc = jnp.dot(kc, qt, preferred_element_type=F32)
        n_idx = lax.broadcasted_iota(jnp.int32, (LANES, 1), 0)
        sc = jnp.where((n_idx * CMP_STRIDE + (CMP_LEN - 1)) <= tq, sc, NEG)
        m_c = jnp.max(sc, axis=0, keepdims=True)
        e_c = jnp.exp2(sc - m_c)
        inv_c = 1.0 / jnp.maximum(jnp.sum(e_c, axis=0, keepdims=True), 1e-30)
        p_c = e_c * jnp.where(m_c > 0.5 * NEG, inv_c, 0.0)
        o_c = jnp.dot(vct, p_c.astype(BF16), preferred_element_type=F32)

        p_sum = p_c[:, 0:Q_BLOCK]
        for h in range(1, NSA_HPG):
            p_sum = p_sum + p_c[:, h * Q_BLOCK:(h + 1) * Q_BLOCK]
        imp = jnp.dot(c2st_ref[...], p_sum, preferred_element_type=F32,
                      precision=lax.Precision.HIGHEST)
        forced = (blk == 0) | (blk == cur) | (blk == cur - 1)
        val = jnp.where(forced, jnp.inf, jnp.where(blk > cur, -jnp.inf, imp))
        n_grp = n_sel // SUBLANES
        grp = [val[v * SUBLANES:(v + 1) * SUBLANES] for v in range(n_grp)]
        cnt = [jnp.zeros((SUBLANES, Q_BLOCK), F32) for _ in range(n_grp)]
        for sp in range(n_sel):
            r = val[sp:sp + 1, :]
            for v in range(n_grp):
                if v * SUBLANES > sp:
                    cnt[v] = cnt[v] + jnp.where(r >= grp[v], 1.0, 0.0)
                elif v * SUBLANES + SUBLANES - 1 <= sp:
                    cnt[v] = cnt[v] + jnp.where(r > grp[v], 1.0, 0.0)
                else:
                    later = jnp.where(sub > sp - v * SUBLANES, 1.0, 0.0)
                    cnt[v] = cnt[v] + jnp.where(r > grp[v], 1.0, jnp.where(r == grp[v], later, 0.0))
        chosen = jnp.concatenate([jnp.where(c < float(SEL_TOPK), 0.0, NEG) for c in cnt], axis=0)
        block_bias = tile_heads(jnp.where(blk > cur, NEG, chosen))
        row0 = pl.multiple_of((g2 * NSA_GPS + gg) * GATE_STRIDE, GATE_STRIDE)
        gt = gt_scr[pl.ds(row0, 2 * SUBLANES), :]
        return qt, o_c, block_bias, gt

    def window(gg, qt):
        tiles = []
        for d in range(1 - WIN_TILES, 1):
            td = i + d
            tc = jnp.maximum(td, 0)
            k0 = pl.multiple_of(tc * Q_BLOCK, Q_BLOCK)
            before_start = jnp.where(td >= 0, 0.0, NEG) if d < 0 else 0.0
            tri = tri_diag if d == 0 else (tri_old if d == 1 - WIN_TILES else None)
            tiles.append((tc, k0, before_start, tri))

        zero_row = jnp.zeros((1, NSA_LANES), F32)
        scores = []
        for tc, k0, before_start, tri in tiles:
            q_aug = jnp.concatenate([qt, extra_rows(zero_row + before_start)], axis=0)
            s = jnp.dot(kw_ref[0, gg, pl.ds(k0, Q_BLOCK), :], q_aug, preferred_element_type=F32)
            scores.append(s if tri is None else s + tri)
        m_w = jnp.max(functools.reduce(jnp.maximum, [_fold_rows(s, jnp.maximum) for s in scores]),
                      axis=0, keepdims=True)
        n_pad = jnp.maximum(WINDOW - 1 - tq, 0).astype(F32)
        m_w = jnp.where(n_pad > 0.0, jnp.maximum(m_w, 0.0), m_w)
        acc = jnp.zeros((V_AUG, NSA_LANES), F32)
        for (tc, _, _, _), s in zip(tiles, scores):
            acc = acc + jnp.dot(vwt_ref[0, gg, tc], jnp.exp2(s - m_w).astype(BF16), preferred_element_type=F32)
        l_w = acc[HEAD_DIM:HEAD_DIM + 1] + n_pad * jnp.exp2(-m_w)
        return acc[:HEAD_DIM] * (1.0 / jnp.maximum(l_w, 1e-30))

    def selected(gg, k, qt, block_bias):
        rel = i - k * q_per_chunk
        diag = jnp.concatenate([jnp.where(rel == u, tri_diag, 0.0) for u in range(q_per_chunk)], axis=0)
        pad_rows = jnp.zeros((K_AUG - HEAD_DIM - n_sel, NSA_LANES), BF16)
        q_aug = jnp.concatenate([qt, block_bias.astype(BF16), pad_rows], axis=0)
        m8 = jnp.full((SUBLANES, NSA_LANES), NEG, F32)
        for c in range(k + 1):
            s = jnp.dot(ks_ref[0, gg, c * SEL_CHUNK:(c + 1) * SEL_CHUNK, :], q_aug,
                        preferred_element_type=F32)
            if c == k:
                s = s + diag
            s_scr[gg, c * SEL_CHUNK:(c + 1) * SEL_CHUNK, :] = s
            m8 = jnp.maximum(m8, _fold_rows(s, jnp.maximum))
        m_s = jnp.max(m8, axis=0, keepdims=True)
        acc = jnp.zeros((V_AUG, NSA_LANES), F32)
        for c in range(k + 1):
            p = jnp.exp2(s_scr[gg, c * SEL_CHUNK:(c + 1) * SEL_CHUNK, :] - m_s)
            acc = acc + jnp.dot(vst_ref[0, gg, c], p.astype(BF16), preferred_element_type=F32)
        l_s = acc[HEAD_DIM:HEAD_DIM + 1]
        return acc[:HEAD_DIM] * (1.0 / jnp.maximum(l_s, 1e-30))

    prepared = [prepare(gg) for gg in range(NSA_GPS)]

    def finish(k):
        outs = []
        for gg, (qt, o_c, block_bias, gt) in enumerate(prepared):
            o_w = window(gg, qt)
            o_s = selected(gg, k, qt, block_bias)
            for h in range(NSA_HPG):
                sl = slice(h * Q_BLOCK, (h + 1) * Q_BLOCK)
                row = NSA_BRANCHES * h
                outs.append(gt[row:row + 1, :] * o_c[:, sl] + gt[row + 1:row + 2, :] * o_s[:, sl]
                            + gt[row + 2:row + 3, :] * o_w[:, sl])
        o_ref[...] = jnp.transpose(jnp.concatenate(outs, axis=0))

    for k in range(n_sel * SEL_BLOCK // SEL_CHUNK):
        pl.when(i // q_per_chunk == k)(functools.partial(finish, k))


def _nsa(z, cos_q, sin_q, ks, vst, kw, vwt, kvc, c2st, *, batch, seq):
    nq = seq // Q_BLOCK
    n_sel = seq // SEL_BLOCK
    gw = NSA_GPS * NSA_HPG * HEAD_DIM
    gates_w = NSA_GROUPS * GATE_STRIDE
    k_spec = pl.BlockSpec((1, NSA_GPS, seq, K_AUG), lambda b, g, i: (b, g, 0, 0))
    vt_spec = lambda tile: pl.BlockSpec((1, NSA_GPS, seq // tile, V_AUG, tile), lambda b, g, i: (b, g, 0, 0, 0))
    rope_spec = pl.BlockSpec((ROPE_HALF, Q_BLOCK), lambda b, g, i: (0, i))
    return pl.pallas_call(
        functools.partial(_nsa_kernel, n_sel=n_sel),
        grid=(batch, NSA_GROUPS // NSA_GPS, nq),
        in_specs=[
            pl.BlockSpec((Q_BLOCK, gw), lambda b, g, i: (b * nq + i, OFF_Q // gw + g)),
            pl.BlockSpec((Q_BLOCK, gates_w), lambda b, g, i: (b * nq + i, OFF_GATE // gates_w)),
            rope_spec, rope_spec,
            k_spec, vt_spec(SEL_CHUNK), k_spec, vt_spec(Q_BLOCK),
            pl.BlockSpec((NSA_GPS * LANES, 2 * HEAD_DIM), lambda b, g, i: (b * (NSA_GROUPS // NSA_GPS) + g, 0)),
            pl.BlockSpec((n_sel, LANES), lambda b, g, i: (0, 0)),
        ],
        out_specs=pl.BlockSpec((Q_BLOCK, gw), lambda b, g, i: (b * nq + i, g)),
        out_shape=jax.ShapeDtypeStruct((batch * seq, NSA_HEADS * HEAD_DIM), F32),
        scratch_shapes=[pltpu.VMEM((gates_w, Q_BLOCK), F32), pltpu.VMEM((NSA_GPS, seq, NSA_LANES), F32)],
        compiler_params=pltpu.CompilerParams(
            dimension_semantics=("parallel", "parallel", "arbitrary"), vmem_limit_bytes=VMEM_LIMIT),
        name="nsa",
    )(z, z, cos_q, sin_q, ks, vst, kw, vwt, kvc, c2st)


def _rglru_kernel(x_ref, gr_ref, cw_ref, cb_ref, wa_ref, ba_ref, wi_ref, bi_ref, lam_ref, y_ref,
                  xprev_scr, h_scr):
    ts, c = x_ref.shape

    @pl.when(pl.program_id(1) == 0)
    def _():
        xprev_scr[...] = jnp.zeros(xprev_scr.shape, F32)
        h_scr[...] = jnp.zeros(h_scr.shape, F32)

    x = x_ref[...].astype(F32)
    xx = jnp.concatenate([xprev_scr[...], x], axis=0)
    xprev_scr[...] = x[ts - SUBLANES:, :]
    xc = cb_ref[...] + cw_ref[CONV_WIDTH - 1:CONV_WIDTH, :] * x
    for k in range(CONV_WIDTH - 1):
        off = SUBLANES - (CONV_WIDTH - 1) + k
        xc = xc + cw_ref[k:k + 1, :] * xx[off:off + ts, :]

    ra, ri = [], []
    for mblk in range(c // LANES):
        xb = xc[:, mblk * LANES:(mblk + 1) * LANES].astype(BF16)
        ra.append(jnp.dot(xb, wa_ref[mblk], preferred_element_type=F32))
        ri.append(jnp.dot(xb, wi_ref[mblk], preferred_element_type=F32))
    r = _sigmoid(jnp.concatenate(ra, axis=1) + ba_ref[...])
    gi = _sigmoid(jnp.concatenate(ri, axis=1) + bi_ref[...])

    nl = -lam_ref[...]
    softplus = jnp.maximum(nl, 0.0) + jnp.log1p(jnp.exp(-jnp.abs(nl)))
    log_a = (-RGLRU_C) * r * softplus
    a = jnp.exp(log_a)
    th = jnp.tanh(log_a)
    b = jnp.sqrt(-2.0 * th / (1.0 - th)) * (gi * xc)

    ridx = lax.broadcasted_iota(jnp.int32, (ts, c), 0)
    d = 1
    while d < ts:
        keep = ridx >= d
        a_sh = jnp.where(keep, pltpu.roll(a, d, 0), 1.0)
        b_sh = jnp.where(keep, pltpu.roll(b, d, 0), 0.0)
        b = a * b_sh + b
        a = a * a_sh
        d *= 2
    h = b + a * h_scr[...]
    h_scr[...] = h[ts - 1:ts, :]
    y_ref[...] = _gelu_tanh(gr_ref[...].astype(F32)) * h


def _rglru(z, cw, cb, wa2, ba, wi2, bi, lam, *, batch, seq, ts):
    c = D_MODEL
    nt = seq // ts
    vec = pl.BlockSpec((1, c), lambda b, t: (0, 0))
    return pl.pallas_call(
        _rglru_kernel,
        grid=(batch, nt),
        in_specs=[
            pl.BlockSpec((ts, c), lambda b, t: (b * nt + t, OFF_XRNN // c)),
            pl.BlockSpec((ts, c), lambda b, t: (b * nt + t, OFF_GRNN // c)),
            pl.BlockSpec((CONV_WIDTH, c), lambda b, t: (0, 0)),
            vec,
            pl.BlockSpec((c // LANES, LANES, LANES), lambda b, t: (0, 0, 0)),
            vec,
            pl.BlockSpec((c // LANES, LANES, LANES), lambda b, t: (0, 0, 0)),
            vec, vec,
        ],
        out_specs=pl.BlockSpec((ts, c), lambda b, t: (b * nt + t, 0)),
        out_shape=jax.ShapeDtypeStruct((batch * seq, c), F32),
        scratch_shapes=[pltpu.VMEM((SUBLANES, c), F32), pltpu.VMEM((1, c), F32)],
        compiler_params=pltpu.CompilerParams(
            dimension_semantics=("parallel", "arbitrary"), vmem_limit_bytes=VMEM_LIMIT),
        name="rglru",
    )(z, z, cw, cb, wa2, ba, wi2, bi, lam)


def _merge_kernel(x_ref, m0_ref, m1_ref, m2_ref, ynsa_ref, yrnn_ref, qx_ref, kvm_ref, wxo_ref, wo_ref, h_ref):
    qx = qx_ref[...].astype(F32) * SCALE
    kvm = kvm_ref[...]
    yx = None
    for h in range(XATTN_HEADS):
        lo = h * HEAD_DIM
        qh = qx[:, lo:lo + HEAD_DIM].astype(BF16)
        kh = kvm[:, lo:lo + HEAD_DIM].astype(BF16)
        vh = kvm[:, XATTN_WIDTH + lo:XATTN_WIDTH + lo + HEAD_DIM].astype(BF16)
        s = lax.dot_general(qh, kh, NT_DIMS, preferred_element_type=F32)
        e = jnp.exp(s - jnp.max(s, axis=-1, keepdims=True))
        p = e / jnp.sum(e, axis=-1, keepdims=True)
        oh = jnp.dot(p.astype(BF16), vh, preferred_element_type=F32)
        part = jnp.dot(oh.astype(BF16), wxo_ref[lo:lo + HEAD_DIM, :], preferred_element_type=F32)
        yx = part if yx is None else yx + part
    y = (_sigmoid(m0_ref[...].astype(F32)) * ynsa_ref[...] + _sigmoid(m1_ref[...].astype(F32)) * yrnn_ref[...]
         + _sigmoid(m2_ref[...].astype(F32)) * yx)
    h_ref[...] = x_ref[...] + jnp.dot(y.astype(BF16), wo_ref[...], preferred_element_type=F32)


def _merge(x, z, ynsa, yrnn, kvm, wxo, wo, *, seq, mem_len, tq):
    t, d = x.shape
    nt = seq // tq
    row = lambda cb: pl.BlockSpec((tq, d), lambda i, cb=cb: (i, cb))
    return pl.pallas_call(
        _merge_kernel,
        grid=(t // tq,),
        in_specs=[
            row(0),
            row(OFF_MERGE // d), row(OFF_MERGE // d + 1), row(OFF_MERGE // d + 2),
            row(0), row(0),
            pl.BlockSpec((tq, XATTN_WIDTH), lambda i: (i, OFF_QX // XATTN_WIDTH)),
            pl.BlockSpec((mem_len, 2 * XATTN_WIDTH), lambda i: (i // nt, 0)),
            pl.BlockSpec((XATTN_WIDTH, d), lambda i: (0, 0)),
            pl.BlockSpec((d, d), lambda i: (0, 0)),
        ],
        out_specs=row(0),
        out_shape=jax.ShapeDtypeStruct((t, d), F32),
        compiler_params=pltpu.CompilerParams(
            dimension_semantics=("parallel",), vmem_limit_bytes=VMEM_LIMIT),
        name="merge",
    )(x, z, z, z, ynsa, yrnn, z, kvm, wxo, wo)


def _mlp_kernel(h_ref, g_ref, wup_ref, wdn_ref, gf_ref, o_ref, v_scr, acc_scr):
    f = pl.program_id(1)

    @pl.when(f == 0)
    def _():
        h = h_ref[...]
        ms = jnp.mean(h * h, axis=-1, keepdims=True)
        v_scr[...] = (h * lax.rsqrt(ms + RMS_EPS) * g_ref[...]).astype(BF16)
        acc_scr[...] = jnp.zeros(acc_scr.shape, F32)

    up = jnp.dot(v_scr[...], wup_ref[...], preferred_element_type=F32)
    act = jnp.square(jnp.maximum(up, 0.0)).astype(BF16)
    acc_scr[...] += jnp.dot(act, wdn_ref[...], preferred_element_type=F32)

    @pl.when(f == pl.num_programs(1) - 1)
    def _():
        h2 = h_ref[...] + acc_scr[...]
        ms = jnp.mean(h2 * h2, axis=-1, keepdims=True)
        o_ref[...] = h2 * lax.rsqrt(ms + RMS_EPS) * gf_ref[...]


def _mlp(h, g, wup, wdn, gf, *, tm, tf):
    t, d = h.shape
    dff = wup.shape[1]
    return pl.pallas_call(
        _mlp_kernel,
        grid=(t // tm, dff // tf),
        in_specs=[
            pl.BlockSpec((tm, d), lambda i, f: (i, 0)),
            pl.BlockSpec((1, d), lambda i, f: (0, 0)),
            pl.BlockSpec((d, tf), lambda i, f: (0, f)),
            pl.BlockSpec((tf, d), lambda i, f: (f, 0)),
            pl.BlockSpec((1, d), lambda i, f: (0, 0)),
        ],
        out_specs=pl.BlockSpec((tm, d), lambda i, f: (i, 0)),
        out_shape=jax.ShapeDtypeStruct((t, d), F32),
        scratch_shapes=[pltpu.VMEM((tm, d), BF16), pltpu.VMEM((tm, d), F32)],
        compiler_params=pltpu.CompilerParams(
            dimension_semantics=("parallel", "arbitrary"), vmem_limit_bytes=VMEM_LIMIT),
        name="mlp",
    )(h, g, wup, wdn, gf)


def _rope_angles(pos):
    inv = 1.0 / (ROPE_THETA ** (jnp.arange(0, ROPE_DIM, 2, dtype=F32) / ROPE_DIM))
    ang = pos.astype(F32)[:, None] * inv[None, :]
    return jnp.cos(ang), jnp.sin(ang)


def _rope_tables(pos, width, rope_width=None):
    rope_width = width if rope_width is None else rope_width
    cos, sin = _rope_angles(pos)
    n = pos.shape[0]
    pad = HEAD_DIM - ROPE_DIM
    cos_h = jnp.concatenate([cos, cos, jnp.ones((n, pad), F32)], axis=1)
    sin_h = jnp.concatenate([-sin, sin, jnp.zeros((n, pad), F32)], axis=1)
    reps = rope_width // HEAD_DIM
    rest = width - rope_width
    cos_t = jnp.concatenate([jnp.tile(cos_h, (1, reps)), jnp.ones((n, rest), F32)], axis=1)
    sin_t = jnp.concatenate([jnp.tile(sin_h, (1, reps)), jnp.zeros((n, rest), F32)], axis=1)
    return cos_t, sin_t


def _pack_w_in(w_in):
    d = w_in.shape[0]
    w = w_in.astype(BF16)
    n_qkv = NSA_HEADS * HEAD_DIM + 6 * NSA_KV
    n_gates = NSA_HEADS * NSA_BRANCHES
    rnn_lo = n_qkv + n_gates
    qx_lo = rnn_lo + 2 * D_MODEL
    mg_lo = qx_lo + XATTN_WIDTH
    per_group = n_gates // NSA_GROUPS
    gates = w[:, n_qkv:rnn_lo].reshape(d, NSA_GROUPS, per_group)
    gates = jnp.pad(gates, ((0, 0), (0, 0), (0, GATE_STRIDE - per_group))).reshape(d, NSA_GROUPS * GATE_STRIDE)
    return jnp.concatenate([w[:, :n_qkv], w[:, qx_lo:mg_lo], gates, w[:, rnn_lo:qx_lo], w[:, mg_lo:]], axis=1)


def _block_diag_pairs(w):
    nb, k, _ = w.shape
    w = w.reshape(nb // 2, 2, k, k)
    zero = jnp.zeros((nb // 2, k, k), w.dtype)
    top = jnp.concatenate([w[:, 0], zero], axis=2)
    bot = jnp.concatenate([zero, w[:, 1]], axis=2)
    return jnp.concatenate([top, bot], axis=1).astype(BF16)


def _cmp_to_sel_t(n_cmp, n_cmp_pad, n_sel):
    c0 = np.arange(n_cmp_pad)[None, :] * CMP_STRIDE
    s0 = np.arange(n_sel)[:, None] * SEL_BLOCK
    ov = np.clip(np.minimum(c0 + CMP_LEN, s0 + SEL_BLOCK) - np.maximum(c0, s0), 0, None)
    ov = np.where(np.arange(n_cmp_pad)[None, :] < n_cmp, ov, 0)
    return (ov / CMP_LEN).astype(np.float32)


def _key_features(seq, n_sel):
    extra = K_AUG - HEAD_DIM
    onehot = (np.arange(seq)[:, None] // SEL_BLOCK == np.arange(extra)[None, :]) & (np.arange(extra)[None, :] < n_sel)
    one = np.arange(extra)[None, :] == np.zeros((seq, 1), np.int64)
    return jnp.asarray(onehot, dtype=BF16), jnp.asarray(one, dtype=BF16)


def kernel(x, mem, g_mix, w_in, cmp_pos_k, cmp_pos_v, w_cmp_k1, w_cmp_k2, w_cmp_v1, w_cmp_v2, conv_w, conv_b, w_rg_a, b_rg_a, w_rg_i, b_rg_i, rg_lambda, g_mem, w_mem_kv, w_xo, w_o, g_mlp, w_up, w_down, g_final):
    batch, seq, d = x.shape
    mem_len = mem.shape[1]
    t = batch * seq
    depth = g_mix.shape[0]
    n_sel = seq // SEL_BLOCK
    n_cmp_pad = seq // CMP_STRIDE
    assert n_cmp_pad == LANES and n_sel <= K_AUG - HEAD_DIM and d == D_MODEL

    pos = jnp.arange(seq)
    cos_k, sin_k = _rope_tables(pos, NSA_KV)
    cos_q, sin_q = (a.T for a in _rope_angles(pos))
    no_rope = jnp.zeros((mem_len, LANES), F32)
    cmp_pos = jnp.arange(n_cmp_pad) * CMP_STRIDE + (CMP_LEN - 1)
    cmp_tm = 8 * n_cmp_pad
    cos_c, sin_c = _rope_tables(cmp_pos, 2 * HEAD_DIM, HEAD_DIM)
    cos_c = jnp.tile(cos_c, (cmp_tm // n_cmp_pad, 1))
    sin_c = jnp.tile(sin_c, (cmp_tm // n_cmp_pad, 1))
    c2st = jnp.asarray(_cmp_to_sel_t((seq - CMP_LEN) // CMP_STRIDE + 1, n_cmp_pad, n_sel))
    feat_sel, feat_win = _key_features(seq, n_sel)

    h = x.reshape(t, d)
    for l in range(depth):
        z = _norm_proj(h, g_mix[l][None, :], _pack_w_in(w_in[l]), cos_k, sin_k, tm=1024, tn=INPROJ_TN,
                       rope_tiles=ROPE_TILES, out_dtype=BF16, name="inproj")

        def groups(off):
            a = z[:, off:off + NSA_KV].reshape(batch, seq, NSA_GROUPS, HEAD_DIM)
            return a.transpose(0, 2, 1, 3)

        def with_features(k, feat):
            return jnp.concatenate([k, jnp.broadcast_to(feat, k.shape[:2] + feat.shape)], axis=-1)

        def groups_t(off, tile):
            a = z[:, off:off + NSA_KV].reshape(batch, seq // tile, tile, NSA_GROUPS, HEAD_DIM)
            a = a.transpose(0, 3, 1, 4, 2)
            ones_row = (jnp.arange(V_AUG - HEAD_DIM) == 0).astype(a.dtype)[:, None]
            return jnp.concatenate([a, jnp.broadcast_to(ones_row, a.shape[:3] + (V_AUG - HEAD_DIM, tile))], axis=3)

        ks, vst = with_features(groups(OFF_KS), feat_sel), groups_t(OFF_VS, SEL_CHUNK)
        kw, vwt = with_features(groups(OFF_KW), feat_win), groups_t(OFF_VW, Q_BLOCK)
        half = CMP_STRIDE * HEAD_DIM
        rk = groups(OFF_KC).reshape(batch * NSA_GROUPS * n_cmp_pad, half)
        rv = groups(OFF_VC).reshape(batch * NSA_GROUPS * n_cmp_pad, half)
        pk = cmp_pos_k[l].reshape(2, half)
        pv = cmp_pos_v[l].reshape(2, half)
        w2 = jnp.zeros((2 * CMP_HIDDEN, 2 * HEAD_DIM), F32)
        w2 = w2.at[:CMP_HIDDEN, :HEAD_DIM].set(w_cmp_k2[l]).at[CMP_HIDDEN:, HEAD_DIM:].set(w_cmp_v2[l])
        kvc = _compress(rk, rv, pk[0:1], pk[1:2], pv[0:1], pv[1:2],
                        w_cmp_k1[l][:half].astype(BF16), w_cmp_k1[l][half:].astype(BF16),
                        w_cmp_v1[l][:half].astype(BF16), w_cmp_v1[l][half:].astype(BF16),
                        w2.astype(BF16), cos_c, sin_c, tm=cmp_tm)
        y_nsa = _nsa(z, cos_q, sin_q, ks, vst, kw, vwt, kvc, c2st, batch=batch, seq=seq)

        y_rnn = _rglru(z, conv_w[l], conv_b[l][None, :], _block_diag_pairs(w_rg_a[l]), b_rg_a[l][None, :],
                       _block_diag_pairs(w_rg_i[l]), b_rg_i[l][None, :], rg_lambda[l][None, :],
                       batch=batch, seq=seq, ts=256)

        kvm = _norm_proj(mem.reshape(batch * mem_len, d), g_mem[l][None, :], w_mem_kv[l].astype(BF16),
                         no_rope, no_rope, tm=mem_len, tn=2 * XATTN_WIDTH, rope_tiles=(),
                         out_dtype=F32, name="memkv")

        h1 = _merge(h, z, y_nsa, y_rnn, kvm, w_xo[l].astype(BF16), w_o[l].astype(BF16),
                    seq=seq, mem_len=mem_len, tq=256)
        last = l == depth - 1
        gf = g_final if last else jnp.ones_like(g_final)
        h = _mlp(h1, g_mlp[l][None, :], w_up[l].astype(BF16), w_down[l].astype(BF16), gf[None, :],
                 tm=1024, tf=1024)
        assert last, "final norm is fused into the last layer's MLP kernel"
    return h.reshape(batch, seq, d)
```

```python
import functools
import math

import numpy as np
import jax
import jax.numpy as jnp
from jax import lax
from jax.experimental import pallas as pl
from jax.experimental.pallas import tpu as pltpu

F32 = jnp.float32
BF16 = jnp.bfloat16

D_MODEL = 1024
HEAD_DIM = 64
NSA_HEADS = 16
NSA_GROUPS = 4
NSA_HPG = NSA_HEADS // NSA_GROUPS
NSA_BRANCHES = 3
CMP_LEN = 32
CMP_STRIDE = 16
CMP_HIDDEN = 256
SEL_BLOCK = 64
SEL_TOPK = 8
WINDOW = 512
Q_BLOCK = 128
ROPE_THETA = 500000.0
ROPE_DIM = HEAD_DIM // 4
ROPE_HALF = ROPE_DIM // 2
RNN_BLOCKS = 16
RNN_BLOCK_DIM = D_MODEL // RNN_BLOCKS
CONV_WIDTH = 4
RGLRU_C = 8.0
XATTN_HEADS = 4
XATTN_WIDTH = XATTN_HEADS * HEAD_DIM
D_FF = 4 * D_MODEL
RMS_EPS = 1e-6
SCALE = HEAD_DIM ** -0.5
LOG2E = math.log2(math.e)
NEG = -1e30

LANES = 128
SUBLANES = 8
BF16_ROWS = 16
VMEM_LIMIT = 48 * 1024 * 1024

NSA_KV = NSA_GROUPS * HEAD_DIM
OFF_Q = 0
OFF_KC = 1024
OFF_VC = 1280
OFF_KS = 1536
OFF_VS = 1792
OFF_KW = 2048
OFF_VW = 2304
OFF_QX = 2560
OFF_GATE = 2816
GATE_STRIDE = 64
OFF_XRNN = 3072
OFF_GRNN = 4096
OFF_MERGE = 5120
Z_WIDTH = 8192
INPROJ_TN = 512
ROPE_TILES = (OFF_KS // INPROJ_TN, OFF_KW // INPROJ_TN)

NSA_GPS = 2
NSA_LANES = NSA_HPG * Q_BLOCK
SEL_CHUNK = 4 * Q_BLOCK
WIN_TILES = WINDOW // Q_BLOCK + 1
K_AUG = 2 * HEAD_DIM
V_AUG = HEAD_DIM + BF16_ROWS

NT_DIMS = (((1,), (1,)), ((), ()))


def _sigmoid(x):
    return 1.0 / (1.0 + jnp.exp(-x))


def _gelu_tanh(x):
    return 0.5 * x * (1.0 + jnp.tanh(0.7978845608028654 * (x + 0.044715 * (x * x * x))))


def _rope_swap(z):
    n = z.shape[-1]
    lane = lax.broadcasted_iota(jnp.int32, z.shape, z.ndim - 1)
    first_half = (lane & (HEAD_DIM - 1)) < ROPE_HALF
    return jnp.where(first_half, pltpu.roll(z, n - ROPE_HALF, z.ndim - 1), pltpu.roll(z, ROPE_HALF, z.ndim - 1))


def _fold_rows(x, op):
    parts = [x[r * SUBLANES:(r + 1) * SUBLANES] for r in range(x.shape[0] // SUBLANES)]
    return functools.reduce(op, parts)


def _norm_proj_kernel(x_ref, g_ref, w_ref, cos_ref, sin_ref, z_ref, u_scr, *, rope_tiles):
    j = pl.program_id(1)

    @pl.when(j == 0)
    def _():
        x = x_ref[...]
        ms = jnp.mean(x * x, axis=-1, keepdims=True)
        u_scr[...] = (x * lax.rsqrt(ms + RMS_EPS) * g_ref[...]).astype(BF16)

    z = jnp.dot(u_scr[...], w_ref[...], preferred_element_type=F32)
    if not rope_tiles:
        z_ref[...] = z.astype(z_ref.dtype)
        return
    is_rope = functools.reduce(jnp.logical_or, [j == t for t in rope_tiles])
    rw = cos_ref.shape[1]

    @pl.when(is_rope)
    def _():
        zr = z[:, :rw]
        zr = zr * cos_ref[...] + _rope_swap(zr) * sin_ref[...]
        z_ref[...] = jnp.concatenate([zr, z[:, rw:]], axis=1).astype(z_ref.dtype)

    @pl.when(jnp.logical_not(is_rope))
    def _():
        z_ref[...] = z.astype(z_ref.dtype)


def _norm_proj(x, g, w, cos_t, sin_t, *, tm, tn, rope_tiles, out_dtype, name):
    t, d = x.shape
    n = w.shape[1]
    s_tiles = cos_t.shape[0] // tm
    rw = cos_t.shape[1]
    table_spec = pl.BlockSpec((tm, rw), lambda i, j: (i % s_tiles, 0))
    return pl.pallas_call(
        functools.partial(_norm_proj_kernel, rope_tiles=tuple(rope_tiles)),
        grid=(t // tm, n // tn),
        in_specs=[
            pl.BlockSpec((tm, d), lambda i, j: (i, 0)),
            pl.BlockSpec((1, d), lambda i, j: (0, 0)),
            pl.BlockSpec((d, tn), lambda i, j: (0, j)),
            table_spec, table_spec,
        ],
        out_specs=pl.BlockSpec((tm, tn), lambda i, j: (i, j)),
        out_shape=jax.ShapeDtypeStruct((t, n), out_dtype),
        scratch_shapes=[pltpu.VMEM((tm, d), BF16)],
        compiler_params=pltpu.CompilerParams(
            dimension_semantics=("parallel", "arbitrary"), vmem_limit_bytes=VMEM_LIMIT),
        name=name,
    )(x, g, w, cos_t, sin_t)


def _compress_kernel(rk_ref, rv_ref, pka_ref, pkb_ref, pva_ref, pvb_ref,
                     wk1a_ref, wk1b_ref, wv1a_ref, wv1b_ref, w2_ref, cos_ref, sin_ref, o_ref):
    m = rk_ref.shape[0]

    def hidden(r_ref, pa_ref, pb_ref, w1a_ref, w1b_ref):
        r = r_ref[...].astype(F32)
        pa = jnp.dot((r + pa_ref[...]).astype(BF16), w1a_ref[...], preferred_element_type=F32)
        pb = jnp.dot((r + pb_ref[...]).astype(BF16), w1b_ref[...], preferred_element_type=F32)
        return _gelu_tanh(pa + pltpu.roll(pb, m - 1, 0)).astype(BF16)

    hk = hidden(rk_ref, pka_ref, pkb_ref, wk1a_ref, wk1b_ref)
    hv = hidden(rv_ref, pva_ref, pvb_ref, wv1a_ref, wv1b_ref)
    hcat = jnp.concatenate([hk, hv], axis=1)
    kv = jnp.dot(hcat, w2_ref[...], preferred_element_type=F32)
    o_ref[...] = kv * cos_ref[...] + _rope_swap(kv) * sin_ref[...]


def _compress(rk, rv, pka, pkb, pva, pvb, wk1a, wk1b, wv1a, wv1b, w2, cos_t, sin_t, *, tm):
    m, k = rk.shape
    full = lambda shape: pl.BlockSpec(shape, lambda i: (0, 0))
    return pl.pallas_call(
        _compress_kernel,
        grid=(m // tm,),
        in_specs=[
            pl.BlockSpec((tm, k), lambda i: (i, 0)),
            pl.BlockSpec((tm, k), lambda i: (i, 0)),
            full((1, k)), full((1, k)), full((1, k)), full((1, k)),
            full((k, CMP_HIDDEN)), full((k, CMP_HIDDEN)), full((k, CMP_HIDDEN)), full((k, CMP_HIDDEN)),
            full((2 * CMP_HIDDEN, 2 * HEAD_DIM)),
            full((tm, 2 * HEAD_DIM)), full((tm, 2 * HEAD_DIM)),
        ],
        out_specs=pl.BlockSpec((tm, 2 * HEAD_DIM), lambda i: (i, 0)),
        out_shape=jax.ShapeDtypeStruct((m, 2 * HEAD_DIM), F32),
        compiler_params=pltpu.CompilerParams(
            dimension_semantics=("parallel",), vmem_limit_bytes=VMEM_LIMIT),
        name="compress",
    )(rk, rv, pka, pkb, pva, pvb, wk1a, wk1b, wv1a, wv1b, w2, cos_t, sin_t)


def _nsa_kernel(q_ref, g_ref, cos_ref, sin_ref, ks_ref, vs_ref, kw_ref, vw_ref, feat_ref, kvc_ref, c2st_ref, o_ref,
                gt_scr, s_scr, vt_scr, *, n_sel):
    g2 = pl.program_id(1)
    i = pl.program_id(2)
    q0 = i * Q_BLOCK
    seq = ks_ref.shape[0]

    @pl.when(i == 0)
    def _():
        ones_rows = jnp.where(lax.broadcasted_iota(jnp.int32, (V_AUG - HEAD_DIM, Q_BLOCK), 0) == 0,
                              1.0, 0.0).astype(BF16)

        def fill(t, carry):
            r0 = pl.multiple_of(t * Q_BLOCK, Q_BLOCK)
            for kind, ref in enumerate((vs_ref, vw_ref)):
                vt = jnp.transpose(ref[pl.ds(r0, Q_BLOCK), :].astype(F32)).astype(BF16)
                for gg in range(NSA_GPS):
                    vt_scr[kind, gg, t, 0:HEAD_DIM, :] = vt[gg * HEAD_DIM:(gg + 1) * HEAD_DIM]
                    vt_scr[kind, gg, t, HEAD_DIM:V_AUG, :] = ones_rows
            return carry

        lax.fori_loop(0, seq // Q_BLOCK, fill, 0)

    def query_side(gg, qt, bias_rows, win_row):
        zeros_q = jnp.zeros((HEAD_DIM, NSA_LANES), BF16)
        q_rows = [qt, zeros_q] if gg == 0 else [zeros_q, qt]
        rid = lax.broadcasted_iota(jnp.int32, (BF16_ROWS, NSA_LANES), 0)
        win_rows = jnp.where(rid == 0, win_row, 0.0).astype(BF16)
        tail = jnp.zeros((2 * LANES - NSA_GPS * HEAD_DIM - n_sel - BF16_ROWS, NSA_LANES), BF16)
        return jnp.concatenate(q_rows + [bias_rows.astype(BF16), win_rows, tail], axis=0)

    def key_side(k_ref, row0, rows):
        return jnp.concatenate([k_ref[pl.ds(row0, rows), :], feat_ref[pl.ds(row0, rows), :]], axis=1)
    blocks_per_chunk = SEL_CHUNK // SEL_BLOCK
    q_per_chunk = SEL_CHUNK // Q_BLOCK

    def tile_heads(x):
        return jnp.concatenate([x] * NSA_HPG, axis=1)

    tq = q0 + (lax.broadcasted_iota(jnp.int32, (1, NSA_LANES), 1) & (Q_BLOCK - 1))
    c_idx = lax.broadcasted_iota(jnp.int32, (Q_BLOCK, 1), 0)
    r_idx = lax.broadcasted_iota(jnp.int32, (1, Q_BLOCK), 1)
    tri_diag = tile_heads(jnp.where(c_idx <= r_idx, 0.0, NEG))
    tri_old = tile_heads(jnp.where(c_idx > r_idx, 0.0, NEG))
    cos_q = tile_heads(cos_ref[...])
    sin_q = tile_heads(sin_ref[...])
    blk = lax.broadcasted_iota(jnp.int32, (n_sel, Q_BLOCK), 0)
    cur = (q0 + lax.broadcasted_iota(jnp.int32, (n_sel, Q_BLOCK), 1)) // SEL_BLOCK
    sub = lax.broadcasted_iota(jnp.int32, (SUBLANES, Q_BLOCK), 0)

    gt_scr[...] = jnp.transpose(_sigmoid(g_ref[...].astype(F32)))

    def prepare(gg):
        qx = jnp.transpose(q_ref[:, gg * NSA_HPG * HEAD_DIM:(gg + 1) * NSA_HPG * HEAD_DIM].astype(F32)
                           * (SCALE * LOG2E))
        qf = jnp.concatenate([qx[h * HEAD_DIM:(h + 1) * HEAD_DIM, :] for h in range(NSA_HPG)], axis=1)
        x1, x2 = qf[0:ROPE_HALF], qf[ROPE_HALF:ROPE_DIM]
        qt = jnp.concatenate([x1 * cos_q - x2 * sin_q, x2 * cos_q + x1 * sin_q, qf[ROPE_DIM:]],
                             axis=0).astype(BF16)

        kvc = kvc_ref[gg * LANES:(gg + 1) * LANES, :]
        kc = kvc[:, :HEAD_DIM].astype(BF16)
        vct = jnp.transpose(kvc)[HEAD_DIM:, :].astype(BF16)
        sc = jnp.dot(kc, qt, preferred_element_type=F32)
        n_idx = lax.broadcasted_iota(jnp.int32, (LANES, 1), 0)
        sc = jnp.where((n_idx * CMP_STRIDE + (CMP_LEN - 1)) <= tq, sc, NEG)
        m_c = jnp.max(sc, axis=0, keepdims=True)
        e_c = jnp.exp2(sc - m_c)
        inv_c = 1.0 / jnp.maximum(jnp.sum(e_c, axis=0, keepdims=True), 1e-30)
        p_c = e_c * jnp.where(m_c > 0.5 * NEG, inv_c, 0.0)
        o_c = jnp.dot(vct, p_c.astype(BF16), preferred_element_type=F32)

        p_sum = p_c[:, 0:Q_BLOCK]
        for h in range(1, NSA_HPG):
            p_sum = p_sum + p_c[:, h * Q_BLOCK:(h + 1) * Q_BLOCK]
        imp = jnp.dot(c2st_ref[...], p_sum, preferred_element_type=F32,
                      precision=lax.Precision.HIGHEST)
        forced = (blk == 0) | (blk == cur) | (blk == cur - 1)
        val = jnp.where(forced, jnp.inf, jnp.where(blk > cur, -jnp.inf, imp))
        n_grp = n_sel // SUBLANES
        grp = [val[v * SUBLANES:(v + 1) * SUBLANES] for v in range(n_grp)]
        cnt = [jnp.zeros((SUBLANES, Q_BLOCK), F32) for _ in range(n_grp)]
        for sp in range(n_sel):
            r = val[sp:sp + 1, :]
            for v in range(n_grp):
                if v * SUBLANES > sp:
                    cnt[v] = cnt[v] + jnp.where(r >= grp[v], 1.0, 0.0)
                elif v * SUBLANES + SUBLANES - 1 <= sp:
                    cnt[v] = cnt[v] + jnp.where(r > grp[v], 1.0, 0.0)
                else:
                    later = jnp.where(sub > sp - v * SUBLANES, 1.0, 0.0)
                    cnt[v] = cnt[v] + jnp.where(r > grp[v], 1.0, jnp.where(r == grp[v], later, 0.0))
        chosen = jnp.concatenate([jnp.where(c < float(SEL_TOPK), 0.0, NEG) for c in cnt], axis=0)
        block_bias = tile_heads(jnp.where(blk > cur, NEG, chosen))
        row0 = pl.multiple_of((g2 * NSA_GPS + gg) * GATE_STRIDE, GATE_STRIDE)
        gt = gt_scr[pl.ds(row0, 2 * SUBLANES), :]
        return qt, o_c, block_bias, gt

    def window(gg, qt):
        tiles = []
        for d in range(1 - WIN_TILES, 1):
            td = i + d
            tc = jnp.maximum(td, 0)
            k0 = pl.multiple_of(tc * Q_BLOCK, Q_BLOCK)
            before_start = jnp.where(td >= 0, 0.0, NEG) if d < 0 else 0.0
            tri = tri_diag if d == 0 else (tri_old if d == 1 - WIN_TILES else None)
            tiles.append((tc, k0, before_start, tri))

        zero_row = jnp.zeros((1, NSA_LANES), F32)
        no_bias = jnp.zeros((n_sel, NSA_LANES), BF16)
        scores = []
        for tc, k0, before_start, tri in tiles:
            q_aug = query_side(gg, qt, no_bias, zero_row + before_start)
            s = jnp.dot(key_side(kw_ref, k0, Q_BLOCK), q_aug, preferred_element_type=F32)
            scores.append(s if tri is None else s + tri)
        m_w = jnp.max(functools.reduce(jnp.maximum, [_fold_rows(s, jnp.maximum) for s in scores]),
                      axis=0, keepdims=True)
        n_pad = jnp.maximum(WINDOW - 1 - tq, 0).astype(F32)
        m_w = jnp.where(n_pad > 0.0, jnp.maximum(m_w, 0.0), m_w)
        acc = jnp.zeros((V_AUG, NSA_LANES), F32)
        for (tc, _, _, _), s in zip(tiles, scores):
            acc = acc + jnp.dot(vt_scr[1, gg, tc], jnp.exp2(s - m_w).astype(BF16), preferred_element_type=F32)
        l_w = acc[HEAD_DIM:HEAD_DIM + 1] + n_pad * jnp.exp2(-m_w)
        return acc[:HEAD_DIM] * (1.0 / jnp.maximum(l_w, 1e-30))

    def selected(gg, k, qt, block_bias):
        rel = i - k * q_per_chunk
        diag = jnp.concatenate([jnp.where(rel == u, tri_diag, 0.0) for u in range(q_per_chunk)], axis=0)
        q_aug = query_side(gg, qt, block_bias, jnp.zeros((1, NSA_LANES), F32))
        tiles_per_chunk = SEL_CHUNK // Q_BLOCK
        m8 = jnp.full((SUBLANES, NSA_LANES), NEG, F32)
        for c in range(k + 1):
            s = jnp.dot(key_side(ks_ref, c * SEL_CHUNK, SEL_CHUNK), q_aug,
                        preferred_element_type=F32)
            if c == k:
                s = s + diag
            s_scr[gg, c * SEL_CHUNK:(c + 1) * SEL_CHUNK, :] = s
            m8 = jnp.maximum(m8, _fold_rows(s, jnp.maximum))
        m_s = jnp.max(m8, axis=0, keepdims=True)
        acc = jnp.zeros((V_AUG, NSA_LANES), F32)
        for c in range(k + 1):
            p = jnp.exp2(s_scr[gg, c * SEL_CHUNK:(c + 1) * SEL_CHUNK, :] - m_s)
            vt = jnp.concatenate([vt_scr[0, gg, c * tiles_per_chunk + u] for u in range(tiles_per_chunk)], axis=1)
            acc = acc + jnp.dot(vt, p.astype(BF16), preferred_element_type=F32)
        l_s = acc[HEAD_DIM:HEAD_DIM + 1]
        return acc[:HEAD_DIM] * (1.0 / jnp.maximum(l_s, 1e-30))

    prepared = [prepare(gg) for gg in range(NSA_GPS)]

    def finish(k):
        outs = []
        for gg, (qt, o_c, block_bias, gt) in enumerate(prepared):
            o_w = window(gg, qt)
            o_s = selected(gg, k, qt, block_bias)
            for h in range(NSA_HPG):
                sl = slice(h * Q_BLOCK, (h + 1) * Q_BLOCK)
                row = NSA_BRANCHES * h
                outs.append(gt[row:row + 1, :] * o_c[:, sl] + gt[row + 1:row + 2, :] * o_s[:, sl]
                            + gt[row + 2:row + 3, :] * o_w[:, sl])
        o_ref[...] = jnp.transpose(jnp.concatenate(outs, axis=0))

    for k in range(n_sel * SEL_BLOCK // SEL_CHUNK):
        pl.when(i // q_per_chunk == k)(functools.partial(finish, k))


def _nsa(z, cos_q, sin_q, feat, kvc, c2st, *, batch, seq):
    nq = seq // Q_BLOCK
    n_sel = seq // SEL_BLOCK
    gw = NSA_GPS * NSA_HPG * HEAD_DIM
    gates_w = NSA_GROUPS * GATE_STRIDE
    pair_w = NSA_GPS * HEAD_DIM
    assert pair_w == LANES
    kv_spec = lambda off: pl.BlockSpec((seq, pair_w), lambda b, g, i: (b, off // pair_w + g))
    rope_spec = pl.BlockSpec((ROPE_HALF, Q_BLOCK), lambda b, g, i: (0, i))
    return pl.pallas_call(
        functools.partial(_nsa_kernel, n_sel=n_sel),
        grid=(batch, NSA_GROUPS // NSA_GPS, nq),
        in_specs=[
            pl.BlockSpec((Q_BLOCK, gw), lambda b, g, i: (b * nq + i, OFF_Q // gw + g)),
            pl.BlockSpec((Q_BLOCK, gates_w), lambda b, g, i: (b * nq + i, OFF_GATE // gates_w)),
            rope_spec, rope_spec,
            kv_spec(OFF_KS), kv_spec(OFF_VS), kv_spec(OFF_KW), kv_spec(OFF_VW),
            pl.BlockSpec((seq, LANES), lambda b, g, i: (0, 0)),
            pl.BlockSpec((NSA_GPS * LANES, 2 * HEAD_DIM), lambda b, g, i: (b * (NSA_GROUPS // NSA_GPS) + g, 0)),
            pl.BlockSpec((n_sel, LANES), lambda b, g, i: (0, 0)),
        ],
        out_specs=pl.BlockSpec((Q_BLOCK, gw), lambda b, g, i: (b * nq + i, g)),
        out_shape=jax.ShapeDtypeStruct((batch * seq, NSA_HEADS * HEAD_DIM), F32),
        scratch_shapes=[pltpu.VMEM((gates_w, Q_BLOCK), F32), pltpu.VMEM((NSA_GPS, seq, NSA_LANES), F32),
                        pltpu.VMEM((2, NSA_GPS, nq, V_AUG, Q_BLOCK), BF16)],
        compiler_params=pltpu.CompilerParams(
            dimension_semantics=("arbitrary", "arbitrary", "arbitrary"), vmem_limit_bytes=VMEM_LIMIT),
        name="nsa",
    )(z, z, cos_q, sin_q, z, z, z, z, feat, kvc, c2st)


def _rglru_kernel(x_ref, gr_ref, cw_ref, cb_ref, wa_ref, ba_ref, wi_ref, bi_ref, lam_ref, y_ref,
                  xprev_scr, h_scr):
    ts, c = x_ref.shape

    @pl.when(pl.program_id(1) == 0)
    def _():
        xprev_scr[...] = jnp.zeros(xprev_scr.shape, F32)
        h_scr[...] = jnp.zeros(h_scr.shape, F32)

    x = x_ref[...].astype(F32)
    xx = jnp.concatenate([xprev_scr[...], x], axis=0)
    xprev_scr[...] = x[ts - SUBLANES:, :]
    xc = cb_ref[...] + cw_ref[CONV_WIDTH - 1:CONV_WIDTH, :] * x
    for k in range(CONV_WIDTH - 1):
        off = SUBLANES - (CONV_WIDTH - 1) + k
        xc = xc + cw_ref[k:k + 1, :] * xx[off:off + ts, :]

    ra, ri = [], []
    for mblk in range(c // LANES):
        xb = xc[:, mblk * LANES:(mblk + 1) * LANES].astype(BF16)
        ra.append(jnp.dot(xb, wa_ref[mblk], preferred_element_type=F32))
        ri.append(jnp.dot(xb, wi_ref[mblk], preferred_element_type=F32))
    r = _sigmoid(jnp.concatenate(ra, axis=1) + ba_ref[...])
    gi = _sigmoid(jnp.concatenate(ri, axis=1) + bi_ref[...])

    nl = -lam_ref[...]
    softplus = jnp.maximum(nl, 0.0) + jnp.log1p(jnp.exp(-jnp.abs(nl)))
    log_a = (-RGLRU_C) * r * softplus
    a = jnp.exp(log_a)
    th = jnp.tanh(log_a)
    b = jnp.sqrt(-2.0 * th / (1.0 - th)) * (gi * xc)

    ridx = lax.broadcasted_iota(jnp.int32, (ts, c), 0)
    d = 1
    while d < ts:
        keep = ridx >= d
        a_sh = jnp.where(keep, pltpu.roll(a, d, 0), 1.0)
        b_sh = jnp.where(keep, pltpu.roll(b, d, 0), 0.0)
        b = a * b_sh + b
        a = a * a_sh
        d *= 2
    h = b + a * h_scr[...]
    h_scr[...] = h[ts - 1:ts, :]
    y_ref[...] = _gelu_tanh(gr_ref[...].astype(F32)) * h


def _rglru(z, cw, cb, wa2, ba, wi2, bi, lam, *, batch, seq, ts):
    c = D_MODEL
    nt = seq // ts
    vec = pl.BlockSpec((1, c), lambda b, t: (0, 0))
    return pl.pallas_call(
        _rglru_kernel,
        grid=(batch, nt),
        in_specs=[
            pl.BlockSpec((ts, c), lambda b, t: (b * nt + t, OFF_XRNN // c)),
            pl.BlockSpec((ts, c), lambda b, t: (b * nt + t, OFF_GRNN // c)),
            pl.BlockSpec((CONV_WIDTH, c), lambda b, t: (0, 0)),
            vec,
            pl.BlockSpec((c // LANES, LANES, LANES), lambda b, t: (0, 0, 0)),
            vec,
            pl.BlockSpec((c // LANES, LANES, LANES), lambda b, t: (0, 0, 0)),
            vec, vec,
        ],
        out_specs=pl.BlockSpec((ts, c), lambda b, t: (b * nt + t, 0)),
        out_shape=jax.ShapeDtypeStruct((batch * seq, c), F32),
        scratch_shapes=[pltpu.VMEM((SUBLANES, c), F32), pltpu.VMEM((1, c), F32)],
        compiler_params=pltpu.CompilerParams(
            dimension_semantics=("parallel", "arbitrary"), vmem_limit_bytes=VMEM_LIMIT),
        name="rglru",
    )(z, z, cw, cb, wa2, ba, wi2, bi, lam)


def _merge_kernel(x_ref, m0_ref, m1_ref, m2_ref, ynsa_ref, yrnn_ref, qx_ref, kvm_ref, wxo_ref, wo_ref, h_ref):
    qx = qx_ref[...].astype(F32) * SCALE
    kvm = kvm_ref[...]
    yx = None
    for h in range(XATTN_HEADS):
        lo = h * HEAD_DIM
        qh = qx[:, lo:lo + HEAD_DIM].astype(BF16)
        kh = kvm[:, lo:lo + HEAD_DIM].astype(BF16)
        vh = kvm[:, XATTN_WIDTH + lo:XATTN_WIDTH + lo + HEAD_DIM].astype(BF16)
        s = lax.dot_general(qh, kh, NT_DIMS, preferred_element_type=F32)
        e = jnp.exp(s - jnp.max(s, axis=-1, keepdims=True))
        p = e / jnp.sum(e, axis=-1, keepdims=True)
        oh = jnp.dot(p.astype(BF16), vh, preferred_element_type=F32)
        part = jnp.dot(oh.astype(BF16), wxo_ref[lo:lo + HEAD_DIM, :], preferred_element_type=F32)
        yx = part if yx is None else yx + part
    y = (_sigmoid(m0_ref[...].astype(F32)) * ynsa_ref[...] + _sigmoid(m1_ref[...].astype(F32)) * yrnn_ref[...]
         + _sigmoid(m2_ref[...].astype(F32)) * yx)
    h_ref[...] = x_ref[...] + jnp.dot(y.astype(BF16), wo_ref[...], preferred_element_type=F32)


def _merge(x, z, ynsa, yrnn, kvm, wxo, wo, *, seq, mem_len, tq):
    t, d = x.shape
    nt = seq // tq
    row = lambda cb: pl.BlockSpec((tq, d), lambda i, cb=cb: (i, cb))
    return pl.pallas_call(
        _merge_kernel,
        grid=(t // tq,),
        in_specs=[
            row(0),
            row(OFF_MERGE // d), row(OFF_MERGE // d + 1), row(OFF_MERGE // d + 2),
            row(0), row(0),
            pl.BlockSpec((tq, XATTN_WIDTH), lambda i: (i, OFF_QX // XATTN_WIDTH)),
            pl.BlockSpec((mem_len, 2 * XATTN_WIDTH), lambda i: (i // nt, 0)),
            pl.BlockSpec((XATTN_WIDTH, d), lambda i: (0, 0)),
            pl.BlockSpec((d, d), lambda i: (0, 0)),
        ],
        out_specs=row(0),
        out_shape=jax.ShapeDtypeStruct((t, d), F32),
        compiler_params=pltpu.CompilerParams(
            dimension_semantics=("parallel",), vmem_limit_bytes=VMEM_LIMIT),
        name="merge",
    )(x, z, z, z, ynsa, yrnn, z, kvm, wxo, wo)


def _mlp_kernel(h_ref, g_ref, wup_ref, wdn_ref, gf_ref, o_ref, v_scr, acc_scr):
    f = pl.program_id(1)

    @pl.when(f == 0)
    def _():
        h = h_ref[...]
        ms = jnp.mean(h * h, axis=-1, keepdims=True)
        v_scr[...] = (h * lax.rsqrt(ms + RMS_EPS) * g_ref[...]).astype(BF16)
        acc_scr[...] = jnp.zeros(acc_scr.shape, F32)

    up = jnp.dot(v_scr[...], wup_ref[...], preferred_element_type=F32)
    act = jnp.square(jnp.maximum(up, 0.0)).astype(BF16)
    acc_scr[...] += jnp.dot(act, wdn_ref[...], preferred_element_type=F32)

    @pl.when(f == pl.num_programs(1) - 1)
    def _():
        h2 = h_ref[...] + acc_scr[...]
        ms = jnp.mean(h2 * h2, axis=-1, keepdims=True)
        o_ref[...] = h2 * lax.rsqrt(ms + RMS_EPS) * gf_ref[...]


def _mlp(h, g, wup, wdn, gf, *, tm, tf):
    t, d = h.shape
    dff = wup.shape[1]
    return pl.pallas_call(
        _mlp_kernel,
        grid=(t // tm, dff // tf),
        in_specs=[
            pl.BlockSpec((tm, d), lambda i, f: (i, 0)),
            pl.BlockSpec((1, d), lambda i, f: (0, 0)),
            pl.BlockSpec((d, tf), lambda i, f: (0, f)),
            pl.BlockSpec((tf, d), lambda i, f: (f, 0)),
            pl.BlockSpec((1, d), lambda i, f: (0, 0)),
        ],
        out_specs=pl.BlockSpec((tm, d), lambda i, f: (i, 0)),
        out_shape=jax.ShapeDtypeStruct((t, d), F32),
        scratch_shapes=[pltpu.VMEM((tm, d), BF16), pltpu.VMEM((tm, d), F32)],
        compiler_params=pltpu.CompilerParams(
            dimension_semantics=("parallel", "arbitrary"), vmem_limit_bytes=VMEM_LIMIT),
        name="mlp",
    )(h, g, wup, wdn, gf)


def _rope_angles(pos):
    inv = 1.0 / (ROPE_THETA ** (jnp.arange(0, ROPE_DIM, 2, dtype=F32) / ROPE_DIM))
    ang = pos.astype(F32)[:, None] * inv[None, :]
    return jnp.cos(ang), jnp.sin(ang)


def _rope_tables(pos, width, rope_width=None):
    rope_width = width if rope_width is None else rope_width
    cos, sin = _rope_angles(pos)
    n = pos.shape[0]
    pad = HEAD_DIM - ROPE_DIM
    cos_h = jnp.concatenate([cos, cos, jnp.ones((n, pad), F32)], axis=1)
    sin_h = jnp.concatenate([-sin, sin, jnp.zeros((n, pad), F32)], axis=1)
    reps = rope_width // HEAD_DIM
    rest = width - rope_width
    cos_t = jnp.concatenate([jnp.tile(cos_h, (1, reps)), jnp.ones((n, rest), F32)], axis=1)
    sin_t = jnp.concatenate([jnp.tile(sin_h, (1, reps)), jnp.zeros((n, rest), F32)], axis=1)
    return cos_t, sin_t


def _pack_w_in(w_in):
    d = w_in.shape[0]
    w = w_in.astype(BF16)
    n_qkv = NSA_HEADS * HEAD_DIM + 6 * NSA_KV
    n_gates = NSA_HEADS * NSA_BRANCHES
    rnn_lo = n_qkv + n_gates
    qx_lo = rnn_lo + 2 * D_MODEL
    mg_lo = qx_lo + XATTN_WIDTH
    per_group = n_gates // NSA_GROUPS
    gates = w[:, n_qkv:rnn_lo].reshape(d, NSA_GROUPS, per_group)
    gates = jnp.pad(gates, ((0, 0), (0, 0), (0, GATE_STRIDE - per_group))).reshape(d, NSA_GROUPS * GATE_STRIDE)
    return jnp.concatenate([w[:, :n_qkv], w[:, qx_lo:mg_lo], gates, w[:, rnn_lo:qx_lo], w[:, mg_lo:]], axis=1)


def _block_diag_pairs(w):
    nb, k, _ = w.shape
    w = w.reshape(nb // 2, 2, k, k)
    zero = jnp.zeros((nb // 2, k, k), w.dtype)
    top = jnp.concatenate([w[:, 0], zero], axis=2)
    bot = jnp.concatenate([zero, w[:, 1]], axis=2)
    return jnp.concatenate([top, bot], axis=1).astype(BF16)


def _cmp_to_sel_t(n_cmp, n_cmp_pad, n_sel):
    c0 = np.arange(n_cmp_pad)[None, :] * CMP_STRIDE
    s0 = np.arange(n_sel)[:, None] * SEL_BLOCK
    ov = np.clip(np.minimum(c0 + CMP_LEN, s0 + SEL_BLOCK) - np.maximum(c0, s0), 0, None)
    ov = np.where(np.arange(n_cmp_pad)[None, :] < n_cmp, ov, 0)
    return (ov / CMP_LEN).astype(np.float32)


def _key_features(seq, n_sel):
    col = np.arange(LANES)[None, :]
    onehot = (np.arange(seq)[:, None] // SEL_BLOCK == col) & (col < n_sel)
    return jnp.asarray(onehot | (col == n_sel), dtype=BF16)


def kernel(x, mem, g_mix, w_in, cmp_pos_k, cmp_pos_v, w_cmp_k1, w_cmp_k2, w_cmp_v1, w_cmp_v2, conv_w, conv_b, w_rg_a, b_rg_a, w_rg_i, b_rg_i, rg_lambda, g_mem, w_mem_kv, w_xo, w_o, g_mlp, w_up, w_down, g_final):
    batch, seq, d = x.shape
    mem_len = mem.shape[1]
    t = batch * seq
    depth = g_mix.shape[0]
    n_sel = seq // SEL_BLOCK
    n_cmp_pad = seq // CMP_STRIDE
    assert n_cmp_pad == LANES and n_sel <= K_AUG - HEAD_DIM and d == D_MODEL

    pos = jnp.arange(seq)
    cos_k, sin_k = _rope_tables(pos, NSA_KV)
    cos_q, sin_q = (a.T for a in _rope_angles(pos))
    no_rope = jnp.zeros((mem_len, LANES), F32)
    cmp_pos = jnp.arange(n_cmp_pad) * CMP_STRIDE + (CMP_LEN - 1)
    cmp_tm = 8 * n_cmp_pad
    cos_c, sin_c = _rope_tables(cmp_pos, 2 * HEAD_DIM, HEAD_DIM)
    cos_c = jnp.tile(cos_c, (cmp_tm // n_cmp_pad, 1))
    sin_c = jnp.tile(sin_c, (cmp_tm // n_cmp_pad, 1))
    c2st = jnp.asarray(_cmp_to_sel_t((seq - CMP_LEN) // CMP_STRIDE + 1, n_cmp_pad, n_sel))
    feat = _key_features(seq, n_sel)

    h = x.reshape(t, d)
    for l in range(depth):
        z = _norm_proj(h, g_mix[l][None, :], _pack_w_in(w_in[l]), cos_k, sin_k, tm=1024, tn=INPROJ_TN,
                       rope_tiles=ROPE_TILES, out_dtype=BF16, name="inproj")

        def groups(off):
            a = z[:, off:off + NSA_KV].reshape(batch, seq, NSA_GROUPS, HEAD_DIM)
            return a.transpose(0, 2, 1, 3)

        half = CMP_STRIDE * HEAD_DIM
        rk = groups(OFF_KC).reshape(batch * NSA_GROUPS * n_cmp_pad, half)
        rv = groups(OFF_VC).reshape(batch * NSA_GROUPS * n_cmp_pad, half)
        pk = cmp_pos_k[l].reshape(2, half)
        pv = cmp_pos_v[l].reshape(2, half)
        w2 = jnp.zeros((2 * CMP_HIDDEN, 2 * HEAD_DIM), F32)
        w2 = w2.at[:CMP_HIDDEN, :HEAD_DIM].set(w_cmp_k2[l]).at[CMP_HIDDEN:, HEAD_DIM:].set(w_cmp_v2[l])
        kvc = _compress(rk, rv, pk[0:1], pk[1:2], pv[0:1], pv[1:2],
                        w_cmp_k1[l][:half].astype(BF16), w_cmp_k1[l][half:].astype(BF16),
                        w_cmp_v1[l][:half].astype(BF16), w_cmp_v1[l][half:].astype(BF16),
                        w2.astype(BF16), cos_c, sin_c, tm=cmp_tm)
        y_nsa = _nsa(z, cos_q, sin_q, feat, kvc, c2st, batch=batch, seq=seq)

        y_rnn = _rglru(z, conv_w[l], conv_b[l][None, :], _block_diag_pairs(w_rg_a[l]), b_rg_a[l][None, :],
                       _block_diag_pairs(w_rg_i[l]), b_rg_i[l][None, :], rg_lambda[l][None, :],
                       batch=batch, seq=seq, ts=256)

        kvm = _norm_proj(mem.reshape(batch * mem_len, d), g_mem[l][None, :], w_mem_kv[l].astype(BF16),
                         no_rope, no_rope, tm=mem_len, tn=2 * XATTN_WIDTH, rope_tiles=(),
                         out_dtype=F32, name="memkv")

        h1 = _merge(h, z, y_nsa, y_rnn, kvm, w_xo[l].astype(BF16), w_o[l].astype(BF16),
                    seq=seq, mem_len=mem_len, tq=256)
        last = l == depth - 1
        gf = g_final if last else jnp.ones_like(g_final)
        h = _mlp(h1, g_mlp[l][None, :], w_up[l].astype(BF16), w_down[l].astype(BF16), gf[None, :],
                 tm=1024, tf=1024)
        assert last, "final norm is fused into the last layer's MLP kernel"
    return h.reshape(batch, seq, d)
```

```python
import functools
import math

import numpy as np
import jax
import jax.numpy as jnp
from jax import lax
from jax.experimental import pallas as pl
from jax.experimental.pallas import tpu as pltpu

F32 = jnp.float32
BF16 = jnp.bfloat16

D_MODEL = 1024
HEAD_DIM = 64
NSA_HEADS = 16
NSA_GROUPS = 4
NSA_HPG = NSA_HEADS // NSA_GROUPS
NSA_BRANCHES = 3
CMP_LEN = 32
CMP_STRIDE = 16
CMP_HIDDEN = 256
SEL_BLOCK = 64
SEL_TOPK = 8
WINDOW = 512
Q_BLOCK = 128
ROPE_THETA = 500000.0
ROPE_DIM = HEAD_DIM // 4
ROPE_HALF = ROPE_DIM // 2
RNN_BLOCKS = 16
RNN_BLOCK_DIM = D_MODEL // RNN_BLOCKS
CONV_WIDTH = 4
RGLRU_C = 8.0
XATTN_HEADS = 4
XATTN_WIDTH = XATTN_HEADS * HEAD_DIM
D_FF = 4 * D_MODEL
RMS_EPS = 1e-6
SCALE = HEAD_DIM ** -0.5
LOG2E = math.log2(math.e)
NEG = -1e30

LANES = 128
SUBLANES = 8
BF16_ROWS = 16
VMEM_LIMIT = 48 * 1024 * 1024

NSA_KV = NSA_GROUPS * HEAD_DIM
OFF_Q = 0
OFF_KC = 1024
OFF_VC = 1280
OFF_KS = 1536
OFF_VS = 1792
OFF_KW = 2048
OFF_VW = 2304
OFF_QX = 2560
OFF_GATE = 2816
GATE_STRIDE = 64
OFF_XRNN = 3072
OFF_GRNN = 4096
OFF_MERGE = 5120
Z_WIDTH = 8192
INPROJ_TN = 512
ROPE_TILES = (OFF_KS // INPROJ_TN, OFF_KW // INPROJ_TN)

NSA_PAIRS = 2
NSA_GPS = 2 * NSA_PAIRS
NSA_LANES = NSA_HPG * Q_BLOCK
SEL_CHUNK = 4 * Q_BLOCK
WIN_TILES = WINDOW // Q_BLOCK + 1
K_AUG = 2 * HEAD_DIM
V_AUG = HEAD_DIM + BF16_ROWS

NT_DIMS = (((1,), (1,)), ((), ()))


def _sigmoid(x):
    return 1.0 / (1.0 + jnp.exp(-x))


def _gelu_tanh(x):
    return 0.5 * x * (1.0 + jnp.tanh(0.7978845608028654 * (x + 0.044715 * (x * x * x))))


def _rope_swap(z):
    n = z.shape[-1]
    lane = lax.broadcasted_iota(jnp.int32, z.shape, z.ndim - 1)
    first_half = (lane & (HEAD_DIM - 1)) < ROPE_HALF
    return jnp.where(first_half, pltpu.roll(z, n - ROPE_HALF, z.ndim - 1), pltpu.roll(z, ROPE_HALF, z.ndim - 1))


def _round_robin(gens):
    results = [None] * len(gens)
    alive = list(range(len(gens)))
    while alive:
        for idx in list(alive):
            try:
                next(gens[idx])
            except StopIteration as stop:
                results[idx] = stop.value
                alive.remove(idx)
    return results


def _fold_rows(x, op):
    parts = [x[r * SUBLANES:(r + 1) * SUBLANES] for r in range(x.shape[0] // SUBLANES)]
    return functools.reduce(op, parts)


def _norm_proj_kernel(x_ref, g_ref, w_ref, cos_ref, sin_ref, z_ref, u_scr, *, rope_tiles):
    j = pl.program_id(1)

    @pl.when(j == 0)
    def _():
        x = x_ref[...]
        ms = jnp.mean(x * x, axis=-1, keepdims=True)
        u_scr[...] = (x * lax.rsqrt(ms + RMS_EPS) * g_ref[...]).astype(BF16)

    z = jnp.dot(u_scr[...], w_ref[...], preferred_element_type=F32)
    if not rope_tiles:
        z_ref[...] = z.astype(z_ref.dtype)
        return
    is_rope = functools.reduce(jnp.logical_or, [j == t for t in rope_tiles])
    rw = cos_ref.shape[1]

    @pl.when(is_rope)
    def _():
        zr = z[:, :rw]
        zr = zr * cos_ref[...] + _rope_swap(zr) * sin_ref[...]
        z_ref[...] = jnp.concatenate([zr, z[:, rw:]], axis=1).astype(z_ref.dtype)

    @pl.when(jnp.logical_not(is_rope))
    def _():
        z_ref[...] = z.astype(z_ref.dtype)


def _norm_proj(x, g, w, cos_t, sin_t, *, tm, tn, rope_tiles, out_dtype, name):
    t, d = x.shape
    n = w.shape[1]
    s_tiles = cos_t.shape[0] // tm
    rw = cos_t.shape[1]
    table_spec = pl.BlockSpec((tm, rw), lambda i, j: (i % s_tiles, 0))
    return pl.pallas_call(
        functools.partial(_norm_proj_kernel, rope_tiles=tuple(rope_tiles)),
        grid=(t // tm, n // tn),
        in_specs=[
            pl.BlockSpec((tm, d), lambda i, j: (i, 0)),
            pl.BlockSpec((1, d), lambda i, j: (0, 0)),
            pl.BlockSpec((d, tn), lambda i, j: (0, j)),
            table_spec, table_spec,
        ],
        out_specs=pl.BlockSpec((tm, tn), lambda i, j: (i, j)),
        out_shape=jax.ShapeDtypeStruct((t, n), out_dtype),
        scratch_shapes=[pltpu.VMEM((tm, d), BF16)],
        compiler_params=pltpu.CompilerParams(
            dimension_semantics=("parallel", "arbitrary"), vmem_limit_bytes=VMEM_LIMIT),
        name=name,
    )(x, g, w, cos_t, sin_t)


def _compress_kernel(rk_ref, rv_ref, pka_ref, pkb_ref, pva_ref, pvb_ref,
                     wk1a_ref, wk1b_ref, wv1a_ref, wv1b_ref, w2_ref, cos_ref, sin_ref, o_ref):
    m = rk_ref.shape[0]

    def hidden(r_ref, pa_ref, pb_ref, w1a_ref, w1b_ref):
        r = r_ref[...].astype(F32)
        pa = jnp.dot((r + pa_ref[...]).astype(BF16), w1a_ref[...], preferred_element_type=F32)
        pb = jnp.dot((r + pb_ref[...]).astype(BF16), w1b_ref[...], preferred_element_type=F32)
        return _gelu_tanh(pa + pltpu.roll(pb, m - 1, 0)).astype(BF16)

    hk = hidden(rk_ref, pka_ref, pkb_ref, wk1a_ref, wk1b_ref)
    hv = hidden(rv_ref, pva_ref, pvb_ref, wv1a_ref, wv1b_ref)
    hcat = jnp.concatenate([hk, hv], axis=1)
    kv = jnp.dot(hcat, w2_ref[...], preferred_element_type=F32)
    o_ref[...] = kv * cos_ref[...] + _rope_swap(kv) * sin_ref[...]


def _compress(rk, rv, pka, pkb, pva, pvb, wk1a, wk1b, wv1a, wv1b, w2, cos_t, sin_t, *, tm):
    m, k = rk.shape
    full = lambda shape: pl.BlockSpec(shape, lambda i: (0, 0))
    return pl.pallas_call(
        _compress_kernel,
        grid=(m // tm,),
        in_specs=[
            pl.BlockSpec((tm, k), lambda i: (i, 0)),
            pl.BlockSpec((tm, k), lambda i: (i, 0)),
            full((1, k)), full((1, k)), full((1, k)), full((1, k)),
            full((k, CMP_HIDDEN)), full((k, CMP_HIDDEN)), full((k, CMP_HIDDEN)), full((k, CMP_HIDDEN)),
            full((2 * CMP_HIDDEN, 2 * HEAD_DIM)),
            full((tm, 2 * HEAD_DIM)), full((tm, 2 * HEAD_DIM)),
        ],
        out_specs=pl.BlockSpec((tm, 2 * HEAD_DIM), lambda i: (i, 0)),
        out_shape=jax.ShapeDtypeStruct((m, 2 * HEAD_DIM), F32),
        compiler_params=pltpu.CompilerParams(
            dimension_semantics=("parallel",), vmem_limit_bytes=VMEM_LIMIT),
        name="compress",
    )(rk, rv, pka, pkb, pva, pvb, wk1a, wk1b, wv1a, wv1b, w2, cos_t, sin_t)


def _nsa_kernel(q_ref, g_ref, cos_ref, sin_ref, *rest, n_sel):
    kv_refs, rest = rest[:4 * NSA_PAIRS], rest[4 * NSA_PAIRS:]
    feat_ref, kvc_ref, c2st_ref, o_ref, gt_scr, s_scr, vt_scr = rest
    ks_refs, vs_refs, kw_refs, vw_refs = (kv_refs[n::4] for n in range(4))
    g2 = pl.program_id(1)
    i = pl.program_id(2)
    q0 = i * Q_BLOCK
    seq = feat_ref.shape[0]

    @pl.when(i == 0)
    def _():
        ones_rows = jnp.where(lax.broadcasted_iota(jnp.int32, (V_AUG - HEAD_DIM, Q_BLOCK), 0) == 0,
                              1.0, 0.0).astype(BF16)

        def fill(t, carry):
            r0 = pl.multiple_of(t * Q_BLOCK, Q_BLOCK)
            for kind, refs in enumerate((vs_refs, vw_refs)):
                for pair, ref in enumerate(refs):
                    vt = jnp.transpose(ref[pl.ds(r0, Q_BLOCK), :].astype(F32)).astype(BF16)
                    for half in range(2):
                        gg = 2 * pair + half
                        vt_scr[kind, gg, t, 0:HEAD_DIM, :] = vt[half * HEAD_DIM:(half + 1) * HEAD_DIM]
                        vt_scr[kind, gg, t, HEAD_DIM:V_AUG, :] = ones_rows
            return carry

        lax.fori_loop(0, seq // Q_BLOCK, fill, 0)

    def query_side(gg, qt, bias_rows, win_row):
        zeros_q = jnp.zeros((HEAD_DIM, NSA_LANES), BF16)
        q_rows = [qt, zeros_q] if gg % 2 == 0 else [zeros_q, qt]
        rid = lax.broadcasted_iota(jnp.int32, (BF16_ROWS, NSA_LANES), 0)
        win_rows = jnp.where(rid == 0, win_row, 0.0).astype(BF16)
        tail = jnp.zeros((2 * LANES - 2 * HEAD_DIM - n_sel - BF16_ROWS, NSA_LANES), BF16)
        return jnp.concatenate(q_rows + [bias_rows.astype(BF16), win_rows, tail], axis=0)

    def key_side(k_ref, row0, rows):
        return jnp.concatenate([k_ref[pl.ds(row0, rows), :], feat_ref[pl.ds(row0, rows), :]], axis=1)
    blocks_per_chunk = SEL_CHUNK // SEL_BLOCK
    q_per_chunk = SEL_CHUNK // Q_BLOCK

    def tile_heads(x):
        return jnp.concatenate([x] * NSA_HPG, axis=1)

    tq = q0 + (lax.broadcasted_iota(jnp.int32, (1, NSA_LANES), 1) & (Q_BLOCK - 1))
    c_idx = lax.broadcasted_iota(jnp.int32, (Q_BLOCK, 1), 0)
    r_idx = lax.broadcasted_iota(jnp.int32, (1, Q_BLOCK), 1)
    tri_diag = tile_heads(jnp.where(c_idx <= r_idx, 0.0, NEG))
    tri_old = tile_heads(jnp.where(c_idx > r_idx, 0.0, NEG))
    cos_q = tile_heads(cos_ref[...])
    sin_q = tile_heads(sin_ref[...])
    blk = lax.broadcasted_iota(jnp.int32, (n_sel, Q_BLOCK), 0)
    cur = (q0 + lax.broadcasted_iota(jnp.int32, (n_sel, Q_BLOCK), 1)) // SEL_BLOCK
    sub = lax.broadcasted_iota(jnp.int32, (SUBLANES, Q_BLOCK), 0)

    gt_scr[...] = jnp.transpose(_sigmoid(g_ref[...].astype(F32)))

    def prepare(gg):
        qx = jnp.transpose(q_ref[:, gg * NSA_HPG * HEAD_DIM:(gg + 1) * NSA_HPG * HEAD_DIM].astype(F32)
                           * (SCALE * LOG2E))
        qf = jnp.concatenate([qx[h * HEAD_DIM:(h + 1) * HEAD_DIM, :] for h in range(NSA_HPG)], axis=1)
        x1, x2 = qf[0:ROPE_HALF], qf[ROPE_HALF:ROPE_DIM]
        qt = jnp.concatenate([x1 * cos_q - x2 * sin_q, x2 * cos_q + x1 * sin_q, qf[ROPE_DIM:]],
                             axis=0).astype(BF16)
        qts[gg] = qt
        yield

        kvc = kvc_ref[gg * LANES:(gg + 1) * LANES, :]
        kc = kvc[:, :HEAD_DIM].astype(BF16)
        vct = jnp.transpose(kvc)[HEAD_DIM:, :].astype(BF16)
        sc = jnp.dot(kc, qt, preferred_element_type=F32)
        yield
        n_idx = lax.broadcasted_iota(jnp.int32, (LANES, 1), 0)
        sc = jnp.where((n_idx * CMP_STRIDE + (CMP_LEN - 1)) <= tq, sc, NEG)
        m_c = jnp.max(sc, axis=0, keepdims=True)
        e_c = jnp.exp2(sc - m_c)
        inv_c = 1.0 / jnp.maximum(jnp.sum(e_c, axis=0, keepdims=True), 1e-30)
        p_c = e_c * jnp.where(m_c > 0.5 * NEG, inv_c, 0.0)
        o_c = jnp.dot(vct, p_c.astype(BF16), preferred_element_type=F32)
        yield

        p_sum = p_c[:, 0:Q_BLOCK]
        for h in range(1, NSA_HPG):
            p_sum = p_sum + p_c[:, h * Q_BLOCK:(h + 1) * Q_BLOCK]
        imp = jnp.dot(c2st_ref[...], p_sum, preferred_element_type=F32,
                      precision=lax.Precision.HIGHEST)
        yield
        forced = (blk == 0) | (blk == cur) | (blk == cur - 1)
        val = jnp.where(forced, jnp.inf, jnp.where(blk > cur, -jnp.inf, imp))
        n_grp = n_sel // SUBLANES
        grp = [val[v * SUBLANES:(v + 1) * SUBLANES] for v in range(n_grp)]
        cnt = [jnp.zeros((SUBLANES, Q_BLOCK), F32) for _ in range(n_grp)]
        for sp in range(n_sel):
            r = val[sp:sp + 1, :]
            for v in range(n_grp):
                if v * SUBLANES > sp:
                    cnt[v] = cnt[v] + jnp.where(r >= grp[v], 1.0, 0.0)
                elif v * SUBLANES + SUBLANES - 1 <= sp:
                    cnt[v] = cnt[v] + jnp.where(r > grp[v], 1.0, 0.0)
                else:
                    later = jnp.where(sub > sp - v * SUBLANES, 1.0, 0.0)
                    cnt[v] = cnt[v] + jnp.where(r > grp[v], 1.0, jnp.where(r == grp[v], later, 0.0))
            if sp % SUBLANES == SUBLANES - 1:
                yield
        chosen = jnp.concatenate([jnp.where(c < float(SEL_TOPK), 0.0, NEG) for c in cnt], axis=0)
        block_bias = tile_heads(jnp.where(blk > cur, NEG, chosen))
        row0 = pl.multiple_of((g2 * NSA_GPS + gg) * GATE_STRIDE, GATE_STRIDE)
        gt = gt_scr[pl.ds(row0, 2 * SUBLANES), :]
        return qt, o_c, block_bias, gt

    def window(gg):
        qt = qts[gg]
        tiles = []
        for d in range(1 - WIN_TILES, 1):
            td = i + d
            tc = jnp.maximum(td, 0)
            k0 = pl.multiple_of(tc * Q_BLOCK, Q_BLOCK)
            before_start = jnp.where(td >= 0, 0.0, NEG) if d < 0 else 0.0
            tri = tri_diag if d == 0 else (tri_old if d == 1 - WIN_TILES else None)
            tiles.append((tc, k0, before_start, tri))

        zero_row = jnp.zeros((1, NSA_LANES), F32)
        no_bias = jnp.zeros((n_sel, NSA_LANES), BF16)
        scores = []
        for tc, k0, before_start, tri in tiles:
            q_aug = query_side(gg, qt, no_bias, zero_row + before_start)
            s = jnp.dot(key_side(kw_refs[gg // 2], k0, Q_BLOCK), q_aug, preferred_element_type=F32)
            scores.append(s if tri is None else s + tri)
            yield
        m_w = jnp.max(functools.reduce(jnp.maximum, [_fold_rows(s, jnp.maximum) for s in scores]),
                      axis=0, keepdims=True)
        n_pad = jnp.maximum(WINDOW - 1 - tq, 0).astype(F32)
        m_w = jnp.where(n_pad > 0.0, jnp.maximum(m_w, 0.0), m_w)
        yield
        acc = jnp.zeros((V_AUG, NSA_LANES), F32)
        for (tc, _, _, _), s in zip(tiles, scores):
            acc = acc + jnp.dot(vt_scr[1, gg, tc], jnp.exp2(s - m_w).astype(BF16), preferred_element_type=F32)
            yield
        l_w = acc[HEAD_DIM:HEAD_DIM + 1] + n_pad * jnp.exp2(-m_w)
        return acc[:HEAD_DIM] * (1.0 / jnp.maximum(l_w, 1e-30))

    def chunk_scores(gg, c, diag_rel):
        zeros_q = jnp.zeros((HEAD_DIM, NSA_LANES), BF16)
        q_rows = jnp.concatenate([qts[gg], zeros_q] if gg % 2 == 0 else [zeros_q, qts[gg]], axis=0)
        s = jnp.dot(ks_refs[gg // 2][c * SEL_CHUNK:(c + 1) * SEL_CHUNK, :], q_rows,
                    preferred_element_type=F32)
        if diag_rel is not None:
            s = s + jnp.concatenate([jnp.where(diag_rel == u, tri_diag, 0.0) for u in range(q_per_chunk)], axis=0)
        s_scr[gg, c * SEL_CHUNK:(c + 1) * SEL_CHUNK, :] = s
        return [_fold_rows(s[b * SEL_BLOCK:(b + 1) * SEL_BLOCK], jnp.maximum) for b in range(blocks_per_chunk)]

    def first_chunk(gg):
        yield
        return chunk_scores(gg, 0, i)

    def selected(gg, k, block_bias, bmax0):
        bmax = list(bmax0)
        for c in range(1, k + 1):
            bmax += chunk_scores(gg, c, i - k * q_per_chunk if c == k else None)
            yield
        m8 = functools.reduce(jnp.maximum, [bm + block_bias[n:n + 1, :] for n, bm in enumerate(bmax)])
        m_s = jnp.max(m8, axis=0, keepdims=True)
        shift = block_bias - m_s
        yield
        tiles_per_chunk = SEL_CHUNK // Q_BLOCK
        acc = jnp.zeros((V_AUG, NSA_LANES), F32)
        for c in range(k + 1):
            ps = []
            for b in range(blocks_per_chunk):
                n = c * blocks_per_chunk + b
                ps.append(jnp.exp2(s_scr[gg, n * SEL_BLOCK:(n + 1) * SEL_BLOCK, :] + shift[n:n + 1, :]).astype(BF16))
            vt = jnp.concatenate([vt_scr[0, gg, c * tiles_per_chunk + u] for u in range(tiles_per_chunk)], axis=1)
            acc = acc + jnp.dot(vt, jnp.concatenate(ps, axis=0), preferred_element_type=F32)
            yield
        l_s = acc[HEAD_DIM:HEAD_DIM + 1]
        return acc[:HEAD_DIM] * (1.0 / jnp.maximum(l_s, 1e-30))

    qts = [None] * NSA_GPS
    groups = range(NSA_GPS)
    shared = _round_robin([prepare(gg) for gg in groups] + [window(gg) for gg in groups]
                          + [first_chunk(gg) for gg in groups])
    prepared, windows, bmax0 = (shared[n * NSA_GPS:(n + 1) * NSA_GPS] for n in range(3))

    def finish(k):
        branches = []
        for gg, (qt, o_c, block_bias, gt) in enumerate(prepared):
            branches.append(selected(gg, k, block_bias, bmax0[gg]))
        results = _round_robin(branches)
        outs = []
        for gg, (qt, o_c, block_bias, gt) in enumerate(prepared):
            o_w, o_s = windows[gg], results[gg]
            for h in range(NSA_HPG):
                sl = slice(h * Q_BLOCK, (h + 1) * Q_BLOCK)
                row = NSA_BRANCHES * h
                outs.append(gt[row:row + 1, :] * o_c[:, sl] + gt[row + 1:row + 2, :] * o_s[:, sl]
                            + gt[row + 2:row + 3, :] * o_w[:, sl])
        o_ref[...] = jnp.transpose(jnp.concatenate(outs, axis=0))

    for k in range(n_sel * SEL_BLOCK // SEL_CHUNK):
        pl.when(i // q_per_chunk == k)(functools.partial(finish, k))


def _nsa(z, cos_q, sin_q, feat, kvc, c2st, *, batch, seq):
    nq = seq // Q_BLOCK
    n_sel = seq // SEL_BLOCK
    gw = NSA_GPS * NSA_HPG * HEAD_DIM
    gates_w = NSA_GROUPS * GATE_STRIDE
    pair_w = 2 * HEAD_DIM
    assert pair_w == LANES
    kv_spec = lambda off, pair: pl.BlockSpec(
        (seq, pair_w), lambda b, g, i: (b, off // pair_w + g * NSA_PAIRS + pair))
    kv_specs = [kv_spec(off, pair) for pair in range(NSA_PAIRS) for off in (OFF_KS, OFF_VS, OFF_KW, OFF_VW)]
    rope_spec = pl.BlockSpec((ROPE_HALF, Q_BLOCK), lambda b, g, i: (0, i))
    return pl.pallas_call(
        functools.partial(_nsa_kernel, n_sel=n_sel),
        grid=(batch, NSA_GROUPS // NSA_GPS, nq),
        in_specs=[
            pl.BlockSpec((Q_BLOCK, gw), lambda b, g, i: (b * nq + i, OFF_Q // gw + g)),
            pl.BlockSpec((Q_BLOCK, gates_w), lambda b, g, i: (b * nq + i, OFF_GATE // gates_w)),
            rope_spec, rope_spec,
            *kv_specs,
            pl.BlockSpec((seq, LANES), lambda b, g, i: (0, 0)),
            pl.BlockSpec((NSA_GPS * LANES, 2 * HEAD_DIM), lambda b, g, i: (b * (NSA_GROUPS // NSA_GPS) + g, 0)),
            pl.BlockSpec((n_sel, LANES), lambda b, g, i: (0, 0)),
        ],
        out_specs=pl.BlockSpec((Q_BLOCK, gw), lambda b, g, i: (b * nq + i, g)),
        out_shape=jax.ShapeDtypeStruct((batch * seq, NSA_HEADS * HEAD_DIM), F32),
        scratch_shapes=[pltpu.VMEM((gates_w, Q_BLOCK), F32), pltpu.VMEM((NSA_GPS, seq, NSA_LANES), F32),
                        pltpu.VMEM((2, NSA_GPS, nq, V_AUG, Q_BLOCK), BF16)],
        compiler_params=pltpu.CompilerParams(
            dimension_semantics=("arbitrary", "arbitrary", "arbitrary"), vmem_limit_bytes=VMEM_LIMIT),
        name="nsa",
    )(z, z, cos_q, sin_q, *([z] * len(kv_specs)), feat, kvc, c2st)


def _rglru_kernel(x_ref, gr_ref, cw_ref, cb_ref, wa_ref, ba_ref, wi_ref, bi_ref, lam_ref, y_ref,
                  xprev_scr, h_scr):
    ts, c = x_ref.shape

    @pl.when(pl.program_id(1) == 0)
    def _():
        xprev_scr[...] = jnp.zeros(xprev_scr.shape, F32)
        h_scr[...] = jnp.zeros(h_scr.shape, F32)

    x = x_ref[...].astype(F32)
    xx = jnp.concatenate([xprev_scr[...], x], axis=0)
    xprev_scr[...] = x[ts - SUBLANES:, :]
    xc = cb_ref[...] + cw_ref[CONV_WIDTH - 1:CONV_WIDTH, :] * x
    for k in range(CONV_WIDTH - 1):
        off = SUBLANES - (CONV_WIDTH - 1) + k
        xc = xc + cw_ref[k:k + 1, :] * xx[off:off + ts, :]

    ra, ri = [], []
    for mblk in range(c // LANES):
        xb = xc[:, mblk * LANES:(mblk + 1) * LANES].astype(BF16)
        ra.append(jnp.dot(xb, wa_ref[mblk], preferred_element_type=F32))
        ri.append(jnp.dot(xb, wi_ref[mblk], preferred_element_type=F32))
    r = _sigmoid(jnp.concatenate(ra, axis=1) + ba_ref[...])
    gi = _sigmoid(jnp.concatenate(ri, axis=1) + bi_ref[...])

    nl = -lam_ref[...]
    softplus = jnp.maximum(nl, 0.0) + jnp.log1p(jnp.exp(-jnp.abs(nl)))
    log_a = (-RGLRU_C) * r * softplus
    a = jnp.exp(log_a)
    th = jnp.tanh(log_a)
    b = jnp.sqrt(-2.0 * th / (1.0 - th)) * (gi * xc)

    ridx = lax.broadcasted_iota(jnp.int32, (ts, c), 0)
    d = 1
    while d < ts:
        keep = ridx >= d
        a_sh = jnp.where(keep, pltpu.roll(a, d, 0), 1.0)
        b_sh = jnp.where(keep, pltpu.roll(b, d, 0), 0.0)
        b = a * b_sh + b
        a = a * a_sh
        d *= 2
    h = b + a * h_scr[...]
    h_scr[...] = h[ts - 1:ts, :]
    y_ref[...] = _gelu_tanh(gr_ref[...].astype(F32)) * h


def _rglru(z, cw, cb, wa2, ba, wi2, bi, lam, *, batch, seq, ts):
    c = D_MODEL
    nt = seq // ts
    vec = pl.BlockSpec((1, c), lambda b, t: (0, 0))
    return pl.pallas_call(
        _rglru_kernel,
        grid=(batch, nt),
        in_specs=[
            pl.BlockSpec((ts, c), lambda b, t: (b * nt + t, OFF_XRNN // c)),
            pl.BlockSpec((ts, c), lambda b, t: (b * nt + t, OFF_GRNN // c)),
            pl.BlockSpec((CONV_WIDTH, c), lambda b, t: (0, 0)),
            vec,
            pl.BlockSpec((c // LANES, LANES, LANES), lambda b, t: (0, 0, 0)),
            vec,
            pl.BlockSpec((c // LANES, LANES, LANES), lambda b, t: (0, 0, 0)),
            vec, vec,
        ],
        out_specs=pl.BlockSpec((ts, c), lambda b, t: (b * nt + t, 0)),
        out_shape=jax.ShapeDtypeStruct((batch * seq, c), F32),
        scratch_shapes=[pltpu.VMEM((SUBLANES, c), F32), pltpu.VMEM((1, c), F32)],
        compiler_params=pltpu.CompilerParams(
            dimension_semantics=("parallel", "arbitrary"), vmem_limit_bytes=VMEM_LIMIT),
        name="rglru",
    )(z, z, cw, cb, wa2, ba, wi2, bi, lam)


def _merge_kernel(x_ref, m0_ref, m1_ref, m2_ref, ynsa_ref, yrnn_ref, qx_ref, kvm_ref, wxo_ref, wo_ref, h_ref):
    qx = qx_ref[...].astype(F32) * SCALE
    kvm = kvm_ref[...]
    yx = None
    for h in range(XATTN_HEADS):
        lo = h * HEAD_DIM
        qh = qx[:, lo:lo + HEAD_DIM].astype(BF16)
        kh = kvm[:, lo:lo + HEAD_DIM].astype(BF16)
        vh = kvm[:, XATTN_WIDTH + lo:XATTN_WIDTH + lo + HEAD_DIM].astype(BF16)
        s = lax.dot_general(qh, kh, NT_DIMS, preferred_element_type=F32)
        e = jnp.exp(s - jnp.max(s, axis=-1, keepdims=True))
        p = e / jnp.sum(e, axis=-1, keepdims=True)
        oh = jnp.dot(p.astype(BF16), vh, preferred_element_type=F32)
        part = jnp.dot(oh.astype(BF16), wxo_ref[lo:lo + HEAD_DIM, :], preferred_element_type=F32)
        yx = part if yx is None else yx + part
    y = (_sigmoid(m0_ref[...].astype(F32)) * ynsa_ref[...] + _sigmoid(m1_ref[...].astype(F32)) * yrnn_ref[...]
         + _sigmoid(m2_ref[...].astype(F32)) * yx)
    h_ref[...] = x_ref[...] + jnp.dot(y.astype(BF16), wo_ref[...], preferred_element_type=F32)


def _merge(x, z, ynsa, yrnn, kvm, wxo, wo, *, seq, mem_len, tq):
    t, d = x.shape
    nt = seq // tq
    row = lambda cb: pl.BlockSpec((tq, d), lambda i, cb=cb: (i, cb))
    return pl.pallas_call(
        _merge_kernel,
        grid=(t // tq,),
        in_specs=[
            row(0),
            row(OFF_MERGE // d), row(OFF_MERGE // d + 1), row(OFF_MERGE // d + 2),
            row(0), row(0),
            pl.BlockSpec((tq, XATTN_WIDTH), lambda i: (i, OFF_QX // XATTN_WIDTH)),
            pl.BlockSpec((mem_len, 2 * XATTN_WIDTH), lambda i: (i // nt, 0)),
            pl.BlockSpec((XATTN_WIDTH, d), lambda i: (0, 0)),
            pl.BlockSpec((d, d), lambda i: (0, 0)),
        ],
        out_specs=row(0),
        out_shape=jax.ShapeDtypeStruct((t, d), F32),
        compiler_params=pltpu.CompilerParams(
            dimension_semantics=("parallel",), vmem_limit_bytes=VMEM_LIMIT),
        name="merge",
    )(x, z, z, z, ynsa, yrnn, z, kvm, wxo, wo)


def _mlp_kernel(h_ref, g_ref, wup_ref, wdn_ref, gf_ref, o_ref, v_scr, acc_scr):
    f = pl.program_id(1)

    @pl.when(f == 0)
    def _():
        h = h_ref[...]
        ms = jnp.mean(h * h, axis=-1, keepdims=True)
        v_scr[...] = (h * lax.rsqrt(ms + RMS_EPS) * g_ref[...]).astype(BF16)
        acc_scr[...] = jnp.zeros(acc_scr.shape, F32)

    up = jnp.dot(v_scr[...], wup_ref[...], preferred_element_type=F32)
    act = jnp.square(jnp.maximum(up, 0.0)).astype(BF16)
    acc_scr[...] += jnp.dot(act, wdn_ref[...], preferred_element_type=F32)

    @pl.when(f == pl.num_programs(1) - 1)
    def _():
        h2 = h_ref[...] + acc_scr[...]
        ms = jnp.mean(h2 * h2, axis=-1, keepdims=True)
        o_ref[...] = h2 * lax.rsqrt(ms + RMS_EPS) * gf_ref[...]


def _mlp(h, g, wup, wdn, gf, *, tm, tf):
    t, d = h.shape
    dff = wup.shape[1]
    return pl.pallas_call(
        _mlp_kernel,
        grid=(t // tm, dff // tf),
        in_specs=[
            pl.BlockSpec((tm, d), lambda i, f: (i, 0)),
            pl.BlockSpec((1, d), lambda i, f: (0, 0)),
            pl.BlockSpec((d, tf), lambda i, f: (0, f)),
            pl.BlockSpec((tf, d), lambda i, f: (f, 0)),
            pl.BlockSpec((1, d), lambda i, f: (0, 0)),
        ],
        out_specs=pl.BlockSpec((tm, d), lambda i, f: (i, 0)),
        out_shape=jax.ShapeDtypeStruct((t, d), F32),
        scratch_shapes=[pltpu.VMEM((tm, d), BF16), pltpu.VMEM((tm, d), F32)],
        compiler_params=pltpu.CompilerParams(
            dimension_semantics=("parallel", "arbitrary"), vmem_limit_bytes=VMEM_LIMIT),
        name="mlp",
    )(h, g, wup, wdn, gf)


def _rope_angles(pos):
    inv = 1.0 / (ROPE_THETA ** (jnp.arange(0, ROPE_DIM, 2, dtype=F32) / ROPE_DIM))
    ang = pos.astype(F32)[:, None] * inv[None, :]
    return jnp.cos(ang), jnp.sin(ang)


def _rope_tables(pos, width, rope_width=None):
    rope_width = width if rope_width is None else rope_width
    cos, sin = _rope_angles(pos)
    n = pos.shape[0]
    pad = HEAD_DIM - ROPE_DIM
    cos_h = jnp.concatenate([cos, cos, jnp.ones((n, pad), F32)], axis=1)
    sin_h = jnp.concatenate([-sin, sin, jnp.zeros((n, pad), F32)], axis=1)
    reps = rope_width // HEAD_DIM
    rest = width - rope_width
    cos_t = jnp.concatenate([jnp.tile(cos_h, (1, reps)), jnp.ones((n, rest), F32)], axis=1)
    sin_t = jnp.concatenate([jnp.tile(sin_h, (1, reps)), jnp.zeros((n, rest), F32)], axis=1)
    return cos_t, sin_t


def _pack_w_in(w_in):
    d = w_in.shape[0]
    w = w_in.astype(BF16)
    n_qkv = NSA_HEADS * HEAD_DIM + 6 * NSA_KV
    n_gates = NSA_HEADS * NSA_BRANCHES
    rnn_lo = n_qkv + n_gates
    qx_lo = rnn_lo + 2 * D_MODEL
    mg_lo = qx_lo + XATTN_WIDTH
    per_group = n_gates // NSA_GROUPS
    gates = w[:, n_qkv:rnn_lo].reshape(d, NSA_GROUPS, per_group)
    gates = jnp.pad(gates, ((0, 0), (0, 0), (0, GATE_STRIDE - per_group))).reshape(d, NSA_GROUPS * GATE_STRIDE)
    return jnp.concatenate([w[:, :n_qkv], w[:, qx_lo:mg_lo], gates, w[:, rnn_lo:qx_lo], w[:, mg_lo:]], axis=1)


def _block_diag_pairs(w):
    nb, k, _ = w.shape
    w = w.reshape(nb // 2, 2, k, k)
    zero = jnp.zeros((nb // 2, k, k), w.dtype)
    top = jnp.concatenate([w[:, 0], zero], axis=2)
    bot = jnp.concatenate([zero, w[:, 1]], axis=2)
    return jnp.concatenate([top, bot], axis=1).astype(BF16)


def _cmp_to_sel_t(n_cmp, n_cmp_pad, n_sel):
    c0 = np.arange(n_cmp_pad)[None, :] * CMP_STRIDE
    s0 = np.arange(n_sel)[:, None] * SEL_BLOCK
    ov = np.clip(np.minimum(c0 + CMP_LEN, s0 + SEL_BLOCK) - np.maximum(c0, s0), 0, None)
    ov = np.where(np.arange(n_cmp_pad)[None, :] < n_cmp, ov, 0)
    return (ov / CMP_LEN).astype(np.float32)


def _key_features(seq, n_sel):
    col = np.arange(LANES)[None, :]
    onehot = (np.arange(seq)[:, None] // SEL_BLOCK == col) & (col < n_sel)
    return jnp.asarray(onehot | (col == n_sel), dtype=BF16)


def kernel(x, mem, g_mix, w_in, cmp_pos_k, cmp_pos_v, w_cmp_k1, w_cmp_k2, w_cmp_v1, w_cmp_v2, conv_w, conv_b, w_rg_a, b_rg_a, w_rg_i, b_rg_i, rg_lambda, g_mem, w_mem_kv, w_xo, w_o, g_mlp, w_up, w_down, g_final):
    batch, seq, d = x.shape
    mem_len = mem.shape[1]
    t = batch * seq
    depth = g_mix.shape[0]
    n_sel = seq // SEL_BLOCK
    n_cmp_pad = seq // CMP_STRIDE
    assert n_cmp_pad == LANES and n_sel <= K_AUG - HEAD_DIM and d == D_MODEL

    pos = jnp.arange(seq)
    cos_k, sin_k = _rope_tables(pos, NSA_KV)
    cos_q, sin_q = (a.T for a in _rope_angles(pos))
    no_rope = jnp.zeros((mem_len, LANES), F32)
    cmp_pos = jnp.arange(n_cmp_pad) * CMP_STRIDE + (CMP_LEN - 1)
    cmp_tm = 8 * n_cmp_pad
    cos_c, sin_c = _rope_tables(cmp_pos, 2 * HEAD_DIM, HEAD_DIM)
    cos_c = jnp.tile(cos_c, (cmp_tm // n_cmp_pad, 1))
    sin_c = jnp.tile(sin_c, (cmp_tm // n_cmp_pad, 1))
    c2st = jnp.asarray(_cmp_to_sel_t((seq - CMP_LEN) // CMP_STRIDE + 1, n_cmp_pad, n_sel))
    feat = _key_features(seq, n_sel)

    h = x.reshape(t, d)
    for l in range(depth):
        z = _norm_proj(h, g_mix[l][None, :], _pack_w_in(w_in[l]), cos_k, sin_k, tm=1024, tn=INPROJ_TN,
                       rope_tiles=ROPE_TILES, out_dtype=BF16, name="inproj")

        def groups(off):
            a = z[:, off:off + NSA_KV].reshape(batch, seq, NSA_GROUPS, HEAD_DIM)
            return a.transpose(0, 2, 1, 3)

        half = CMP_STRIDE * HEAD_DIM
        rk = groups(OFF_KC).reshape(batch * NSA_GROUPS * n_cmp_pad, half)
        rv = groups(OFF_VC).reshape(batch * NSA_GROUPS * n_cmp_pad, half)
        pk = cmp_pos_k[l].reshape(2, half)
        pv = cmp_pos_v[l].reshape(2, half)
        w2 = jnp.zeros((2 * CMP_HIDDEN, 2 * HEAD_DIM), F32)
        w2 = w2.at[:CMP_HIDDEN, :HEAD_DIM].set(w_cmp_k2[l]).at[CMP_HIDDEN:, HEAD_DIM:].set(w_cmp_v2[l])
        kvc = _compress(rk, rv, pk[0:1], pk[1:2], pv[0:1], pv[1:2],
                        w_cmp_k1[l][:half].astype(BF16), w_cmp_k1[l][half:].astype(BF16),
                        w_cmp_v1[l][:half].astype(BF16), w_cmp_v1[l][half:].astype(BF16),
                        w2.astype(BF16), cos_c, sin_c, tm=cmp_tm)
        y_nsa = _nsa(z, cos_q, sin_q, feat, kvc, c2st, batch=batch, seq=seq)

        y_rnn = _rglru(z, conv_w[l], conv_b[l][None, :], _block_diag_pairs(w_rg_a[l]), b_rg_a[l][None, :],
                       _block_diag_pairs(w_rg_i[l]), b_rg_i[l][None, :], rg_lambda[l][None, :],
                       batch=batch, seq=seq, ts=256)

        kvm = _norm_proj(mem.reshape(batch * mem_len, d), g_mem[l][None, :], w_mem_kv[l].astype(BF16),
                         no_rope, no_rope, tm=mem_len, tn=2 * XATTN_WIDTH, rope_tiles=(),
                         out_dtype=F32, name="memkv")

        h1 = _merge(h, z, y_nsa, y_rnn, kvm, w_xo[l].astype(BF16), w_o[l].astype(BF16),
                    seq=seq, mem_len=mem_len, tq=256)
        last = l == depth - 1
        gf = g_final if last else jnp.ones_like(g_final)
        h = _mlp(h1, g_mlp[l][None, :], w_up[l].astype(BF16), w_down[l].astype(BF16), gf[None, :],
                 tm=1024, tf=1024)
        assert last, "final norm is fused into the last layer's MLP kernel"
    return h.reshape(batch, seq, d)
```

```python
import functools
import math

import numpy as np
import jax
import jax.numpy as jnp
from jax import lax
from jax.experimental import pallas as pl
from jax.experimental.pallas import tpu as pltpu

F32 = jnp.float32
BF16 = jnp.bfloat16

D_MODEL = 1024
HEAD_DIM = 64
NSA_HEADS = 16
NSA_GROUPS = 4
NSA_HPG = NSA_HEADS // NSA_GROUPS
NSA_BRANCHES = 3
CMP_LEN = 32
CMP_STRIDE = 16
CMP_HIDDEN = 256
SEL_BLOCK = 64
SEL_TOPK = 8
WINDOW = 512
Q_BLOCK = 128
ROPE_THETA = 500000.0
ROPE_DIM = HEAD_DIM // 4
ROPE_HALF = ROPE_DIM // 2
RNN_BLOCKS = 16
RNN_BLOCK_DIM = D_MODEL // RNN_BLOCKS
CONV_WIDTH = 4
RGLRU_C = 8.0
XATTN_HEADS = 4
XATTN_WIDTH = XATTN_HEADS * HEAD_DIM
D_FF = 4 * D_MODEL
RMS_EPS = 1e-6
SCALE = HEAD_DIM ** -0.5
LOG2E = math.log2(math.e)
NEG = -1e30

LANES = 128
SUBLANES = 8
BF16_ROWS = 16
VMEM_LIMIT = 48 * 1024 * 1024

NSA_KV = NSA_GROUPS * HEAD_DIM
OFF_Q = 0
OFF_KC = 1024
OFF_VC = 1280
OFF_KS = 1536
OFF_VS = 1792
OFF_KW = 2048
OFF_VW = 2304
OFF_QX = 2560
OFF_GATE = 2816
GATE_STRIDE = 64
OFF_XRNN = 3072
OFF_GRNN = 4096
OFF_MERGE = 5120
Z_WIDTH = 8192
INPROJ_TN = 512
ROPE_TILES = (OFF_KS // INPROJ_TN, OFF_KW // INPROJ_TN)

NSA_PAIRS = 2
NSA_GPS = 2 * NSA_PAIRS
NSA_LANES = NSA_HPG * Q_BLOCK
SEL_CHUNK = 4 * Q_BLOCK
WIN_TILES = WINDOW // Q_BLOCK + 1
K_AUG = 2 * HEAD_DIM
V_AUG = HEAD_DIM + BF16_ROWS

NT_DIMS = (((1,), (1,)), ((), ()))


def _sigmoid(x):
    return 1.0 / (1.0 + jnp.exp(-x))


def _sigmoid_tanh(x):
    return 0.5 * jnp.tanh(0.5 * x) + 0.5


def _gelu_tanh(x):
    return 0.5 * x * (1.0 + jnp.tanh(0.7978845608028654 * (x + 0.044715 * (x * x * x))))


def _rope_swap(z):
    n = z.shape[-1]
    lane = lax.broadcasted_iota(jnp.int32, z.shape, z.ndim - 1)
    first_half = (lane & (HEAD_DIM - 1)) < ROPE_HALF
    return jnp.where(first_half, pltpu.roll(z, n - ROPE_HALF, z.ndim - 1), pltpu.roll(z, ROPE_HALF, z.ndim - 1))


def _round_robin(gens):
    results = [None] * len(gens)
    alive = list(range(len(gens)))
    while alive:
        for idx in list(alive):
            try:
                next(gens[idx])
            except StopIteration as stop:
                results[idx] = stop.value
                alive.remove(idx)
    return results


def _fold_rows(x, op):
    parts = [x[r * SUBLANES:(r + 1) * SUBLANES] for r in range(x.shape[0] // SUBLANES)]
    return functools.reduce(op, parts)


def _norm_proj_kernel(x_ref, g_ref, w_ref, cos_ref, sin_ref, z_ref, u_scr, *, rope_tiles):
    j = pl.program_id(1)

    @pl.when(j == 0)
    def _():
        x = x_ref[...]
        ms = jnp.mean(x * x, axis=-1, keepdims=True)
        u_scr[...] = (x * lax.rsqrt(ms + RMS_EPS) * g_ref[...]).astype(BF16)

    z = jnp.dot(u_scr[...], w_ref[...], preferred_element_type=F32)
    z_ref[...] = z.astype(z_ref.dtype)
    if not rope_tiles:
        return
    is_rope = functools.reduce(jnp.logical_or, [j == t for t in rope_tiles])
    rw = cos_ref.shape[1]

    @pl.when(is_rope)
    def _():
        zr = z[:, :rw]
        z_ref[:, :rw] = (zr * cos_ref[...] + _rope_swap(zr) * sin_ref[...]).astype(z_ref.dtype)


def _norm_proj(x, g, w, cos_t, sin_t, *, tm, tn, rope_tiles, out_dtype, name):
    t, d = x.shape
    n = w.shape[1]
    s_tiles = cos_t.shape[0] // tm
    rw = cos_t.shape[1]
    table_spec = pl.BlockSpec((tm, rw), lambda i, j: (i % s_tiles, 0))
    return pl.pallas_call(
        functools.partial(_norm_proj_kernel, rope_tiles=tuple(rope_tiles)),
        grid=(t // tm, n // tn),
        in_specs=[
            pl.BlockSpec((tm, d), lambda i, j: (i, 0)),
            pl.BlockSpec((1, d), lambda i, j: (0, 0)),
            pl.BlockSpec((d, tn), lambda i, j: (0, j)),
            table_spec, table_spec,
        ],
        out_specs=pl.BlockSpec((tm, tn), lambda i, j: (i, j)),
        out_shape=jax.ShapeDtypeStruct((t, n), out_dtype),
        scratch_shapes=[pltpu.VMEM((tm, d), BF16)],
        compiler_params=pltpu.CompilerParams(
            dimension_semantics=("parallel", "arbitrary"), vmem_limit_bytes=VMEM_LIMIT),
        name=name,
    )(x, g, w, cos_t, sin_t)


def _compress_kernel(rk_ref, rv_ref, pka_ref, pkb_ref, pva_ref, pvb_ref,
                     wk1a_ref, wk1b_ref, wv1a_ref, wv1b_ref, w2_ref, cos_ref, sin_ref, o_ref):
    m = rk_ref.shape[0]

    def hidden(r_ref, pa_ref, pb_ref, w1a_ref, w1b_ref):
        r = r_ref[...].astype(F32)
        pa = jnp.dot((r + pa_ref[...]).astype(BF16), w1a_ref[...], preferred_element_type=F32)
        pb = jnp.dot((r + pb_ref[...]).astype(BF16), w1b_ref[...], preferred_element_type=F32)
        return _gelu_tanh(pa + pltpu.roll(pb, m - 1, 0)).astype(BF16)

    hk = hidden(rk_ref, pka_ref, pkb_ref, wk1a_ref, wk1b_ref)
    hv = hidden(rv_ref, pva_ref, pvb_ref, wv1a_ref, wv1b_ref)
    hcat = jnp.concatenate([hk, hv], axis=1)
    kv = jnp.dot(hcat, w2_ref[...], preferred_element_type=F32)
    o_ref[...] = kv * cos_ref[...] + _rope_swap(kv) * sin_ref[...]


def _compress(rk, rv, pka, pkb, pva, pvb, wk1a, wk1b, wv1a, wv1b, w2, cos_t, sin_t, *, tm):
    m, k = rk.shape
    full = lambda shape: pl.BlockSpec(shape, lambda i: (0, 0))
    return pl.pallas_call(
        _compress_kernel,
        grid=(m // tm,),
        in_specs=[
            pl.BlockSpec((tm, k), lambda i: (i, 0)),
            pl.BlockSpec((tm, k), lambda i: (i, 0)),
            full((1, k)), full((1, k)), full((1, k)), full((1, k)),
            full((k, CMP_HIDDEN)), full((k, CMP_HIDDEN)), full((k, CMP_HIDDEN)), full((k, CMP_HIDDEN)),
            full((2 * CMP_HIDDEN, 2 * HEAD_DIM)),
            full((tm, 2 * HEAD_DIM)), full((tm, 2 * HEAD_DIM)),
        ],
        out_specs=pl.BlockSpec((tm, 2 * HEAD_DIM), lambda i: (i, 0)),
        out_shape=jax.ShapeDtypeStruct((m, 2 * HEAD_DIM), F32),
        compiler_params=pltpu.CompilerParams(
            dimension_semantics=("parallel",), vmem_limit_bytes=VMEM_LIMIT),
        name="compress",
    )(rk, rv, pka, pkb, pva, pvb, wk1a, wk1b, wv1a, wv1b, w2, cos_t, sin_t)


def _nsa_kernel(q_ref, g_ref, cos_ref, sin_ref, *rest, n_sel):
    kv_refs, rest = rest[:4 * NSA_PAIRS], rest[4 * NSA_PAIRS:]
    feat_ref, kvc_ref, c2st_ref, o_ref, gt_scr, s_scr, vt_scr = rest
    ks_refs, vs_refs, kw_refs, vw_refs = (kv_refs[n::4] for n in range(4))
    g2 = pl.program_id(1)
    i = pl.program_id(2)
    q0 = i * Q_BLOCK
    seq = feat_ref.shape[0]

    @pl.when(i == 0)
    def _():
        ones_rows = jnp.where(lax.broadcasted_iota(jnp.int32, (V_AUG - HEAD_DIM, Q_BLOCK), 0) == 0,
                              1.0, 0.0).astype(BF16)

        def fill(t, carry):
            r0 = pl.multiple_of(t * Q_BLOCK, Q_BLOCK)
            for kind, refs in enumerate((vs_refs, vw_refs)):
                for pair, ref in enumerate(refs):
                    vt = jnp.transpose(ref[pl.ds(r0, Q_BLOCK), :].astype(F32)).astype(BF16)
                    for half in range(2):
                        gg = 2 * pair + half
                        vt_scr[kind, gg, t, 0:HEAD_DIM, :] = vt[half * HEAD_DIM:(half + 1) * HEAD_DIM]
                        vt_scr[kind, gg, t, HEAD_DIM:V_AUG, :] = ones_rows
            return carry

        lax.fori_loop(0, seq // Q_BLOCK, fill, 0)

    def query_side(gg, qt, bias_rows, win_row):
        zeros_q = jnp.zeros((HEAD_DIM, NSA_LANES), BF16)
        q_rows = [qt, zeros_q] if gg % 2 == 0 else [zeros_q, qt]
        rid = lax.broadcasted_iota(jnp.int32, (BF16_ROWS, NSA_LANES), 0)
        win_rows = jnp.where(rid == 0, win_row, 0.0).astype(BF16)
        tail = jnp.zeros((2 * LANES - 2 * HEAD_DIM - n_sel - BF16_ROWS, NSA_LANES), BF16)
        return jnp.concatenate(q_rows + [bias_rows.astype(BF16), win_rows, tail], axis=0)

    def key_side(k_ref, row0, rows):
        return jnp.concatenate([k_ref[pl.ds(row0, rows), :], feat_ref[pl.ds(row0, rows), :]], axis=1)
    blocks_per_chunk = SEL_CHUNK // SEL_BLOCK
    q_per_chunk = SEL_CHUNK // Q_BLOCK

    def tile_heads(x):
        return jnp.concatenate([x] * NSA_HPG, axis=1)

    tq = q0 + (lax.broadcasted_iota(jnp.int32, (1, NSA_LANES), 1) & (Q_BLOCK - 1))
    c_idx = lax.broadcasted_iota(jnp.int32, (Q_BLOCK, 1), 0)
    r_idx = lax.broadcasted_iota(jnp.int32, (1, Q_BLOCK), 1)
    tri_diag = tile_heads(jnp.where(c_idx <= r_idx, 0.0, NEG))
    tri_old = tile_heads(jnp.where(c_idx > r_idx, 0.0, NEG))
    cos_q = tile_heads(cos_ref[...])
    sin_q = tile_heads(sin_ref[...])
    blk = lax.broadcasted_iota(jnp.int32, (n_sel, Q_BLOCK), 0)
    cur = (q0 + lax.broadcasted_iota(jnp.int32, (n_sel, Q_BLOCK), 1)) // SEL_BLOCK
    sub = lax.broadcasted_iota(jnp.int32, (SUBLANES, Q_BLOCK), 0)

    gt_scr[...] = jnp.transpose(_sigmoid(g_ref[...].astype(F32)))

    def prepare(gg):
        qx = jnp.transpose(q_ref[:, gg * NSA_HPG * HEAD_DIM:(gg + 1) * NSA_HPG * HEAD_DIM].astype(F32)
                           * (SCALE * LOG2E))
        qf = jnp.concatenate([qx[h * HEAD_DIM:(h + 1) * HEAD_DIM, :] for h in range(NSA_HPG)], axis=1)
        x1, x2 = qf[0:ROPE_HALF], qf[ROPE_HALF:ROPE_DIM]
        qt = jnp.concatenate([x1 * cos_q - x2 * sin_q, x2 * cos_q + x1 * sin_q, qf[ROPE_DIM:]],
                             axis=0).astype(BF16)
        qts[gg] = qt
        yield

        kvc = kvc_ref[gg * LANES:(gg + 1) * LANES, :]
        kc = kvc[:, :HEAD_DIM].astype(BF16)
        vct = jnp.transpose(kvc)[HEAD_DIM:, :].astype(BF16)
        sc = jnp.dot(kc, qt, preferred_element_type=F32)
        yield
        n_idx = lax.broadcasted_iota(jnp.int32, (LANES, 1), 0)
        sc = jnp.where((n_idx * CMP_STRIDE + (CMP_LEN - 1)) <= tq, sc, NEG)
        m_c = jnp.max(sc, axis=0, keepdims=True)
        e_c = jnp.exp2(sc - m_c)
        inv_c = 1.0 / jnp.maximum(jnp.sum(e_c, axis=0, keepdims=True), 1e-30)
        p_c = e_c * jnp.where(m_c > 0.5 * NEG, inv_c, 0.0)
        o_c = jnp.dot(vct, p_c.astype(BF16), preferred_element_type=F32)
        yield

        p_sum = p_c[:, 0:Q_BLOCK]
        for h in range(1, NSA_HPG):
            p_sum = p_sum + p_c[:, h * Q_BLOCK:(h + 1) * Q_BLOCK]
        imp = jnp.dot(c2st_ref[...], p_sum, preferred_element_type=F32,
                      precision=lax.Precision.HIGHEST)
        yield
        forced = (blk == 0) | (blk == cur) | (blk == cur - 1)
        val = jnp.where(forced, jnp.inf, jnp.where(blk > cur, -jnp.inf, imp))
        n_grp = n_sel // SUBLANES
        grp = [val[v * SUBLANES:(v + 1) * SUBLANES] for v in range(n_grp)]
        cnt = [jnp.zeros((SUBLANES, Q_BLOCK), F32) for _ in range(n_grp)]
        for sp in range(n_sel):
            r = val[sp:sp + 1, :]
            for v in range(n_grp):
                if v * SUBLANES > sp:
                    cnt[v] = cnt[v] + jnp.where(r >= grp[v], 1.0, 0.0)
                elif v * SUBLANES + SUBLANES - 1 <= sp:
                    cnt[v] = cnt[v] + jnp.where(r > grp[v], 1.0, 0.0)
                else:
                    later = jnp.where(sub > sp - v * SUBLANES, 1.0, 0.0)
                    cnt[v] = cnt[v] + jnp.where(r > grp[v], 1.0, jnp.where(r == grp[v], later, 0.0))
            if sp % SUBLANES == SUBLANES - 1:
                yield
        chosen = jnp.concatenate([jnp.where(c < float(SEL_TOPK), 0.0, NEG) for c in cnt], axis=0)
        block_bias = tile_heads(jnp.where(blk > cur, NEG, chosen))
        row0 = pl.multiple_of((g2 * NSA_GPS + gg) * GATE_STRIDE, GATE_STRIDE)
        gt = gt_scr[pl.ds(row0, 2 * SUBLANES), :]
        return qt, o_c, block_bias, gt

    def window(gg):
        qt = qts[gg]
        tiles = []
        for d in range(1 - WIN_TILES, 1):
            td = i + d
            tc = jnp.maximum(td, 0)
            k0 = pl.multiple_of(tc * Q_BLOCK, Q_BLOCK)
            before_start = jnp.where(td >= 0, 0.0, NEG) if d < 0 else 0.0
            tri = tri_diag if d == 0 else (tri_old if d == 1 - WIN_TILES else None)
            tiles.append((tc, k0, before_start, tri))

        zero_row = jnp.zeros((1, NSA_LANES), F32)
        no_bias = jnp.zeros((n_sel, NSA_LANES), BF16)
        scores = []
        for tc, k0, before_start, tri in tiles:
            q_aug = query_side(gg, qt, no_bias, zero_row + before_start)
            s = jnp.dot(key_side(kw_refs[gg // 2], k0, Q_BLOCK), q_aug, preferred_element_type=F32)
            scores.append(s if tri is None else s + tri)
            yield
        m_w = jnp.max(functools.reduce(jnp.maximum, [_fold_rows(s, jnp.maximum) for s in scores]),
                      axis=0, keepdims=True)
        n_pad = jnp.maximum(WINDOW - 1 - tq, 0).astype(F32)
        m_w = jnp.where(n_pad > 0.0, jnp.maximum(m_w, 0.0), m_w)
        yield
        acc = jnp.zeros((V_AUG, NSA_LANES), F32)
        for (tc, _, _, _), s in zip(tiles, scores):
            acc = acc + jnp.dot(vt_scr[1, gg, tc], jnp.exp2(s - m_w).astype(BF16), preferred_element_type=F32)
            yield
        l_w = acc[HEAD_DIM:HEAD_DIM + 1] + n_pad * jnp.exp2(-m_w)
        return acc[:HEAD_DIM] * (1.0 / jnp.maximum(l_w, 1e-30))

    def chunk_scores(gg, c, diag_rel):
        zeros_q = jnp.zeros((HEAD_DIM, NSA_LANES), BF16)
        q_rows = jnp.concatenate([qts[gg], zeros_q] if gg % 2 == 0 else [zeros_q, qts[gg]], axis=0)
        s = jnp.dot(ks_refs[gg // 2][c * SEL_CHUNK:(c + 1) * SEL_CHUNK, :], q_rows,
                    preferred_element_type=F32)
        if diag_rel is not None:
            s = s + jnp.concatenate([jnp.where(diag_rel == u, tri_diag, 0.0) for u in range(q_per_chunk)], axis=0)
        s_scr[gg, c * SEL_CHUNK:(c + 1) * SEL_CHUNK, :] = s
        return [_fold_rows(s[b * SEL_BLOCK:(b + 1) * SEL_BLOCK], jnp.maximum) for b in range(blocks_per_chunk)]

    def first_chunk(gg):
        yield
        return chunk_scores(gg, 0, i)

    def selected(gg, k, block_bias, bmax0):
        bmax = list(bmax0)
        for c in range(1, k + 1):
            bmax += chunk_scores(gg, c, i - k * q_per_chunk if c == k else None)
            yield
        m8 = functools.reduce(jnp.maximum, [bm + block_bias[n:n + 1, :] for n, bm in enumerate(bmax)])
        m_s = jnp.max(m8, axis=0, keepdims=True)
        shift = block_bias - m_s
        yield
        tiles_per_chunk = SEL_CHUNK // Q_BLOCK
        acc = jnp.zeros((V_AUG, NSA_LANES), F32)
        for c in range(k + 1):
            ps = []
            for b in range(blocks_per_chunk):
                n = c * blocks_per_chunk + b
                ps.append(jnp.exp2(s_scr[gg, n * SEL_BLOCK:(n + 1) * SEL_BLOCK, :] + shift[n:n + 1, :]).astype(BF16))
            vt = jnp.concatenate([vt_scr[0, gg, c * tiles_per_chunk + u] for u in range(tiles_per_chunk)], axis=1)
            acc = acc + jnp.dot(vt, jnp.concatenate(ps, axis=0), preferred_element_type=F32)
            yield
        l_s = acc[HEAD_DIM:HEAD_DIM + 1]
        return acc[:HEAD_DIM] * (1.0 / jnp.maximum(l_s, 1e-30))

    qts = [None] * NSA_GPS
    groups = range(NSA_GPS)
    shared = _round_robin([prepare(gg) for gg in groups] + [window(gg) for gg in groups]
                          + [first_chunk(gg) for gg in groups])
    prepared, windows, bmax0 = (shared[n * NSA_GPS:(n + 1) * NSA_GPS] for n in range(3))

    def finish(k):
        branches = []
        for gg, (qt, o_c, block_bias, gt) in enumerate(prepared):
            branches.append(selected(gg, k, block_bias, bmax0[gg]))
        results = _round_robin(branches)
        outs = []
        for gg, (qt, o_c, block_bias, gt) in enumerate(prepared):
            o_w, o_s = windows[gg], results[gg]
            for h in range(NSA_HPG):
                sl = slice(h * Q_BLOCK, (h + 1) * Q_BLOCK)
                row = NSA_BRANCHES * h
                outs.append(gt[row:row + 1, :] * o_c[:, sl] + gt[row + 1:row + 2, :] * o_s[:, sl]
                            + gt[row + 2:row + 3, :] * o_w[:, sl])
        o_ref[...] = jnp.transpose(jnp.concatenate(outs, axis=0))

    for k in range(n_sel * SEL_BLOCK // SEL_CHUNK):
        pl.when(i // q_per_chunk == k)(functools.partial(finish, k))


def _nsa(z, cos_q, sin_q, feat, kvc, c2st, *, batch, seq):
    nq = seq // Q_BLOCK
    n_sel = seq // SEL_BLOCK
    gw = NSA_GPS * NSA_HPG * HEAD_DIM
    gates_w = NSA_GROUPS * GATE_STRIDE
    pair_w = 2 * HEAD_DIM
    assert pair_w == LANES
    kv_spec = lambda off, pair: pl.BlockSpec(
        (seq, pair_w), lambda b, g, i: (b, off // pair_w + g * NSA_PAIRS + pair))
    kv_specs = [kv_spec(off, pair) for pair in range(NSA_PAIRS) for off in (OFF_KS, OFF_VS, OFF_KW, OFF_VW)]
    rope_spec = pl.BlockSpec((ROPE_HALF, Q_BLOCK), lambda b, g, i: (0, i))
    return pl.pallas_call(
        functools.partial(_nsa_kernel, n_sel=n_sel),
        grid=(batch, NSA_GROUPS // NSA_GPS, nq),
        in_specs=[
            pl.BlockSpec((Q_BLOCK, gw), lambda b, g, i: (b * nq + i, OFF_Q // gw + g)),
            pl.BlockSpec((Q_BLOCK, gates_w), lambda b, g, i: (b * nq + i, OFF_GATE // gates_w)),
            rope_spec, rope_spec,
            *kv_specs,
            pl.BlockSpec((seq, LANES), lambda b, g, i: (0, 0)),
            pl.BlockSpec((NSA_GPS * LANES, 2 * HEAD_DIM), lambda b, g, i: (b * (NSA_GROUPS // NSA_GPS) + g, 0)),
            pl.BlockSpec((n_sel, LANES), lambda b, g, i: (0, 0)),
        ],
        out_specs=pl.BlockSpec((Q_BLOCK, gw), lambda b, g, i: (b * nq + i, g)),
        out_shape=jax.ShapeDtypeStruct((batch * seq, NSA_HEADS * HEAD_DIM), F32),
        scratch_shapes=[pltpu.VMEM((gates_w, Q_BLOCK), F32), pltpu.VMEM((NSA_GPS, seq, NSA_LANES), F32),
                        pltpu.VMEM((2, NSA_GPS, nq, V_AUG, Q_BLOCK), BF16)],
        compiler_params=pltpu.CompilerParams(
            dimension_semantics=("arbitrary", "arbitrary", "arbitrary"), vmem_limit_bytes=VMEM_LIMIT),
        name="nsa",
    )(z, z, cos_q, sin_q, *([z] * len(kv_specs)), feat, kvc, c2st)


def _rglru_kernel(x_ref, gr_ref, cw_ref, cb_ref, wa_ref, ba_ref, wi_ref, bi_ref, lam_ref, y_ref,
                  xs_scr, h_scr):
    ts, c = x_ref.shape

    @pl.when(pl.program_id(1) == 0)
    def _():
        xs_scr[0:SUBLANES, :] = jnp.zeros((SUBLANES, c), F32)
        h_scr[...] = jnp.zeros(h_scr.shape, F32)

    x = x_ref[...].astype(F32)
    xs_scr[SUBLANES:SUBLANES + ts, :] = x
    xc = cb_ref[...] + cw_ref[CONV_WIDTH - 1:CONV_WIDTH, :] * x
    for k in range(CONV_WIDTH - 1):
        off = SUBLANES - (CONV_WIDTH - 1) + k
        xc = xc + cw_ref[k:k + 1, :] * xs_scr[off:off + ts, :]
    xs_scr[0:SUBLANES, :] = x[ts - SUBLANES:, :]

    ra, ri = [], []
    for mblk in range(c // LANES):
        xb = xc[:, mblk * LANES:(mblk + 1) * LANES].astype(BF16)
        ra.append(jnp.dot(xb, wa_ref[mblk], preferred_element_type=F32))
        ri.append(jnp.dot(xb, wi_ref[mblk], preferred_element_type=F32))
    r = _sigmoid(jnp.concatenate(ra, axis=1) + ba_ref[...])
    gi = _sigmoid(jnp.concatenate(ri, axis=1) + bi_ref[...])

    nl = -lam_ref[...]
    softplus = jnp.maximum(nl, 0.0) + jnp.log1p(jnp.exp(-jnp.abs(nl)))
    log_a = (-RGLRU_C) * r * softplus
    a = jnp.exp(log_a)
    th = jnp.tanh(log_a)
    y2 = -2.0 * th / (1.0 - th)
    b = jnp.where(y2 > 0.0, y2 * lax.rsqrt(y2), 0.0) * (gi * xc)

    groups = ts // SUBLANES
    a = a.reshape(groups, SUBLANES, c)
    b = b.reshape(groups, SUBLANES, c)
    in_group = lax.broadcasted_iota(jnp.int32, (groups, SUBLANES, c), 1)
    d = 1
    while d < SUBLANES:
        keep = in_group >= d
        a_sh = jnp.where(keep, pltpu.roll(a, d, 1), 1.0)
        b_sh = jnp.where(keep, pltpu.roll(b, d, 1), 0.0)
        b = a * b_sh + b
        a = a * a_sh
        d *= 2
    carry = h_scr[...]
    hs = []
    for r in range(groups):
        hg = b[r] + a[r] * carry
        hs.append(hg)
        carry = hg[SUBLANES - 1:SUBLANES, :]
    h_scr[...] = carry
    y_ref[...] = _gelu_tanh(gr_ref[...].astype(F32)) * jnp.concatenate(hs, axis=0)


def _rglru(z, cw, cb, wa2, ba, wi2, bi, lam, *, batch, seq, ts):
    c = D_MODEL
    nt = seq // ts
    vec = pl.BlockSpec((1, c), lambda b, t: (0, 0))
    return pl.pallas_call(
        _rglru_kernel,
        grid=(batch, nt),
        in_specs=[
            pl.BlockSpec((ts, c), lambda b, t: (b * nt + t, OFF_XRNN // c)),
            pl.BlockSpec((ts, c), lambda b, t: (b * nt + t, OFF_GRNN // c)),
            pl.BlockSpec((CONV_WIDTH, c), lambda b, t: (0, 0)),
            vec,
            pl.BlockSpec((c // LANES, LANES, LANES), lambda b, t: (0, 0, 0)),
            vec,
            pl.BlockSpec((c // LANES, LANES, LANES), lambda b, t: (0, 0, 0)),
            vec, vec,
        ],
        out_specs=pl.BlockSpec((ts, c), lambda b, t: (b * nt + t, 0)),
        out_shape=jax.ShapeDtypeStruct((batch * seq, c), F32),
        scratch_shapes=[pltpu.VMEM((SUBLANES + ts, c), F32), pltpu.VMEM((1, c), F32)],
        compiler_params=pltpu.CompilerParams(
            dimension_semantics=("parallel", "arbitrary"), vmem_limit_bytes=VMEM_LIMIT),
        name="rglru",
    )(z, z, cw, cb, wa2, ba, wi2, bi, lam)


def _merge_kernel(x_ref, m0_ref, m1_ref, m2_ref, ynsa_ref, yrnn_ref, qx_ref, km_ref, vm_ref, wxo_ref, wo_ref, h_ref):
    tq = x_ref.shape[0]
    mem_len = km_ref.shape[0]
    row_chunk = tq // 4

    def cross_attention():
        qt = jnp.transpose(qx_ref[...].astype(F32) * (SCALE * LOG2E))
        row_head = lax.broadcasted_iota(jnp.int32, qt.shape, 0) // HEAD_DIM
        q_bd = jnp.concatenate([jnp.where(row_head == h, qt, 0.0) for h in range(XATTN_HEADS)],
                               axis=1).astype(BF16)
        yield
        s = jnp.dot(km_ref[...], q_bd, preferred_element_type=F32)
        yield
        m = jnp.max(_fold_rows(s, jnp.maximum), axis=0, keepdims=True)
        p = jnp.exp2(s - m).astype(BF16)
        yield
        ones_rows = jnp.where(lax.broadcasted_iota(jnp.int32, (BF16_ROWS, mem_len), 0) == 0, 1.0, 0.0)
        vt = jnp.concatenate([jnp.transpose(vm_ref[...].astype(F32)), ones_rows], axis=0).astype(BF16)
        o_all = jnp.dot(vt, p, preferred_element_type=F32)
        yield
        outs = []
        for h in range(XATTN_HEADS):
            lanes = slice(h * tq, (h + 1) * tq)
            inv_l = 1.0 / o_all[XATTN_WIDTH:XATTN_WIDTH + 1, lanes]
            outs.append(o_all[h * HEAD_DIM:(h + 1) * HEAD_DIM, lanes] * inv_l)
        o = jnp.transpose(jnp.concatenate(outs, axis=0)).astype(BF16)
        yield
        return jnp.dot(o, wxo_ref[...], preferred_element_type=F32)

    def gated(gate_ref, val_ref):
        parts = []
        for r in range(0, tq, row_chunk):
            parts.append(_sigmoid_tanh(gate_ref[r:r + row_chunk, :].astype(F32)) * val_ref[r:r + row_chunk, :])
            yield
        return jnp.concatenate(parts, axis=0)

    def gate_only(gate_ref):
        parts = []
        for r in range(0, tq, row_chunk):
            parts.append(_sigmoid_tanh(gate_ref[r:r + row_chunk, :].astype(F32)))
            yield
        return jnp.concatenate(parts, axis=0)

    yx, y_a, y_b, g_x = _round_robin([cross_attention(), gated(m0_ref, ynsa_ref), gated(m1_ref, yrnn_ref),
                                      gate_only(m2_ref)])
    y = y_a + y_b + g_x * yx
    h_ref[...] = x_ref[...] + jnp.dot(y.astype(BF16), wo_ref[...], preferred_element_type=F32)


def _merge(x, z, ynsa, yrnn, kvm, wxo, wo, *, seq, mem_len, tq):
    t, d = x.shape
    nt = seq // tq
    row = lambda cb: pl.BlockSpec((tq, d), lambda i, cb=cb: (i, cb))
    mem_spec = lambda cb: pl.BlockSpec((mem_len, XATTN_WIDTH), lambda i, cb=cb: (i // nt, cb))
    return pl.pallas_call(
        _merge_kernel,
        grid=(t // tq,),
        in_specs=[
            row(0),
            row(OFF_MERGE // d), row(OFF_MERGE // d + 1), row(OFF_MERGE // d + 2),
            row(0), row(0),
            pl.BlockSpec((tq, XATTN_WIDTH), lambda i: (i, OFF_QX // XATTN_WIDTH)),
            mem_spec(0), mem_spec(1),
            pl.BlockSpec((XATTN_WIDTH, d), lambda i: (0, 0)),
            pl.BlockSpec((d, d), lambda i: (0, 0)),
        ],
        out_specs=row(0),
        out_shape=jax.ShapeDtypeStruct((t, d), F32),
        compiler_params=pltpu.CompilerParams(
            dimension_semantics=("parallel",), vmem_limit_bytes=VMEM_LIMIT),
        name="merge",
    )(x, z, z, z, ynsa, yrnn, z, kvm, kvm, wxo, wo)


def _mlp_kernel(h_ref, g_ref, wup_ref, wdn_ref, gf_ref, o_ref, v_scr, acc_scr):
    f = pl.program_id(1)

    @pl.when(f == 0)
    def _():
        h = h_ref[...]
        ms = jnp.mean(h * h, axis=-1, keepdims=True)
        v_scr[...] = (h * lax.rsqrt(ms + RMS_EPS) * g_ref[...]).astype(BF16)
        acc_scr[...] = jnp.zeros(acc_scr.shape, F32)

    up = jnp.dot(v_scr[...], wup_ref[...], preferred_element_type=F32)
    act = jnp.square(jnp.maximum(up, 0.0)).astype(BF16)
    acc_scr[...] += jnp.dot(act, wdn_ref[...], preferred_element_type=F32)

    @pl.when(f == pl.num_programs(1) - 1)
    def _():
        h2 = h_ref[...] + acc_scr[...]
        ms = jnp.mean(h2 * h2, axis=-1, keepdims=True)
        o_ref[...] = h2 * lax.rsqrt(ms + RMS_EPS) * gf_ref[...]


def _mlp(h, g, wup, wdn, gf, *, tm, tf):
    t, d = h.shape
    dff = wup.shape[1]
    return pl.pallas_call(
        _mlp_kernel,
        grid=(t // tm, dff // tf),
        in_specs=[
            pl.BlockSpec((tm, d), lambda i, f: (i, 0)),
            pl.BlockSpec((1, d), lambda i, f: (0, 0)),
            pl.BlockSpec((d, tf), lambda i, f: (0, f)),
            pl.BlockSpec((tf, d), lambda i, f: (f, 0)),
            pl.BlockSpec((1, d), lambda i, f: (0, 0)),
        ],
        out_specs=pl.BlockSpec((tm, d), lambda i, f: (i, 0)),
        out_shape=jax.ShapeDtypeStruct((t, d), F32),
        scratch_shapes=[pltpu.VMEM((tm, d), BF16), pltpu.VMEM((tm, d), F32)],
        compiler_params=pltpu.CompilerParams(
            dimension_semantics=("parallel", "arbitrary"), vmem_limit_bytes=VMEM_LIMIT),
        name="mlp",
    )(h, g, wup, wdn, gf)


def _rope_angles(pos):
    inv = 1.0 / (ROPE_THETA ** (jnp.arange(0, ROPE_DIM, 2, dtype=F32) / ROPE_DIM))
    ang = pos.astype(F32)[:, None] * inv[None, :]
    return jnp.cos(ang), jnp.sin(ang)


def _rope_tables(pos, width, rope_width=None):
    rope_width = width if rope_width is None else rope_width
    cos, sin = _rope_angles(pos)
    n = pos.shape[0]
    pad = HEAD_DIM - ROPE_DIM
    cos_h = jnp.concatenate([cos, cos, jnp.ones((n, pad), F32)], axis=1)
    sin_h = jnp.concatenate([-sin, sin, jnp.zeros((n, pad), F32)], axis=1)
    reps = rope_width // HEAD_DIM
    rest = width - rope_width
    cos_t = jnp.concatenate([jnp.tile(cos_h, (1, reps)), jnp.ones((n, rest), F32)], axis=1)
    sin_t = jnp.concatenate([jnp.tile(sin_h, (1, reps)), jnp.zeros((n, rest), F32)], axis=1)
    return cos_t, sin_t


def _pack_w_in(w_in):
    d = w_in.shape[0]
    w = w_in.astype(BF16)
    n_qkv = NSA_HEADS * HEAD_DIM + 6 * NSA_KV
    n_gates = NSA_HEADS * NSA_BRANCHES
    rnn_lo = n_qkv + n_gates
    qx_lo = rnn_lo + 2 * D_MODEL
    mg_lo = qx_lo + XATTN_WIDTH
    per_group = n_gates // NSA_GROUPS
    gates = w[:, n_qkv:rnn_lo].reshape(d, NSA_GROUPS, per_group)
    gates = jnp.pad(gates, ((0, 0), (0, 0), (0, GATE_STRIDE - per_group))).reshape(d, NSA_GROUPS * GATE_STRIDE)
    return jnp.concatenate([w[:, :n_qkv], w[:, qx_lo:mg_lo], gates, w[:, rnn_lo:qx_lo], w[:, mg_lo:]], axis=1)


def _block_diag_pairs(w):
    nb, k, _ = w.shape
    w = w.reshape(nb // 2, 2, k, k)
    zero = jnp.zeros((nb // 2, k, k), w.dtype)
    top = jnp.concatenate([w[:, 0], zero], axis=2)
    bot = jnp.concatenate([zero, w[:, 1]], axis=2)
    return jnp.concatenate([top, bot], axis=1).astype(BF16)


def _cmp_to_sel_t(n_cmp, n_cmp_pad, n_sel):
    c0 = np.arange(n_cmp_pad)[None, :] * CMP_STRIDE
    s0 = np.arange(n_sel)[:, None] * SEL_BLOCK
    ov = np.clip(np.minimum(c0 + CMP_LEN, s0 + SEL_BLOCK) - np.maximum(c0, s0), 0, None)
    ov = np.where(np.arange(n_cmp_pad)[None, :] < n_cmp, ov, 0)
    return (ov / CMP_LEN).astype(np.float32)


def _key_features(seq, n_sel):
    col = np.arange(LANES)[None, :]
    onehot = (np.arange(seq)[:, None] // SEL_BLOCK == col) & (col < n_sel)
    return jnp.asarray(onehot | (col == n_sel), dtype=BF16)


def kernel(x, mem, g_mix, w_in, cmp_pos_k, cmp_pos_v, w_cmp_k1, w_cmp_k2, w_cmp_v1, w_cmp_v2, conv_w, conv_b, w_rg_a, b_rg_a, w_rg_i, b_rg_i, rg_lambda, g_mem, w_mem_kv, w_xo, w_o, g_mlp, w_up, w_down, g_final):
    batch, seq, d = x.shape
    mem_len = mem.shape[1]
    t = batch * seq
    depth = g_mix.shape[0]
    n_sel = seq // SEL_BLOCK
    n_cmp_pad = seq // CMP_STRIDE
    assert n_cmp_pad == LANES and n_sel <= K_AUG - HEAD_DIM and d == D_MODEL

    pos = jnp.arange(seq)
    cos_k, sin_k = _rope_tables(pos, NSA_KV)
    cos_q, sin_q = (a.T for a in _rope_angles(pos))
    no_rope = jnp.zeros((mem_len, LANES), F32)
    cmp_pos = jnp.arange(n_cmp_pad) * CMP_STRIDE + (CMP_LEN - 1)
    cmp_tm = 8 * n_cmp_pad
    cos_c, sin_c = _rope_tables(cmp_pos, 2 * HEAD_DIM, HEAD_DIM)
    cos_c = jnp.tile(cos_c, (cmp_tm // n_cmp_pad, 1))
    sin_c = jnp.tile(sin_c, (cmp_tm // n_cmp_pad, 1))
    c2st = jnp.asarray(_cmp_to_sel_t((seq - CMP_LEN) // CMP_STRIDE + 1, n_cmp_pad, n_sel))
    feat = _key_features(seq, n_sel)

    h = x.reshape(t, d)
    for l in range(depth):
        z = _norm_proj(h, g_mix[l][None, :], _pack_w_in(w_in[l]), cos_k, sin_k, tm=1024, tn=INPROJ_TN,
                       rope_tiles=ROPE_TILES, out_dtype=BF16, name="inproj")

        def groups(off):
            a = z[:, off:off + NSA_KV].reshape(batch, seq, NSA_GROUPS, HEAD_DIM)
            return a.transpose(0, 2, 1, 3)

        half = CMP_STRIDE * HEAD_DIM
        rk = groups(OFF_KC).reshape(batch * NSA_GROUPS * n_cmp_pad, half)
        rv = groups(OFF_VC).reshape(batch * NSA_GROUPS * n_cmp_pad, half)
        pk = cmp_pos_k[l].reshape(2, half)
        pv = cmp_pos_v[l].reshape(2, half)
        w2 = jnp.zeros((2 * CMP_HIDDEN, 2 * HEAD_DIM), F32)
        w2 = w2.at[:CMP_HIDDEN, :HEAD_DIM].set(w_cmp_k2[l]).at[CMP_HIDDEN:, HEAD_DIM:].set(w_cmp_v2[l])
        kvc = _compress(rk, rv, pk[0:1], pk[1:2], pv[0:1], pv[1:2],
                        w_cmp_k1[l][:half].astype(BF16), w_cmp_k1[l][half:].astype(BF16),
                        w_cmp_v1[l][:half].astype(BF16), w_cmp_v1[l][half:].astype(BF16),
                        w2.astype(BF16), cos_c, sin_c, tm=cmp_tm)
        y_nsa = _nsa(z, cos_q, sin_q, feat, kvc, c2st, batch=batch, seq=seq)

        y_rnn = _rglru(z, conv_w[l], conv_b[l][None, :], _block_diag_pairs(w_rg_a[l]), b_rg_a[l][None, :],
                       _block_diag_pairs(w_rg_i[l]), b_rg_i[l][None, :], rg_lambda[l][None, :],
                       batch=batch, seq=seq, ts=256)

        kvm = _norm_proj(mem.reshape(batch * mem_len, d), g_mem[l][None, :], w_mem_kv[l].astype(BF16),
                         no_rope, no_rope, tm=mem_len, tn=2 * XATTN_WIDTH, rope_tiles=(),
                         out_dtype=BF16, name="memkv")

        h1 = _merge(h, z, y_nsa, y_rnn, kvm, w_xo[l].astype(BF16), w_o[l].astype(BF16),
                    seq=seq, mem_len=mem_len, tq=512)
        last = l == depth - 1
        gf = g_final if last else jnp.ones_like(g_final)
        h = _mlp(h1, g_mlp[l][None, :], w_up[l].astype(BF16), w_down[l].astype(BF16), gf[None, :],
                 tm=1024, tf=1024)
        assert last, "final norm is fused into the last layer's MLP kernel"
    return h.reshape(batch, seq, d)
```

```python
import functools
import math

import numpy as np
import jax
import jax.numpy as jnp
from jax import lax
from jax.experimental import pallas as pl
from jax.experimental.pallas import tpu as pltpu

F32 = jnp.float32
BF16 = jnp.bfloat16

D_MODEL = 1024
HEAD_DIM = 64
NSA_HEADS = 16
NSA_GROUPS = 4
NSA_HPG = NSA_HEADS // NSA_GROUPS
NSA_BRANCHES = 3
CMP_LEN = 32
CMP_STRIDE = 16
CMP_HIDDEN = 256
SEL_BLOCK = 64
SEL_TOPK = 8
WINDOW = 512
Q_BLOCK = 128
ROPE_THETA = 500000.0
ROPE_DIM = HEAD_DIM // 4
ROPE_HALF = ROPE_DIM // 2
RNN_BLOCKS = 16
RNN_BLOCK_DIM = D_MODEL // RNN_BLOCKS
CONV_WIDTH = 4
RGLRU_C = 8.0
XATTN_HEADS = 4
XATTN_WIDTH = XATTN_HEADS * HEAD_DIM
D_FF = 4 * D_MODEL
RMS_EPS = 1e-6
SCALE = HEAD_DIM ** -0.5
LOG2E = math.log2(math.e)
NEG = -1e30

LANES = 128
SUBLANES = 8
BF16_ROWS = 16
VMEM_LIMIT = 48 * 1024 * 1024

NSA_KV = NSA_GROUPS * HEAD_DIM
OFF_Q = 0
OFF_KC = 1024
OFF_VC = 1280
OFF_KS = 1536
OFF_VS = 1792
OFF_KW = 2048
OFF_VW = 2304
OFF_QX = 2560
OFF_GATE = 2816
GATE_STRIDE = 64
OFF_XRNN = 3072
OFF_GRNN = 4096
OFF_MERGE = 5120
Z_WIDTH = 8192
INPROJ_TN = 1024
ROPE_TILES = tuple((off // INPROJ_TN, off % INPROJ_TN) for off in (OFF_KS, OFF_KW))

NSA_PAIRS = 2
NSA_GPS = 2 * NSA_PAIRS
NSA_LANES = NSA_HPG * Q_BLOCK
SEL_CHUNK = 4 * Q_BLOCK
WIN_TILES = WINDOW // Q_BLOCK + 1
K_AUG = 2 * HEAD_DIM
V_AUG = HEAD_DIM + BF16_ROWS

NT_DIMS = (((1,), (1,)), ((), ()))


def _sigmoid(x):
    return 1.0 / (1.0 + jnp.exp(-x))


def _sigmoid_tanh(x):
    return 0.5 * jnp.tanh(0.5 * x) + 0.5


def _gelu_tanh(x):
    return 0.5 * x * (1.0 + jnp.tanh(0.7978845608028654 * (x + 0.044715 * (x * x * x))))


def _rope_swap(z):
    n = z.shape[-1]
    lane = lax.broadcasted_iota(jnp.int32, z.shape, z.ndim - 1)
    first_half = (lane & (HEAD_DIM - 1)) < ROPE_HALF
    return jnp.where(first_half, pltpu.roll(z, n - ROPE_HALF, z.ndim - 1), pltpu.roll(z, ROPE_HALF, z.ndim - 1))


def _round_robin(gens):
    results = [None] * len(gens)
    alive = list(range(len(gens)))
    while alive:
        for idx in list(alive):
            try:
                next(gens[idx])
            except StopIteration as stop:
                results[idx] = stop.value
                alive.remove(idx)
    return results


def _fold_rows(x, op):
    parts = [x[r * SUBLANES:(r + 1) * SUBLANES] for r in range(x.shape[0] // SUBLANES)]
    return functools.reduce(op, parts)


def _norm_proj_kernel(x_ref, g_ref, w_ref, cos_ref, sin_ref, z_ref, u_scr, *, rope_tiles):
    j = pl.program_id(1)

    @pl.when(j == 0)
    def _():
        x = x_ref[...]
        ms = jnp.mean(x * x, axis=-1, keepdims=True)
        u_scr[...] = (x * lax.rsqrt(ms + RMS_EPS) * g_ref[...]).astype(BF16)

    z = jnp.dot(u_scr[...], w_ref[...], preferred_element_type=F32)
    z_ref[...] = z.astype(z_ref.dtype)
    rw = cos_ref.shape[1]
    for tile, col in rope_tiles:
        @pl.when(j == tile)
        def _(col=col):
            zr = z[:, col:col + rw]
            z_ref[:, col:col + rw] = (zr * cos_ref[...] + _rope_swap(zr) * sin_ref[...]).astype(z_ref.dtype)


def _norm_proj(x, g, w, cos_t, sin_t, *, tm, tn, rope_tiles, out_dtype, name):
    t, d = x.shape
    n = w.shape[1]
    s_tiles = cos_t.shape[0] // tm
    rw = cos_t.shape[1]
    table_spec = pl.BlockSpec((tm, rw), lambda i, j: (i % s_tiles, 0))
    return pl.pallas_call(
        functools.partial(_norm_proj_kernel, rope_tiles=tuple(rope_tiles)),
        grid=(t // tm, n // tn),
        in_specs=[
            pl.BlockSpec((tm, d), lambda i, j: (i, 0)),
            pl.BlockSpec((1, d), lambda i, j: (0, 0)),
            pl.BlockSpec((d, tn), lambda i, j: (0, j)),
            table_spec, table_spec,
        ],
        out_specs=pl.BlockSpec((tm, tn), lambda i, j: (i, j)),
        out_shape=jax.ShapeDtypeStruct((t, n), out_dtype),
        scratch_shapes=[pltpu.VMEM((tm, d), BF16)],
        compiler_params=pltpu.CompilerParams(
            dimension_semantics=("parallel", "arbitrary"), vmem_limit_bytes=VMEM_LIMIT),
        name=name,
    )(x, g, w, cos_t, sin_t)


def _compress_kernel(rk_ref, rv_ref, pka_ref, pkb_ref, pva_ref, pvb_ref,
                     wk1a_ref, wk1b_ref, wv1a_ref, wv1b_ref, w2_ref, cos_ref, sin_ref, o_ref):
    m = rk_ref.shape[0]

    def hidden(r_ref, pa_ref, pb_ref, w1a_ref, w1b_ref):
        r = r_ref[...].astype(F32)
        pa = jnp.dot((r + pa_ref[...]).astype(BF16), w1a_ref[...], preferred_element_type=F32)
        pb = jnp.dot((r + pb_ref[...]).astype(BF16), w1b_ref[...], preferred_element_type=F32)
        return _gelu_tanh(pa + pltpu.roll(pb, m - 1, 0)).astype(BF16)

    hk = hidden(rk_ref, pka_ref, pkb_ref, wk1a_ref, wk1b_ref)
    hv = hidden(rv_ref, pva_ref, pvb_ref, wv1a_ref, wv1b_ref)
    hcat = jnp.concatenate([hk, hv], axis=1)
    kv = jnp.dot(hcat, w2_ref[...], preferred_element_type=F32)
    o_ref[...] = kv * cos_ref[...] + _rope_swap(kv) * sin_ref[...]


def _compress(rk, rv, pka, pkb, pva, pvb, wk1a, wk1b, wv1a, wv1b, w2, cos_t, sin_t, *, tm):
    pairs, m, k = rk.shape
    hid = wk1a.shape[1]
    out_w = w2.shape[1]
    full = lambda shape: pl.BlockSpec(shape, lambda p, i: (0, 0))
    rows = pl.BlockSpec((None, tm, k), lambda p, i: (p, i, 0))
    return pl.pallas_call(
        _compress_kernel,
        grid=(pairs, m // tm),
        in_specs=[
            rows, rows,
            full((1, k)), full((1, k)), full((1, k)), full((1, k)),
            full((k, hid)), full((k, hid)), full((k, hid)), full((k, hid)),
            full((2 * hid, out_w)),
            full((tm, out_w)), full((tm, out_w)),
        ],
        out_specs=pl.BlockSpec((tm, out_w), lambda p, i: (i, p)),
        out_shape=jax.ShapeDtypeStruct((m, pairs * out_w), F32),
        compiler_params=pltpu.CompilerParams(
            dimension_semantics=("parallel", "parallel"), vmem_limit_bytes=VMEM_LIMIT),
        name="compress",
    )(rk, rv, pka, pkb, pva, pvb, wk1a, wk1b, wv1a, wv1b, w2, cos_t, sin_t)


def _nsa_kernel(q_ref, g_ref, cos_ref, sin_ref, *rest, n_sel):
    kv_refs, rest = rest[:4 * NSA_PAIRS], rest[4 * NSA_PAIRS:]
    feat_ref, kvc_ref, c2st_ref, o_ref, gt_scr, s_scr, vt_scr = rest
    ks_refs, vs_refs, kw_refs, vw_refs = (kv_refs[n::4] for n in range(4))
    g2 = pl.program_id(1)
    i = pl.program_id(2)
    q0 = i * Q_BLOCK
    seq = feat_ref.shape[0]

    @pl.when(i == 0)
    def _():
        ones_rows = jnp.where(lax.broadcasted_iota(jnp.int32, (V_AUG - HEAD_DIM, Q_BLOCK), 0) == 0,
                              1.0, 0.0).astype(BF16)

        def fill(t, carry):
            r0 = pl.multiple_of(t * Q_BLOCK, Q_BLOCK)
            for kind, refs in enumerate((vs_refs, vw_refs)):
                for pair, ref in enumerate(refs):
                    vt = jnp.transpose(ref[pl.ds(r0, Q_BLOCK), :].astype(F32)).astype(BF16)
                    for half in range(2):
                        gg = 2 * pair + half
                        vt_scr[kind, gg, t, 0:HEAD_DIM, :] = vt[half * HEAD_DIM:(half + 1) * HEAD_DIM]
                        vt_scr[kind, gg, t, HEAD_DIM:V_AUG, :] = ones_rows
            return carry

        lax.fori_loop(0, seq // Q_BLOCK, fill, 0)

    def query_side(gg, qt, bias_rows, win_row):
        zeros_q = jnp.zeros((HEAD_DIM, NSA_LANES), BF16)
        q_rows = [qt, zeros_q] if gg % 2 == 0 else [zeros_q, qt]
        rid = lax.broadcasted_iota(jnp.int32, (BF16_ROWS, NSA_LANES), 0)
        win_rows = jnp.where(rid == 0, win_row, 0.0).astype(BF16)
        tail = jnp.zeros((2 * LANES - 2 * HEAD_DIM - n_sel - BF16_ROWS, NSA_LANES), BF16)
        return jnp.concatenate(q_rows + [bias_rows.astype(BF16), win_rows, tail], axis=0)

    def key_side(k_ref, row0, rows):
        return jnp.concatenate([k_ref[pl.ds(row0, rows), :], feat_ref[pl.ds(row0, rows), :]], axis=1)
    blocks_per_chunk = SEL_CHUNK // SEL_BLOCK
    q_per_chunk = SEL_CHUNK // Q_BLOCK

    def tile_heads(x):
        return jnp.concatenate([x] * NSA_HPG, axis=1)

    tq = q0 + (lax.broadcasted_iota(jnp.int32, (1, NSA_LANES), 1) & (Q_BLOCK - 1))
    c_idx = lax.broadcasted_iota(jnp.int32, (Q_BLOCK, 1), 0)
    r_idx = lax.broadcasted_iota(jnp.int32, (1, Q_BLOCK), 1)
    tri_diag = tile_heads(jnp.where(c_idx <= r_idx, 0.0, NEG))
    tri_old = tile_heads(jnp.where(c_idx > r_idx, 0.0, NEG))
    cos_q = tile_heads(cos_ref[...])
    sin_q = tile_heads(sin_ref[...])
    blk = lax.broadcasted_iota(jnp.int32, (n_sel, Q_BLOCK), 0)
    cur = (q0 + lax.broadcasted_iota(jnp.int32, (n_sel, Q_BLOCK), 1)) // SEL_BLOCK
    sub = lax.broadcasted_iota(jnp.int32, (SUBLANES, Q_BLOCK), 0)

    gt_scr[...] = jnp.transpose(_sigmoid(g_ref[...].astype(F32)))

    def prepare(gg):
        qx = jnp.transpose(q_ref[:, gg * NSA_HPG * HEAD_DIM:(gg + 1) * NSA_HPG * HEAD_DIM].astype(F32)
                           * (SCALE * LOG2E))
        qf = jnp.concatenate([qx[h * HEAD_DIM:(h + 1) * HEAD_DIM, :] for h in range(NSA_HPG)], axis=1)
        x1, x2 = qf[0:ROPE_HALF], qf[ROPE_HALF:ROPE_DIM]
        qt = jnp.concatenate([x1 * cos_q - x2 * sin_q, x2 * cos_q + x1 * sin_q, qf[ROPE_DIM:]],
                             axis=0).astype(BF16)
        qts[gg] = qt
        yield

        kvc = kvc_ref[:, gg * LANES:(gg + 1) * LANES]
        kc = kvc[:, :HEAD_DIM].astype(BF16)
        vct = jnp.transpose(kvc)[HEAD_DIM:, :].astype(BF16)
        sc = jnp.dot(kc, qt, preferred_element_type=F32)
        yield
        n_idx = lax.broadcasted_iota(jnp.int32, (LANES, 1), 0)
        sc = jnp.where((n_idx * CMP_STRIDE + (CMP_LEN - 1)) <= tq, sc, NEG)
        m_c = jnp.max(sc, axis=0, keepdims=True)
        e_c = jnp.exp2(sc - m_c)
        inv_c = 1.0 / jnp.maximum(jnp.sum(e_c, axis=0, keepdims=True), 1e-30)
        p_c = e_c * jnp.where(m_c > 0.5 * NEG, inv_c, 0.0)
        o_c = jnp.dot(vct, p_c.astype(BF16), preferred_element_type=F32)
        yield

        p_sum = p_c[:, 0:Q_BLOCK]
        for h in range(1, NSA_HPG):
            p_sum = p_sum + p_c[:, h * Q_BLOCK:(h + 1) * Q_BLOCK]
        imp = jnp.dot(c2st_ref[...], p_sum, preferred_element_type=F32,
                      precision=lax.Precision.HIGHEST)
        yield
        forced = (blk == 0) | (blk == cur) | (blk == cur - 1)
        val = jnp.where(forced, jnp.inf, jnp.where(blk > cur, -jnp.inf, imp))
        n_grp = n_sel // SUBLANES
        grp = [val[v * SUBLANES:(v + 1) * SUBLANES] for v in range(n_grp)]
        cnt = [jnp.zeros((SUBLANES, Q_BLOCK), F32) for _ in range(n_grp)]
        for sp in range(n_sel):
            r = val[sp:sp + 1, :]
            for v in range(n_grp):
                if v * SUBLANES > sp:
                    cnt[v] = cnt[v] + jnp.where(r >= grp[v], 1.0, 0.0)
                elif v * SUBLANES + SUBLANES - 1 <= sp:
                    cnt[v] = cnt[v] + jnp.where(r > grp[v], 1.0, 0.0)
                else:
                    later = jnp.where(sub > sp - v * SUBLANES, 1.0, 0.0)
                    cnt[v] = cnt[v] + jnp.where(r > grp[v], 1.0, jnp.where(r == grp[v], later, 0.0))
            if sp % SUBLANES == SUBLANES - 1:
                yield
        chosen = jnp.concatenate([jnp.where(c < float(SEL_TOPK), 0.0, NEG) for c in cnt], axis=0)
        block_bias = tile_heads(jnp.where(blk > cur, NEG, chosen))
        row0 = pl.multiple_of((g2 * NSA_GPS + gg) * GATE_STRIDE, GATE_STRIDE)
        gt = gt_scr[pl.ds(row0, 2 * SUBLANES), :]
        return qt, o_c, block_bias, gt

    def window(gg):
        qt = qts[gg]
        tiles = []
        for d in range(1 - WIN_TILES, 1):
            td = i + d
            tc = jnp.maximum(td, 0)
            k0 = pl.multiple_of(tc * Q_BLOCK, Q_BLOCK)
            before_start = jnp.where(td >= 0, 0.0, NEG) if d < 0 else 0.0
            tri = tri_diag if d == 0 else (tri_old if d == 1 - WIN_TILES else None)
            tiles.append((tc, k0, before_start, tri))

        zero_row = jnp.zeros((1, NSA_LANES), F32)
        no_bias = jnp.zeros((n_sel, NSA_LANES), BF16)
        scores = []
        for tc, k0, before_start, tri in tiles:
            q_aug = query_side(gg, qt, no_bias, zero_row + before_start)
            s = jnp.dot(key_side(kw_refs[gg // 2], k0, Q_BLOCK), q_aug, preferred_element_type=F32)
            scores.append(s if tri is None else s + tri)
            yield
        m_w = jnp.max(functools.reduce(jnp.maximum, [_fold_rows(s, jnp.maximum) for s in scores]),
                      axis=0, keepdims=True)
        n_pad = jnp.maximum(WINDOW - 1 - tq, 0).astype(F32)
        m_w = jnp.where(n_pad > 0.0, jnp.maximum(m_w, 0.0), m_w)
        yield
        acc = jnp.zeros((V_AUG, NSA_LANES), F32)
        for (tc, _, _, _), s in zip(tiles, scores):
            acc = acc + jnp.dot(vt_scr[1, gg, tc], jnp.exp2(s - m_w).astype(BF16), preferred_element_type=F32)
            yield
        l_w = acc[HEAD_DIM:HEAD_DIM + 1] + n_pad * jnp.exp2(-m_w)
        return acc[:HEAD_DIM] * (1.0 / jnp.maximum(l_w, 1e-30))

    def chunk_scores(gg, c, diag_rel):
        zeros_q = jnp.zeros((HEAD_DIM, NSA_LANES), BF16)
        q_rows = jnp.concatenate([qts[gg], zeros_q] if gg % 2 == 0 else [zeros_q, qts[gg]], axis=0)
        s = jnp.dot(ks_refs[gg // 2][c * SEL_CHUNK:(c + 1) * SEL_CHUNK, :], q_rows,
                    preferred_element_type=F32)
        if diag_rel is not None:
            s = s + jnp.concatenate([jnp.where(diag_rel == u, tri_diag, 0.0) for u in range(q_per_chunk)], axis=0)
        s_scr[gg, c * SEL_CHUNK:(c + 1) * SEL_CHUNK, :] = s
        return [_fold_rows(s[b * SEL_BLOCK:(b + 1) * SEL_BLOCK], jnp.maximum) for b in range(blocks_per_chunk)]

    def first_chunk(gg):
        yield
        return chunk_scores(gg, 0, i)

    def selected(gg, k, block_bias, bmax0):
        bmax = list(bmax0)
        for c in range(1, k + 1):
            bmax += chunk_scores(gg, c, i - k * q_per_chunk if c == k else None)
            yield
        m8 = functools.reduce(jnp.maximum, [bm + block_bias[n:n + 1, :] for n, bm in enumerate(bmax)])
        m_s = jnp.max(m8, axis=0, keepdims=True)
        shift = block_bias - m_s
        yield
        tiles_per_chunk = SEL_CHUNK // Q_BLOCK
        acc = jnp.zeros((V_AUG, NSA_LANES), F32)
        for c in range(k + 1):
            ps = []
            for b in range(blocks_per_chunk):
                n = c * blocks_per_chunk + b
                ps.append(jnp.exp2(s_scr[gg, n * SEL_BLOCK:(n + 1) * SEL_BLOCK, :] + shift[n:n + 1, :]).astype(BF16))
            vt = jnp.concatenate([vt_scr[0, gg, c * tiles_per_chunk + u] for u in range(tiles_per_chunk)], axis=1)
            acc = acc + jnp.dot(vt, jnp.concatenate(ps, axis=0), preferred_element_type=F32)
            yield
        l_s = acc[HEAD_DIM:HEAD_DIM + 1]
        return acc[:HEAD_DIM] * (1.0 / jnp.maximum(l_s, 1e-30))

    qts = [None] * NSA_GPS
    groups = range(NSA_GPS)
    shared = _round_robin([prepare(gg) for gg in groups] + [window(gg) for gg in groups]
                          + [first_chunk(gg) for gg in groups])
    prepared, windows, bmax0 = (shared[n * NSA_GPS:(n + 1) * NSA_GPS] for n in range(3))

    def finish(k):
        branches = []
        for gg, (qt, o_c, block_bias, gt) in enumerate(prepared):
            branches.append(selected(gg, k, block_bias, bmax0[gg]))
        results = _round_robin(branches)
        outs = []
        for gg, (qt, o_c, block_bias, gt) in enumerate(prepared):
            o_w, o_s = windows[gg], results[gg]
            for h in range(NSA_HPG):
                sl = slice(h * Q_BLOCK, (h + 1) * Q_BLOCK)
                row = NSA_BRANCHES * h
                outs.append(gt[row:row + 1, :] * o_c[:, sl] + gt[row + 1:row + 2, :] * o_s[:, sl]
                            + gt[row + 2:row + 3, :] * o_w[:, sl])
        o_ref[...] = jnp.transpose(jnp.concatenate(outs, axis=0))

    for k in range(n_sel * SEL_BLOCK // SEL_CHUNK):
        pl.when(i // q_per_chunk == k)(functools.partial(finish, k))


def _nsa(z, cos_q, sin_q, feat, kvc, c2st, *, batch, seq):
    nq = seq // Q_BLOCK
    n_sel = seq // SEL_BLOCK
    gw = NSA_GPS * NSA_HPG * HEAD_DIM
    gates_w = NSA_GROUPS * GATE_STRIDE
    pair_w = 2 * HEAD_DIM
    assert pair_w == LANES
    kv_spec = lambda off, pair: pl.BlockSpec(
        (seq, pair_w), lambda b, g, i: (b, off // pair_w + g * NSA_PAIRS + pair))
    kv_specs = [kv_spec(off, pair) for pair in range(NSA_PAIRS) for off in (OFF_KS, OFF_VS, OFF_KW, OFF_VW)]
    rope_spec = pl.BlockSpec((ROPE_HALF, Q_BLOCK), lambda b, g, i: (0, i))
    return pl.pallas_call(
        functools.partial(_nsa_kernel, n_sel=n_sel),
        grid=(batch, NSA_GROUPS // NSA_GPS, nq),
        in_specs=[
            pl.BlockSpec((Q_BLOCK, gw), lambda b, g, i: (b * nq + i, OFF_Q // gw + g)),
            pl.BlockSpec((Q_BLOCK, gates_w), lambda b, g, i: (b * nq + i, OFF_GATE // gates_w)),
            rope_spec, rope_spec,
            *kv_specs,
            pl.BlockSpec((seq, LANES), lambda b, g, i: (0, 0)),
            pl.BlockSpec((LANES, NSA_GPS * 2 * HEAD_DIM), lambda b, g, i: (b, g)),
            pl.BlockSpec((n_sel, LANES), lambda b, g, i: (0, 0)),
        ],
        out_specs=pl.BlockSpec((Q_BLOCK, gw), lambda b, g, i: (b * nq + i, g)),
        out_shape=jax.ShapeDtypeStruct((batch * seq, NSA_HEADS * HEAD_DIM), F32),
        scratch_shapes=[pltpu.VMEM((gates_w, Q_BLOCK), F32), pltpu.VMEM((NSA_GPS, seq, NSA_LANES), F32),
                        pltpu.VMEM((2, NSA_GPS, nq, V_AUG, Q_BLOCK), BF16)],
        compiler_params=pltpu.CompilerParams(
            dimension_semantics=("arbitrary", "arbitrary", "arbitrary"), vmem_limit_bytes=VMEM_LIMIT),
        name="nsa",
    )(z, z, cos_q, sin_q, *([z] * len(kv_specs)), feat, kvc, c2st)


def _rglru_kernel(x_ref, gr_ref, cw_ref, cb_ref, wa_ref, ba_ref, wi_ref, bi_ref, lam_ref, y_ref,
                  xs_scr, h_scr):
    ts, c = x_ref.shape

    @pl.when(pl.program_id(1) == 0)
    def _():
        xs_scr[0:SUBLANES, :] = jnp.zeros((SUBLANES, c), F32)
        h_scr[...] = jnp.zeros(h_scr.shape, F32)

    x = x_ref[...].astype(F32)
    xs_scr[SUBLANES:SUBLANES + ts, :] = x
    xc = cb_ref[...] + cw_ref[CONV_WIDTH - 1:CONV_WIDTH, :] * x
    for k in range(CONV_WIDTH - 1):
        off = SUBLANES - (CONV_WIDTH - 1) + k
        xc = xc + cw_ref[k:k + 1, :] * xs_scr[off:off + ts, :]
    xs_scr[0:SUBLANES, :] = x[ts - SUBLANES:, :]

    ra, ri = [], []
    for mblk in range(c // LANES):
        xb = xc[:, mblk * LANES:(mblk + 1) * LANES].astype(BF16)
        ra.append(jnp.dot(xb, wa_ref[mblk], preferred_element_type=F32))
        ri.append(jnp.dot(xb, wi_ref[mblk], preferred_element_type=F32))
    r = _sigmoid(jnp.concatenate(ra, axis=1) + ba_ref[...])
    gi = _sigmoid(jnp.concatenate(ri, axis=1) + bi_ref[...])

    nl = -lam_ref[...]
    softplus = jnp.maximum(nl, 0.0) + jnp.log1p(jnp.exp(-jnp.abs(nl)))
    log_a = (-RGLRU_C) * r * softplus
    a = jnp.exp(log_a)
    th = jnp.tanh(log_a)
    y2 = -2.0 * th / (1.0 - th)
    b = jnp.where(y2 > 0.0, y2 * lax.rsqrt(y2), 0.0) * (gi * xc)

    groups = ts // SUBLANES
    a = a.reshape(groups, SUBLANES, c)
    b = b.reshape(groups, SUBLANES, c)
    in_group = lax.broadcasted_iota(jnp.int32, (groups, SUBLANES, c), 1)
    d = 1
    while d < SUBLANES:
        keep = in_group >= d
        a_sh = jnp.where(keep, pltpu.roll(a, d, 1), 1.0)
        b_sh = jnp.where(keep, pltpu.roll(b, d, 1), 0.0)
        b = a * b_sh + b
        a = a * a_sh
        d *= 2
    carry = h_scr[...]
    hs = []
    for r in range(groups):
        hg = b[r] + a[r] * carry
        hs.append(hg)
        carry = hg[SUBLANES - 1:SUBLANES, :]
    h_scr[...] = carry
    y_ref[...] = _gelu_tanh(gr_ref[...].astype(F32)) * jnp.concatenate(hs, axis=0)


def _rglru(z, cw, cb, wa2, ba, wi2, bi, lam, *, batch, seq, ts):
    c = D_MODEL
    nt = seq // ts
    vec = pl.BlockSpec((1, c), lambda b, t: (0, 0))
    return pl.pallas_call(
        _rglru_kernel,
        grid=(batch, nt),
        in_specs=[
            pl.BlockSpec((ts, c), lambda b, t: (b * nt + t, OFF_XRNN // c)),
            pl.BlockSpec((ts, c), lambda b, t: (b * nt + t, OFF_GRNN // c)),
            pl.BlockSpec((CONV_WIDTH, c), lambda b, t: (0, 0)),
            vec,
            pl.BlockSpec((c // LANES, LANES, LANES), lambda b, t: (0, 0, 0)),
            vec,
            pl.BlockSpec((c // LANES, LANES, LANES), lambda b, t: (0, 0, 0)),
            vec, vec,
        ],
        out_specs=pl.BlockSpec((ts, c), lambda b, t: (b * nt + t, 0)),
        out_shape=jax.ShapeDtypeStruct((batch * seq, c), F32),
        scratch_shapes=[pltpu.VMEM((SUBLANES + ts, c), F32), pltpu.VMEM((1, c), F32)],
        compiler_params=pltpu.CompilerParams(
            dimension_semantics=("parallel", "arbitrary"), vmem_limit_bytes=VMEM_LIMIT),
        name="rglru",
    )(z, z, cw, cb, wa2, ba, wi2, bi, lam)


def _merge_kernel(x_ref, m0_ref, m1_ref, m2_ref, ynsa_ref, yrnn_ref, qx_ref, km_ref, vm_ref, wxo_ref, wo_ref, h_ref):
    tq = x_ref.shape[0]
    mem_len = km_ref.shape[0]
    row_chunk = tq // 4

    def cross_attention():
        qt = jnp.transpose(qx_ref[...].astype(F32) * (SCALE * LOG2E))
        row_head = lax.broadcasted_iota(jnp.int32, qt.shape, 0) // HEAD_DIM
        q_bd = jnp.concatenate([jnp.where(row_head == h, qt, 0.0) for h in range(XATTN_HEADS)],
                               axis=1).astype(BF16)
        yield
        s = jnp.dot(km_ref[...], q_bd, preferred_element_type=F32)
        yield
        m = jnp.max(_fold_rows(s, jnp.maximum), axis=0, keepdims=True)
        p = jnp.exp2(s - m).astype(BF16)
        yield
        ones_rows = jnp.where(lax.broadcasted_iota(jnp.int32, (BF16_ROWS, mem_len), 0) == 0, 1.0, 0.0)
        vt = jnp.concatenate([jnp.transpose(vm_ref[...].astype(F32)), ones_rows], axis=0).astype(BF16)
        o_all = jnp.dot(vt, p, preferred_element_type=F32)
        yield
        outs = []
        for h in range(XATTN_HEADS):
            lanes = slice(h * tq, (h + 1) * tq)
            inv_l = 1.0 / o_all[XATTN_WIDTH:XATTN_WIDTH + 1, lanes]
            outs.append(o_all[h * HEAD_DIM:(h + 1) * HEAD_DIM, lanes] * inv_l)
        o = jnp.transpose(jnp.concatenate(outs, axis=0)).astype(BF16)
        yield
        return jnp.dot(o, wxo_ref[...], preferred_element_type=F32)

    def gated(gate_ref, val_ref):
        parts = []
        for r in range(0, tq, row_chunk):
            parts.append(_sigmoid_tanh(gate_ref[r:r + row_chunk, :].astype(F32)) * val_ref[r:r + row_chunk, :])
            yield
        return jnp.concatenate(parts, axis=0)

    def gate_only(gate_ref):
        parts = []
        for r in range(0, tq, row_chunk):
            parts.append(_sigmoid_tanh(gate_ref[r:r + row_chunk, :].astype(F32)))
            yield
        return jnp.concatenate(parts, axis=0)

    yx, y_a, y_b, g_x = _round_robin([cross_attention(), gated(m0_ref, ynsa_ref), gated(m1_ref, yrnn_ref),
                                      gate_only(m2_ref)])
    y = y_a + y_b + g_x * yx
    h_ref[...] = x_ref[...] + jnp.dot(y.astype(BF16), wo_ref[...], preferred_element_type=F32)


def _merge(x, z, ynsa, yrnn, kvm, wxo, wo, *, seq, mem_len, tq):
    t, d = x.shape
    nt = seq // tq
    row = lambda cb: pl.BlockSpec((tq, d), lambda i, cb=cb: (i, cb))
    mem_spec = lambda cb: pl.BlockSpec((mem_len, XATTN_WIDTH), lambda i, cb=cb: (i // nt, cb))
    return pl.pallas_call(
        _merge_kernel,
        grid=(t // tq,),
        in_specs=[
            row(0),
            row(OFF_MERGE // d), row(OFF_MERGE // d + 1), row(OFF_MERGE // d + 2),
            row(0), row(0),
            pl.BlockSpec((tq, XATTN_WIDTH), lambda i: (i, OFF_QX // XATTN_WIDTH)),
            mem_spec(0), mem_spec(1),
            pl.BlockSpec((XATTN_WIDTH, d), lambda i: (0, 0)),
            pl.BlockSpec((d, d), lambda i: (0, 0)),
        ],
        out_specs=row(0),
        out_shape=jax.ShapeDtypeStruct((t, d), F32),
        compiler_params=pltpu.CompilerParams(
            dimension_semantics=("parallel",), vmem_limit_bytes=VMEM_LIMIT),
        name="merge",
    )(x, z, z, z, ynsa, yrnn, z, kvm, kvm, wxo, wo)


def _mlp_kernel(h_ref, g_ref, wup_ref, wdn_ref, gf_ref, o_ref, v_scr, acc_scr):
    f = pl.program_id(1)

    @pl.when(f == 0)
    def _():
        h = h_ref[...]
        ms = jnp.mean(h * h, axis=-1, keepdims=True)
        v_scr[...] = (h * lax.rsqrt(ms + RMS_EPS) * g_ref[...]).astype(BF16)
        acc_scr[...] = jnp.zeros(acc_scr.shape, F32)

    up = jnp.dot(v_scr[...], wup_ref[...], preferred_element_type=F32)
    act = jnp.square(jnp.maximum(up, 0.0)).astype(BF16)
    acc_scr[...] += jnp.dot(act, wdn_ref[...], preferred_element_type=F32)

    @pl.when(f == pl.num_programs(1) - 1)
    def _():
        h2 = h_ref[...] + acc_scr[...]
        ms = jnp.mean(h2 * h2, axis=-1, keepdims=True)
        o_ref[...] = h2 * lax.rsqrt(ms + RMS_EPS) * gf_ref[...]


def _mlp(h, g, wup, wdn, gf, *, tm, tf):
    t, d = h.shape
    dff = wup.shape[1]
    return pl.pallas_call(
        _mlp_kernel,
        grid=(t // tm, dff // tf),
        in_specs=[
            pl.BlockSpec((tm, d), lambda i, f: (i, 0)),
            pl.BlockSpec((1, d), lambda i, f: (0, 0)),
            pl.BlockSpec((d, tf), lambda i, f: (0, f)),
            pl.BlockSpec((tf, d), lambda i, f: (f, 0)),
            pl.BlockSpec((1, d), lambda i, f: (0, 0)),
        ],
        out_specs=pl.BlockSpec((tm, d), lambda i, f: (i, 0)),
        out_shape=jax.ShapeDtypeStruct((t, d), F32),
        scratch_shapes=[pltpu.VMEM((tm, d), BF16), pltpu.VMEM((tm, d), F32)],
        compiler_params=pltpu.CompilerParams(
            dimension_semantics=("parallel", "arbitrary"), vmem_limit_bytes=VMEM_LIMIT),
        name="mlp",
    )(h, g, wup, wdn, gf)


def _rope_angles(pos):
    inv = 1.0 / (ROPE_THETA ** (jnp.arange(0, ROPE_DIM, 2, dtype=F32) / ROPE_DIM))
    ang = pos.astype(F32)[:, None] * inv[None, :]
    return jnp.cos(ang), jnp.sin(ang)


def _rope_tables(pos, width, rope_width=None):
    rope_width = width if rope_width is None else rope_width
    cos, sin = _rope_angles(pos)
    n = pos.shape[0]
    pad = HEAD_DIM - ROPE_DIM
    cos_h = jnp.concatenate([cos, cos, jnp.ones((n, pad), F32)], axis=1)
    sin_h = jnp.concatenate([-sin, sin, jnp.zeros((n, pad), F32)], axis=1)
    reps = rope_width // HEAD_DIM
    rest = width - rope_width
    cos_t = jnp.concatenate([jnp.tile(cos_h, (1, reps)), jnp.ones((n, rest), F32)], axis=1)
    sin_t = jnp.concatenate([jnp.tile(sin_h, (1, reps)), jnp.zeros((n, rest), F32)], axis=1)
    return cos_t, sin_t


def _pack_w_in(w_in):
    d = w_in.shape[0]
    w = w_in.astype(BF16)
    n_qkv = NSA_HEADS * HEAD_DIM + 6 * NSA_KV
    n_gates = NSA_HEADS * NSA_BRANCHES
    rnn_lo = n_qkv + n_gates
    qx_lo = rnn_lo + 2 * D_MODEL
    mg_lo = qx_lo + XATTN_WIDTH
    per_group = n_gates // NSA_GROUPS
    gates = w[:, n_qkv:rnn_lo].reshape(d, NSA_GROUPS, per_group)
    gates = jnp.pad(gates, ((0, 0), (0, 0), (0, GATE_STRIDE - per_group))).reshape(d, NSA_GROUPS * GATE_STRIDE)
    return jnp.concatenate([w[:, :n_qkv], w[:, qx_lo:mg_lo], gates, w[:, rnn_lo:qx_lo], w[:, mg_lo:]], axis=1)


def _block_diag_pairs(w):
    nb, k, _ = w.shape
    w = w.reshape(nb // 2, 2, k, k)
    zero = jnp.zeros((nb // 2, k, k), w.dtype)
    top = jnp.concatenate([w[:, 0], zero], axis=2)
    bot = jnp.concatenate([zero, w[:, 1]], axis=2)
    return jnp.concatenate([top, bot], axis=1).astype(BF16)


def _cmp_to_sel_t(n_cmp, n_cmp_pad, n_sel):
    c0 = np.arange(n_cmp_pad)[None, :] * CMP_STRIDE
    s0 = np.arange(n_sel)[:, None] * SEL_BLOCK
    ov = np.clip(np.minimum(c0 + CMP_LEN, s0 + SEL_BLOCK) - np.maximum(c0, s0), 0, None)
    ov = np.where(np.arange(n_cmp_pad)[None, :] < n_cmp, ov, 0)
    return (ov / CMP_LEN).astype(np.float32)


def _key_features(seq, n_sel):
    col = np.arange(LANES)[None, :]
    onehot = (np.arange(seq)[:, None] // SEL_BLOCK == col) & (col < n_sel)
    return jnp.asarray(onehot | (col == n_sel), dtype=BF16)


def kernel(x, mem, g_mix, w_in, cmp_pos_k, cmp_pos_v, w_cmp_k1, w_cmp_k2, w_cmp_v1, w_cmp_v2, conv_w, conv_b, w_rg_a, b_rg_a, w_rg_i, b_rg_i, rg_lambda, g_mem, w_mem_kv, w_xo, w_o, g_mlp, w_up, w_down, g_final):
    batch, seq, d = x.shape
    mem_len = mem.shape[1]
    t = batch * seq
    depth = g_mix.shape[0]
    n_sel = seq // SEL_BLOCK
    n_cmp_pad = seq // CMP_STRIDE
    assert n_cmp_pad == LANES and n_sel <= K_AUG - HEAD_DIM and d == D_MODEL

    pos = jnp.arange(seq)
    cos_k, sin_k = _rope_tables(pos, NSA_KV)
    cos_q, sin_q = (a.T for a in _rope_angles(pos))
    no_rope = jnp.zeros((mem_len, LANES), F32)
    cmp_pos = jnp.arange(n_cmp_pad) * CMP_STRIDE + (CMP_LEN - 1)
    cmp_tm = 4 * n_cmp_pad
    cos_c, sin_c = _rope_tables(cmp_pos, 2 * HEAD_DIM, HEAD_DIM)
    cos_c = jnp.tile(cos_c, (cmp_tm // n_cmp_pad, 2))
    sin_c = jnp.tile(sin_c, (cmp_tm // n_cmp_pad, 2))
    c2st = jnp.asarray(_cmp_to_sel_t((seq - CMP_LEN) // CMP_STRIDE + 1, n_cmp_pad, n_sel))
    feat = _key_features(seq, n_sel)

    h = x.reshape(t, d)
    for l in range(depth):
        z = _norm_proj(h, g_mix[l][None, :], _pack_w_in(w_in[l]), cos_k, sin_k, tm=1024, tn=INPROJ_TN,
                       rope_tiles=ROPE_TILES, out_dtype=BF16, name="inproj")

        pair_w = 2 * HEAD_DIM
        n_pairs = NSA_GROUPS // 2
        eye2 = jnp.eye(2, dtype=F32)

        def pair_rows(off):
            a = z[:, off:off + NSA_KV].reshape(batch * n_cmp_pad, CMP_STRIDE, n_pairs, pair_w)
            return a.transpose(2, 0, 1, 3).reshape(n_pairs, batch * n_cmp_pad, CMP_STRIDE * pair_w)

        def pair_w1(w1_half):
            w3 = w1_half.reshape(CMP_STRIDE, HEAD_DIM, CMP_HIDDEN)
            return jnp.einsum('jdc,hg->jhdgc', w3, eye2).reshape(CMP_STRIDE * pair_w, 2 * CMP_HIDDEN).astype(BF16)

        def pair_pos(p_half):
            return jnp.broadcast_to(p_half[:, None, :], (CMP_STRIDE, 2, HEAD_DIM)).reshape(1, CMP_STRIDE * pair_w)

        half = CMP_STRIDE * HEAD_DIM
        w2 = jnp.zeros((2, 2, CMP_HIDDEN, 2, 2, HEAD_DIM), F32)
        for grp in range(2):
            w2 = w2.at[0, grp, :, grp, 0, :].set(w_cmp_k2[l]).at[1, grp, :, grp, 1, :].set(w_cmp_v2[l])
        w2 = w2.reshape(4 * CMP_HIDDEN, 4 * HEAD_DIM)
        kvc = _compress(pair_rows(OFF_KC), pair_rows(OFF_VC),
                        pair_pos(cmp_pos_k[l][:CMP_STRIDE]), pair_pos(cmp_pos_k[l][CMP_STRIDE:]),
                        pair_pos(cmp_pos_v[l][:CMP_STRIDE]), pair_pos(cmp_pos_v[l][CMP_STRIDE:]),
                        pair_w1(w_cmp_k1[l][:half]), pair_w1(w_cmp_k1[l][half:]),
                        pair_w1(w_cmp_v1[l][:half]), pair_w1(w_cmp_v1[l][half:]),
                        w2.astype(BF16), cos_c, sin_c, tm=cmp_tm)
        y_nsa = _nsa(z, cos_q, sin_q, feat, kvc, c2st, batch=batch, seq=seq)

        y_rnn = _rglru(z, conv_w[l], conv_b[l][None, :], _block_diag_pairs(w_rg_a[l]), b_rg_a[l][None, :],
                       _block_diag_pairs(w_rg_i[l]), b_rg_i[l][None, :], rg_lambda[l][None, :],
                       batch=batch, seq=seq, ts=256)

        kvm = _norm_proj(mem.reshape(batch * mem_len, d), g_mem[l][None, :], w_mem_kv[l].astype(BF16),
                         no_rope, no_rope, tm=mem_len, tn=2 * XATTN_WIDTH, rope_tiles=(),
                         out_dtype=BF16, name="memkv")

        h1 = _merge(h, z, y_nsa, y_rnn, kvm, w_xo[l].astype(BF16), w_o[l].astype(BF16),
                    seq=seq, mem_len=mem_len, tq=512)
        last = l == depth - 1
        gf = g_final if last else jnp.ones_like(g_final)
        h = _mlp(h1, g_mlp[l][None, :], w_up[l].astype(BF16), w_down[l].astype(BF16), gf[None, :],
                 tm=1024, tf=1024)
        assert last, "final norm is fused into the last layer's MLP kernel"
    return h.reshape(batch, seq, d)
```

```python
import functools
import math

import numpy as np
import jax
import jax.numpy as jnp
from jax import lax
from jax.experimental import pallas as pl
from jax.experimental.pallas import tpu as pltpu

F32 = jnp.float32
BF16 = jnp.bfloat16

D_MODEL = 1024
HEAD_DIM = 64
NSA_HEADS = 16
NSA_GROUPS = 4
NSA_HPG = NSA_HEADS // NSA_GROUPS
NSA_BRANCHES = 3
CMP_LEN = 32
CMP_STRIDE = 16
CMP_HIDDEN = 256
SEL_BLOCK = 64
SEL_TOPK = 8
WINDOW = 512
Q_BLOCK = 128
ROPE_THETA = 500000.0
ROPE_DIM = HEAD_DIM // 4
ROPE_HALF = ROPE_DIM // 2
RNN_BLOCKS = 16
RNN_BLOCK_DIM = D_MODEL // RNN_BLOCKS
CONV_WIDTH = 4
RGLRU_C = 8.0
XATTN_HEADS = 4
XATTN_WIDTH = XATTN_HEADS * HEAD_DIM
D_FF = 4 * D_MODEL
RMS_EPS = 1e-6
SCALE = HEAD_DIM ** -0.5
LOG2E = math.log2(math.e)
NEG = -1e30

LANES = 128
SUBLANES = 8
BF16_ROWS = 16
VMEM_LIMIT = 48 * 1024 * 1024

NSA_KV = NSA_GROUPS * HEAD_DIM
OFF_Q = 0
OFF_KC = 1024
OFF_VC = 1280
OFF_KS = 1536
OFF_VS = 1792
OFF_KW = 2048
OFF_VW = 2304
OFF_QX = 2560
OFF_GATE = 2816
GATE_STRIDE = 64
OFF_XRNN = 3072
OFF_GRNN = 4096
OFF_MERGE = 5120
Z_WIDTH = 8192
INPROJ_TN = 2048
ROPE_TILES = tuple((off // INPROJ_TN, off % INPROJ_TN) for off in (OFF_KS, OFF_KW))

NSA_PAIRS = 2
NSA_GPS = 2 * NSA_PAIRS
NSA_LANES = NSA_HPG * Q_BLOCK
SEL_CHUNK = 4 * Q_BLOCK
SEL_PIECE = SEL_CHUNK
WIN_TILES = WINDOW // Q_BLOCK + 1
K_AUG = 2 * HEAD_DIM
V_AUG = HEAD_DIM + BF16_ROWS

NT_DIMS = (((1,), (1,)), ((), ()))


def _sigmoid(x):
    return 1.0 / (1.0 + jnp.exp(-x))


def _sigmoid_tanh(x):
    return 0.5 * jnp.tanh(0.5 * x) + 0.5


def _gelu_tanh(x):
    return 0.5 * x * (1.0 + jnp.tanh(0.7978845608028654 * (x + 0.044715 * (x * x * x))))


def _rope_swap(z):
    n = z.shape[-1]
    lane = lax.broadcasted_iota(jnp.int32, z.shape, z.ndim - 1)
    first_half = (lane & (HEAD_DIM - 1)) < ROPE_HALF
    return jnp.where(first_half, pltpu.roll(z, n - ROPE_HALF, z.ndim - 1), pltpu.roll(z, ROPE_HALF, z.ndim - 1))


def _round_robin(gens, lead=0):
    results = [None] * len(gens)
    alive = list(range(len(gens)))

    def advance(idx):
        try:
            next(gens[idx])
        except StopIteration as stop:
            results[idx] = stop.value
            alive.remove(idx)

    for _ in range(lead):
        for idx in [n for n in alive if n < len(gens) // 2]:
            advance(idx)
    while alive:
        for idx in list(alive):
            advance(idx)
    return results


def _fold_rows(x, op):
    parts = [x[r * SUBLANES:(r + 1) * SUBLANES] for r in range(x.shape[0] // SUBLANES)]
    return functools.reduce(op, parts)


def _norm_proj_kernel(x_ref, g_ref, w_ref, cos_ref, sin_ref, z_ref, u_scr, *, rope_tiles):
    j = pl.program_id(1)

    @pl.when(j == 0)
    def _():
        x = x_ref[...]
        ms = jnp.mean(x * x, axis=-1, keepdims=True)
        u_scr[...] = (x * lax.rsqrt(ms + RMS_EPS) * g_ref[...]).astype(BF16)

    z = jnp.dot(u_scr[...], w_ref[...], preferred_element_type=F32)
    z_ref[...] = z.astype(z_ref.dtype)
    rw = cos_ref.shape[1]
    for tile, col in rope_tiles:
        @pl.when(j == tile)
        def _(col=col):
            zr = z[:, col:col + rw]
            z_ref[:, col:col + rw] = (zr * cos_ref[...] + _rope_swap(zr) * sin_ref[...]).astype(z_ref.dtype)


def _norm_proj(x, g, w, cos_t, sin_t, *, tm, tn, rope_tiles, out_dtype, name):
    t, d = x.shape
    n = w.shape[1]
    s_tiles = cos_t.shape[0] // tm
    rw = cos_t.shape[1]
    table_spec = pl.BlockSpec((tm, rw), lambda i, j: (i % s_tiles, 0))
    return pl.pallas_call(
        functools.partial(_norm_proj_kernel, rope_tiles=tuple(rope_tiles)),
        grid=(t // tm, n // tn),
        in_specs=[
            pl.BlockSpec((tm, d), lambda i, j: (i, 0)),
            pl.BlockSpec((1, d), lambda i, j: (0, 0)),
            pl.BlockSpec((d, tn), lambda i, j: (0, j)),
            table_spec, table_spec,
        ],
        out_specs=pl.BlockSpec((tm, tn), lambda i, j: (i, j)),
        out_shape=jax.ShapeDtypeStruct((t, n), out_dtype),
        scratch_shapes=[pltpu.VMEM((tm, d), BF16)],
        compiler_params=pltpu.CompilerParams(
            dimension_semantics=("parallel", "arbitrary"), vmem_limit_bytes=VMEM_LIMIT),
        name=name,
    )(x, g, w, cos_t, sin_t)


def _compress_kernel(rk_ref, rv_ref, pka_ref, pkb_ref, pva_ref, pvb_ref,
                     wk1a_ref, wk1b_ref, wv1a_ref, wv1b_ref, w2_ref, cos_ref, sin_ref, o_ref):
    m = rk_ref.shape[0]

    def hidden(r_ref, pa_ref, pb_ref, w1a_ref, w1b_ref):
        r = r_ref[...].astype(F32)
        pa = jnp.dot((r + pa_ref[...]).astype(BF16), w1a_ref[...], preferred_element_type=F32)
        pb = jnp.dot((r + pb_ref[...]).astype(BF16), w1b_ref[...], preferred_element_type=F32)
        return _gelu_tanh(pa + pltpu.roll(pb, m - 1, 0)).astype(BF16)

    hk = hidden(rk_ref, pka_ref, pkb_ref, wk1a_ref, wk1b_ref)
    hv = hidden(rv_ref, pva_ref, pvb_ref, wv1a_ref, wv1b_ref)
    hcat = jnp.concatenate([hk, hv], axis=1)
    kv = jnp.dot(hcat, w2_ref[...], preferred_element_type=F32)
    o_ref[...] = kv * cos_ref[...] + _rope_swap(kv) * sin_ref[...]


def _compress(rk, rv, pka, pkb, pva, pvb, wk1a, wk1b, wv1a, wv1b, w2, cos_t, sin_t, *, tm):
    pairs, m, k = rk.shape
    hid = wk1a.shape[1]
    out_w = w2.shape[1]
    full = lambda shape: pl.BlockSpec(shape, lambda p, i: (0, 0))
    rows = pl.BlockSpec((None, tm, k), lambda p, i: (p, i, 0))
    return pl.pallas_call(
        _compress_kernel,
        grid=(pairs, m // tm),
        in_specs=[
            rows, rows,
            full((1, k)), full((1, k)), full((1, k)), full((1, k)),
            full((k, hid)), full((k, hid)), full((k, hid)), full((k, hid)),
            full((2 * hid, out_w)),
            full((tm, out_w)), full((tm, out_w)),
        ],
        out_specs=pl.BlockSpec((tm, out_w), lambda p, i: (i, p)),
        out_shape=jax.ShapeDtypeStruct((m, pairs * out_w), F32),
        compiler_params=pltpu.CompilerParams(
            dimension_semantics=("parallel", "parallel"), vmem_limit_bytes=VMEM_LIMIT),
        name="compress",
    )(rk, rv, pka, pkb, pva, pvb, wk1a, wk1b, wv1a, wv1b, w2, cos_t, sin_t)


def _nsa_kernel(q_ref, g_ref, cos_ref, sin_ref, *rest, n_sel):
    kv_refs, rest = rest[:4 * NSA_PAIRS], rest[4 * NSA_PAIRS:]
    feat_ref, kvc_ref, c2st_ref, o_ref, gt_scr, s_scr, vt_scr = rest
    ks_refs, vs_refs, kw_refs, vw_refs = (kv_refs[n::4] for n in range(4))
    g2 = pl.program_id(1)
    i = pl.program_id(2)
    q0 = i * Q_BLOCK
    seq = feat_ref.shape[0]

    @pl.when(i == 0)
    def _():
        ones_rows = jnp.where(lax.broadcasted_iota(jnp.int32, (V_AUG - HEAD_DIM, Q_BLOCK), 0) == 0,
                              1.0, 0.0).astype(BF16)

        def fill(t, carry):
            r0 = pl.multiple_of(t * Q_BLOCK, Q_BLOCK)
            for kind, refs in enumerate((vs_refs, vw_refs)):
                for pair, ref in enumerate(refs):
                    vt = jnp.transpose(ref[pl.ds(r0, Q_BLOCK), :].astype(F32)).astype(BF16)
                    for half in range(2):
                        gg = 2 * pair + half
                        vt_scr[kind, gg, t, 0:HEAD_DIM, :] = vt[half * HEAD_DIM:(half + 1) * HEAD_DIM]
                        vt_scr[kind, gg, t, HEAD_DIM:V_AUG, :] = ones_rows
            return carry

        lax.fori_loop(0, seq // Q_BLOCK, fill, 0)

    def query_side(gg, qt, bias_rows, win_row):
        zeros_q = jnp.zeros((HEAD_DIM, NSA_LANES), BF16)
        q_rows = [qt, zeros_q] if gg % 2 == 0 else [zeros_q, qt]
        rid = lax.broadcasted_iota(jnp.int32, (BF16_ROWS, NSA_LANES), 0)
        win_rows = jnp.where(rid == 0, win_row, 0.0).astype(BF16)
        tail = jnp.zeros((2 * LANES - 2 * HEAD_DIM - n_sel - BF16_ROWS, NSA_LANES), BF16)
        return jnp.concatenate(q_rows + [bias_rows.astype(BF16), win_rows, tail], axis=0)

    def key_side(k_ref, row0, rows):
        return jnp.concatenate([k_ref[pl.ds(row0, rows), :], feat_ref[pl.ds(row0, rows), :]], axis=1)
    blocks_per_chunk = SEL_CHUNK // SEL_BLOCK
    q_per_chunk = SEL_CHUNK // Q_BLOCK

    def tile_heads(x):
        return jnp.concatenate([x] * NSA_HPG, axis=1)

    tq = q0 + (lax.broadcasted_iota(jnp.int32, (1, NSA_LANES), 1) & (Q_BLOCK - 1))
    c_idx = lax.broadcasted_iota(jnp.int32, (Q_BLOCK, 1), 0)
    r_idx = lax.broadcasted_iota(jnp.int32, (1, Q_BLOCK), 1)
    tri_diag = tile_heads(jnp.where(c_idx <= r_idx, 0.0, NEG))
    tri_old = tile_heads(jnp.where(c_idx > r_idx, 0.0, NEG))
    cos_q = tile_heads(cos_ref[...])
    sin_q = tile_heads(sin_ref[...])
    blk = lax.broadcasted_iota(jnp.int32, (n_sel, Q_BLOCK), 0)
    cur = (q0 + lax.broadcasted_iota(jnp.int32, (n_sel, Q_BLOCK), 1)) // SEL_BLOCK
    sub = lax.broadcasted_iota(jnp.int32, (SUBLANES, Q_BLOCK), 0)

    gt_scr[...] = jnp.transpose(_sigmoid(g_ref[...].astype(F32)))

    def prepare(gg):
        qx = jnp.transpose(q_ref[:, gg * NSA_HPG * HEAD_DIM:(gg + 1) * NSA_HPG * HEAD_DIM].astype(F32)
                           * (SCALE * LOG2E))
        qf = jnp.concatenate([qx[h * HEAD_DIM:(h + 1) * HEAD_DIM, :] for h in range(NSA_HPG)], axis=1)
        x1, x2 = qf[0:ROPE_HALF], qf[ROPE_HALF:ROPE_DIM]
        qt = jnp.concatenate([x1 * cos_q - x2 * sin_q, x2 * cos_q + x1 * sin_q, qf[ROPE_DIM:]],
                             axis=0).astype(BF16)
        qts[gg] = qt
        yield

        kvc = kvc_ref[:, gg * LANES:(gg + 1) * LANES]
        kc = kvc[:, :HEAD_DIM].astype(BF16)
        vct = jnp.transpose(kvc)[HEAD_DIM:, :].astype(BF16)
        sc = jnp.dot(kc, qt, preferred_element_type=F32)
        yield
        n_idx = lax.broadcasted_iota(jnp.int32, (LANES, 1), 0)
        sc = jnp.where((n_idx * CMP_STRIDE + (CMP_LEN - 1)) <= tq, sc, NEG)
        m_c = jnp.max(sc, axis=0, keepdims=True)
        e_c = jnp.exp2(sc - m_c)
        inv_c = 1.0 / jnp.maximum(jnp.sum(e_c, axis=0, keepdims=True), 1e-30)
        p_c = e_c * jnp.where(m_c > 0.5 * NEG, inv_c, 0.0)
        o_c = jnp.dot(vct, p_c.astype(BF16), preferred_element_type=F32)
        yield

        p_sum = p_c[:, 0:Q_BLOCK]
        for h in range(1, NSA_HPG):
            p_sum = p_sum + p_c[:, h * Q_BLOCK:(h + 1) * Q_BLOCK]
        imp = jnp.dot(c2st_ref[...], p_sum, preferred_element_type=F32,
                      precision=lax.Precision.HIGHEST)
        yield
        forced = (blk == 0) | (blk == cur) | (blk == cur - 1)
        val = jnp.where(forced, jnp.inf, jnp.where(blk > cur, -jnp.inf, imp))
        n_grp = n_sel // SUBLANES
        grp = [val[v * SUBLANES:(v + 1) * SUBLANES] for v in range(n_grp)]
        cnt = [jnp.zeros((SUBLANES, Q_BLOCK), F32) for _ in range(n_grp)]
        for sp in range(n_sel):
            r = val[sp:sp + 1, :]
            for v in range(n_grp):
                if v * SUBLANES > sp:
                    cnt[v] = cnt[v] + jnp.where(r >= grp[v], 1.0, 0.0)
                elif v * SUBLANES + SUBLANES - 1 <= sp:
                    cnt[v] = cnt[v] + jnp.where(r > grp[v], 1.0, 0.0)
                else:
                    later = jnp.where(sub > sp - v * SUBLANES, 1.0, 0.0)
                    cnt[v] = cnt[v] + jnp.where(r > grp[v], 1.0, jnp.where(r == grp[v], later, 0.0))
            if sp % SUBLANES == SUBLANES - 1:
                yield
        chosen = jnp.concatenate([jnp.where(c < float(SEL_TOPK), 0.0, NEG) for c in cnt], axis=0)
        block_bias = tile_heads(jnp.where(blk > cur, NEG, chosen))
        row0 = pl.multiple_of((g2 * NSA_GPS + gg) * GATE_STRIDE, GATE_STRIDE)
        gt = gt_scr[pl.ds(row0, 2 * SUBLANES), :]
        return qt, o_c, block_bias, gt

    def window(gg):
        qt = qts[gg]
        tiles = []
        for d in range(1 - WIN_TILES, 1):
            td = i + d
            tc = jnp.maximum(td, 0)
            k0 = pl.multiple_of(tc * Q_BLOCK, Q_BLOCK)
            before_start = jnp.where(td >= 0, 0.0, NEG) if d < 0 else 0.0
            tri = tri_diag if d == 0 else (tri_old if d == 1 - WIN_TILES else None)
            tiles.append((tc, k0, before_start, tri))

        zero_row = jnp.zeros((1, NSA_LANES), F32)
        no_bias = jnp.zeros((n_sel, NSA_LANES), BF16)
        scores = []
        for tc, k0, before_start, tri in tiles:
            q_aug = query_side(gg, qt, no_bias, zero_row + before_start)
            s = jnp.dot(key_side(kw_refs[gg // 2], k0, Q_BLOCK), q_aug, preferred_element_type=F32)
            scores.append(s if tri is None else s + tri)
            yield
        m_w = jnp.max(functools.reduce(jnp.maximum, [_fold_rows(s, jnp.maximum) for s in scores]),
                      axis=0, keepdims=True)
        n_pad = jnp.maximum(WINDOW - 1 - tq, 0).astype(F32)
        m_w = jnp.where(n_pad > 0.0, jnp.maximum(m_w, 0.0), m_w)
        yield
        acc = jnp.zeros((V_AUG, NSA_LANES), F32)
        for (tc, _, _, _), s in zip(tiles, scores):
            acc = acc + jnp.dot(vt_scr[1, gg, tc], jnp.exp2(s - m_w).astype(BF16), preferred_element_type=F32)
            yield
        l_w = acc[HEAD_DIM:HEAD_DIM + 1] + n_pad * jnp.exp2(-m_w)
        return acc[:HEAD_DIM] * (1.0 / jnp.maximum(l_w, 1e-30))

    pieces_per_chunk = SEL_CHUNK // SEL_PIECE
    tiles_per_piece = SEL_PIECE // Q_BLOCK
    blocks_per_piece = SEL_PIECE // SEL_BLOCK

    def chunk_scores(gg, c, diag_rel):
        zeros_q = jnp.zeros((HEAD_DIM, NSA_LANES), BF16)
        q_rows = jnp.concatenate([qts[gg], zeros_q] if gg % 2 == 0 else [zeros_q, qts[gg]], axis=0)
        bmax = []
        for piece in range(c * pieces_per_chunk, (c + 1) * pieces_per_chunk):
            rows = slice(piece * SEL_PIECE, (piece + 1) * SEL_PIECE)
            s = jnp.dot(ks_refs[gg // 2][rows, :], q_rows, preferred_element_type=F32)
            if diag_rel is not None:
                first_tile = (piece - c * pieces_per_chunk) * tiles_per_piece
                s = s + jnp.concatenate([jnp.where(diag_rel == first_tile + u, tri_diag, 0.0)
                                         for u in range(tiles_per_piece)], axis=0)
            s_scr[gg, rows, :] = s
            bmax += [_fold_rows(s[b * SEL_BLOCK:(b + 1) * SEL_BLOCK], jnp.maximum) for b in range(blocks_per_piece)]
            yield
        return bmax

    def first_chunk(gg):
        yield
        return (yield from chunk_scores(gg, 0, i))

    def selected(gg, k, block_bias, bmax0):
        bmax = list(bmax0)
        for c in range(1, k + 1):
            bmax += yield from chunk_scores(gg, c, i - k * q_per_chunk if c == k else None)
        m8 = functools.reduce(jnp.maximum, [bm + block_bias[n:n + 1, :] for n, bm in enumerate(bmax)])
        m_s = jnp.max(m8, axis=0, keepdims=True)
        shift = block_bias - m_s
        yield
        acc = jnp.zeros((V_AUG, NSA_LANES), F32)
        for piece in range((k + 1) * pieces_per_chunk):
            ps = []
            for b in range(blocks_per_piece):
                n = piece * blocks_per_piece + b
                ps.append(jnp.exp2(s_scr[gg, n * SEL_BLOCK:(n + 1) * SEL_BLOCK, :] + shift[n:n + 1, :]).astype(BF16))
            vt = jnp.concatenate([vt_scr[0, gg, piece * tiles_per_piece + u] for u in range(tiles_per_piece)], axis=1)
            acc = acc + jnp.dot(vt, jnp.concatenate(ps, axis=0), preferred_element_type=F32)
            yield
        l_s = acc[HEAD_DIM:HEAD_DIM + 1]
        return acc[:HEAD_DIM] * (1.0 / jnp.maximum(l_s, 1e-30))

    qts = [None] * NSA_GPS
    groups = range(NSA_GPS)
    shared = _round_robin([prepare(gg) for gg in groups] + [window(gg) for gg in groups]
                          + [first_chunk(gg) for gg in groups])
    prepared, windows, bmax0 = (shared[n * NSA_GPS:(n + 1) * NSA_GPS] for n in range(3))

    def finish(k):
        branches = []
        for gg, (qt, o_c, block_bias, gt) in enumerate(prepared):
            branches.append(selected(gg, k, block_bias, bmax0[gg]))
        results = _round_robin(branches, lead=k + 1)
        outs = []
        for gg, (qt, o_c, block_bias, gt) in enumerate(prepared):
            o_w, o_s = windows[gg], results[gg]
            for h in range(NSA_HPG):
                sl = slice(h * Q_BLOCK, (h + 1) * Q_BLOCK)
                row = NSA_BRANCHES * h
                outs.append(gt[row:row + 1, :] * o_c[:, sl] + gt[row + 1:row + 2, :] * o_s[:, sl]
                            + gt[row + 2:row + 3, :] * o_w[:, sl])
        o_ref[...] = jnp.transpose(jnp.concatenate(outs, axis=0))

    for k in range(n_sel * SEL_BLOCK // SEL_CHUNK):
        pl.when(i // q_per_chunk == k)(functools.partial(finish, k))


def _nsa(z, cos_q, sin_q, feat, kvc, c2st, *, batch, seq):
    nq = seq // Q_BLOCK
    n_sel = seq // SEL_BLOCK
    gw = NSA_GPS * NSA_HPG * HEAD_DIM
    gates_w = NSA_GROUPS * GATE_STRIDE
    pair_w = 2 * HEAD_DIM
    assert pair_w == LANES
    kv_spec = lambda off, pair: pl.BlockSpec(
        (seq, pair_w), lambda b, g, i: (b, off // pair_w + g * NSA_PAIRS + pair))
    kv_specs = [kv_spec(off, pair) for pair in range(NSA_PAIRS) for off in (OFF_KS, OFF_VS, OFF_KW, OFF_VW)]
    rope_spec = pl.BlockSpec((ROPE_HALF, Q_BLOCK), lambda b, g, i: (0, i))
    return pl.pallas_call(
        functools.partial(_nsa_kernel, n_sel=n_sel),
        grid=(batch, NSA_GROUPS // NSA_GPS, nq),
        in_specs=[
            pl.BlockSpec((Q_BLOCK, gw), lambda b, g, i: (b * nq + i, OFF_Q // gw + g)),
            pl.BlockSpec((Q_BLOCK, gates_w), lambda b, g, i: (b * nq + i, OFF_GATE // gates_w)),
            rope_spec, rope_spec,
            *kv_specs,
            pl.BlockSpec((seq, LANES), lambda b, g, i: (0, 0)),
            pl.BlockSpec((LANES, NSA_GPS * 2 * HEAD_DIM), lambda b, g, i: (b, g)),
            pl.BlockSpec((n_sel, LANES), lambda b, g, i: (0, 0)),
        ],
        out_specs=pl.BlockSpec((Q_BLOCK, gw), lambda b, g, i: (b * nq + i, g)),
        out_shape=jax.ShapeDtypeStruct((batch * seq, NSA_HEADS * HEAD_DIM), F32),
        scratch_shapes=[pltpu.VMEM((gates_w, Q_BLOCK), F32), pltpu.VMEM((NSA_GPS, seq, NSA_LANES), F32),
                        pltpu.VMEM((2, NSA_GPS, nq, V_AUG, Q_BLOCK), BF16)],
        compiler_params=pltpu.CompilerParams(
            dimension_semantics=("arbitrary", "arbitrary", "arbitrary"), vmem_limit_bytes=VMEM_LIMIT),
        name="nsa",
    )(z, z, cos_q, sin_q, *([z] * len(kv_specs)), feat, kvc, c2st)


def _rglru_kernel(x_ref, gr_ref, cw_ref, cb_ref, wa_ref, ba_ref, wi_ref, bi_ref, lam_ref, y_ref,
                  xs_scr, h_scr):
    ts, c = x_ref.shape

    @pl.when(pl.program_id(1) == 0)
    def _():
        xs_scr[0:SUBLANES, :] = jnp.zeros((SUBLANES, c), F32)
        h_scr[...] = jnp.zeros(h_scr.shape, F32)

    x = x_ref[...].astype(F32)
    xs_scr[SUBLANES:SUBLANES + ts, :] = x
    xc = cb_ref[...] + cw_ref[CONV_WIDTH - 1:CONV_WIDTH, :] * x
    for k in range(CONV_WIDTH - 1):
        off = SUBLANES - (CONV_WIDTH - 1) + k
        xc = xc + cw_ref[k:k + 1, :] * xs_scr[off:off + ts, :]
    xs_scr[0:SUBLANES, :] = x[ts - SUBLANES:, :]

    ra, ri = [], []
    for mblk in range(c // LANES):
        xb = xc[:, mblk * LANES:(mblk + 1) * LANES].astype(BF16)
        ra.append(jnp.dot(xb, wa_ref[mblk], preferred_element_type=F32))
        ri.append(jnp.dot(xb, wi_ref[mblk], preferred_element_type=F32))
    r = _sigmoid(jnp.concatenate(ra, axis=1) + ba_ref[...])
    gi = _sigmoid(jnp.concatenate(ri, axis=1) + bi_ref[...])

    nl = -lam_ref[...]
    softplus = jnp.maximum(nl, 0.0) + jnp.log1p(jnp.exp(-jnp.abs(nl)))
    log_a = (-RGLRU_C) * r * softplus
    a = jnp.exp(log_a)
    th = jnp.tanh(log_a)
    y2 = -2.0 * th / (1.0 - th)
    b = jnp.where(y2 > 0.0, y2 * lax.rsqrt(y2), 0.0) * (gi * xc)

    groups = ts // SUBLANES
    a = a.reshape(groups, SUBLANES, c)
    b = b.reshape(groups, SUBLANES, c)
    in_group = lax.broadcasted_iota(jnp.int32, (groups, SUBLANES, c), 1)
    d = 1
    while d < SUBLANES:
        keep = in_group >= d
        a_sh = jnp.where(keep, pltpu.roll(a, d, 1), 1.0)
        b_sh = jnp.where(keep, pltpu.roll(b, d, 1), 0.0)
        b = a * b_sh + b
        a = a * a_sh
        d *= 2
    carry = h_scr[...]
    hs = []
    for r in range(groups):
        hg = b[r] + a[r] * carry
        hs.append(hg)
        carry = hg[SUBLANES - 1:SUBLANES, :]
    h_scr[...] = carry
    y_ref[...] = _gelu_tanh(gr_ref[...].astype(F32)) * jnp.concatenate(hs, axis=0)


def _rglru(z, cw, cb, wa2, ba, wi2, bi, lam, *, batch, seq, ts):
    c = D_MODEL
    nt = seq // ts
    vec = pl.BlockSpec((1, c), lambda b, t: (0, 0))
    return pl.pallas_call(
        _rglru_kernel,
        grid=(batch, nt),
        in_specs=[
            pl.BlockSpec((ts, c), lambda b, t: (b * nt + t, OFF_XRNN // c)),
            pl.BlockSpec((ts, c), lambda b, t: (b * nt + t, OFF_GRNN // c)),
            pl.BlockSpec((CONV_WIDTH, c), lambda b, t: (0, 0)),
            vec,
            pl.BlockSpec((c // LANES, LANES, LANES), lambda b, t: (0, 0, 0)),
            vec,
            pl.BlockSpec((c // LANES, LANES, LANES), lambda b, t: (0, 0, 0)),
            vec, vec,
        ],
        out_specs=pl.BlockSpec((ts, c), lambda b, t: (b * nt + t, 0)),
        out_shape=jax.ShapeDtypeStruct((batch * seq, c), F32),
        scratch_shapes=[pltpu.VMEM((SUBLANES + ts, c), F32), pltpu.VMEM((1, c), F32)],
        compiler_params=pltpu.CompilerParams(
            dimension_semantics=("parallel", "arbitrary"), vmem_limit_bytes=VMEM_LIMIT),
        name="rglru",
    )(z, z, cw, cb, wa2, ba, wi2, bi, lam)


def _merge_kernel(x_ref, m0_ref, m1_ref, m2_ref, ynsa_ref, yrnn_ref, qx_ref, km_ref, vm_ref, wxo_ref, wo_ref, h_ref):
    tq = x_ref.shape[0]
    mem_len = km_ref.shape[0]
    row_chunk = tq // 4

    def cross_attention():
        qt = jnp.transpose(qx_ref[...].astype(F32) * (SCALE * LOG2E))
        row_head = lax.broadcasted_iota(jnp.int32, qt.shape, 0) // HEAD_DIM
        q_bd = jnp.concatenate([jnp.where(row_head == h, qt, 0.0) for h in range(XATTN_HEADS)],
                               axis=1).astype(BF16)
        yield
        s = jnp.dot(km_ref[...], q_bd, preferred_element_type=F32)
        yield
        m = jnp.max(_fold_rows(s, jnp.maximum), axis=0, keepdims=True)
        p = jnp.exp2(s - m).astype(BF16)
        yield
        ones_rows = jnp.where(lax.broadcasted_iota(jnp.int32, (BF16_ROWS, mem_len), 0) == 0, 1.0, 0.0)
        vt = jnp.concatenate([jnp.transpose(vm_ref[...].astype(F32)), ones_rows], axis=0).astype(BF16)
        o_all = jnp.dot(vt, p, preferred_element_type=F32)
        yield
        outs = []
        for h in range(XATTN_HEADS):
            lanes = slice(h * tq, (h + 1) * tq)
            inv_l = 1.0 / o_all[XATTN_WIDTH:XATTN_WIDTH + 1, lanes]
            outs.append(o_all[h * HEAD_DIM:(h + 1) * HEAD_DIM, lanes] * inv_l)
        o = jnp.transpose(jnp.concatenate(outs, axis=0)).astype(BF16)
        yield
        return jnp.dot(o, wxo_ref[...], preferred_element_type=F32)

    def gated(gate_ref, val_ref):
        parts = []
        for r in range(0, tq, row_chunk):
            parts.append(_sigmoid_tanh(gate_ref[r:r + row_chunk, :].astype(F32)) * val_ref[r:r + row_chunk, :])
            yield
        return jnp.concatenate(parts, axis=0)

    def gate_only(gate_ref):
        parts = []
        for r in range(0, tq, row_chunk):
            parts.append(_sigmoid_tanh(gate_ref[r:r + row_chunk, :].astype(F32)))
            yield
        return jnp.concatenate(parts, axis=0)

    yx, y_a, y_b, g_x = _round_robin([cross_attention(), gated(m0_ref, ynsa_ref), gated(m1_ref, yrnn_ref),
                                      gate_only(m2_ref)])
    y = y_a + y_b + g_x * yx
    h_ref[...] = x_ref[...] + jnp.dot(y.astype(BF16), wo_ref[...], preferred_element_type=F32)


def _merge(x, z, ynsa, yrnn, kvm, wxo, wo, *, seq, mem_len, tq):
    t, d = x.shape
    nt = seq // tq
    row = lambda cb: pl.BlockSpec((tq, d), lambda i, cb=cb: (i, cb))
    mem_spec = lambda cb: pl.BlockSpec((mem_len, XATTN_WIDTH), lambda i, cb=cb: (i // nt, cb))
    return pl.pallas_call(
        _merge_kernel,
        grid=(t // tq,),
        in_specs=[
            row(0),
            row(OFF_MERGE // d), row(OFF_MERGE // d + 1), row(OFF_MERGE // d + 2),
            row(0), row(0),
            pl.BlockSpec((tq, XATTN_WIDTH), lambda i: (i, OFF_QX // XATTN_WIDTH)),
            mem_spec(0), mem_spec(1),
            pl.BlockSpec((XATTN_WIDTH, d), lambda i: (0, 0)),
            pl.BlockSpec((d, d), lambda i: (0, 0)),
        ],
        out_specs=row(0),
        out_shape=jax.ShapeDtypeStruct((t, d), F32),
        compiler_params=pltpu.CompilerParams(
            dimension_semantics=("parallel",), vmem_limit_bytes=VMEM_LIMIT),
        name="merge",
    )(x, z, z, z, ynsa, yrnn, z, kvm, kvm, wxo, wo)


def _mlp_kernel(h_ref, g_ref, wup_ref, wdn_ref, gf_ref, o_ref, v_scr, acc_scr):
    f = pl.program_id(1)

    @pl.when(f == 0)
    def _():
        h = h_ref[...]
        ms = jnp.mean(h * h, axis=-1, keepdims=True)
        v_scr[...] = (h * lax.rsqrt(ms + RMS_EPS) * g_ref[...]).astype(BF16)
        acc_scr[...] = jnp.zeros(acc_scr.shape, F32)

    up = jnp.dot(v_scr[...], wup_ref[...], preferred_element_type=F32)
    act = jnp.square(jnp.maximum(up, 0.0)).astype(BF16)
    acc_scr[...] += jnp.dot(act, wdn_ref[...], preferred_element_type=F32)

    @pl.when(f == pl.num_programs(1) - 1)
    def _():
        h2 = h_ref[...] + acc_scr[...]
        ms = jnp.mean(h2 * h2, axis=-1, keepdims=True)
        o_ref[...] = h2 * lax.rsqrt(ms + RMS_EPS) * gf_ref[...]


def _mlp(h, g, wup, wdn, gf, *, tm, tf):
    t, d = h.shape
    dff = wup.shape[1]
    return pl.pallas_call(
        _mlp_kernel,
        grid=(t // tm, dff // tf),
        in_specs=[
            pl.BlockSpec((tm, d), lambda i, f: (i, 0)),
            pl.BlockSpec((1, d), lambda i, f: (0, 0)),
            pl.BlockSpec((d, tf), lambda i, f: (0, f)),
            pl.BlockSpec((tf, d), lambda i, f: (f, 0)),
            pl.BlockSpec((1, d), lambda i, f: (0, 0)),
        ],
        out_specs=pl.BlockSpec((tm, d), lambda i, f: (i, 0)),
        out_shape=jax.ShapeDtypeStruct((t, d), F32),
        scratch_shapes=[pltpu.VMEM((tm, d), BF16), pltpu.VMEM((tm, d), F32)],
        compiler_params=pltpu.CompilerParams(
            dimension_semantics=("parallel", "arbitrary"), vmem_limit_bytes=VMEM_LIMIT),
        name="mlp",
    )(h, g, wup, wdn, gf)


def _rope_angles(pos):
    inv = 1.0 / (ROPE_THETA ** (jnp.arange(0, ROPE_DIM, 2, dtype=F32) / ROPE_DIM))
    ang = pos.astype(F32)[:, None] * inv[None, :]
    return jnp.cos(ang), jnp.sin(ang)


def _rope_tables(pos, width, rope_width=None):
    rope_width = width if rope_width is None else rope_width
    cos, sin = _rope_angles(pos)
    n = pos.shape[0]
    pad = HEAD_DIM - ROPE_DIM
    cos_h = jnp.concatenate([cos, cos, jnp.ones((n, pad), F32)], axis=1)
    sin_h = jnp.concatenate([-sin, sin, jnp.zeros((n, pad), F32)], axis=1)
    reps = rope_width // HEAD_DIM
    rest = width - rope_width
    cos_t = jnp.concatenate([jnp.tile(cos_h, (1, reps)), jnp.ones((n, rest), F32)], axis=1)
    sin_t = jnp.concatenate([jnp.tile(sin_h, (1, reps)), jnp.zeros((n, rest), F32)], axis=1)
    return cos_t, sin_t


def _pack_w_in(w_in):
    d = w_in.shape[0]
    w = w_in.astype(BF16)
    n_qkv = NSA_HEADS * HEAD_DIM + 6 * NSA_KV
    n_gates = NSA_HEADS * NSA_BRANCHES
    rnn_lo = n_qkv + n_gates
    qx_lo = rnn_lo + 2 * D_MODEL
    mg_lo = qx_lo + XATTN_WIDTH
    per_group = n_gates // NSA_GROUPS
    gates = w[:, n_qkv:rnn_lo].reshape(d, NSA_GROUPS, per_group)
    gates = jnp.pad(gates, ((0, 0), (0, 0), (0, GATE_STRIDE - per_group))).reshape(d, NSA_GROUPS * GATE_STRIDE)
    return jnp.concatenate([w[:, :n_qkv], w[:, qx_lo:mg_lo], gates, w[:, rnn_lo:qx_lo], w[:, mg_lo:]], axis=1)


def _block_diag_pairs(w):
    nb, k, _ = w.shape
    w = w.reshape(nb // 2, 2, k, k)
    zero = jnp.zeros((nb // 2, k, k), w.dtype)
    top = jnp.concatenate([w[:, 0], zero], axis=2)
    bot = jnp.concatenate([zero, w[:, 1]], axis=2)
    return jnp.concatenate([top, bot], axis=1).astype(BF16)


def _cmp_to_sel_t(n_cmp, n_cmp_pad, n_sel):
    c0 = np.arange(n_cmp_pad)[None, :] * CMP_STRIDE
    s0 = np.arange(n_sel)[:, None] * SEL_BLOCK
    ov = np.clip(np.minimum(c0 + CMP_LEN, s0 + SEL_BLOCK) - np.maximum(c0, s0), 0, None)
    ov = np.where(np.arange(n_cmp_pad)[None, :] < n_cmp, ov, 0)
    return (ov / CMP_LEN).astype(np.float32)


def _key_features(seq, n_sel):
    col = np.arange(LANES)[None, :]
    onehot = (np.arange(seq)[:, None] // SEL_BLOCK == col) & (col < n_sel)
    return jnp.asarray(onehot | (col == n_sel), dtype=BF16)


def kernel(x, mem, g_mix, w_in, cmp_pos_k, cmp_pos_v, w_cmp_k1, w_cmp_k2, w_cmp_v1, w_cmp_v2, conv_w, conv_b, w_rg_a, b_rg_a, w_rg_i, b_rg_i, rg_lambda, g_mem, w_mem_kv, w_xo, w_o, g_mlp, w_up, w_down, g_final):
    batch, seq, d = x.shape
    mem_len = mem.shape[1]
    t = batch * seq
    depth = g_mix.shape[0]
    n_sel = seq // SEL_BLOCK
    n_cmp_pad = seq // CMP_STRIDE
    assert n_cmp_pad == LANES and n_sel <= K_AUG - HEAD_DIM and d == D_MODEL

    pos = jnp.arange(seq)
    cos_k, sin_k = _rope_tables(pos, NSA_KV)
    cos_q, sin_q = (a.T for a in _rope_angles(pos))
    no_rope = jnp.zeros((mem_len, LANES), F32)
    cmp_pos = jnp.arange(n_cmp_pad) * CMP_STRIDE + (CMP_LEN - 1)
    cmp_tm = 4 * n_cmp_pad
    cos_c, sin_c = _rope_tables(cmp_pos, 2 * HEAD_DIM, HEAD_DIM)
    cos_c = jnp.tile(cos_c, (cmp_tm // n_cmp_pad, 2))
    sin_c = jnp.tile(sin_c, (cmp_tm // n_cmp_pad, 2))
    c2st = jnp.asarray(_cmp_to_sel_t((seq - CMP_LEN) // CMP_STRIDE + 1, n_cmp_pad, n_sel))
    feat = _key_features(seq, n_sel)

    h = x.reshape(t, d)
    for l in range(depth):
        z = _norm_proj(h, g_mix[l][None, :], _pack_w_in(w_in[l]), cos_k, sin_k, tm=1024, tn=INPROJ_TN,
                       rope_tiles=ROPE_TILES, out_dtype=BF16, name="inproj")

        pair_w = 2 * HEAD_DIM
        n_pairs = NSA_GROUPS // 2

        def pair_rows(off):
            a = z[:, off:off + NSA_KV].reshape(batch * n_cmp_pad, CMP_STRIDE, n_pairs, pair_w)
            return a.transpose(2, 0, 1, 3).reshape(n_pairs, batch * n_cmp_pad, CMP_STRIDE * pair_w)

        def pair_w1(w1_half):
            w3 = w1_half.astype(BF16).reshape(CMP_STRIDE, 1, HEAD_DIM, CMP_HIDDEN)
            zero = jnp.zeros_like(w3)
            both = jnp.concatenate([jnp.concatenate([w3, zero], axis=3), jnp.concatenate([zero, w3], axis=3)], axis=1)
            return both.reshape(CMP_STRIDE * pair_w, 2 * CMP_HIDDEN)

        def pair_pos(p_half):
            return jnp.broadcast_to(p_half[:, None, :], (CMP_STRIDE, 2, HEAD_DIM)).reshape(1, CMP_STRIDE * pair_w)

        half = CMP_STRIDE * HEAD_DIM
        w2 = jnp.zeros((2, 2, CMP_HIDDEN, 2, 2, HEAD_DIM), F32)
        for grp in range(2):
            w2 = w2.at[0, grp, :, grp, 0, :].set(w_cmp_k2[l]).at[1, grp, :, grp, 1, :].set(w_cmp_v2[l])
        w2 = w2.reshape(4 * CMP_HIDDEN, 4 * HEAD_DIM)
        kvc = _compress(pair_rows(OFF_KC), pair_rows(OFF_VC),
                        pair_pos(cmp_pos_k[l][:CMP_STRIDE]), pair_pos(cmp_pos_k[l][CMP_STRIDE:]),
                        pair_pos(cmp_pos_v[l][:CMP_STRIDE]), pair_pos(cmp_pos_v[l][CMP_STRIDE:]),
                        pair_w1(w_cmp_k1[l][:half]), pair_w1(w_cmp_k1[l][half:]),
                        pair_w1(w_cmp_v1[l][:half]), pair_w1(w_cmp_v1[l][half:]),
                        w2.astype(BF16), cos_c, sin_c, tm=cmp_tm)
        y_nsa = _nsa(z, cos_q, sin_q, feat, kvc, c2st, batch=batch, seq=seq)

        y_rnn = _rglru(z, conv_w[l], conv_b[l][None, :], _block_diag_pairs(w_rg_a[l]), b_rg_a[l][None, :],
                       _block_diag_pairs(w_rg_i[l]), b_rg_i[l][None, :], rg_lambda[l][None, :],
                       batch=batch, seq=seq, ts=256)

        kvm = _norm_proj(mem.reshape(batch * mem_len, d), g_mem[l][None, :], w_mem_kv[l].astype(BF16),
                         no_rope, no_rope, tm=mem_len, tn=2 * XATTN_WIDTH, rope_tiles=(),
                         out_dtype=BF16, name="memkv")

        h1 = _merge(h, z, y_nsa, y_rnn, kvm, w_xo[l].astype(BF16), w_o[l].astype(BF16),
                    seq=seq, mem_len=mem_len, tq=512)
        last = l == depth - 1
        gf = g_final if last else jnp.ones_like(g_final)
        h = _mlp(h1, g_mlp[l][None, :], w_up[l].astype(BF16), w_down[l].astype(BF16), gf[None, :],
                 tm=1024, tf=1024)
        assert last, "final norm is fused into the last layer's MLP kernel"
    return h.reshape(batch, seq, d)
```

```python
import functools
import math

import numpy as np
import jax
import jax.numpy as jnp
from jax import lax
from jax.experimental import pallas as pl
from jax.experimental.pallas import tpu as pltpu

F32 = jnp.float32
BF16 = jnp.bfloat16

D_MODEL = 1024
HEAD_DIM = 64
NSA_HEADS = 16
NSA_GROUPS = 4
NSA_HPG = NSA_HEADS // NSA_GROUPS
NSA_BRANCHES = 3
CMP_LEN = 32
CMP_STRIDE = 16
CMP_HIDDEN = 256
SEL_BLOCK = 64
SEL_TOPK = 8
WINDOW = 512
Q_BLOCK = 128
ROPE_THETA = 500000.0
ROPE_DIM = HEAD_DIM // 4
ROPE_HALF = ROPE_DIM // 2
RNN_BLOCKS = 16
RNN_BLOCK_DIM = D_MODEL // RNN_BLOCKS
CONV_WIDTH = 4
RGLRU_C = 8.0
XATTN_HEADS = 4
XATTN_WIDTH = XATTN_HEADS * HEAD_DIM
D_FF = 4 * D_MODEL
RMS_EPS = 1e-6
SCALE = HEAD_DIM ** -0.5
LOG2E = math.log2(math.e)
NEG = -1e30

LANES = 128
SUBLANES = 8
BF16_ROWS = 16
VMEM_LIMIT = 48 * 1024 * 1024

NSA_KV = NSA_GROUPS * HEAD_DIM
OFF_Q = 0
OFF_KC = 1024
OFF_VC = 1280
OFF_KS = 1536
OFF_VS = 1792
OFF_KW = 2048
OFF_VW = 2304
OFF_QX = 2560
OFF_GATE = 2816
GATE_STRIDE = 64
OFF_XRNN = 3072
OFF_GRNN = 4096
OFF_MERGE = 5120
Z_WIDTH = 8192
INPROJ_TN = 2048
ROPE_TILES = tuple((off // INPROJ_TN, off % INPROJ_TN) for off in (OFF_KS, OFF_KW))

NSA_PAIRS = 2
NSA_GPS = 2 * NSA_PAIRS
NSA_LANES = NSA_HPG * Q_BLOCK
SEL_CHUNK = 4 * Q_BLOCK
SEL_PIECE = SEL_CHUNK
WIN_TILES = WINDOW // Q_BLOCK + 1
K_AUG = 2 * HEAD_DIM
V_AUG = HEAD_DIM + BF16_ROWS

NT_DIMS = (((1,), (1,)), ((), ()))


def _sigmoid(x):
    return 1.0 / (1.0 + jnp.exp(-x))


def _sigmoid_tanh(x):
    return 0.5 * jnp.tanh(0.5 * x) + 0.5


def _gelu_tanh(x):
    return 0.5 * x * (1.0 + jnp.tanh(0.7978845608028654 * (x + 0.044715 * (x * x * x))))


def _rope_swap(z):
    n = z.shape[-1]
    lane = lax.broadcasted_iota(jnp.int32, z.shape, z.ndim - 1)
    first_half = (lane & (HEAD_DIM - 1)) < ROPE_HALF
    return jnp.where(first_half, pltpu.roll(z, n - ROPE_HALF, z.ndim - 1), pltpu.roll(z, ROPE_HALF, z.ndim - 1))


def _round_robin(gens, lead=0):
    results = [None] * len(gens)
    alive = list(range(len(gens)))

    def advance(idx):
        try:
            next(gens[idx])
        except StopIteration as stop:
            results[idx] = stop.value
            alive.remove(idx)

    for _ in range(lead):
        for idx in [n for n in alive if n < len(gens) // 2]:
            advance(idx)
    while alive:
        for idx in list(alive):
            advance(idx)
    return results


def _fold_rows(x, op):
    parts = [x[r * SUBLANES:(r + 1) * SUBLANES] for r in range(x.shape[0] // SUBLANES)]
    return functools.reduce(op, parts)


def _norm_proj_kernel(x_ref, g_ref, w_ref, cos_ref, sin_ref, z_ref, u_scr, *, rope_tiles):
    j = pl.program_id(1)

    @pl.when(j == 0)
    def _():
        x = x_ref[...]
        ms = jnp.mean(x * x, axis=-1, keepdims=True)
        u_scr[...] = (x * lax.rsqrt(ms + RMS_EPS) * g_ref[...]).astype(BF16)

    z = jnp.dot(u_scr[...], w_ref[...], preferred_element_type=F32)
    z_ref[...] = z.astype(z_ref.dtype)
    rw = cos_ref.shape[1]
    for tile, col in rope_tiles:
        @pl.when(j == tile)
        def _(col=col):
            zr = z[:, col:col + rw]
            z_ref[:, col:col + rw] = (zr * cos_ref[...] + _rope_swap(zr) * sin_ref[...]).astype(z_ref.dtype)


def _norm_proj(x, g, w, cos_t, sin_t, *, tm, tn, rope_tiles, out_dtype, name):
    t, d = x.shape
    n = w.shape[1]
    s_tiles = cos_t.shape[0] // tm
    rw = cos_t.shape[1]
    table_spec = pl.BlockSpec((tm, rw), lambda i, j: (i % s_tiles, 0))
    return pl.pallas_call(
        functools.partial(_norm_proj_kernel, rope_tiles=tuple(rope_tiles)),
        grid=(t // tm, n // tn),
        in_specs=[
            pl.BlockSpec((tm, d), lambda i, j: (i, 0)),
            pl.BlockSpec((1, d), lambda i, j: (0, 0)),
            pl.BlockSpec((d, tn), lambda i, j: (0, j)),
            table_spec, table_spec,
        ],
        out_specs=pl.BlockSpec((tm, tn), lambda i, j: (i, j)),
        out_shape=jax.ShapeDtypeStruct((t, n), out_dtype),
        scratch_shapes=[pltpu.VMEM((tm, d), BF16)],
        compiler_params=pltpu.CompilerParams(
            dimension_semantics=("parallel", "arbitrary"), vmem_limit_bytes=VMEM_LIMIT),
        name=name,
    )(x, g, w, cos_t, sin_t)


def _compress_kernel(xk_ref, xv_ref, pk_ref, pv_ref, wk1_ref, wv1_ref, w2_ref, cos_ref, sin_ref, o_ref, xs_scr):
    tm = o_ref.shape[0]

    def hidden(x_ref, p_ref, w_ref):
        xs_scr[...] = x_ref[...].astype(F32)
        first = jnp.zeros((tm, w_ref.shape[2]), F32)
        second = jnp.zeros((tm, w_ref.shape[2]), F32)
        for j in range(CMP_STRIDE):
            xj = xs_scr[pl.ds(j, tm, stride=CMP_STRIDE), :]
            first = first + jnp.dot((xj + p_ref[j:j + 1, :]).astype(BF16), w_ref[j],
                                    preferred_element_type=F32)
            second = second + jnp.dot((xj + p_ref[CMP_STRIDE + j:CMP_STRIDE + j + 1, :]).astype(BF16),
                                      w_ref[CMP_STRIDE + j], preferred_element_type=F32)
        return _gelu_tanh(first + pltpu.roll(second, tm - 1, 0)).astype(BF16)

    hcat = jnp.concatenate([hidden(xk_ref, pk_ref, wk1_ref), hidden(xv_ref, pv_ref, wv1_ref)], axis=1)
    kv = jnp.dot(hcat, w2_ref[...], preferred_element_type=F32)
    o_ref[...] = kv * cos_ref[...] + _rope_swap(kv) * sin_ref[...]


def _compress(z, pk, pv, wk1, wv1, w2, cos_t, sin_t, *, tm):
    t = z.shape[0]
    m = t // CMP_STRIDE
    pairs = NSA_GROUPS // 2
    pair_w = 2 * HEAD_DIM
    hid2 = wk1.shape[2]
    out_w = w2.shape[1]
    full = lambda shape: pl.BlockSpec(shape, lambda p, i: (0,) * len(shape))
    tokens = lambda off: pl.BlockSpec((tm * CMP_STRIDE, pair_w), lambda p, i: (i, off // pair_w + p))
    return pl.pallas_call(
        _compress_kernel,
        grid=(pairs, m // tm),
        in_specs=[
            tokens(OFF_KC), tokens(OFF_VC),
            full((CMP_LEN, pair_w)), full((CMP_LEN, pair_w)),
            full((CMP_LEN, pair_w, hid2)), full((CMP_LEN, pair_w, hid2)),
            full((2 * hid2, out_w)),
            full((tm, out_w)), full((tm, out_w)),
        ],
        out_specs=pl.BlockSpec((tm, out_w), lambda p, i: (i, p)),
        out_shape=jax.ShapeDtypeStruct((m, pairs * out_w), F32),
        scratch_shapes=[pltpu.VMEM((tm * CMP_STRIDE, pair_w), F32)],
        compiler_params=pltpu.CompilerParams(
            dimension_semantics=("parallel", "parallel"), vmem_limit_bytes=VMEM_LIMIT),
        name="compress",
    )(z, z, pk, pv, wk1, wv1, w2, cos_t, sin_t)


def _nsa_kernel(q_ref, g_ref, cos_ref, sin_ref, *rest, n_sel):
    kv_refs, rest = rest[:4 * NSA_PAIRS], rest[4 * NSA_PAIRS:]
    feat_ref, kvc_ref, c2st_ref, o_ref, gt_scr, s_scr, vt_scr = rest
    ks_refs, vs_refs, kw_refs, vw_refs = (kv_refs[n::4] for n in range(4))
    g2 = pl.program_id(1)
    i = pl.program_id(2)
    q0 = i * Q_BLOCK
    seq = feat_ref.shape[0]

    @pl.when(i == 0)
    def _():
        ones_rows = jnp.where(lax.broadcasted_iota(jnp.int32, (V_AUG - HEAD_DIM, Q_BLOCK), 0) == 0,
                              1.0, 0.0).astype(BF16)

        def fill(t, carry):
            r0 = pl.multiple_of(t * Q_BLOCK, Q_BLOCK)
            for kind, refs in enumerate((vs_refs, vw_refs)):
                for pair, ref in enumerate(refs):
                    vt = jnp.transpose(ref[pl.ds(r0, Q_BLOCK), :].astype(F32)).astype(BF16)
                    for half in range(2):
                        gg = 2 * pair + half
                        vt_scr[kind, gg, t, 0:HEAD_DIM, :] = vt[half * HEAD_DIM:(half + 1) * HEAD_DIM]
                        vt_scr[kind, gg, t, HEAD_DIM:V_AUG, :] = ones_rows
            return carry

        lax.fori_loop(0, seq // Q_BLOCK, fill, 0)

    def query_side(gg, qt, bias_rows, win_row):
        zeros_q = jnp.zeros((HEAD_DIM, NSA_LANES), BF16)
        q_rows = [qt, zeros_q] if gg % 2 == 0 else [zeros_q, qt]
        rid = lax.broadcasted_iota(jnp.int32, (BF16_ROWS, NSA_LANES), 0)
        win_rows = jnp.where(rid == 0, win_row, 0.0).astype(BF16)
        tail = jnp.zeros((2 * LANES - 2 * HEAD_DIM - n_sel - BF16_ROWS, NSA_LANES), BF16)
        return jnp.concatenate(q_rows + [bias_rows.astype(BF16), win_rows, tail], axis=0)

    def key_side(k_ref, row0, rows):
        return jnp.concatenate([k_ref[pl.ds(row0, rows), :], feat_ref[pl.ds(row0, rows), :]], axis=1)
    blocks_per_chunk = SEL_CHUNK // SEL_BLOCK
    q_per_chunk = SEL_CHUNK // Q_BLOCK

    def tile_heads(x):
        return jnp.concatenate([x] * NSA_HPG, axis=1)

    tq = q0 + (lax.broadcasted_iota(jnp.int32, (1, NSA_LANES), 1) & (Q_BLOCK - 1))
    c_idx = lax.broadcasted_iota(jnp.int32, (Q_BLOCK, 1), 0)
    r_idx = lax.broadcasted_iota(jnp.int32, (1, Q_BLOCK), 1)
    tri_diag = tile_heads(jnp.where(c_idx <= r_idx, 0.0, NEG))
    tri_old = tile_heads(jnp.where(c_idx > r_idx, 0.0, NEG))
    cos_q = tile_heads(cos_ref[...])
    sin_q = tile_heads(sin_ref[...])
    blk = lax.broadcasted_iota(jnp.int32, (n_sel, Q_BLOCK), 0)
    cur = (q0 + lax.broadcasted_iota(jnp.int32, (n_sel, Q_BLOCK), 1)) // SEL_BLOCK
    sub = lax.broadcasted_iota(jnp.int32, (SUBLANES, Q_BLOCK), 0)

    gt_scr[...] = jnp.transpose(_sigmoid(g_ref[...].astype(F32)))

    def prepare(gg):
        qx = jnp.transpose(q_ref[:, gg * NSA_HPG * HEAD_DIM:(gg + 1) * NSA_HPG * HEAD_DIM].astype(F32)
                           * (SCALE * LOG2E))
        qf = jnp.concatenate([qx[h * HEAD_DIM:(h + 1) * HEAD_DIM, :] for h in range(NSA_HPG)], axis=1)
        x1, x2 = qf[0:ROPE_HALF], qf[ROPE_HALF:ROPE_DIM]
        qt = jnp.concatenate([x1 * cos_q - x2 * sin_q, x2 * cos_q + x1 * sin_q, qf[ROPE_DIM:]],
                             axis=0).astype(BF16)
        qts[gg] = qt
        yield

        kvc = kvc_ref[:, gg * LANES:(gg + 1) * LANES]
        kc = kvc[:, :HEAD_DIM].astype(BF16)
        vct = jnp.transpose(kvc)[HEAD_DIM:, :].astype(BF16)
        sc = jnp.dot(kc, qt, preferred_element_type=F32)
        yield
        n_idx = lax.broadcasted_iota(jnp.int32, (LANES, 1), 0)
        sc = jnp.where((n_idx * CMP_STRIDE + (CMP_LEN - 1)) <= tq, sc, NEG)
        m_c = jnp.max(sc, axis=0, keepdims=True)
        e_c = jnp.exp2(sc - m_c)
        inv_c = 1.0 / jnp.maximum(jnp.sum(e_c, axis=0, keepdims=True), 1e-30)
        p_c = e_c * jnp.where(m_c > 0.5 * NEG, inv_c, 0.0)
        o_c = jnp.dot(vct, p_c.astype(BF16), preferred_element_type=F32)
        yield

        p_sum = p_c[:, 0:Q_BLOCK]
        for h in range(1, NSA_HPG):
            p_sum = p_sum + p_c[:, h * Q_BLOCK:(h + 1) * Q_BLOCK]
        imp = jnp.dot(c2st_ref[...], p_sum, preferred_element_type=F32,
                      precision=lax.Precision.HIGHEST)
        yield
        forced = (blk == 0) | (blk == cur) | (blk == cur - 1)
        val = jnp.where(forced, jnp.inf, jnp.where(blk > cur, -jnp.inf, imp))
        n_grp = n_sel // SUBLANES
        grp = [val[v * SUBLANES:(v + 1) * SUBLANES] for v in range(n_grp)]
        cnt = [jnp.zeros((SUBLANES, Q_BLOCK), F32) for _ in range(n_grp)]
        for sp in range(n_sel):
            r = val[sp:sp + 1, :]
            for v in range(n_grp):
                if v * SUBLANES > sp:
                    cnt[v] = cnt[v] + jnp.where(r >= grp[v], 1.0, 0.0)
                elif v * SUBLANES + SUBLANES - 1 <= sp:
                    cnt[v] = cnt[v] + jnp.where(r > grp[v], 1.0, 0.0)
                else:
                    later = jnp.where(sub > sp - v * SUBLANES, 1.0, 0.0)
                    cnt[v] = cnt[v] + jnp.where(r > grp[v], 1.0, jnp.where(r == grp[v], later, 0.0))
            if sp % SUBLANES == SUBLANES - 1:
                yield
        chosen = jnp.concatenate([jnp.where(c < float(SEL_TOPK), 0.0, NEG) for c in cnt], axis=0)
        block_bias = tile_heads(jnp.where(blk > cur, NEG, chosen))
        row0 = pl.multiple_of((g2 * NSA_GPS + gg) * GATE_STRIDE, GATE_STRIDE)
        gt = gt_scr[pl.ds(row0, 2 * SUBLANES), :]
        return qt, o_c, block_bias, gt

    def window(gg):
        qt = qts[gg]
        tiles = []
        for d in range(1 - WIN_TILES, 1):
            td = i + d
            tc = jnp.maximum(td, 0)
            k0 = pl.multiple_of(tc * Q_BLOCK, Q_BLOCK)
            before_start = jnp.where(td >= 0, 0.0, NEG) if d < 0 else 0.0
            tri = tri_diag if d == 0 else (tri_old if d == 1 - WIN_TILES else None)
            tiles.append((tc, k0, before_start, tri))

        zero_row = jnp.zeros((1, NSA_LANES), F32)
        no_bias = jnp.zeros((n_sel, NSA_LANES), BF16)
        scores = []
        for tc, k0, before_start, tri in tiles:
            q_aug = query_side(gg, qt, no_bias, zero_row + before_start)
            s = jnp.dot(key_side(kw_refs[gg // 2], k0, Q_BLOCK), q_aug, preferred_element_type=F32)
            scores.append(s if tri is None else s + tri)
            yield
        m_w = jnp.max(functools.reduce(jnp.maximum, [_fold_rows(s, jnp.maximum) for s in scores]),
                      axis=0, keepdims=True)
        n_pad = jnp.maximum(WINDOW - 1 - tq, 0).astype(F32)
        m_w = jnp.where(n_pad > 0.0, jnp.maximum(m_w, 0.0), m_w)
        yield
        acc = jnp.zeros((V_AUG, NSA_LANES), F32)
        for (tc, _, _, _), s in zip(tiles, scores):
            acc = acc + jnp.dot(vt_scr[1, gg, tc], jnp.exp2(s - m_w).astype(BF16), preferred_element_type=F32)
            yield
        l_w = acc[HEAD_DIM:HEAD_DIM + 1] + n_pad * jnp.exp2(-m_w)
        return acc[:HEAD_DIM] * (1.0 / jnp.maximum(l_w, 1e-30))

    pieces_per_chunk = SEL_CHUNK // SEL_PIECE
    tiles_per_piece = SEL_PIECE // Q_BLOCK
    blocks_per_piece = SEL_PIECE // SEL_BLOCK

    def chunk_scores(gg, c, diag_rel):
        zeros_q = jnp.zeros((HEAD_DIM, NSA_LANES), BF16)
        q_rows = jnp.concatenate([qts[gg], zeros_q] if gg % 2 == 0 else [zeros_q, qts[gg]], axis=0)
        bmax = []
        for piece in range(c * pieces_per_chunk, (c + 1) * pieces_per_chunk):
            rows = slice(piece * SEL_PIECE, (piece + 1) * SEL_PIECE)
            s = jnp.dot(ks_refs[gg // 2][rows, :], q_rows, preferred_element_type=F32)
            if diag_rel is not None:
                first_tile = (piece - c * pieces_per_chunk) * tiles_per_piece
                s = s + jnp.concatenate([jnp.where(diag_rel == first_tile + u, tri_diag, 0.0)
                                         for u in range(tiles_per_piece)], axis=0)
            s_scr[gg, rows, :] = s
            bmax += [_fold_rows(s[b * SEL_BLOCK:(b + 1) * SEL_BLOCK], jnp.maximum) for b in range(blocks_per_piece)]
            yield
        return bmax

    def first_chunk(gg):
        yield
        return (yield from chunk_scores(gg, 0, i))

    def selected(gg, k, block_bias, bmax0):
        bmax = list(bmax0)
        for c in range(1, k + 1):
            bmax += yield from chunk_scores(gg, c, i - k * q_per_chunk if c == k else None)
        m8 = functools.reduce(jnp.maximum, [bm + block_bias[n:n + 1, :] for n, bm in enumerate(bmax)])
        m_s = jnp.max(m8, axis=0, keepdims=True)
        shift = block_bias - m_s
        yield
        acc = jnp.zeros((V_AUG, NSA_LANES), F32)
        for piece in range((k + 1) * pieces_per_chunk):
            ps = []
            for b in range(blocks_per_piece):
                n = piece * blocks_per_piece + b
                ps.append(jnp.exp2(s_scr[gg, n * SEL_BLOCK:(n + 1) * SEL_BLOCK, :] + shift[n:n + 1, :]).astype(BF16))
            vt = jnp.concatenate([vt_scr[0, gg, piece * tiles_per_piece + u] for u in range(tiles_per_piece)], axis=1)
            acc = acc + jnp.dot(vt, jnp.concatenate(ps, axis=0), preferred_element_type=F32)
            yield
        l_s = acc[HEAD_DIM:HEAD_DIM + 1]
        return acc[:HEAD_DIM] * (1.0 / jnp.maximum(l_s, 1e-30))

    qts = [None] * NSA_GPS
    groups = range(NSA_GPS)
    shared = _round_robin([prepare(gg) for gg in groups] + [window(gg) for gg in groups]
                          + [first_chunk(gg) for gg in groups])
    prepared, windows, bmax0 = (shared[n * NSA_GPS:(n + 1) * NSA_GPS] for n in range(3))

    def finish(k):
        branches = []
        for gg, (qt, o_c, block_bias, gt) in enumerate(prepared):
            branches.append(selected(gg, k, block_bias, bmax0[gg]))
        results = _round_robin(branches, lead=k + 1)
        outs = []
        for gg, (qt, o_c, block_bias, gt) in enumerate(prepared):
            o_w, o_s = windows[gg], results[gg]
            for h in range(NSA_HPG):
                sl = slice(h * Q_BLOCK, (h + 1) * Q_BLOCK)
                row = NSA_BRANCHES * h
                outs.append(gt[row:row + 1, :] * o_c[:, sl] + gt[row + 1:row + 2, :] * o_s[:, sl]
                            + gt[row + 2:row + 3, :] * o_w[:, sl])
        o_ref[...] = jnp.transpose(jnp.concatenate(outs, axis=0))

    for k in range(n_sel * SEL_BLOCK // SEL_CHUNK):
        pl.when(i // q_per_chunk == k)(functools.partial(finish, k))


def _nsa(z, cos_q, sin_q, feat, kvc, c2st, *, batch, seq):
    nq = seq // Q_BLOCK
    n_sel = seq // SEL_BLOCK
    gw = NSA_GPS * NSA_HPG * HEAD_DIM
    gates_w = NSA_GROUPS * GATE_STRIDE
    pair_w = 2 * HEAD_DIM
    assert pair_w == LANES
    kv_spec = lambda off, pair: pl.BlockSpec(
        (seq, pair_w), lambda b, g, i: (b, off // pair_w + g * NSA_PAIRS + pair))
    kv_specs = [kv_spec(off, pair) for pair in range(NSA_PAIRS) for off in (OFF_KS, OFF_VS, OFF_KW, OFF_VW)]
    rope_spec = pl.BlockSpec((ROPE_HALF, Q_BLOCK), lambda b, g, i: (0, i))
    return pl.pallas_call(
        functools.partial(_nsa_kernel, n_sel=n_sel),
        grid=(batch, NSA_GROUPS // NSA_GPS, nq),
        in_specs=[
            pl.BlockSpec((Q_BLOCK, gw), lambda b, g, i: (b * nq + i, OFF_Q // gw + g)),
            pl.BlockSpec((Q_BLOCK, gates_w), lambda b, g, i: (b * nq + i, OFF_GATE // gates_w)),
            rope_spec, rope_spec,
            *kv_specs,
            pl.BlockSpec((seq, LANES), lambda b, g, i: (0, 0)),
            pl.BlockSpec((LANES, NSA_GPS * 2 * HEAD_DIM), lambda b, g, i: (b, g)),
            pl.BlockSpec((n_sel, LANES), lambda b, g, i: (0, 0)),
        ],
        out_specs=pl.BlockSpec((Q_BLOCK, gw), lambda b, g, i: (b * nq + i, g)),
        out_shape=jax.ShapeDtypeStruct((batch * seq, NSA_HEADS * HEAD_DIM), F32),
        scratch_shapes=[pltpu.VMEM((gates_w, Q_BLOCK), F32), pltpu.VMEM((NSA_GPS, seq, NSA_LANES), F32),
                        pltpu.VMEM((2, NSA_GPS, nq, V_AUG, Q_BLOCK), BF16)],
        compiler_params=pltpu.CompilerParams(
            dimension_semantics=("arbitrary", "arbitrary", "arbitrary"), vmem_limit_bytes=VMEM_LIMIT),
        name="nsa",
    )(z, z, cos_q, sin_q, *([z] * len(kv_specs)), feat, kvc, c2st)


def _rglru_kernel(x_ref, gr_ref, cw_ref, cb_ref, wa_ref, ba_ref, wi_ref, bi_ref, lam_ref, y_ref,
                  xs_scr, h_scr):
    ts, c = x_ref.shape

    @pl.when(pl.program_id(1) == 0)
    def _():
        xs_scr[0:SUBLANES, :] = jnp.zeros((SUBLANES, c), F32)
        h_scr[...] = jnp.zeros(h_scr.shape, F32)

    x = x_ref[...].astype(F32)
    xs_scr[SUBLANES:SUBLANES + ts, :] = x
    xc = cb_ref[...] + cw_ref[CONV_WIDTH - 1:CONV_WIDTH, :] * x
    for k in range(CONV_WIDTH - 1):
        off = SUBLANES - (CONV_WIDTH - 1) + k
        xc = xc + cw_ref[k:k + 1, :] * xs_scr[off:off + ts, :]
    xs_scr[0:SUBLANES, :] = x[ts - SUBLANES:, :]

    ra, ri = [], []
    for mblk in range(c // LANES):
        xb = xc[:, mblk * LANES:(mblk + 1) * LANES].astype(BF16)
        ra.append(jnp.dot(xb, wa_ref[mblk], preferred_element_type=F32))
        ri.append(jnp.dot(xb, wi_ref[mblk], preferred_element_type=F32))
    r = _sigmoid(jnp.concatenate(ra, axis=1) + ba_ref[...])
    gi = _sigmoid(jnp.concatenate(ri, axis=1) + bi_ref[...])

    nl = -lam_ref[...]
    softplus = jnp.maximum(nl, 0.0) + jnp.log1p(jnp.exp(-jnp.abs(nl)))
    log_a = (-RGLRU_C) * r * softplus
    a = jnp.exp(log_a)
    th = jnp.tanh(log_a)
    y2 = -2.0 * th / (1.0 - th)
    b = jnp.where(y2 > 0.0, y2 * lax.rsqrt(y2), 0.0) * (gi * xc)

    groups = ts // SUBLANES
    a = a.reshape(groups, SUBLANES, c)
    b = b.reshape(groups, SUBLANES, c)
    in_group = lax.broadcasted_iota(jnp.int32, (groups, SUBLANES, c), 1)
    d = 1
    while d < SUBLANES:
        keep = in_group >= d
        a_sh = jnp.where(keep, pltpu.roll(a, d, 1), 1.0)
        b_sh = jnp.where(keep, pltpu.roll(b, d, 1), 0.0)
        b = a * b_sh + b
        a = a * a_sh
        d *= 2
    carry = h_scr[...]
    hs = []
    for r in range(groups):
        hg = b[r] + a[r] * carry
        hs.append(hg)
        carry = hg[SUBLANES - 1:SUBLANES, :]
    h_scr[...] = carry
    y_ref[...] = _gelu_tanh(gr_ref[...].astype(F32)) * jnp.concatenate(hs, axis=0)


def _rglru(z, cw, cb, wa2, ba, wi2, bi, lam, *, batch, seq, ts):
    c = D_MODEL
    nt = seq // ts
    vec = pl.BlockSpec((1, c), lambda b, t: (0, 0))
    return pl.pallas_call(
        _rglru_kernel,
        grid=(batch, nt),
        in_specs=[
            pl.BlockSpec((ts, c), lambda b, t: (b * nt + t, OFF_XRNN // c)),
            pl.BlockSpec((ts, c), lambda b, t: (b * nt + t, OFF_GRNN // c)),
            pl.BlockSpec((CONV_WIDTH, c), lambda b, t: (0, 0)),
            vec,
            pl.BlockSpec((c // LANES, LANES, LANES), lambda b, t: (0, 0, 0)),
            vec,
            pl.BlockSpec((c // LANES, LANES, LANES), lambda b, t: (0, 0, 0)),
            vec, vec,
        ],
        out_specs=pl.BlockSpec((ts, c), lambda b, t: (b * nt + t, 0)),
        out_shape=jax.ShapeDtypeStruct((batch * seq, c), F32),
        scratch_shapes=[pltpu.VMEM((SUBLANES + ts, c), F32), pltpu.VMEM((1, c), F32)],
        compiler_params=pltpu.CompilerParams(
            dimension_semantics=("parallel", "arbitrary"), vmem_limit_bytes=VMEM_LIMIT),
        name="rglru",
    )(z, z, cw, cb, wa2, ba, wi2, bi, lam)


def _merge_kernel(x_ref, m0_ref, m1_ref, m2_ref, ynsa_ref, yrnn_ref, qx_ref, km_ref, vm_ref, wxo_ref, wo_ref, h_ref):
    tq = x_ref.shape[0]
    mem_len = km_ref.shape[0]
    row_chunk = tq // 4

    def cross_attention():
        qt = jnp.transpose(qx_ref[...].astype(F32) * (SCALE * LOG2E))
        row_head = lax.broadcasted_iota(jnp.int32, qt.shape, 0) // HEAD_DIM
        q_bd = jnp.concatenate([jnp.where(row_head == h, qt, 0.0) for h in range(XATTN_HEADS)],
                               axis=1).astype(BF16)
        yield
        s = jnp.dot(km_ref[...], q_bd, preferred_element_type=F32)
        yield
        m = jnp.max(_fold_rows(s, jnp.maximum), axis=0, keepdims=True)
        p = jnp.exp2(s - m).astype(BF16)
        yield
        ones_rows = jnp.where(lax.broadcasted_iota(jnp.int32, (BF16_ROWS, mem_len), 0) == 0, 1.0, 0.0)
        vt = jnp.concatenate([jnp.transpose(vm_ref[...].astype(F32)), ones_rows], axis=0).astype(BF16)
        o_all = jnp.dot(vt, p, preferred_element_type=F32)
        yield
        outs = []
        for h in range(XATTN_HEADS):
            lanes = slice(h * tq, (h + 1) * tq)
            inv_l = 1.0 / o_all[XATTN_WIDTH:XATTN_WIDTH + 1, lanes]
            outs.append(o_all[h * HEAD_DIM:(h + 1) * HEAD_DIM, lanes] * inv_l)
        o = jnp.transpose(jnp.concatenate(outs, axis=0)).astype(BF16)
        yield
        return jnp.dot(o, wxo_ref[...], preferred_element_type=F32)

    def gated(gate_ref, val_ref):
        parts = []
        for r in range(0, tq, row_chunk):
            parts.append(_sigmoid_tanh(gate_ref[r:r + row_chunk, :].astype(F32)) * val_ref[r:r + row_chunk, :])
            yield
        return jnp.concatenate(parts, axis=0)

    def gate_only(gate_ref):
        parts = []
        for r in range(0, tq, row_chunk):
            parts.append(_sigmoid_tanh(gate_ref[r:r + row_chunk, :].astype(F32)))
            yield
        return jnp.concatenate(parts, axis=0)

    yx, y_a, y_b, g_x = _round_robin([cross_attention(), gated(m0_ref, ynsa_ref), gated(m1_ref, yrnn_ref),
                                      gate_only(m2_ref)])
    y = y_a + y_b + g_x * yx
    h_ref[...] = x_ref[...] + jnp.dot(y.astype(BF16), wo_ref[...], preferred_element_type=F32)


def _merge(x, z, ynsa, yrnn, kvm, wxo, wo, *, seq, mem_len, tq):
    t, d = x.shape
    nt = seq // tq
    row = lambda cb: pl.BlockSpec((tq, d), lambda i, cb=cb: (i, cb))
    mem_spec = lambda cb: pl.BlockSpec((mem_len, XATTN_WIDTH), lambda i, cb=cb: (i // nt, cb))
    return pl.pallas_call(
        _merge_kernel,
        grid=(t // tq,),
        in_specs=[
            row(0),
            row(OFF_MERGE // d), row(OFF_MERGE // d + 1), row(OFF_MERGE // d + 2),
            row(0), row(0),
            pl.BlockSpec((tq, XATTN_WIDTH), lambda i: (i, OFF_QX // XATTN_WIDTH)),
            mem_spec(0), mem_spec(1),
            pl.BlockSpec((XATTN_WIDTH, d), lambda i: (0, 0)),
            pl.BlockSpec((d, d), lambda i: (0, 0)),
        ],
        out_specs=row(0),
        out_shape=jax.ShapeDtypeStruct((t, d), F32),
        compiler_params=pltpu.CompilerParams(
            dimension_semantics=("parallel",), vmem_limit_bytes=VMEM_LIMIT),
        name="merge",
    )(x, z, z, z, ynsa, yrnn, z, kvm, kvm, wxo, wo)


def _mlp_kernel(h_ref, g_ref, wup_ref, wdn_ref, gf_ref, o_ref, v_scr, acc_scr):
    f = pl.program_id(1)

    @pl.when(f == 0)
    def _():
        h = h_ref[...]
        ms = jnp.mean(h * h, axis=-1, keepdims=True)
        v_scr[...] = (h * lax.rsqrt(ms + RMS_EPS) * g_ref[...]).astype(BF16)
        acc_scr[...] = jnp.zeros(acc_scr.shape, F32)

    up = jnp.dot(v_scr[...], wup_ref[...], preferred_element_type=F32)
    act = jnp.square(jnp.maximum(up, 0.0)).astype(BF16)
    acc_scr[...] += jnp.dot(act, wdn_ref[...], preferred_element_type=F32)

    @pl.when(f == pl.num_programs(1) - 1)
    def _():
        h2 = h_ref[...] + acc_scr[...]
        ms = jnp.mean(h2 * h2, axis=-1, keepdims=True)
        o_ref[...] = h2 * lax.rsqrt(ms + RMS_EPS) * gf_ref[...]


def _mlp(h, g, wup, wdn, gf, *, tm, tf):
    t, d = h.shape
    dff = wup.shape[1]
    return pl.pallas_call(
        _mlp_kernel,
        grid=(t // tm, dff // tf),
        in_specs=[
            pl.BlockSpec((tm, d), lambda i, f: (i, 0)),
            pl.BlockSpec((1, d), lambda i, f: (0, 0)),
            pl.BlockSpec((d, tf), lambda i, f: (0, f)),
            pl.BlockSpec((tf, d), lambda i, f: (f, 0)),
            pl.BlockSpec((1, d), lambda i, f: (0, 0)),
        ],
        out_specs=pl.BlockSpec((tm, d), lambda i, f: (i, 0)),
        out_shape=jax.ShapeDtypeStruct((t, d), F32),
        scratch_shapes=[pltpu.VMEM((tm, d), BF16), pltpu.VMEM((tm, d), F32)],
        compiler_params=pltpu.CompilerParams(
            dimension_semantics=("parallel", "arbitrary"), vmem_limit_bytes=VMEM_LIMIT),
        name="mlp",
    )(h, g, wup, wdn, gf)


def _rope_angles(pos):
    inv = 1.0 / (ROPE_THETA ** (jnp.arange(0, ROPE_DIM, 2, dtype=F32) / ROPE_DIM))
    ang = pos.astype(F32)[:, None] * inv[None, :]
    return jnp.cos(ang), jnp.sin(ang)


def _rope_tables(pos, width, rope_width=None):
    rope_width = width if rope_width is None else rope_width
    cos, sin = _rope_angles(pos)
    n = pos.shape[0]
    pad = HEAD_DIM - ROPE_DIM
    cos_h = jnp.concatenate([cos, cos, jnp.ones((n, pad), F32)], axis=1)
    sin_h = jnp.concatenate([-sin, sin, jnp.zeros((n, pad), F32)], axis=1)
    reps = rope_width // HEAD_DIM
    rest = width - rope_width
    cos_t = jnp.concatenate([jnp.tile(cos_h, (1, reps)), jnp.ones((n, rest), F32)], axis=1)
    sin_t = jnp.concatenate([jnp.tile(sin_h, (1, reps)), jnp.zeros((n, rest), F32)], axis=1)
    return cos_t, sin_t


def _pack_w_in(w_in):
    d = w_in.shape[0]
    w = w_in.astype(BF16)
    n_qkv = NSA_HEADS * HEAD_DIM + 6 * NSA_KV
    n_gates = NSA_HEADS * NSA_BRANCHES
    rnn_lo = n_qkv + n_gates
    qx_lo = rnn_lo + 2 * D_MODEL
    mg_lo = qx_lo + XATTN_WIDTH
    per_group = n_gates // NSA_GROUPS
    gates = w[:, n_qkv:rnn_lo].reshape(d, NSA_GROUPS, per_group)
    gates = jnp.pad(gates, ((0, 0), (0, 0), (0, GATE_STRIDE - per_group))).reshape(d, NSA_GROUPS * GATE_STRIDE)
    return jnp.concatenate([w[:, :n_qkv], w[:, qx_lo:mg_lo], gates, w[:, rnn_lo:qx_lo], w[:, mg_lo:]], axis=1)


def _block_diag_pairs(w):
    nb, k, _ = w.shape
    w = w.reshape(nb // 2, 2, k, k)
    zero = jnp.zeros((nb // 2, k, k), w.dtype)
    top = jnp.concatenate([w[:, 0], zero], axis=2)
    bot = jnp.concatenate([zero, w[:, 1]], axis=2)
    return jnp.concatenate([top, bot], axis=1).astype(BF16)


def _cmp_to_sel_t(n_cmp, n_cmp_pad, n_sel):
    c0 = np.arange(n_cmp_pad)[None, :] * CMP_STRIDE
    s0 = np.arange(n_sel)[:, None] * SEL_BLOCK
    ov = np.clip(np.minimum(c0 + CMP_LEN, s0 + SEL_BLOCK) - np.maximum(c0, s0), 0, None)
    ov = np.where(np.arange(n_cmp_pad)[None, :] < n_cmp, ov, 0)
    return (ov / CMP_LEN).astype(np.float32)


def _key_features(seq, n_sel):
    col = np.arange(LANES)[None, :]
    onehot = (np.arange(seq)[:, None] // SEL_BLOCK == col) & (col < n_sel)
    return jnp.asarray(onehot | (col == n_sel), dtype=BF16)


def kernel(x, mem, g_mix, w_in, cmp_pos_k, cmp_pos_v, w_cmp_k1, w_cmp_k2, w_cmp_v1, w_cmp_v2, conv_w, conv_b, w_rg_a, b_rg_a, w_rg_i, b_rg_i, rg_lambda, g_mem, w_mem_kv, w_xo, w_o, g_mlp, w_up, w_down, g_final):
    batch, seq, d = x.shape
    mem_len = mem.shape[1]
    t = batch * seq
    depth = g_mix.shape[0]
    n_sel = seq // SEL_BLOCK
    n_cmp_pad = seq // CMP_STRIDE
    assert n_cmp_pad == LANES and n_sel <= K_AUG - HEAD_DIM and d == D_MODEL

    pos = jnp.arange(seq)
    cos_k, sin_k = _rope_tables(pos, NSA_KV)
    cos_q, sin_q = (a.T for a in _rope_angles(pos))
    no_rope = jnp.zeros((mem_len, LANES), F32)
    cmp_pos = jnp.arange(n_cmp_pad) * CMP_STRIDE + (CMP_LEN - 1)
    cmp_tm = 4 * n_cmp_pad
    cos_c, sin_c = _rope_tables(cmp_pos, 2 * HEAD_DIM, HEAD_DIM)
    cos_c = jnp.tile(cos_c, (cmp_tm // n_cmp_pad, 2))
    sin_c = jnp.tile(sin_c, (cmp_tm // n_cmp_pad, 2))
    c2st = jnp.asarray(_cmp_to_sel_t((seq - CMP_LEN) // CMP_STRIDE + 1, n_cmp_pad, n_sel))
    feat = _key_features(seq, n_sel)

    h = x.reshape(t, d)
    for l in range(depth):
        z = _norm_proj(h, g_mix[l][None, :], _pack_w_in(w_in[l]), cos_k, sin_k, tm=1024, tn=INPROJ_TN,
                       rope_tiles=ROPE_TILES, out_dtype=BF16, name="inproj")

        def pair_w1(w1):
            w3 = w1.astype(BF16).reshape(CMP_LEN, 1, HEAD_DIM, CMP_HIDDEN)
            zero = jnp.zeros_like(w3)
            both = jnp.concatenate([jnp.concatenate([w3, zero], axis=3), jnp.concatenate([zero, w3], axis=3)], axis=1)
            return both.reshape(CMP_LEN, 2 * HEAD_DIM, 2 * CMP_HIDDEN)

        def pair_pos(p):
            return jnp.concatenate([p, p], axis=1)

        w2 = jnp.zeros((2, 2, CMP_HIDDEN, 2, 2, HEAD_DIM), F32)
        for grp in range(2):
            w2 = w2.at[0, grp, :, grp, 0, :].set(w_cmp_k2[l]).at[1, grp, :, grp, 1, :].set(w_cmp_v2[l])
        w2 = w2.reshape(4 * CMP_HIDDEN, 4 * HEAD_DIM)
        kvc = _compress(z, pair_pos(cmp_pos_k[l]), pair_pos(cmp_pos_v[l]), pair_w1(w_cmp_k1[l]), pair_w1(w_cmp_v1[l]),
                        w2.astype(BF16), cos_c, sin_c, tm=cmp_tm)
        y_nsa = _nsa(z, cos_q, sin_q, feat, kvc, c2st, batch=batch, seq=seq)

        y_rnn = _rglru(z, conv_w[l], conv_b[l][None, :], _block_diag_pairs(w_rg_a[l]), b_rg_a[l][None, :],
                       _block_diag_pairs(w_rg_i[l]), b_rg_i[l][None, :], rg_lambda[l][None, :],
                       batch=batch, seq=seq, ts=256)

        kvm = _norm_proj(mem.reshape(batch * mem_len, d), g_mem[l][None, :], w_mem_kv[l].astype(BF16),
                         no_rope, no_rope, tm=mem_len, tn=2 * XATTN_WIDTH, rope_tiles=(),
                         out_dtype=BF16, name="memkv")

        h1 = _merge(h, z, y_nsa, y_rnn, kvm, w_xo[l].astype(BF16), w_o[l].astype(BF16),
                    seq=seq, mem_len=mem_len, tq=512)
        last = l == depth - 1
        gf = g_final if last else jnp.ones_like(g_final)
        h = _mlp(h1, g_mlp[l][None, :], w_up[l].astype(BF16), w_down[l].astype(BF16), gf[None, :],
                 tm=1024, tf=1024)
        assert last, "final norm is fused into the last layer's MLP kernel"
    return h.reshape(batch, seq, d)
```

```python
import functools
import math

import numpy as np
import jax
import jax.numpy as jnp
from jax import lax
from jax.experimental import pallas as pl
from jax.experimental.pallas import tpu as pltpu

F32 = jnp.float32
BF16 = jnp.bfloat16

D_MODEL = 1024
HEAD_DIM = 64
NSA_HEADS = 16
NSA_GROUPS = 4
NSA_HPG = NSA_HEADS // NSA_GROUPS
NSA_BRANCHES = 3
CMP_LEN = 32
CMP_STRIDE = 16
CMP_HIDDEN = 256
SEL_BLOCK = 64
SEL_TOPK = 8
WINDOW = 512
Q_BLOCK = 128
ROPE_THETA = 500000.0
ROPE_DIM = HEAD_DIM // 4
ROPE_HALF = ROPE_DIM // 2
RNN_BLOCKS = 16
RNN_BLOCK_DIM = D_MODEL // RNN_BLOCKS
CONV_WIDTH = 4
RGLRU_C = 8.0
XATTN_HEADS = 4
XATTN_WIDTH = XATTN_HEADS * HEAD_DIM
D_FF = 4 * D_MODEL
RMS_EPS = 1e-6
SCALE = HEAD_DIM ** -0.5
LOG2E = math.log2(math.e)
NEG = -1e30

LANES = 128
SUBLANES = 8
BF16_ROWS = 16
VMEM_LIMIT = 48 * 1024 * 1024

NSA_KV = NSA_GROUPS * HEAD_DIM
OFF_Q = 0
OFF_KC = 1024
OFF_VC = 1280
OFF_KS = 1536
OFF_VS = 1792
OFF_KW = 2048
OFF_VW = 2304
OFF_QX = 2560
OFF_GATE = 2816
GATE_STRIDE = 64
OFF_XRNN = 3072
OFF_GRNN = 4096
OFF_MERGE = 5120
Z_WIDTH = 8192
INPROJ_TN = 2048
INPROJ_TM = 512
ROPE_COLS = (OFF_KS, OFF_KW)

NSA_PAIRS = 2
NSA_GPS = 2 * NSA_PAIRS
NSA_LANES = NSA_HPG * Q_BLOCK
SEL_CHUNK = 4 * Q_BLOCK
SEL_PIECE = SEL_CHUNK
WIN_TILES = WINDOW // Q_BLOCK + 1
K_AUG = 2 * HEAD_DIM
V_AUG = HEAD_DIM + BF16_ROWS

NT_DIMS = (((1,), (1,)), ((), ()))


def _sigmoid(x):
    return 1.0 / (1.0 + jnp.exp(-x))


def _sigmoid_tanh(x):
    return 0.5 * jnp.tanh(0.5 * x) + 0.5


def _gelu_tanh(x):
    return 0.5 * x * (1.0 + jnp.tanh(0.7978845608028654 * (x + 0.044715 * (x * x * x))))


def _rope_swap(z):
    n = z.shape[-1]
    lane = lax.broadcasted_iota(jnp.int32, z.shape, z.ndim - 1)
    first_half = (lane & (HEAD_DIM - 1)) < ROPE_HALF
    return jnp.where(first_half, pltpu.roll(z, n - ROPE_HALF, z.ndim - 1), pltpu.roll(z, ROPE_HALF, z.ndim - 1))


def _round_robin(gens, lead=0):
    results = [None] * len(gens)
    alive = list(range(len(gens)))

    def advance(idx):
        try:
            next(gens[idx])
        except StopIteration as stop:
            results[idx] = stop.value
            alive.remove(idx)

    for _ in range(lead):
        for idx in [n for n in alive if n < len(gens) // 2]:
            advance(idx)
    while alive:
        for idx in list(alive):
            advance(idx)
    return results


def _fold_rows(x, op):
    parts = [x[r * SUBLANES:(r + 1) * SUBLANES] for r in range(x.shape[0] // SUBLANES)]
    return functools.reduce(op, parts)


def _norm_proj_kernel(x_ref, g_ref, w_ref, cos_ref, sin_ref, z_ref, *, tn, rope_cols):
    x = x_ref[...]
    ms = jnp.mean(x * x, axis=-1, keepdims=True)
    u = (x * lax.rsqrt(ms + RMS_EPS) * g_ref[...]).astype(BF16)
    rw = cos_ref.shape[1]
    for n0 in range(0, w_ref.shape[1], tn):
        z = jnp.dot(u, w_ref[:, n0:n0 + tn], preferred_element_type=F32)
        z_ref[:, n0:n0 + tn] = z.astype(z_ref.dtype)
        for col in rope_cols:
            if n0 <= col < n0 + tn:
                zr = z[:, col - n0:col - n0 + rw]
                z_ref[:, col:col + rw] = (zr * cos_ref[...] + _rope_swap(zr) * sin_ref[...]).astype(z_ref.dtype)


def _norm_proj(x, g, w, cos_t, sin_t, *, tm, tn, rope_cols, out_dtype, name):
    t, d = x.shape
    n = w.shape[1]
    s_tiles = cos_t.shape[0] // tm
    rw = cos_t.shape[1]
    table_spec = pl.BlockSpec((tm, rw), lambda i: (i % s_tiles, 0))
    return pl.pallas_call(
        functools.partial(_norm_proj_kernel, tn=tn, rope_cols=tuple(rope_cols)),
        grid=(t // tm,),
        in_specs=[
            pl.BlockSpec((tm, d), lambda i: (i, 0)),
            pl.BlockSpec((1, d), lambda i: (0, 0)),
            pl.BlockSpec((d, n), lambda i: (0, 0), pipeline_mode=pl.Buffered(1)),
            table_spec, table_spec,
        ],
        out_specs=pl.BlockSpec((tm, n), lambda i: (i, 0)),
        out_shape=jax.ShapeDtypeStruct((t, n), out_dtype),
        compiler_params=pltpu.CompilerParams(
            dimension_semantics=("parallel",), vmem_limit_bytes=VMEM_LIMIT),
        name=name,
    )(x, g, w, cos_t, sin_t)


def _compress_kernel(xk_ref, xv_ref, pk_ref, pv_ref, wk1_ref, wv1_ref, w2_ref, cos_ref, sin_ref, o_ref, xs_scr):
    tm = o_ref.shape[0]

    def hidden(x_ref, p_ref, w_ref):
        xs_scr[...] = x_ref[...].astype(F32)
        first = jnp.zeros((tm, w_ref.shape[2]), F32)
        second = jnp.zeros((tm, w_ref.shape[2]), F32)
        for j in range(CMP_STRIDE):
            xj = xs_scr[pl.ds(j, tm, stride=CMP_STRIDE), :]
            first = first + jnp.dot((xj + p_ref[j:j + 1, :]).astype(BF16), w_ref[j],
                                    preferred_element_type=F32)
            second = second + jnp.dot((xj + p_ref[CMP_STRIDE + j:CMP_STRIDE + j + 1, :]).astype(BF16),
                                      w_ref[CMP_STRIDE + j], preferred_element_type=F32)
        return _gelu_tanh(first + pltpu.roll(second, tm - 1, 0)).astype(BF16)

    hcat = jnp.concatenate([hidden(xk_ref, pk_ref, wk1_ref), hidden(xv_ref, pv_ref, wv1_ref)], axis=1)
    kv = jnp.dot(hcat, w2_ref[...], preferred_element_type=F32)
    o_ref[...] = kv * cos_ref[...] + _rope_swap(kv) * sin_ref[...]


def _compress(z, pk, pv, wk1, wv1, w2, cos_t, sin_t, *, tm):
    t = z.shape[0]
    m = t // CMP_STRIDE
    pairs = NSA_GROUPS // 2
    pair_w = 2 * HEAD_DIM
    hid2 = wk1.shape[2]
    out_w = w2.shape[1]
    full = lambda shape: pl.BlockSpec(shape, lambda p, i: (0,) * len(shape))
    tokens = lambda off: pl.BlockSpec((tm * CMP_STRIDE, pair_w), lambda p, i: (i, off // pair_w + p))
    return pl.pallas_call(
        _compress_kernel,
        grid=(pairs, m // tm),
        in_specs=[
            tokens(OFF_KC), tokens(OFF_VC),
            full((CMP_LEN, pair_w)), full((CMP_LEN, pair_w)),
            full((CMP_LEN, pair_w, hid2)), full((CMP_LEN, pair_w, hid2)),
            full((2 * hid2, out_w)),
            full((tm, out_w)), full((tm, out_w)),
        ],
        out_specs=pl.BlockSpec((tm, out_w), lambda p, i: (i, p)),
        out_shape=jax.ShapeDtypeStruct((m, pairs * out_w), F32),
        scratch_shapes=[pltpu.VMEM((tm * CMP_STRIDE, pair_w), F32)],
        compiler_params=pltpu.CompilerParams(
            dimension_semantics=("parallel", "parallel"), vmem_limit_bytes=VMEM_LIMIT),
        name="compress",
    )(z, z, pk, pv, wk1, wv1, w2, cos_t, sin_t)


def _nsa_kernel(q_ref, g_ref, cos_ref, sin_ref, *rest, n_sel):
    kv_refs, rest = rest[:4 * NSA_PAIRS], rest[4 * NSA_PAIRS:]
    feat_ref, kvc_ref, c2st_ref, o_ref, gt_scr, s_scr, vt_scr = rest
    ks_refs, vs_refs, kw_refs, vw_refs = (kv_refs[n::4] for n in range(4))
    g2 = pl.program_id(1)
    i = pl.program_id(2)
    q0 = i * Q_BLOCK
    seq = feat_ref.shape[0]

    @pl.when(i == 0)
    def _():
        ones_rows = jnp.where(lax.broadcasted_iota(jnp.int32, (V_AUG - HEAD_DIM, Q_BLOCK), 0) == 0,
                              1.0, 0.0).astype(BF16)

        def fill(t, carry):
            r0 = pl.multiple_of(t * Q_BLOCK, Q_BLOCK)
            for kind, refs in enumerate((vs_refs, vw_refs)):
                for pair, ref in enumerate(refs):
                    vt = jnp.transpose(ref[pl.ds(r0, Q_BLOCK), :].astype(F32)).astype(BF16)
                    for half in range(2):
                        gg = 2 * pair + half
                        vt_scr[kind, gg, t, 0:HEAD_DIM, :] = vt[half * HEAD_DIM:(half + 1) * HEAD_DIM]
                        vt_scr[kind, gg, t, HEAD_DIM:V_AUG, :] = ones_rows
            return carry

        lax.fori_loop(0, seq // Q_BLOCK, fill, 0)

    def query_side(gg, qt, bias_rows, win_row):
        zeros_q = jnp.zeros((HEAD_DIM, NSA_LANES), BF16)
        q_rows = [qt, zeros_q] if gg % 2 == 0 else [zeros_q, qt]
        rid = lax.broadcasted_iota(jnp.int32, (BF16_ROWS, NSA_LANES), 0)
        win_rows = jnp.where(rid == 0, win_row, 0.0).astype(BF16)
        tail = jnp.zeros((2 * LANES - 2 * HEAD_DIM - n_sel - BF16_ROWS, NSA_LANES), BF16)
        return jnp.concatenate(q_rows + [bias_rows.astype(BF16), win_rows, tail], axis=0)

    def key_side(k_ref, row0, rows):
        return jnp.concatenate([k_ref[pl.ds(row0, rows), :], feat_ref[pl.ds(row0, rows), :]], axis=1)
    blocks_per_chunk = SEL_CHUNK // SEL_BLOCK
    q_per_chunk = SEL_CHUNK // Q_BLOCK

    def tile_heads(x):
        return jnp.concatenate([x] * NSA_HPG, axis=1)

    tq = q0 + (lax.broadcasted_iota(jnp.int32, (1, NSA_LANES), 1) & (Q_BLOCK - 1))
    c_idx = lax.broadcasted_iota(jnp.int32, (Q_BLOCK, 1), 0)
    r_idx = lax.broadcasted_iota(jnp.int32, (1, Q_BLOCK), 1)
    tri_diag = tile_heads(jnp.where(c_idx <= r_idx, 0.0, NEG))
    tri_old = tile_heads(jnp.where(c_idx > r_idx, 0.0, NEG))
    cos_q = tile_heads(cos_ref[...])
    sin_q = tile_heads(sin_ref[...])
    blk = lax.broadcasted_iota(jnp.int32, (n_sel, Q_BLOCK), 0)
    cur = (q0 + lax.broadcasted_iota(jnp.int32, (n_sel, Q_BLOCK), 1)) // SEL_BLOCK
    sub = lax.broadcasted_iota(jnp.int32, (SUBLANES, Q_BLOCK), 0)

    gt_scr[...] = jnp.transpose(_sigmoid(g_ref[...].astype(F32)))

    def prepare(gg):
        qx = jnp.transpose(q_ref[:, gg * NSA_HPG * HEAD_DIM:(gg + 1) * NSA_HPG * HEAD_DIM].astype(F32)
                           * (SCALE * LOG2E))
        qf = jnp.concatenate([qx[h * HEAD_DIM:(h + 1) * HEAD_DIM, :] for h in range(NSA_HPG)], axis=1)
        x1, x2 = qf[0:ROPE_HALF], qf[ROPE_HALF:ROPE_DIM]
        qt = jnp.concatenate([x1 * cos_q - x2 * sin_q, x2 * cos_q + x1 * sin_q, qf[ROPE_DIM:]],
                             axis=0).astype(BF16)
        qts[gg] = qt
        yield

        kvc = kvc_ref[:, gg * LANES:(gg + 1) * LANES]
        kc = kvc[:, :HEAD_DIM].astype(BF16)
        vct = jnp.transpose(kvc)[HEAD_DIM:, :].astype(BF16)
        sc = jnp.dot(kc, qt, preferred_element_type=F32)
        yield
        n_idx = lax.broadcasted_iota(jnp.int32, (LANES, 1), 0)
        sc = jnp.where((n_idx * CMP_STRIDE + (CMP_LEN - 1)) <= tq, sc, NEG)
        m_c = jnp.max(sc, axis=0, keepdims=True)
        e_c = jnp.exp2(sc - m_c)
        inv_c = 1.0 / jnp.maximum(jnp.sum(e_c, axis=0, keepdims=True), 1e-30)
        p_c = e_c * jnp.where(m_c > 0.5 * NEG, inv_c, 0.0)
        o_c = jnp.dot(vct, p_c.astype(BF16), preferred_element_type=F32)
        yield

        p_sum = p_c[:, 0:Q_BLOCK]
        for h in range(1, NSA_HPG):
            p_sum = p_sum + p_c[:, h * Q_BLOCK:(h + 1) * Q_BLOCK]
        imp = jnp.dot(c2st_ref[...], p_sum, preferred_element_type=F32,
                      precision=lax.Precision.HIGHEST)
        yield
        forced = (blk == 0) | (blk == cur) | (blk == cur - 1)
        val = jnp.where(forced, jnp.inf, jnp.where(blk > cur, -jnp.inf, imp))
        n_grp = n_sel // SUBLANES
        grp = [val[v * SUBLANES:(v + 1) * SUBLANES] for v in range(n_grp)]
        cnt = [jnp.zeros((SUBLANES, Q_BLOCK), F32) for _ in range(n_grp)]
        for sp in range(n_sel):
            r = val[sp:sp + 1, :]
            for v in range(n_grp):
                if v * SUBLANES > sp:
                    cnt[v] = cnt[v] + jnp.where(r >= grp[v], 1.0, 0.0)
                elif v * SUBLANES + SUBLANES - 1 <= sp:
                    cnt[v] = cnt[v] + jnp.where(r > grp[v], 1.0, 0.0)
                else:
                    later = jnp.where(sub > sp - v * SUBLANES, 1.0, 0.0)
                    cnt[v] = cnt[v] + jnp.where(r > grp[v], 1.0, jnp.where(r == grp[v], later, 0.0))
            if sp % SUBLANES == SUBLANES - 1:
                yield
        chosen = jnp.concatenate([jnp.where(c < float(SEL_TOPK), 0.0, NEG) for c in cnt], axis=0)
        block_bias = tile_heads(jnp.where(blk > cur, NEG, chosen))
        row0 = pl.multiple_of((g2 * NSA_GPS + gg) * GATE_STRIDE, GATE_STRIDE)
        gt = gt_scr[pl.ds(row0, 2 * SUBLANES), :]
        return qt, o_c, block_bias, gt

    def window(gg):
        qt = qts[gg]
        tiles = []
        for d in range(1 - WIN_TILES, 1):
            td = i + d
            tc = jnp.maximum(td, 0)
            k0 = pl.multiple_of(tc * Q_BLOCK, Q_BLOCK)
            before_start = jnp.where(td >= 0, 0.0, NEG) if d < 0 else 0.0
            tri = tri_diag if d == 0 else (tri_old if d == 1 - WIN_TILES else None)
            tiles.append((tc, k0, before_start, tri))

        zero_row = jnp.zeros((1, NSA_LANES), F32)
        no_bias = jnp.zeros((n_sel, NSA_LANES), BF16)
        scores = []
        for tc, k0, before_start, tri in tiles:
            q_aug = query_side(gg, qt, no_bias, zero_row + before_start)
            s = jnp.dot(key_side(kw_refs[gg // 2], k0, Q_BLOCK), q_aug, preferred_element_type=F32)
            scores.append(s if tri is None else s + tri)
            yield
        m_w = jnp.max(functools.reduce(jnp.maximum, [_fold_rows(s, jnp.maximum) for s in scores]),
                      axis=0, keepdims=True)
        n_pad = jnp.maximum(WINDOW - 1 - tq, 0).astype(F32)
        m_w = jnp.where(n_pad > 0.0, jnp.maximum(m_w, 0.0), m_w)
        yield
        acc = jnp.zeros((V_AUG, NSA_LANES), F32)
        for (tc, _, _, _), s in zip(tiles, scores):
            acc = acc + jnp.dot(vt_scr[1, gg, tc], jnp.exp2(s - m_w).astype(BF16), preferred_element_type=F32)
            yield
        l_w = acc[HEAD_DIM:HEAD_DIM + 1] + n_pad * jnp.exp2(-m_w)
        return acc[:HEAD_DIM] * (1.0 / jnp.maximum(l_w, 1e-30))

    pieces_per_chunk = SEL_CHUNK // SEL_PIECE
    tiles_per_piece = SEL_PIECE // Q_BLOCK
    blocks_per_piece = SEL_PIECE // SEL_BLOCK

    def chunk_scores(gg, c, diag_rel):
        zeros_q = jnp.zeros((HEAD_DIM, NSA_LANES), BF16)
        q_rows = jnp.concatenate([qts[gg], zeros_q] if gg % 2 == 0 else [zeros_q, qts[gg]], axis=0)
        bmax = []
        for piece in range(c * pieces_per_chunk, (c + 1) * pieces_per_chunk):
            rows = slice(piece * SEL_PIECE, (piece + 1) * SEL_PIECE)
            s = jnp.dot(ks_refs[gg // 2][rows, :], q_rows, preferred_element_type=F32)
            if diag_rel is not None:
                first_tile = (piece - c * pieces_per_chunk) * tiles_per_piece
                s = s + jnp.concatenate([jnp.where(diag_rel == first_tile + u, tri_diag, 0.0)
                                         for u in range(tiles_per_piece)], axis=0)
            s_scr[gg, rows, :] = s
            bmax += [_fold_rows(s[b * SEL_BLOCK:(b + 1) * SEL_BLOCK], jnp.maximum) for b in range(blocks_per_piece)]
            yield
        return bmax

    def first_chunk(gg):
        yield
        return (yield from chunk_scores(gg, 0, i))

    def selected(gg, k, block_bias, bmax0):
        bmax = list(bmax0)
        for c in range(1, k + 1):
            bmax += yield from chunk_scores(gg, c, i - k * q_per_chunk if c == k else None)
        m8 = functools.reduce(jnp.maximum, [bm + block_bias[n:n + 1, :] for n, bm in enumerate(bmax)])
        m_s = jnp.max(m8, axis=0, keepdims=True)
        shift = block_bias - m_s
        yield
        acc = jnp.zeros((V_AUG, NSA_LANES), F32)
        for piece in range((k + 1) * pieces_per_chunk):
            ps = []
            for b in range(blocks_per_piece):
                n = piece * blocks_per_piece + b
                ps.append(jnp.exp2(s_scr[gg, n * SEL_BLOCK:(n + 1) * SEL_BLOCK, :] + shift[n:n + 1, :]).astype(BF16))
            vt = jnp.concatenate([vt_scr[0, gg, piece * tiles_per_piece + u] for u in range(tiles_per_piece)], axis=1)
            acc = acc + jnp.dot(vt, jnp.concatenate(ps, axis=0), preferred_element_type=F32)
            yield
        l_s = acc[HEAD_DIM:HEAD_DIM + 1]
        return acc[:HEAD_DIM] * (1.0 / jnp.maximum(l_s, 1e-30))

    qts = [None] * NSA_GPS
    groups = range(NSA_GPS)
    shared = _round_robin([prepare(gg) for gg in groups] + [window(gg) for gg in groups]
                          + [first_chunk(gg) for gg in groups])
    prepared, windows, bmax0 = (shared[n * NSA_GPS:(n + 1) * NSA_GPS] for n in range(3))

    def finish(k):
        branches = []
        for gg, (qt, o_c, block_bias, gt) in enumerate(prepared):
            branches.append(selected(gg, k, block_bias, bmax0[gg]))
        results = _round_robin(branches, lead=k + 1)
        outs = []
        for gg, (qt, o_c, block_bias, gt) in enumerate(prepared):
            o_w, o_s = windows[gg], results[gg]
            for h in range(NSA_HPG):
                sl = slice(h * Q_BLOCK, (h + 1) * Q_BLOCK)
                row = NSA_BRANCHES * h
                outs.append(gt[row:row + 1, :] * o_c[:, sl] + gt[row + 1:row + 2, :] * o_s[:, sl]
                            + gt[row + 2:row + 3, :] * o_w[:, sl])
        o_ref[...] = jnp.transpose(jnp.concatenate(outs, axis=0))

    for k in range(n_sel * SEL_BLOCK // SEL_CHUNK):
        pl.when(i // q_per_chunk == k)(functools.partial(finish, k))


def _nsa(z, cos_q, sin_q, feat, kvc, c2st, *, batch, seq):
    nq = seq // Q_BLOCK
    n_sel = seq // SEL_BLOCK
    gw = NSA_GPS * NSA_HPG * HEAD_DIM
    gates_w = NSA_GROUPS * GATE_STRIDE
    pair_w = 2 * HEAD_DIM
    assert pair_w == LANES
    kv_spec = lambda off, pair: pl.BlockSpec(
        (seq, pair_w), lambda b, g, i: (b, off // pair_w + g * NSA_PAIRS + pair))
    kv_specs = [kv_spec(off, pair) for pair in range(NSA_PAIRS) for off in (OFF_KS, OFF_VS, OFF_KW, OFF_VW)]
    rope_spec = pl.BlockSpec((ROPE_HALF, Q_BLOCK), lambda b, g, i: (0, i))
    return pl.pallas_call(
        functools.partial(_nsa_kernel, n_sel=n_sel),
        grid=(batch, NSA_GROUPS // NSA_GPS, nq),
        in_specs=[
            pl.BlockSpec((Q_BLOCK, gw), lambda b, g, i: (b * nq + i, OFF_Q // gw + g)),
            pl.BlockSpec((Q_BLOCK, gates_w), lambda b, g, i: (b * nq + i, OFF_GATE // gates_w)),
            rope_spec, rope_spec,
            *kv_specs,
            pl.BlockSpec((seq, LANES), lambda b, g, i: (0, 0)),
            pl.BlockSpec((LANES, NSA_GPS * 2 * HEAD_DIM), lambda b, g, i: (b, g)),
            pl.BlockSpec((n_sel, LANES), lambda b, g, i: (0, 0)),
        ],
        out_specs=pl.BlockSpec((Q_BLOCK, gw), lambda b, g, i: (b * nq + i, g)),
        out_shape=jax.ShapeDtypeStruct((batch * seq, NSA_HEADS * HEAD_DIM), F32),
        scratch_shapes=[pltpu.VMEM((gates_w, Q_BLOCK), F32), pltpu.VMEM((NSA_GPS, seq, NSA_LANES), F32),
                        pltpu.VMEM((2, NSA_GPS, nq, V_AUG, Q_BLOCK), BF16)],
        compiler_params=pltpu.CompilerParams(
            dimension_semantics=("arbitrary", "arbitrary", "arbitrary"), vmem_limit_bytes=VMEM_LIMIT),
        name="nsa",
    )(z, z, cos_q, sin_q, *([z] * len(kv_specs)), feat, kvc, c2st)


def _rglru_kernel(x_ref, gr_ref, cw_ref, cb_ref, wa_ref, ba_ref, wi_ref, bi_ref, lam_ref, y_ref,
                  xs_scr, a_scr, b_scr, h_scr):
    nb, ts, c = x_ref.shape
    pitch = a_scr.shape[1] // nb
    lane_tiles = c // LANES

    def put(scr, rows, val):
        for m in range(lane_tiles):
            scr[m, rows, :] = val[:, m * LANES:(m + 1) * LANES]

    def get(scr, rows):
        return jnp.concatenate([scr[m, rows, :] for m in range(lane_tiles)], axis=1)

    @pl.when(pl.program_id(0) == 0)
    def _():
        xs_scr[:, 0:SUBLANES, :] = jnp.zeros((nb, SUBLANES, c), F32)
        h_scr[...] = jnp.zeros(h_scr.shape, F32)

    nl = -lam_ref[...]
    softplus = jnp.maximum(nl, 0.0) + jnp.log1p(jnp.exp(-jnp.abs(nl)))

    def gates(bi_, carry):
        x = x_ref[bi_].astype(F32)
        xs_scr[bi_, SUBLANES:SUBLANES + ts, :] = x
        xc = cb_ref[...] + cw_ref[CONV_WIDTH - 1:CONV_WIDTH, :] * x
        for k in range(CONV_WIDTH - 1):
            off = SUBLANES - (CONV_WIDTH - 1) + k
            xc = xc + cw_ref[k:k + 1, :] * xs_scr[bi_, off:off + ts, :]
        xs_scr[bi_, 0:SUBLANES, :] = x[ts - SUBLANES:, :]

        ra, ri = [], []
        for mblk in range(c // LANES):
            xb = xc[:, mblk * LANES:(mblk + 1) * LANES].astype(BF16)
            ra.append(jnp.dot(xb, wa_ref[mblk], preferred_element_type=F32))
            ri.append(jnp.dot(xb, wi_ref[mblk], preferred_element_type=F32))
        r = _sigmoid(jnp.concatenate(ra, axis=1) + ba_ref[...])
        gi = _sigmoid(jnp.concatenate(ri, axis=1) + bi_ref[...])

        log_a = (-RGLRU_C) * r * softplus
        th = jnp.tanh(log_a)
        y2 = -2.0 * th / (1.0 - th)
        rows = pl.ds(pl.multiple_of(bi_ * pitch, SUBLANES), ts)
        put(a_scr, rows, jnp.exp(log_a))
        put(b_scr, rows, jnp.where(y2 > 0.0, y2 * lax.rsqrt(y2), 0.0) * (gi * xc))
        return carry

    lax.fori_loop(0, nb, gates, 0)

    def step(t, h):
        rows = pl.ds(t, nb, stride=pitch)
        h = get(a_scr, rows) * h + get(b_scr, rows)
        put(b_scr, rows, h)
        return h

    h_scr[...] = lax.fori_loop(0, ts, step, h_scr[...], unroll=8)

    def outputs(bi_, carry):
        rows = pl.ds(pl.multiple_of(bi_ * pitch, SUBLANES), ts)
        y_ref[bi_] = _gelu_tanh(gr_ref[bi_].astype(F32)) * get(b_scr, rows)
        return carry

    lax.fori_loop(0, nb, outputs, 0)


def _rglru(z, cw, cb, wa2, ba, wi2, bi, lam, *, batch, seq, ts):
    c = D_MODEL
    pitch = ts + SUBLANES
    z3 = z.reshape(batch, seq, z.shape[1])
    vec = pl.BlockSpec((1, c), lambda t: (0, 0))
    y = pl.pallas_call(
        _rglru_kernel,
        grid=(seq // ts,),
        in_specs=[
            pl.BlockSpec((batch, ts, c), lambda t: (0, t, OFF_XRNN // c)),
            pl.BlockSpec((batch, ts, c), lambda t: (0, t, OFF_GRNN // c)),
            pl.BlockSpec((CONV_WIDTH, c), lambda t: (0, 0)),
            vec,
            pl.BlockSpec((c // LANES, LANES, LANES), lambda t: (0, 0, 0)),
            vec,
            pl.BlockSpec((c // LANES, LANES, LANES), lambda t: (0, 0, 0)),
            vec, vec,
        ],
        out_specs=pl.BlockSpec((batch, ts, c), lambda t: (0, t, 0)),
        out_shape=jax.ShapeDtypeStruct((batch, seq, c), F32),
        scratch_shapes=[pltpu.VMEM((batch, SUBLANES + ts, c), F32), pltpu.VMEM((c // LANES, batch * pitch, LANES), F32),
                        pltpu.VMEM((c // LANES, batch * pitch, LANES), F32), pltpu.VMEM((batch, c), F32)],
        compiler_params=pltpu.CompilerParams(
            dimension_semantics=("arbitrary",), vmem_limit_bytes=VMEM_LIMIT),
        name="rglru",
    )(z3, z3, cw, cb, wa2, ba, wi2, bi, lam)
    return y.reshape(batch * seq, c)


def _merge_kernel(x_ref, m0_ref, m1_ref, m2_ref, ynsa_ref, yrnn_ref, qx_ref, km_ref, vm_ref, wxo_ref, wo_ref, h_ref):
    tq = x_ref.shape[0]
    mem_len = km_ref.shape[0]
    row_chunk = tq // 4

    def cross_attention():
        qt = jnp.transpose(qx_ref[...].astype(F32) * (SCALE * LOG2E))
        row_head = lax.broadcasted_iota(jnp.int32, qt.shape, 0) // HEAD_DIM
        q_bd = jnp.concatenate([jnp.where(row_head == h, qt, 0.0) for h in range(XATTN_HEADS)],
                               axis=1).astype(BF16)
        yield
        s = jnp.dot(km_ref[...], q_bd, preferred_element_type=F32)
        yield
        m = jnp.max(_fold_rows(s, jnp.maximum), axis=0, keepdims=True)
        p = jnp.exp2(s - m).astype(BF16)
        yield
        ones_rows = jnp.where(lax.broadcasted_iota(jnp.int32, (BF16_ROWS, mem_len), 0) == 0, 1.0, 0.0)
        vt = jnp.concatenate([jnp.transpose(vm_ref[...].astype(F32)), ones_rows], axis=0).astype(BF16)
        o_all = jnp.dot(vt, p, preferred_element_type=F32)
        yield
        outs = []
        for h in range(XATTN_HEADS):
            lanes = slice(h * tq, (h + 1) * tq)
            inv_l = 1.0 / o_all[XATTN_WIDTH:XATTN_WIDTH + 1, lanes]
            outs.append(o_all[h * HEAD_DIM:(h + 1) * HEAD_DIM, lanes] * inv_l)
        o = jnp.transpose(jnp.concatenate(outs, axis=0)).astype(BF16)
        yield
        return jnp.dot(o, wxo_ref[...], preferred_element_type=F32)

    def gated(gate_ref, val_ref):
        parts = []
        for r in range(0, tq, row_chunk):
            parts.append(_sigmoid_tanh(gate_ref[r:r + row_chunk, :].astype(F32)) * val_ref[r:r + row_chunk, :])
            yield
        return jnp.concatenate(parts, axis=0)

    def gate_only(gate_ref):
        parts = []
        for r in range(0, tq, row_chunk):
            parts.append(_sigmoid_tanh(gate_ref[r:r + row_chunk, :].astype(F32)))
            yield
        return jnp.concatenate(parts, axis=0)

    yx, y_a, y_b, g_x = _round_robin([cross_attention(), gated(m0_ref, ynsa_ref), gated(m1_ref, yrnn_ref),
                                      gate_only(m2_ref)])
    y = y_a + y_b + g_x * yx
    h_ref[...] = x_ref[...] + jnp.dot(y.astype(BF16), wo_ref[...], preferred_element_type=F32)


def _merge(x, z, ynsa, yrnn, kvm, wxo, wo, *, seq, mem_len, tq):
    t, d = x.shape
    nt = seq // tq
    row = lambda cb: pl.BlockSpec((tq, d), lambda i, cb=cb: (i, cb))
    mem_spec = lambda cb: pl.BlockSpec((mem_len, XATTN_WIDTH), lambda i, cb=cb: (i // nt, cb))
    return pl.pallas_call(
        _merge_kernel,
        grid=(t // tq,),
        in_specs=[
            row(0),
            row(OFF_MERGE // d), row(OFF_MERGE // d + 1), row(OFF_MERGE // d + 2),
            row(0), row(0),
            pl.BlockSpec((tq, XATTN_WIDTH), lambda i: (i, OFF_QX // XATTN_WIDTH)),
            mem_spec(0), mem_spec(1),
            pl.BlockSpec((XATTN_WIDTH, d), lambda i: (0, 0)),
            pl.BlockSpec((d, d), lambda i: (0, 0)),
        ],
        out_specs=row(0),
        out_shape=jax.ShapeDtypeStruct((t, d), F32),
        compiler_params=pltpu.CompilerParams(
            dimension_semantics=("parallel",), vmem_limit_bytes=VMEM_LIMIT),
        name="merge",
    )(x, z, z, z, ynsa, yrnn, z, kvm, kvm, wxo, wo)


def _mlp_kernel(h_ref, g_ref, wup_ref, wdn_ref, gf_ref, o_ref, *, tf):
    h = h_ref[...]
    ms = jnp.mean(h * h, axis=-1, keepdims=True)
    v = (h * lax.rsqrt(ms + RMS_EPS) * g_ref[...]).astype(BF16)
    h2 = h
    for f in range(0, wup_ref.shape[1], tf):
        up = jnp.dot(v, wup_ref[:, f:f + tf], preferred_element_type=F32)
        act = jnp.square(jnp.maximum(up, 0.0)).astype(BF16)
        h2 = h2 + jnp.dot(act, wdn_ref[f:f + tf, :], preferred_element_type=F32)
    ms = jnp.mean(h2 * h2, axis=-1, keepdims=True)
    o_ref[...] = h2 * lax.rsqrt(ms + RMS_EPS) * gf_ref[...]


def _mlp(h, g, wup, wdn, gf, *, tm, tf):
    t, d = h.shape
    dff = wup.shape[1]
    resident = lambda shape: pl.BlockSpec(shape, lambda i: (0, 0), pipeline_mode=pl.Buffered(1))
    return pl.pallas_call(
        functools.partial(_mlp_kernel, tf=tf),
        grid=(t // tm,),
        in_specs=[
            pl.BlockSpec((tm, d), lambda i: (i, 0)),
            pl.BlockSpec((1, d), lambda i: (0, 0)),
            resident((d, dff)), resident((dff, d)),
            pl.BlockSpec((1, d), lambda i: (0, 0)),
        ],
        out_specs=pl.BlockSpec((tm, d), lambda i: (i, 0)),
        out_shape=jax.ShapeDtypeStruct((t, d), F32),
        compiler_params=pltpu.CompilerParams(
            dimension_semantics=("parallel",), vmem_limit_bytes=VMEM_LIMIT),
        name="mlp",
    )(h, g, wup, wdn, gf)


def _rope_angles(pos):
    inv = 1.0 / (ROPE_THETA ** (jnp.arange(0, ROPE_DIM, 2, dtype=F32) / ROPE_DIM))
    ang = pos.astype(F32)[:, None] * inv[None, :]
    return jnp.cos(ang), jnp.sin(ang)


def _rope_tables(pos, width, rope_width=None):
    rope_width = width if rope_width is None else rope_width
    cos, sin = _rope_angles(pos)
    n = pos.shape[0]
    pad = HEAD_DIM - ROPE_DIM
    cos_h = jnp.concatenate([cos, cos, jnp.ones((n, pad), F32)], axis=1)
    sin_h = jnp.concatenate([-sin, sin, jnp.zeros((n, pad), F32)], axis=1)
    reps = rope_width // HEAD_DIM
    rest = width - rope_width
    cos_t = jnp.concatenate([jnp.tile(cos_h, (1, reps)), jnp.ones((n, rest), F32)], axis=1)
    sin_t = jnp.concatenate([jnp.tile(sin_h, (1, reps)), jnp.zeros((n, rest), F32)], axis=1)
    return cos_t, sin_t


def _pack_w_in(w_in):
    d = w_in.shape[0]
    w = w_in.astype(BF16)
    n_qkv = NSA_HEADS * HEAD_DIM + 6 * NSA_KV
    n_gates = NSA_HEADS * NSA_BRANCHES
    rnn_lo = n_qkv + n_gates
    qx_lo = rnn_lo + 2 * D_MODEL
    mg_lo = qx_lo + XATTN_WIDTH
    per_group = n_gates // NSA_GROUPS
    gates = w[:, n_qkv:rnn_lo].reshape(d, NSA_GROUPS, per_group)
    gates = jnp.pad(gates, ((0, 0), (0, 0), (0, GATE_STRIDE - per_group))).reshape(d, NSA_GROUPS * GATE_STRIDE)
    return jnp.concatenate([w[:, :n_qkv], w[:, qx_lo:mg_lo], gates, w[:, rnn_lo:qx_lo], w[:, mg_lo:]], axis=1)


def _block_diag_pairs(w):
    nb, k, _ = w.shape
    w = w.reshape(nb // 2, 2, k, k)
    zero = jnp.zeros((nb // 2, k, k), w.dtype)
    top = jnp.concatenate([w[:, 0], zero], axis=2)
    bot = jnp.concatenate([zero, w[:, 1]], axis=2)
    return jnp.concatenate([top, bot], axis=1).astype(BF16)


def _cmp_to_sel_t(n_cmp, n_cmp_pad, n_sel):
    c0 = np.arange(n_cmp_pad)[None, :] * CMP_STRIDE
    s0 = np.arange(n_sel)[:, None] * SEL_BLOCK
    ov = np.clip(np.minimum(c0 + CMP_LEN, s0 + SEL_BLOCK) - np.maximum(c0, s0), 0, None)
    ov = np.where(np.arange(n_cmp_pad)[None, :] < n_cmp, ov, 0)
    return (ov / CMP_LEN).astype(np.float32)


def _key_features(seq, n_sel):
    col = np.arange(LANES)[None, :]
    onehot = (np.arange(seq)[:, None] // SEL_BLOCK == col) & (col < n_sel)
    return jnp.asarray(onehot | (col == n_sel), dtype=BF16)


def kernel(x, mem, g_mix, w_in, cmp_pos_k, cmp_pos_v, w_cmp_k1, w_cmp_k2, w_cmp_v1, w_cmp_v2, conv_w, conv_b, w_rg_a, b_rg_a, w_rg_i, b_rg_i, rg_lambda, g_mem, w_mem_kv, w_xo, w_o, g_mlp, w_up, w_down, g_final):
    batch, seq, d = x.shape
    mem_len = mem.shape[1]
    t = batch * seq
    depth = g_mix.shape[0]
    n_sel = seq // SEL_BLOCK
    n_cmp_pad = seq // CMP_STRIDE
    assert n_cmp_pad == LANES and n_sel <= K_AUG - HEAD_DIM and d == D_MODEL

    pos = jnp.arange(seq)
    cos_k, sin_k = _rope_tables(pos, NSA_KV)
    cos_q, sin_q = (a.T for a in _rope_angles(pos))
    no_rope = jnp.zeros((mem_len, LANES), F32)
    cmp_pos = jnp.arange(n_cmp_pad) * CMP_STRIDE + (CMP_LEN - 1)
    cmp_tm = 4 * n_cmp_pad
    cos_c, sin_c = _rope_tables(cmp_pos, 2 * HEAD_DIM, HEAD_DIM)
    cos_c = jnp.tile(cos_c, (cmp_tm // n_cmp_pad, 2))
    sin_c = jnp.tile(sin_c, (cmp_tm // n_cmp_pad, 2))
    c2st = jnp.asarray(_cmp_to_sel_t((seq - CMP_LEN) // CMP_STRIDE + 1, n_cmp_pad, n_sel))
    feat = _key_features(seq, n_sel)

    h = x.reshape(t, d)
    for l in range(depth):
        z = _norm_proj(h, g_mix[l][None, :], _pack_w_in(w_in[l]), cos_k, sin_k, tm=INPROJ_TM, tn=INPROJ_TN,
                       rope_cols=ROPE_COLS, out_dtype=BF16, name="inproj")

        def pair_w1(w1):
            w3 = w1.astype(BF16).reshape(CMP_LEN, 1, HEAD_DIM, CMP_HIDDEN)
            zero = jnp.zeros_like(w3)
            both = jnp.concatenate([jnp.concatenate([w3, zero], axis=3), jnp.concatenate([zero, w3], axis=3)], axis=1)
            return both.reshape(CMP_LEN, 2 * HEAD_DIM, 2 * CMP_HIDDEN)

        def pair_pos(p):
            return jnp.concatenate([p, p], axis=1)

        w2 = jnp.zeros((2, 2, CMP_HIDDEN, 2, 2, HEAD_DIM), F32)
        for grp in range(2):
            w2 = w2.at[0, grp, :, grp, 0, :].set(w_cmp_k2[l]).at[1, grp, :, grp, 1, :].set(w_cmp_v2[l])
        w2 = w2.reshape(4 * CMP_HIDDEN, 4 * HEAD_DIM)
        kvc = _compress(z, pair_pos(cmp_pos_k[l]), pair_pos(cmp_pos_v[l]), pair_w1(w_cmp_k1[l]), pair_w1(w_cmp_v1[l]),
                        w2.astype(BF16), cos_c, sin_c, tm=cmp_tm)
        y_nsa = _nsa(z, cos_q, sin_q, feat, kvc, c2st, batch=batch, seq=seq)

        y_rnn = _rglru(z, conv_w[l], conv_b[l][None, :], _block_diag_pairs(w_rg_a[l]), b_rg_a[l][None, :],
                       _block_diag_pairs(w_rg_i[l]), b_rg_i[l][None, :], rg_lambda[l][None, :],
                       batch=batch, seq=seq, ts=128)

        kvm = _norm_proj(mem.reshape(batch * mem_len, d), g_mem[l][None, :], w_mem_kv[l].astype(BF16),
                         no_rope, no_rope, tm=mem_len, tn=2 * XATTN_WIDTH, rope_cols=(),
                         out_dtype=BF16, name="memkv")

        h1 = _merge(h, z, y_nsa, y_rnn, kvm, w_xo[l].astype(BF16), w_o[l].astype(BF16),
                    seq=seq, mem_len=mem_len, tq=512)
        last = l == depth - 1
        gf = g_final if last else jnp.ones_like(g_final)
        h = _mlp(h1, g_mlp[l][None, :], w_up[l].astype(BF16), w_down[l].astype(BF16), gf[None, :],
                 tm=1024, tf=1024)
        assert last, "final norm is fused into the last layer's MLP kernel"
    return h.reshape(batch, seq, d)
```

```python
import functools
import math

import numpy as np
import jax
import jax.numpy as jnp
from jax import lax
from jax.experimental import pallas as pl
from jax.experimental.pallas import tpu as pltpu

F32 = jnp.float32
BF16 = jnp.bfloat16

D_MODEL = 1024
HEAD_DIM = 64
NSA_HEADS = 16
NSA_GROUPS = 4
NSA_HPG = NSA_HEADS // NSA_GROUPS
NSA_BRANCHES = 3
CMP_LEN = 32
CMP_STRIDE = 16
CMP_HIDDEN = 256
SEL_BLOCK = 64
SEL_TOPK = 8
WINDOW = 512
Q_BLOCK = 128
ROPE_THETA = 500000.0
ROPE_DIM = HEAD_DIM // 4
ROPE_HALF = ROPE_DIM // 2
RNN_BLOCKS = 16
RNN_BLOCK_DIM = D_MODEL // RNN_BLOCKS
CONV_WIDTH = 4
RGLRU_C = 8.0
XATTN_HEADS = 4
XATTN_WIDTH = XATTN_HEADS * HEAD_DIM
D_FF = 4 * D_MODEL
RMS_EPS = 1e-6
SCALE = HEAD_DIM ** -0.5
LOG2E = math.log2(math.e)
NEG = -1e30

LANES = 128
SUBLANES = 8
BF16_ROWS = 16
VMEM_LIMIT = 48 * 1024 * 1024

NSA_KV = NSA_GROUPS * HEAD_DIM
OFF_Q = 0
OFF_KC = 1024
OFF_VC = 1280
OFF_KS = 1536
OFF_VS = 1792
OFF_KW = 2048
OFF_VW = 2304
OFF_QX = 2560
OFF_GATE = 2816
GATE_STRIDE = 64
OFF_XRNN = 3072
OFF_GRNN = 4096
OFF_MERGE = 5120
Z_WIDTH = 8192
INPROJ_TN = 2048
INPROJ_TM = 512
ROPE_COLS = (OFF_KS, OFF_KW)

NSA_PAIRS = 2
NSA_GPS = 2 * NSA_PAIRS
NSA_LANES = NSA_HPG * Q_BLOCK
SEL_CHUNK = 2 * Q_BLOCK
SEL_PIECE = SEL_CHUNK
WIN_TILES = WINDOW // Q_BLOCK + 1
K_AUG = 2 * HEAD_DIM
V_AUG = HEAD_DIM + BF16_ROWS

NT_DIMS = (((1,), (1,)), ((), ()))


def _sigmoid(x):
    return 1.0 / (1.0 + jnp.exp(-x))


def _sigmoid_tanh(x):
    return 0.5 * jnp.tanh(0.5 * x) + 0.5


def _gelu_tanh(x):
    return 0.5 * x * (1.0 + jnp.tanh(0.7978845608028654 * (x + 0.044715 * (x * x * x))))


def _rope_swap(z):
    n = z.shape[-1]
    lane = lax.broadcasted_iota(jnp.int32, z.shape, z.ndim - 1)
    first_half = (lane & (HEAD_DIM - 1)) < ROPE_HALF
    return jnp.where(first_half, pltpu.roll(z, n - ROPE_HALF, z.ndim - 1), pltpu.roll(z, ROPE_HALF, z.ndim - 1))


def _round_robin(gens, lead=0):
    results = [None] * len(gens)
    alive = list(range(len(gens)))

    def advance(idx):
        try:
            next(gens[idx])
        except StopIteration as stop:
            results[idx] = stop.value
            alive.remove(idx)

    for _ in range(lead):
        for idx in [n for n in alive if n < len(gens) // 2]:
            advance(idx)
    while alive:
        for idx in list(alive):
            advance(idx)
    return results


def _fold_rows(x, op):
    parts = [x[r * SUBLANES:(r + 1) * SUBLANES] for r in range(x.shape[0] // SUBLANES)]
    return functools.reduce(op, parts)


def _norm_proj_kernel(x_ref, g_ref, w_ref, cos_ref, sin_ref, z_ref, *, tn, rope_cols):
    x = x_ref[...]
    ms = jnp.mean(x * x, axis=-1, keepdims=True)
    u = (x * lax.rsqrt(ms + RMS_EPS) * g_ref[...]).astype(BF16)
    rw = cos_ref.shape[1]
    for n0 in range(0, w_ref.shape[1], tn):
        z = jnp.dot(u, w_ref[:, n0:n0 + tn], preferred_element_type=F32)
        z_ref[:, n0:n0 + tn] = z.astype(z_ref.dtype)
        for col in rope_cols:
            if n0 <= col < n0 + tn:
                zr = z[:, col - n0:col - n0 + rw]
                z_ref[:, col:col + rw] = (zr * cos_ref[...] + _rope_swap(zr) * sin_ref[...]).astype(z_ref.dtype)


def _norm_proj(x, g, w, cos_t, sin_t, *, tm, tn, rope_cols, out_dtype, name):
    t, d = x.shape
    n = w.shape[1]
    s_tiles = cos_t.shape[0] // tm
    rw = cos_t.shape[1]
    table_spec = pl.BlockSpec((tm, rw), lambda i: (i % s_tiles, 0))
    return pl.pallas_call(
        functools.partial(_norm_proj_kernel, tn=tn, rope_cols=tuple(rope_cols)),
        grid=(t // tm,),
        in_specs=[
            pl.BlockSpec((tm, d), lambda i: (i, 0)),
            pl.BlockSpec((1, d), lambda i: (0, 0)),
            pl.BlockSpec((d, n), lambda i: (0, 0), pipeline_mode=pl.Buffered(1)),
            table_spec, table_spec,
        ],
        out_specs=pl.BlockSpec((tm, n), lambda i: (i, 0)),
        out_shape=jax.ShapeDtypeStruct((t, n), out_dtype),
        compiler_params=pltpu.CompilerParams(
            dimension_semantics=("parallel",), vmem_limit_bytes=VMEM_LIMIT),
        name=name,
    )(x, g, w, cos_t, sin_t)


def _compress_kernel(xk_ref, xv_ref, pk_ref, pv_ref, wk1_ref, wv1_ref, w2_ref, cos_ref, sin_ref, o_ref, xs_scr):
    tm = o_ref.shape[0]

    def hidden(x_ref, p_ref, w_ref):
        xs_scr[...] = x_ref[...].astype(F32)
        first = jnp.zeros((tm, w_ref.shape[2]), F32)
        second = jnp.zeros((tm, w_ref.shape[2]), F32)
        for j in range(CMP_STRIDE):
            xj = xs_scr[pl.ds(j, tm, stride=CMP_STRIDE), :]
            first = first + jnp.dot((xj + p_ref[j:j + 1, :]).astype(BF16), w_ref[j],
                                    preferred_element_type=F32)
            second = second + jnp.dot((xj + p_ref[CMP_STRIDE + j:CMP_STRIDE + j + 1, :]).astype(BF16),
                                      w_ref[CMP_STRIDE + j], preferred_element_type=F32)
        return _gelu_tanh(first + pltpu.roll(second, tm - 1, 0)).astype(BF16)

    hcat = jnp.concatenate([hidden(xk_ref, pk_ref, wk1_ref), hidden(xv_ref, pv_ref, wv1_ref)], axis=1)
    kv = jnp.dot(hcat, w2_ref[...], preferred_element_type=F32)
    o_ref[...] = kv * cos_ref[...] + _rope_swap(kv) * sin_ref[...]


def _compress(z, pk, pv, wk1, wv1, w2, cos_t, sin_t, *, tm):
    t = z.shape[0]
    m = t // CMP_STRIDE
    pairs = NSA_GROUPS // 2
    pair_w = 2 * HEAD_DIM
    hid2 = wk1.shape[2]
    out_w = w2.shape[1]
    full = lambda shape: pl.BlockSpec(shape, lambda p, i: (0,) * len(shape))
    tokens = lambda off: pl.BlockSpec((tm * CMP_STRIDE, pair_w), lambda p, i: (i, off // pair_w + p))
    return pl.pallas_call(
        _compress_kernel,
        grid=(pairs, m // tm),
        in_specs=[
            tokens(OFF_KC), tokens(OFF_VC),
            full((CMP_LEN, pair_w)), full((CMP_LEN, pair_w)),
            full((CMP_LEN, pair_w, hid2)), full((CMP_LEN, pair_w, hid2)),
            full((2 * hid2, out_w)),
            full((tm, out_w)), full((tm, out_w)),
        ],
        out_specs=pl.BlockSpec((tm, out_w), lambda p, i: (i, p)),
        out_shape=jax.ShapeDtypeStruct((m, pairs * out_w), F32),
        scratch_shapes=[pltpu.VMEM((tm * CMP_STRIDE, pair_w), F32)],
        compiler_params=pltpu.CompilerParams(
            dimension_semantics=("parallel", "parallel"), vmem_limit_bytes=VMEM_LIMIT),
        name="compress",
    )(z, z, pk, pv, wk1, wv1, w2, cos_t, sin_t)


def _nsa_kernel(q_ref, g_ref, cos_ref, sin_ref, *rest, n_sel):
    kv_refs, rest = rest[:4 * NSA_PAIRS], rest[4 * NSA_PAIRS:]
    feat_ref, kvc_ref, c2st_ref, o_ref, gt_scr, s_scr, vt_scr = rest
    ks_refs, vs_refs, kw_refs, vw_refs = (kv_refs[n::4] for n in range(4))
    g2 = pl.program_id(1)
    i = pl.program_id(2)
    q0 = i * Q_BLOCK
    seq = feat_ref.shape[0]

    @pl.when(i == 0)
    def _():
        ones_rows = jnp.where(lax.broadcasted_iota(jnp.int32, (V_AUG - HEAD_DIM, Q_BLOCK), 0) == 0,
                              1.0, 0.0).astype(BF16)

        def fill(t, carry):
            r0 = pl.multiple_of(t * Q_BLOCK, Q_BLOCK)
            for kind, refs in enumerate((vs_refs, vw_refs)):
                for pair, ref in enumerate(refs):
                    vt = jnp.transpose(ref[pl.ds(r0, Q_BLOCK), :].astype(F32)).astype(BF16)
                    for half in range(2):
                        gg = 2 * pair + half
                        vt_scr[kind, gg, t, 0:HEAD_DIM, :] = vt[half * HEAD_DIM:(half + 1) * HEAD_DIM]
                        vt_scr[kind, gg, t, HEAD_DIM:V_AUG, :] = ones_rows
            return carry

        lax.fori_loop(0, seq // Q_BLOCK, fill, 0)

    def query_side(gg, qt, bias_rows, win_row):
        zeros_q = jnp.zeros((HEAD_DIM, NSA_LANES), BF16)
        q_rows = [qt, zeros_q] if gg % 2 == 0 else [zeros_q, qt]
        rid = lax.broadcasted_iota(jnp.int32, (BF16_ROWS, NSA_LANES), 0)
        win_rows = jnp.where(rid == 0, win_row, 0.0).astype(BF16)
        tail = jnp.zeros((2 * LANES - 2 * HEAD_DIM - n_sel - BF16_ROWS, NSA_LANES), BF16)
        return jnp.concatenate(q_rows + [bias_rows.astype(BF16), win_rows, tail], axis=0)

    def key_side(k_ref, row0, rows):
        return jnp.concatenate([k_ref[pl.ds(row0, rows), :], feat_ref[pl.ds(row0, rows), :]], axis=1)
    blocks_per_chunk = SEL_CHUNK // SEL_BLOCK
    q_per_chunk = SEL_CHUNK // Q_BLOCK

    def tile_heads(x):
        return jnp.concatenate([x] * NSA_HPG, axis=1)

    tq = q0 + (lax.broadcasted_iota(jnp.int32, (1, NSA_LANES), 1) & (Q_BLOCK - 1))
    c_idx = lax.broadcasted_iota(jnp.int32, (Q_BLOCK, 1), 0)
    r_idx = lax.broadcasted_iota(jnp.int32, (1, Q_BLOCK), 1)
    tri_diag = tile_heads(jnp.where(c_idx <= r_idx, 0.0, NEG))
    tri_old = tile_heads(jnp.where(c_idx > r_idx, 0.0, NEG))
    cos_q = tile_heads(cos_ref[...])
    sin_q = tile_heads(sin_ref[...])
    blk = lax.broadcasted_iota(jnp.int32, (n_sel, Q_BLOCK), 0)
    cur = (q0 + lax.broadcasted_iota(jnp.int32, (n_sel, Q_BLOCK), 1)) // SEL_BLOCK
    sub = lax.broadcasted_iota(jnp.int32, (SUBLANES, Q_BLOCK), 0)

    gt_scr[...] = jnp.transpose(_sigmoid(g_ref[...].astype(F32)))

    def prepare(gg):
        qx = jnp.transpose(q_ref[:, gg * NSA_HPG * HEAD_DIM:(gg + 1) * NSA_HPG * HEAD_DIM].astype(F32)
                           * (SCALE * LOG2E))
        qf = jnp.concatenate([qx[h * HEAD_DIM:(h + 1) * HEAD_DIM, :] for h in range(NSA_HPG)], axis=1)
        x1, x2 = qf[0:ROPE_HALF], qf[ROPE_HALF:ROPE_DIM]
        qt = jnp.concatenate([x1 * cos_q - x2 * sin_q, x2 * cos_q + x1 * sin_q, qf[ROPE_DIM:]],
                             axis=0).astype(BF16)
        qts[gg] = qt
        yield

        kvc = kvc_ref[:, gg * LANES:(gg + 1) * LANES]
        kc = kvc[:, :HEAD_DIM].astype(BF16)
        vct = jnp.transpose(kvc)[HEAD_DIM:, :].astype(BF16)
        sc = jnp.dot(kc, qt, preferred_element_type=F32)
        yield
        n_idx = lax.broadcasted_iota(jnp.int32, (LANES, 1), 0)
        sc = jnp.where((n_idx * CMP_STRIDE + (CMP_LEN - 1)) <= tq, sc, NEG)
        m_c = jnp.max(sc, axis=0, keepdims=True)
        e_c = jnp.exp2(sc - m_c)
        inv_c = 1.0 / jnp.maximum(jnp.sum(e_c, axis=0, keepdims=True), 1e-30)
        p_c = e_c * jnp.where(m_c > 0.5 * NEG, inv_c, 0.0)
        o_c = jnp.dot(vct, p_c.astype(BF16), preferred_element_type=F32)
        yield

        p_sum = p_c[:, 0:Q_BLOCK]
        for h in range(1, NSA_HPG):
            p_sum = p_sum + p_c[:, h * Q_BLOCK:(h + 1) * Q_BLOCK]
        imp = jnp.dot(c2st_ref[...], p_sum, preferred_element_type=F32,
                      precision=lax.Precision.HIGHEST)
        yield
        forced = (blk == 0) | (blk == cur) | (blk == cur - 1)
        val = jnp.where(forced, jnp.inf, jnp.where(blk > cur, -jnp.inf, imp))
        n_grp = n_sel // SUBLANES
        grp = [val[v * SUBLANES:(v + 1) * SUBLANES] for v in range(n_grp)]
        cnt = [jnp.zeros((SUBLANES, Q_BLOCK), F32) for _ in range(n_grp)]
        for sp in range(n_sel):
            r = val[sp:sp + 1, :]
            for v in range(n_grp):
                if v * SUBLANES > sp:
                    cnt[v] = cnt[v] + jnp.where(r >= grp[v], 1.0, 0.0)
                elif v * SUBLANES + SUBLANES - 1 <= sp:
                    cnt[v] = cnt[v] + jnp.where(r > grp[v], 1.0, 0.0)
                else:
                    later = jnp.where(sub > sp - v * SUBLANES, 1.0, 0.0)
                    cnt[v] = cnt[v] + jnp.where(r > grp[v], 1.0, jnp.where(r == grp[v], later, 0.0))
            if sp % SUBLANES == SUBLANES - 1:
                yield
        chosen = jnp.concatenate([jnp.where(c < float(SEL_TOPK), 0.0, NEG) for c in cnt], axis=0)
        block_bias = tile_heads(jnp.where(blk > cur, NEG, chosen))
        row0 = pl.multiple_of((g2 * NSA_GPS + gg) * GATE_STRIDE, GATE_STRIDE)
        gt = gt_scr[pl.ds(row0, 2 * SUBLANES), :]
        return qt, o_c, block_bias, gt

    def window(gg):
        qt = qts[gg]
        tiles = []
        for d in range(1 - WIN_TILES, 1):
            td = i + d
            tc = jnp.maximum(td, 0)
            k0 = pl.multiple_of(tc * Q_BLOCK, Q_BLOCK)
            before_start = jnp.where(td >= 0, 0.0, NEG) if d < 0 else 0.0
            tri = tri_diag if d == 0 else (tri_old if d == 1 - WIN_TILES else None)
            tiles.append((tc, k0, before_start, tri))

        zero_row = jnp.zeros((1, NSA_LANES), F32)
        no_bias = jnp.zeros((n_sel, NSA_LANES), BF16)
        scores = []
        for tc, k0, before_start, tri in tiles:
            q_aug = query_side(gg, qt, no_bias, zero_row + before_start)
            s = jnp.dot(key_side(kw_refs[gg // 2], k0, Q_BLOCK), q_aug, preferred_element_type=F32)
            scores.append(s if tri is None else s + tri)
            yield
        m_w = jnp.max(functools.reduce(jnp.maximum, [_fold_rows(s, jnp.maximum) for s in scores]),
                      axis=0, keepdims=True)
        n_pad = jnp.maximum(WINDOW - 1 - tq, 0).astype(F32)
        m_w = jnp.where(n_pad > 0.0, jnp.maximum(m_w, 0.0), m_w)
        yield
        acc = jnp.zeros((V_AUG, NSA_LANES), F32)
        for (tc, _, _, _), s in zip(tiles, scores):
            acc = acc + jnp.dot(vt_scr[1, gg, tc], jnp.exp2(s - m_w).astype(BF16), preferred_element_type=F32)
            yield
        l_w = acc[HEAD_DIM:HEAD_DIM + 1] + n_pad * jnp.exp2(-m_w)
        return acc[:HEAD_DIM] * (1.0 / jnp.maximum(l_w, 1e-30))

    pieces_per_chunk = SEL_CHUNK // SEL_PIECE
    tiles_per_piece = SEL_PIECE // Q_BLOCK
    blocks_per_piece = SEL_PIECE // SEL_BLOCK

    def chunk_scores(gg, c, diag_rel):
        zeros_q = jnp.zeros((HEAD_DIM, NSA_LANES), BF16)
        q_rows = jnp.concatenate([qts[gg], zeros_q] if gg % 2 == 0 else [zeros_q, qts[gg]], axis=0)
        bmax = []
        for piece in range(c * pieces_per_chunk, (c + 1) * pieces_per_chunk):
            rows = slice(piece * SEL_PIECE, (piece + 1) * SEL_PIECE)
            s = jnp.dot(ks_refs[gg // 2][rows, :], q_rows, preferred_element_type=F32)
            if diag_rel is not None:
                first_tile = (piece - c * pieces_per_chunk) * tiles_per_piece
                s = s + jnp.concatenate([jnp.where(diag_rel == first_tile + u, tri_diag, 0.0)
                                         for u in range(tiles_per_piece)], axis=0)
            s_scr[gg, rows, :] = s
            bmax += [_fold_rows(s[b * SEL_BLOCK:(b + 1) * SEL_BLOCK], jnp.maximum) for b in range(blocks_per_piece)]
            yield
        return bmax

    def first_chunk(gg):
        yield
        return (yield from chunk_scores(gg, 0, i))

    def selected(gg, k, block_bias, bmax0):
        bmax = list(bmax0)
        for c in range(1, k + 1):
            bmax += yield from chunk_scores(gg, c, i - k * q_per_chunk if c == k else None)
        m8 = functools.reduce(jnp.maximum, [bm + block_bias[n:n + 1, :] for n, bm in enumerate(bmax)])
        m_s = jnp.max(m8, axis=0, keepdims=True)
        shift = block_bias - m_s
        yield
        acc = jnp.zeros((V_AUG, NSA_LANES), F32)
        for piece in range((k + 1) * pieces_per_chunk):
            ps = []
            for b in range(blocks_per_piece):
                n = piece * blocks_per_piece + b
                ps.append(jnp.exp2(s_scr[gg, n * SEL_BLOCK:(n + 1) * SEL_BLOCK, :] + shift[n:n + 1, :]).astype(BF16))
            vt = jnp.concatenate([vt_scr[0, gg, piece * tiles_per_piece + u] for u in range(tiles_per_piece)], axis=1)
            acc = acc + jnp.dot(vt, jnp.concatenate(ps, axis=0), preferred_element_type=F32)
            yield
        l_s = acc[HEAD_DIM:HEAD_DIM + 1]
        return acc[:HEAD_DIM] * (1.0 / jnp.maximum(l_s, 1e-30))

    qts = [None] * NSA_GPS
    groups = range(NSA_GPS)
    shared = _round_robin([prepare(gg) for gg in groups] + [window(gg) for gg in groups]
                          + [first_chunk(gg) for gg in groups])
    prepared, windows, bmax0 = (shared[n * NSA_GPS:(n + 1) * NSA_GPS] for n in range(3))

    def finish(k):
        branches = []
        for gg, (qt, o_c, block_bias, gt) in enumerate(prepared):
            branches.append(selected(gg, k, block_bias, bmax0[gg]))
        results = _round_robin(branches, lead=k + 1)
        outs = []
        for gg, (qt, o_c, block_bias, gt) in enumerate(prepared):
            o_w, o_s = windows[gg], results[gg]
            for h in range(NSA_HPG):
                sl = slice(h * Q_BLOCK, (h + 1) * Q_BLOCK)
                row = NSA_BRANCHES * h
                outs.append(gt[row:row + 1, :] * o_c[:, sl] + gt[row + 1:row + 2, :] * o_s[:, sl]
                            + gt[row + 2:row + 3, :] * o_w[:, sl])
        o_ref[...] = jnp.transpose(jnp.concatenate(outs, axis=0))

    for k in range(n_sel * SEL_BLOCK // SEL_CHUNK):
        pl.when(i // q_per_chunk == k)(functools.partial(finish, k))


def _nsa(z, cos_q, sin_q, feat, kvc, c2st, *, batch, seq):
    nq = seq // Q_BLOCK
    n_sel = seq // SEL_BLOCK
    gw = NSA_GPS * NSA_HPG * HEAD_DIM
    gates_w = NSA_GROUPS * GATE_STRIDE
    pair_w = 2 * HEAD_DIM
    assert pair_w == LANES
    kv_spec = lambda off, pair: pl.BlockSpec(
        (seq, pair_w), lambda b, g, i: (b, off // pair_w + g * NSA_PAIRS + pair))
    kv_specs = [kv_spec(off, pair) for pair in range(NSA_PAIRS) for off in (OFF_KS, OFF_VS, OFF_KW, OFF_VW)]
    rope_spec = pl.BlockSpec((ROPE_HALF, Q_BLOCK), lambda b, g, i: (0, i))
    return pl.pallas_call(
        functools.partial(_nsa_kernel, n_sel=n_sel),
        grid=(batch, NSA_GROUPS // NSA_GPS, nq),
        in_specs=[
            pl.BlockSpec((Q_BLOCK, gw), lambda b, g, i: (b * nq + i, OFF_Q // gw + g)),
            pl.BlockSpec((Q_BLOCK, gates_w), lambda b, g, i: (b * nq + i, OFF_GATE // gates_w)),
            rope_spec, rope_spec,
            *kv_specs,
            pl.BlockSpec((seq, LANES), lambda b, g, i: (0, 0)),
            pl.BlockSpec((LANES, NSA_GPS * 2 * HEAD_DIM), lambda b, g, i: (b, g)),
            pl.BlockSpec((n_sel, LANES), lambda b, g, i: (0, 0)),
        ],
        out_specs=pl.BlockSpec((Q_BLOCK, gw), lambda b, g, i: (b * nq + i, g)),
        out_shape=jax.ShapeDtypeStruct((batch * seq, NSA_HEADS * HEAD_DIM), F32),
        scratch_shapes=[pltpu.VMEM((gates_w, Q_BLOCK), F32), pltpu.VMEM((NSA_GPS, seq, NSA_LANES), F32),
                        pltpu.VMEM((2, NSA_GPS, nq, V_AUG, Q_BLOCK), BF16)],
        compiler_params=pltpu.CompilerParams(
            dimension_semantics=("arbitrary", "arbitrary", "arbitrary"), vmem_limit_bytes=VMEM_LIMIT),
        name="nsa",
    )(z, z, cos_q, sin_q, *([z] * len(kv_specs)), feat, kvc, c2st)


def _rglru_kernel(x_ref, gr_ref, cw_ref, cb_ref, wa_ref, ba_ref, wi_ref, bi_ref, lam_ref, y_ref,
                  xs_scr, a_scr, b_scr, h_scr):
    nb, ts, c = x_ref.shape
    pitch = a_scr.shape[1] // nb
    lane_tiles = c // LANES

    def put(scr, rows, val):
        for m in range(lane_tiles):
            scr[m, rows, :] = val[:, m * LANES:(m + 1) * LANES]

    def get(scr, rows):
        return jnp.concatenate([scr[m, rows, :] for m in range(lane_tiles)], axis=1)

    @pl.when(pl.program_id(0) == 0)
    def _():
        xs_scr[:, 0:SUBLANES, :] = jnp.zeros((nb, SUBLANES, c), F32)
        h_scr[...] = jnp.zeros(h_scr.shape, F32)

    nl = -lam_ref[...]
    softplus = jnp.maximum(nl, 0.0) + jnp.log1p(jnp.exp(-jnp.abs(nl)))

    def gates(bi_, carry):
        x = x_ref[bi_].astype(F32)
        xs_scr[bi_, SUBLANES:SUBLANES + ts, :] = x
        xc = cb_ref[...] + cw_ref[CONV_WIDTH - 1:CONV_WIDTH, :] * x
        for k in range(CONV_WIDTH - 1):
            off = SUBLANES - (CONV_WIDTH - 1) + k
            xc = xc + cw_ref[k:k + 1, :] * xs_scr[bi_, off:off + ts, :]
        xs_scr[bi_, 0:SUBLANES, :] = x[ts - SUBLANES:, :]

        ra, ri = [], []
        for mblk in range(c // LANES):
            xb = xc[:, mblk * LANES:(mblk + 1) * LANES].astype(BF16)
            ra.append(jnp.dot(xb, wa_ref[mblk], preferred_element_type=F32))
            ri.append(jnp.dot(xb, wi_ref[mblk], preferred_element_type=F32))
        r = _sigmoid(jnp.concatenate(ra, axis=1) + ba_ref[...])
        gi = _sigmoid(jnp.concatenate(ri, axis=1) + bi_ref[...])

        log_a = (-RGLRU_C) * r * softplus
        th = jnp.tanh(log_a)
        y2 = -2.0 * th / (1.0 - th)
        rows = pl.ds(pl.multiple_of(bi_ * pitch, SUBLANES), ts)
        put(a_scr, rows, jnp.exp(log_a))
        put(b_scr, rows, jnp.where(y2 > 0.0, y2 * lax.rsqrt(y2), 0.0) * (gi * xc))
        return carry

    lax.fori_loop(0, nb, gates, 0)

    def step(t, h):
        rows = pl.ds(t, nb, stride=pitch)
        h = get(a_scr, rows) * h + get(b_scr, rows)
        put(b_scr, rows, h)
        return h

    h_scr[...] = lax.fori_loop(0, ts, step, h_scr[...], unroll=8)

    def outputs(bi_, carry):
        rows = pl.ds(pl.multiple_of(bi_ * pitch, SUBLANES), ts)
        y_ref[bi_] = _gelu_tanh(gr_ref[bi_].astype(F32)) * get(b_scr, rows)
        return carry

    lax.fori_loop(0, nb, outputs, 0)


def _rglru(z, cw, cb, wa2, ba, wi2, bi, lam, *, batch, seq, ts):
    c = D_MODEL
    pitch = ts + SUBLANES
    z3 = z.reshape(batch, seq, z.shape[1])
    vec = pl.BlockSpec((1, c), lambda t: (0, 0))
    y = pl.pallas_call(
        _rglru_kernel,
        grid=(seq // ts,),
        in_specs=[
            pl.BlockSpec((batch, ts, c), lambda t: (0, t, OFF_XRNN // c)),
            pl.BlockSpec((batch, ts, c), lambda t: (0, t, OFF_GRNN // c)),
            pl.BlockSpec((CONV_WIDTH, c), lambda t: (0, 0)),
            vec,
            pl.BlockSpec((c // LANES, LANES, LANES), lambda t: (0, 0, 0)),
            vec,
            pl.BlockSpec((c // LANES, LANES, LANES), lambda t: (0, 0, 0)),
            vec, vec,
        ],
        out_specs=pl.BlockSpec((batch, ts, c), lambda t: (0, t, 0)),
        out_shape=jax.ShapeDtypeStruct((batch, seq, c), F32),
        scratch_shapes=[pltpu.VMEM((batch, SUBLANES + ts, c), F32), pltpu.VMEM((c // LANES, batch * pitch, LANES), F32),
                        pltpu.VMEM((c // LANES, batch * pitch, LANES), F32), pltpu.VMEM((batch, c), F32)],
        compiler_params=pltpu.CompilerParams(
            dimension_semantics=("arbitrary",), vmem_limit_bytes=VMEM_LIMIT),
        name="rglru",
    )(z3, z3, cw, cb, wa2, ba, wi2, bi, lam)
    return y.reshape(batch * seq, c)


def _merge_kernel(x_ref, m0_ref, m1_ref, m2_ref, ynsa_ref, yrnn_ref, qx_ref, km_ref, vm_ref, wxo_ref, wo_ref, h_ref):
    tq = x_ref.shape[0]
    mem_len = km_ref.shape[0]
    row_chunk = tq // 4

    def cross_attention():
        qt = jnp.transpose(qx_ref[...].astype(F32) * (SCALE * LOG2E))
        row_head = lax.broadcasted_iota(jnp.int32, qt.shape, 0) // HEAD_DIM
        q_bd = jnp.concatenate([jnp.where(row_head == h, qt, 0.0) for h in range(XATTN_HEADS)],
                               axis=1).astype(BF16)
        yield
        s = jnp.dot(km_ref[...], q_bd, preferred_element_type=F32)
        yield
        m = jnp.max(_fold_rows(s, jnp.maximum), axis=0, keepdims=True)
        p = jnp.exp2(s - m).astype(BF16)
        yield
        ones_rows = jnp.where(lax.broadcasted_iota(jnp.int32, (BF16_ROWS, mem_len), 0) == 0, 1.0, 0.0)
        vt = jnp.concatenate([jnp.transpose(vm_ref[...].astype(F32)), ones_rows], axis=0).astype(BF16)
        o_all = jnp.dot(vt, p, preferred_element_type=F32)
        yield
        outs = []
        for h in range(XATTN_HEADS):
            lanes = slice(h * tq, (h + 1) * tq)
            inv_l = 1.0 / o_all[XATTN_WIDTH:XATTN_WIDTH + 1, lanes]
            outs.append(o_all[h * HEAD_DIM:(h + 1) * HEAD_DIM, lanes] * inv_l)
        o = jnp.transpose(jnp.concatenate(outs, axis=0)).astype(BF16)
        yield
        return jnp.dot(o, wxo_ref[...], preferred_element_type=F32)

    def gated(gate_ref, val_ref):
        parts = []
        for r in range(0, tq, row_chunk):
            parts.append(_sigmoid_tanh(gate_ref[r:r + row_chunk, :].astype(F32)) * val_ref[r:r + row_chunk, :])
            yield
        return jnp.concatenate(parts, axis=0)

    def gate_only(gate_ref):
        parts = []
        for r in range(0, tq, row_chunk):
            parts.append(_sigmoid_tanh(gate_ref[r:r + row_chunk, :].astype(F32)))
            yield
        return jnp.concatenate(parts, axis=0)

    yx, y_a, y_b, g_x = _round_robin([cross_attention(), gated(m0_ref, ynsa_ref), gated(m1_ref, yrnn_ref),
                                      gate_only(m2_ref)])
    y = y_a + y_b + g_x * yx
    h_ref[...] = x_ref[...] + jnp.dot(y.astype(BF16), wo_ref[...], preferred_element_type=F32)


def _merge(x, z, ynsa, yrnn, kvm, wxo, wo, *, seq, mem_len, tq):
    t, d = x.shape
    nt = seq // tq
    row = lambda cb: pl.BlockSpec((tq, d), lambda i, cb=cb: (i, cb))
    mem_spec = lambda cb: pl.BlockSpec((mem_len, XATTN_WIDTH), lambda i, cb=cb: (i // nt, cb))
    return pl.pallas_call(
        _merge_kernel,
        grid=(t // tq,),
        in_specs=[
            row(0),
            row(OFF_MERGE // d), row(OFF_MERGE // d + 1), row(OFF_MERGE // d + 2),
            row(0), row(0),
            pl.BlockSpec((tq, XATTN_WIDTH), lambda i: (i, OFF_QX // XATTN_WIDTH)),
            mem_spec(0), mem_spec(1),
            pl.BlockSpec((XATTN_WIDTH, d), lambda i: (0, 0)),
            pl.BlockSpec((d, d), lambda i: (0, 0)),
        ],
        out_specs=row(0),
        out_shape=jax.ShapeDtypeStruct((t, d), F32),
        compiler_params=pltpu.CompilerParams(
            dimension_semantics=("parallel",), vmem_limit_bytes=VMEM_LIMIT),
        name="merge",
    )(x, z, z, z, ynsa, yrnn, z, kvm, kvm, wxo, wo)


def _mlp_kernel(h_ref, g_ref, wup_ref, wdn_ref, gf_ref, o_ref, *, tf):
    h = h_ref[...]
    ms = jnp.mean(h * h, axis=-1, keepdims=True)
    v = (h * lax.rsqrt(ms + RMS_EPS) * g_ref[...]).astype(BF16)
    h2 = h
    for f in range(0, wup_ref.shape[1], tf):
        up = jnp.dot(v, wup_ref[:, f:f + tf], preferred_element_type=F32)
        act = jnp.square(jnp.maximum(up, 0.0)).astype(BF16)
        h2 = h2 + jnp.dot(act, wdn_ref[f:f + tf, :], preferred_element_type=F32)
    ms = jnp.mean(h2 * h2, axis=-1, keepdims=True)
    o_ref[...] = h2 * lax.rsqrt(ms + RMS_EPS) * gf_ref[...]


def _mlp(h, g, wup, wdn, gf, *, tm, tf):
    t, d = h.shape
    dff = wup.shape[1]
    resident = lambda shape: pl.BlockSpec(shape, lambda i: (0, 0), pipeline_mode=pl.Buffered(1))
    return pl.pallas_call(
        functools.partial(_mlp_kernel, tf=tf),
        grid=(t // tm,),
        in_specs=[
            pl.BlockSpec((tm, d), lambda i: (i, 0)),
            pl.BlockSpec((1, d), lambda i: (0, 0)),
            resident((d, dff)), resident((dff, d)),
            pl.BlockSpec((1, d), lambda i: (0, 0)),
        ],
        out_specs=pl.BlockSpec((tm, d), lambda i: (i, 0)),
        out_shape=jax.ShapeDtypeStruct((t, d), F32),
        compiler_params=pltpu.CompilerParams(
            dimension_semantics=("parallel",), vmem_limit_bytes=VMEM_LIMIT),
        name="mlp",
    )(h, g, wup, wdn, gf)


def _rope_angles(pos):
    inv = 1.0 / (ROPE_THETA ** (jnp.arange(0, ROPE_DIM, 2, dtype=F32) / ROPE_DIM))
    ang = pos.astype(F32)[:, None] * inv[None, :]
    return jnp.cos(ang), jnp.sin(ang)


def _rope_tables(pos, width, rope_width=None):
    rope_width = width if rope_width is None else rope_width
    cos, sin = _rope_angles(pos)
    n = pos.shape[0]
    pad = HEAD_DIM - ROPE_DIM
    cos_h = jnp.concatenate([cos, cos, jnp.ones((n, pad), F32)], axis=1)
    sin_h = jnp.concatenate([-sin, sin, jnp.zeros((n, pad), F32)], axis=1)
    reps = rope_width // HEAD_DIM
    rest = width - rope_width
    cos_t = jnp.concatenate([jnp.tile(cos_h, (1, reps)), jnp.ones((n, rest), F32)], axis=1)
    sin_t = jnp.concatenate([jnp.tile(sin_h, (1, reps)), jnp.zeros((n, rest), F32)], axis=1)
    return cos_t, sin_t


def _pack_w_in(w_in):
    d = w_in.shape[0]
    w = w_in.astype(BF16)
    n_qkv = NSA_HEADS * HEAD_DIM + 6 * NSA_KV
    n_gates = NSA_HEADS * NSA_BRANCHES
    rnn_lo = n_qkv + n_gates
    qx_lo = rnn_lo + 2 * D_MODEL
    mg_lo = qx_lo + XATTN_WIDTH
    per_group = n_gates // NSA_GROUPS
    gates = w[:, n_qkv:rnn_lo].reshape(d, NSA_GROUPS, per_group)
    gates = jnp.pad(gates, ((0, 0), (0, 0), (0, GATE_STRIDE - per_group))).reshape(d, NSA_GROUPS * GATE_STRIDE)
    return jnp.concatenate([w[:, :n_qkv], w[:, qx_lo:mg_lo], gates, w[:, rnn_lo:qx_lo], w[:, mg_lo:]], axis=1)


def _block_diag_pairs(w):
    nb, k, _ = w.shape
    w = w.reshape(nb // 2, 2, k, k)
    zero = jnp.zeros((nb // 2, k, k), w.dtype)
    top = jnp.concatenate([w[:, 0], zero], axis=2)
    bot = jnp.concatenate([zero, w[:, 1]], axis=2)
    return jnp.concatenate([top, bot], axis=1).astype(BF16)


def _cmp_to_sel_t(n_cmp, n_cmp_pad, n_sel):
    c0 = np.arange(n_cmp_pad)[None, :] * CMP_STRIDE
    s0 = np.arange(n_sel)[:, None] * SEL_BLOCK
    ov = np.clip(np.minimum(c0 + CMP_LEN, s0 + SEL_BLOCK) - np.maximum(c0, s0), 0, None)
    ov = np.where(np.arange(n_cmp_pad)[None, :] < n_cmp, ov, 0)
    return (ov / CMP_LEN).astype(np.float32)


def _key_features(seq, n_sel):
    col = np.arange(LANES)[None, :]
    onehot = (np.arange(seq)[:, None] // SEL_BLOCK == col) & (col < n_sel)
    return jnp.asarray(onehot | (col == n_sel), dtype=BF16)


def kernel(x, mem, g_mix, w_in, cmp_pos_k, cmp_pos_v, w_cmp_k1, w_cmp_k2, w_cmp_v1, w_cmp_v2, conv_w, conv_b, w_rg_a, b_rg_a, w_rg_i, b_rg_i, rg_lambda, g_mem, w_mem_kv, w_xo, w_o, g_mlp, w_up, w_down, g_final):
    batch, seq, d = x.shape
    mem_len = mem.shape[1]
    t = batch * seq
    depth = g_mix.shape[0]
    n_sel = seq // SEL_BLOCK
    n_cmp_pad = seq // CMP_STRIDE
    assert n_cmp_pad == LANES and n_sel <= K_AUG - HEAD_DIM and d == D_MODEL

    pos = jnp.arange(seq)
    cos_k, sin_k = _rope_tables(pos, NSA_KV)
    cos_q, sin_q = (a.T for a in _rope_angles(pos))
    no_rope = jnp.zeros((mem_len, LANES), F32)
    cmp_pos = jnp.arange(n_cmp_pad) * CMP_STRIDE + (CMP_LEN - 1)
    cmp_tm = 4 * n_cmp_pad
    cos_c, sin_c = _rope_tables(cmp_pos, 2 * HEAD_DIM, HEAD_DIM)
    cos_c = jnp.tile(cos_c, (cmp_tm // n_cmp_pad, 2))
    sin_c = jnp.tile(sin_c, (cmp_tm // n_cmp_pad, 2))
    c2st = jnp.asarray(_cmp_to_sel_t((seq - CMP_LEN) // CMP_STRIDE + 1, n_cmp_pad, n_sel))
    feat = _key_features(seq, n_sel)

    h = x.reshape(t, d)
    for l in range(depth):
        z = _norm_proj(h, g_mix[l][None, :], _pack_w_in(w_in[l]), cos_k, sin_k, tm=INPROJ_TM, tn=INPROJ_TN,
                       rope_cols=ROPE_COLS, out_dtype=BF16, name="inproj")

        def pair_w1(w1):
            w3 = w1.astype(BF16).reshape(CMP_LEN, 1, HEAD_DIM, CMP_HIDDEN)
            zero = jnp.zeros_like(w3)
            both = jnp.concatenate([jnp.concatenate([w3, zero], axis=3), jnp.concatenate([zero, w3], axis=3)], axis=1)
            return both.reshape(CMP_LEN, 2 * HEAD_DIM, 2 * CMP_HIDDEN)

        def pair_pos(p):
            return jnp.concatenate([p, p], axis=1)

        w2 = jnp.zeros((2, 2, CMP_HIDDEN, 2, 2, HEAD_DIM), F32)
        for grp in range(2):
            w2 = w2.at[0, grp, :, grp, 0, :].set(w_cmp_k2[l]).at[1, grp, :, grp, 1, :].set(w_cmp_v2[l])
        w2 = w2.reshape(4 * CMP_HIDDEN, 4 * HEAD_DIM)
        kvc = _compress(z, pair_pos(cmp_pos_k[l]), pair_pos(cmp_pos_v[l]), pair_w1(w_cmp_k1[l]), pair_w1(w_cmp_v1[l]),
                        w2.astype(BF16), cos_c, sin_c, tm=cmp_tm)
        y_nsa = _nsa(z, cos_q, sin_q, feat, kvc, c2st, batch=batch, seq=seq)

        y_rnn = _rglru(z, conv_w[l], conv_b[l][None, :], _block_diag_pairs(w_rg_a[l]), b_rg_a[l][None, :],
                       _block_diag_pairs(w_rg_i[l]), b_rg_i[l][None, :], rg_lambda[l][None, :],
                       batch=batch, seq=seq, ts=128)

        kvm = _norm_proj(mem.reshape(batch * mem_len, d), g_mem[l][None, :], w_mem_kv[l].astype(BF16),
                         no_rope, no_rope, tm=mem_len, tn=2 * XATTN_WIDTH, rope_cols=(),
                         out_dtype=BF16, name="memkv")

        h1 = _merge(h, z, y_nsa, y_rnn, kvm, w_xo[l].astype(BF16), w_o[l].astype(BF16),
                    seq=seq, mem_len=mem_len, tq=512)
        last = l == depth - 1
        gf = g_final if last else jnp.ones_like(g_final)
        h = _mlp(h1, g_mlp[l][None, :], w_up[l].astype(BF16), w_down[l].astype(BF16), gf[None, :],
                 tm=1024, tf=1024)
        assert last, "final norm is fused into the last layer's MLP kernel"
    return h.reshape(batch, seq, d)
```

```python
import functools
import math

import numpy as np
import jax
import jax.numpy as jnp
from jax import lax
from jax.experimental import pallas as pl
from jax.experimental.pallas import tpu as pltpu

F32 = jnp.float32
BF16 = jnp.bfloat16

D_MODEL = 1024
HEAD_DIM = 64
NSA_HEADS = 16
NSA_GROUPS = 4
NSA_HPG = NSA_HEADS // NSA_GROUPS
NSA_BRANCHES = 3
CMP_LEN = 32
CMP_STRIDE = 16
CMP_HIDDEN = 256
SEL_BLOCK = 64
SEL_TOPK = 8
WINDOW = 512
Q_BLOCK = 128
ROPE_THETA = 500000.0
ROPE_DIM = HEAD_DIM // 4
ROPE_HALF = ROPE_DIM // 2
RNN_BLOCKS = 16
RNN_BLOCK_DIM = D_MODEL // RNN_BLOCKS
CONV_WIDTH = 4
RGLRU_C = 8.0
XATTN_HEADS = 4
XATTN_WIDTH = XATTN_HEADS * HEAD_DIM
D_FF = 4 * D_MODEL
RMS_EPS = 1e-6
SCALE = HEAD_DIM ** -0.5
LOG2E = math.log2(math.e)
NEG = -1e30

LANES = 128
SUBLANES = 8
BF16_ROWS = 16
VMEM_LIMIT = 48 * 1024 * 1024

NSA_KV = NSA_GROUPS * HEAD_DIM
OFF_Q = 0
OFF_KC = 1024
OFF_VC = 1280
OFF_KS = 1536
OFF_VS = 1792
OFF_KW = 2048
OFF_VW = 2304
OFF_QX = 2560
OFF_GATE = 2816
GATE_STRIDE = 64
OFF_XRNN = 3072
OFF_GRNN = 4096
OFF_MERGE = 5120
Z_WIDTH = 8192
INPROJ_TN = 2048
INPROJ_TM = 512
ROPE_COLS = (OFF_KS, OFF_KW)

NSA_PAIRS = 2
NSA_GPS = 2 * NSA_PAIRS
NSA_LANES = NSA_HPG * Q_BLOCK
SEL_CHUNK = 2 * Q_BLOCK
WIN_TILES = WINDOW // Q_BLOCK + 1
V_AUG = HEAD_DIM + BF16_ROWS

MERGE_TQ = 512
RGLRU_TS = 128
MLP_TM = 1024
MLP_TF = 1024
CMP_TM = 512


def _sigmoid(x):
    return 1.0 / (1.0 + jnp.exp(-x))


def _sigmoid_tanh(x):
    return 0.5 * jnp.tanh(0.5 * x) + 0.5


def _gelu_tanh(x):
    return 0.5 * x * (1.0 + jnp.tanh(0.7978845608028654 * (x + 0.044715 * (x * x * x))))


def _rope_swap(z):
    n = z.shape[-1]
    lane = lax.broadcasted_iota(jnp.int32, z.shape, z.ndim - 1)
    first_half = (lane & (HEAD_DIM - 1)) < ROPE_HALF
    return jnp.where(first_half, pltpu.roll(z, n - ROPE_HALF, z.ndim - 1), pltpu.roll(z, ROPE_HALF, z.ndim - 1))


def _round_robin(gens, lead=0):
    results = [None] * len(gens)
    alive = list(range(len(gens)))

    def advance(idx):
        try:
            next(gens[idx])
        except StopIteration as stop:
            results[idx] = stop.value
            alive.remove(idx)

    for _ in range(lead):
        for idx in [n for n in alive if n < len(gens) // 2]:
            advance(idx)
    while alive:
        for idx in list(alive):
            advance(idx)
    return results


def _fold_rows(x, op):
    parts = [x[r * SUBLANES:(r + 1) * SUBLANES] for r in range(x.shape[0] // SUBLANES)]
    return functools.reduce(op, parts)


def _norm_proj_kernel(x_ref, g_ref, w_ref, cos_ref, sin_ref, z_ref, *, tn, rope_cols):
    x = x_ref[...]
    ms = jnp.mean(x * x, axis=-1, keepdims=True)
    u = (x * lax.rsqrt(ms + RMS_EPS) * g_ref[...]).astype(BF16)
    rw = cos_ref.shape[1]
    for n0 in range(0, w_ref.shape[1], tn):
        z = jnp.dot(u, w_ref[:, n0:n0 + tn], preferred_element_type=F32)
        z_ref[:, n0:n0 + tn] = z.astype(z_ref.dtype)
        for col in rope_cols:
            if n0 <= col < n0 + tn:
                zr = z[:, col - n0:col - n0 + rw]
                z_ref[:, col:col + rw] = (zr * cos_ref[...] + _rope_swap(zr) * sin_ref[...]).astype(z_ref.dtype)


def _norm_proj(x, g, w, cos_t, sin_t, *, tm, tn, rope_cols, out_dtype, name):
    t, d = x.shape
    n = w.shape[1]
    s_tiles = cos_t.shape[0] // tm
    rw = cos_t.shape[1]
    table_spec = pl.BlockSpec((tm, rw), lambda i: (i % s_tiles, 0))
    return pl.pallas_call(
        functools.partial(_norm_proj_kernel, tn=tn, rope_cols=tuple(rope_cols)),
        grid=(t // tm,),
        in_specs=[
            pl.BlockSpec((tm, d), lambda i: (i, 0)),
            pl.BlockSpec((1, d), lambda i: (0, 0)),
            pl.BlockSpec((d, n), lambda i: (0, 0), pipeline_mode=pl.Buffered(1)),
            table_spec, table_spec,
        ],
        out_specs=pl.BlockSpec((tm, n), lambda i: (i, 0)),
        out_shape=jax.ShapeDtypeStruct((t, n), out_dtype),
        compiler_params=pltpu.CompilerParams(
            dimension_semantics=("parallel",), vmem_limit_bytes=VMEM_LIMIT),
        name=name,
    )(x, g, w, cos_t, sin_t)


def _compress_kernel(xk_ref, xv_ref, pk_ref, pv_ref, wk1_ref, wv1_ref, w2_ref, cos_ref, sin_ref, o_ref, xs_scr):
    tm = o_ref.shape[0]

    def hidden(x_ref, p_ref, w_ref):
        xs_scr[...] = x_ref[...].astype(F32)
        first = jnp.zeros((tm, w_ref.shape[2]), F32)
        second = jnp.zeros((tm, w_ref.shape[2]), F32)
        for j in range(CMP_STRIDE):
            xj = xs_scr[pl.ds(j, tm, stride=CMP_STRIDE), :]
            first = first + jnp.dot((xj + p_ref[j:j + 1, :]).astype(BF16), w_ref[j],
                                    preferred_element_type=F32)
            second = second + jnp.dot((xj + p_ref[CMP_STRIDE + j:CMP_STRIDE + j + 1, :]).astype(BF16),
                                      w_ref[CMP_STRIDE + j], preferred_element_type=F32)
        return _gelu_tanh(first + pltpu.roll(second, tm - 1, 0)).astype(BF16)

    hcat = jnp.concatenate([hidden(xk_ref, pk_ref, wk1_ref), hidden(xv_ref, pv_ref, wv1_ref)], axis=1)
    kv = jnp.dot(hcat, w2_ref[...], preferred_element_type=F32)
    o_ref[...] = kv * cos_ref[...] + _rope_swap(kv) * sin_ref[...]


def _compress(z, pk, pv, wk1, wv1, w2, cos_t, sin_t, *, tm):
    t = z.shape[0]
    m = t // CMP_STRIDE
    pairs = NSA_GROUPS // 2
    pair_w = 2 * HEAD_DIM
    hid2 = wk1.shape[2]
    out_w = w2.shape[1]
    full = lambda shape: pl.BlockSpec(shape, lambda p, i: (0,) * len(shape))
    tokens = lambda off: pl.BlockSpec((tm * CMP_STRIDE, pair_w), lambda p, i: (i, off // pair_w + p))
    return pl.pallas_call(
        _compress_kernel,
        grid=(pairs, m // tm),
        in_specs=[
            tokens(OFF_KC), tokens(OFF_VC),
            full((CMP_LEN, pair_w)), full((CMP_LEN, pair_w)),
            full((CMP_LEN, pair_w, hid2)), full((CMP_LEN, pair_w, hid2)),
            full((2 * hid2, out_w)),
            full((tm, out_w)), full((tm, out_w)),
        ],
        out_specs=pl.BlockSpec((tm, out_w), lambda p, i: (i, p)),
        out_shape=jax.ShapeDtypeStruct((m, pairs * out_w), F32),
        scratch_shapes=[pltpu.VMEM((tm * CMP_STRIDE, pair_w), F32)],
        compiler_params=pltpu.CompilerParams(
            dimension_semantics=("parallel", "parallel"), vmem_limit_bytes=VMEM_LIMIT),
        name="compress",
    )(z, z, pk, pv, wk1, wv1, w2, cos_t, sin_t)


def _nsa_kernel(q_ref, g_ref, cos_ref, sin_ref, *rest, n_sel):
    kv_refs, rest = rest[:4 * NSA_PAIRS], rest[4 * NSA_PAIRS:]
    kvc_ref, c2st_ref, o_ref, gt_scr, s_scr, vt_scr = rest
    ks_refs, vs_refs, kw_refs, vw_refs = (kv_refs[n::4] for n in range(4))
    g2 = pl.program_id(1)
    i = pl.program_id(2)
    q0 = i * Q_BLOCK
    seq = ks_refs[0].shape[0]

    @pl.when(i == 0)
    def _():
        ones_rows = jnp.where(lax.broadcasted_iota(jnp.int32, (V_AUG - HEAD_DIM, Q_BLOCK), 0) == 0,
                              1.0, 0.0).astype(BF16)

        def fill(t, carry):
            r0 = pl.multiple_of(t * Q_BLOCK, Q_BLOCK)
            for kind, refs in enumerate((vs_refs, vw_refs)):
                for pair, ref in enumerate(refs):
                    vt = jnp.transpose(ref[pl.ds(r0, Q_BLOCK), :].astype(F32)).astype(BF16)
                    for half in range(2):
                        gg = 2 * pair + half
                        vt_scr[kind, gg, t, 0:HEAD_DIM, :] = vt[half * HEAD_DIM:(half + 1) * HEAD_DIM]
                        vt_scr[kind, gg, t, HEAD_DIM:V_AUG, :] = ones_rows
            return carry

        lax.fori_loop(0, seq // Q_BLOCK, fill, 0)

    def q_operand(gg):
        zeros_q = jnp.zeros((HEAD_DIM, NSA_LANES), BF16)
        return [qts[gg], zeros_q] if gg % 2 == 0 else [zeros_q, qts[gg]]

    blocks_per_chunk = SEL_CHUNK // SEL_BLOCK
    tiles_per_chunk = SEL_CHUNK // Q_BLOCK
    q_per_chunk = SEL_CHUNK // Q_BLOCK

    def tile_heads(x):
        return jnp.concatenate([x] * NSA_HPG, axis=1)

    tq = q0 + (lax.broadcasted_iota(jnp.int32, (1, NSA_LANES), 1) & (Q_BLOCK - 1))
    c_idx = lax.broadcasted_iota(jnp.int32, (Q_BLOCK, 1), 0)
    r_idx = lax.broadcasted_iota(jnp.int32, (1, Q_BLOCK), 1)
    tri_diag = tile_heads(jnp.where(c_idx <= r_idx, 0.0, NEG))
    tri_old = tile_heads(jnp.where(c_idx > r_idx, 0.0, NEG))
    cos_q = tile_heads(cos_ref[...])
    sin_q = tile_heads(sin_ref[...])
    blk = lax.broadcasted_iota(jnp.int32, (n_sel, Q_BLOCK), 0)
    cur = (q0 + lax.broadcasted_iota(jnp.int32, (n_sel, Q_BLOCK), 1)) // SEL_BLOCK
    sub = lax.broadcasted_iota(jnp.int32, (SUBLANES, Q_BLOCK), 0)

    gt_scr[...] = jnp.transpose(_sigmoid(g_ref[...].astype(F32)))

    def prepare(gg):
        qx = jnp.transpose(q_ref[:, gg * NSA_HPG * HEAD_DIM:(gg + 1) * NSA_HPG * HEAD_DIM].astype(F32)
                           * (SCALE * LOG2E))
        qf = jnp.concatenate([qx[h * HEAD_DIM:(h + 1) * HEAD_DIM, :] for h in range(NSA_HPG)], axis=1)
        x1, x2 = qf[0:ROPE_HALF], qf[ROPE_HALF:ROPE_DIM]
        qt = jnp.concatenate([x1 * cos_q - x2 * sin_q, x2 * cos_q + x1 * sin_q, qf[ROPE_DIM:]],
                             axis=0).astype(BF16)
        qts[gg] = qt
        yield

        kvc = kvc_ref[:, gg * LANES:(gg + 1) * LANES]
        kc = kvc[:, :HEAD_DIM].astype(BF16)
        vct = jnp.transpose(kvc)[HEAD_DIM:, :].astype(BF16)
        sc = jnp.dot(kc, qt, preferred_element_type=F32)
        yield
        n_idx = lax.broadcasted_iota(jnp.int32, (LANES, 1), 0)
        sc = jnp.where((n_idx * CMP_STRIDE + (CMP_LEN - 1)) <= tq, sc, NEG)
        m_c = jnp.max(sc, axis=0, keepdims=True)
        e_c = jnp.exp2(sc - m_c)
        inv_c = 1.0 / jnp.maximum(jnp.sum(e_c, axis=0, keepdims=True), 1e-30)
        p_c = e_c * jnp.where(m_c > 0.5 * NEG, inv_c, 0.0)
        o_c = jnp.dot(vct, p_c.astype(BF16), preferred_element_type=F32)
        yield

        p_sum = p_c[:, 0:Q_BLOCK]
        for h in range(1, NSA_HPG):
            p_sum = p_sum + p_c[:, h * Q_BLOCK:(h + 1) * Q_BLOCK]
        imp = jnp.dot(c2st_ref[...], p_sum, preferred_element_type=F32,
                      precision=lax.Precision.HIGHEST)
        yield
        forced = (blk == 0) | (blk == cur) | (blk == cur - 1)
        val = jnp.where(forced, jnp.inf, jnp.where(blk > cur, -jnp.inf, imp))
        n_grp = n_sel // SUBLANES
        grp = [val[v * SUBLANES:(v + 1) * SUBLANES] for v in range(n_grp)]
        cnt = [jnp.zeros((SUBLANES, Q_BLOCK), F32) for _ in range(n_grp)]
        for sp in range(n_sel):
            r = val[sp:sp + 1, :]
            for v in range(n_grp):
                if v * SUBLANES > sp:
                    cnt[v] = cnt[v] + jnp.where(r >= grp[v], 1.0, 0.0)
                elif v * SUBLANES + SUBLANES - 1 <= sp:
                    cnt[v] = cnt[v] + jnp.where(r > grp[v], 1.0, 0.0)
                else:
                    later = jnp.where(sub > sp - v * SUBLANES, 1.0, 0.0)
                    cnt[v] = cnt[v] + jnp.where(r > grp[v], 1.0, jnp.where(r == grp[v], later, 0.0))
            if sp % SUBLANES == SUBLANES - 1:
                yield
        chosen = jnp.concatenate([jnp.where(c < float(SEL_TOPK), 0.0, NEG) for c in cnt], axis=0)
        block_bias = tile_heads(jnp.where(blk > cur, NEG, chosen))
        row0 = pl.multiple_of((g2 * NSA_GPS + gg) * GATE_STRIDE, GATE_STRIDE)
        gt = gt_scr[pl.ds(row0, 2 * SUBLANES), :]
        return qt, o_c, block_bias, gt

    def window(gg):
        tiles = []
        for d in range(1 - WIN_TILES, 1):
            td = i + d
            tc = jnp.maximum(td, 0)
            k0 = pl.multiple_of(tc * Q_BLOCK, Q_BLOCK)
            before_start = jnp.where(td >= 0, 0.0, NEG) if d < 0 else 0.0
            tri = tri_diag if d == 0 else (tri_old if d == 1 - WIN_TILES else None)
            tiles.append((tc, k0, before_start, tri))

        ones_rows = jnp.where(lax.broadcasted_iota(jnp.int32, (BF16_ROWS, NSA_LANES), 0) == 0, 1.0, 0.0).astype(BF16)
        q_aug = jnp.concatenate(q_operand(gg) + [ones_rows, jnp.zeros((LANES - BF16_ROWS, NSA_LANES), BF16)],
                                axis=0)
        first_lane = lax.broadcasted_iota(jnp.int32, (Q_BLOCK, LANES), 1) == 0
        k_rows = jnp.concatenate([kw_refs[gg // 2][pl.ds(k0, Q_BLOCK), :] for _, k0, _, _ in tiles], axis=0)
        k_bias = jnp.concatenate([jnp.where(first_lane, before_start, 0.0).astype(BF16)
                                  for _, _, before_start, _ in tiles], axis=0)
        s_all = jnp.dot(jnp.concatenate([k_rows, k_bias], axis=1), q_aug, preferred_element_type=F32)
        yield
        scores = []
        for n, (_, _, _, tri) in enumerate(tiles):
            s = s_all[n * Q_BLOCK:(n + 1) * Q_BLOCK]
            scores.append(s if tri is None else s + tri)
        m_w = jnp.max(functools.reduce(jnp.maximum, [_fold_rows(s, jnp.maximum) for s in scores]),
                      axis=0, keepdims=True)
        n_pad = jnp.maximum(WINDOW - 1 - tq, 0).astype(F32)
        m_w = jnp.where(n_pad > 0.0, jnp.maximum(m_w, 0.0), m_w)
        yield
        acc = jnp.zeros((V_AUG, NSA_LANES), F32)
        for (tc, _, _, _), s in zip(tiles, scores):
            acc = acc + jnp.dot(vt_scr[1, gg, tc], jnp.exp2(s - m_w).astype(BF16), preferred_element_type=F32)
            yield
        l_w = acc[HEAD_DIM:HEAD_DIM + 1] + n_pad * jnp.exp2(-m_w)
        return acc[:HEAD_DIM] * (1.0 / jnp.maximum(l_w, 1e-30))

    def chunk_scores(gg, c, diag_rel):
        rows = slice(c * SEL_CHUNK, (c + 1) * SEL_CHUNK)
        s = jnp.dot(ks_refs[gg // 2][rows, :], jnp.concatenate(q_operand(gg), axis=0),
                    preferred_element_type=F32)
        if diag_rel is not None:
            s = s + jnp.concatenate([jnp.where(diag_rel == u, tri_diag, 0.0) for u in range(tiles_per_chunk)], axis=0)
        s_scr[gg, rows, :] = s
        bmax = [_fold_rows(s[b * SEL_BLOCK:(b + 1) * SEL_BLOCK], jnp.maximum) for b in range(blocks_per_chunk)]
        yield
        return bmax

    def first_chunk(gg):
        yield
        return (yield from chunk_scores(gg, 0, i))

    def selected(gg, k, block_bias, bmax0):
        bmax = list(bmax0)
        for c in range(1, k + 1):
            bmax += yield from chunk_scores(gg, c, i - k * q_per_chunk if c == k else None)
        m8 = functools.reduce(jnp.maximum, [bm + block_bias[n:n + 1, :] for n, bm in enumerate(bmax)])
        m_s = jnp.max(m8, axis=0, keepdims=True)
        shift = block_bias - m_s
        yield
        acc = jnp.zeros((V_AUG, NSA_LANES), F32)
        for c in range(k + 1):
            ps = []
            for b in range(blocks_per_chunk):
                n = c * blocks_per_chunk + b
                ps.append(jnp.exp2(s_scr[gg, n * SEL_BLOCK:(n + 1) * SEL_BLOCK, :] + shift[n:n + 1, :]).astype(BF16))
            vt = jnp.concatenate([vt_scr[0, gg, c * tiles_per_chunk + u] for u in range(tiles_per_chunk)], axis=1)
            acc = acc + jnp.dot(vt, jnp.concatenate(ps, axis=0), preferred_element_type=F32)
            yield
        l_s = acc[HEAD_DIM:HEAD_DIM + 1]
        return acc[:HEAD_DIM] * (1.0 / jnp.maximum(l_s, 1e-30))

    qts = [None] * NSA_GPS
    groups = range(NSA_GPS)
    shared = _round_robin([prepare(gg) for gg in groups] + [window(gg) for gg in groups]
                          + [first_chunk(gg) for gg in groups])
    prepared, windows, bmax0 = (shared[n * NSA_GPS:(n + 1) * NSA_GPS] for n in range(3))

    def finish(k):
        branches = []
        for gg, (qt, o_c, block_bias, gt) in enumerate(prepared):
            branches.append(selected(gg, k, block_bias, bmax0[gg]))
        results = _round_robin(branches, lead=k + 1)
        outs = []
        for gg, (qt, o_c, block_bias, gt) in enumerate(prepared):
            o_w, o_s = windows[gg], results[gg]
            for h in range(NSA_HPG):
                sl = slice(h * Q_BLOCK, (h + 1) * Q_BLOCK)
                row = NSA_BRANCHES * h
                outs.append(gt[row:row + 1, :] * o_c[:, sl] + gt[row + 1:row + 2, :] * o_s[:, sl]
                            + gt[row + 2:row + 3, :] * o_w[:, sl])
        o_ref[...] = jnp.transpose(jnp.concatenate(outs, axis=0))

    for k in range(n_sel * SEL_BLOCK // SEL_CHUNK):
        pl.when(i // q_per_chunk == k)(functools.partial(finish, k))


def _nsa(z, cos_q, sin_q, kvc, c2st, *, batch, seq):
    nq = seq // Q_BLOCK
    n_sel = seq // SEL_BLOCK
    gw = NSA_GPS * NSA_HPG * HEAD_DIM
    gates_w = NSA_GROUPS * GATE_STRIDE
    pair_w = 2 * HEAD_DIM
    assert pair_w == LANES
    kv_spec = lambda off, pair: pl.BlockSpec(
        (seq, pair_w), lambda b, g, i: (b, off // pair_w + g * NSA_PAIRS + pair))
    kv_specs = [kv_spec(off, pair) for pair in range(NSA_PAIRS) for off in (OFF_KS, OFF_VS, OFF_KW, OFF_VW)]
    rope_spec = pl.BlockSpec((ROPE_HALF, Q_BLOCK), lambda b, g, i: (0, i))
    return pl.pallas_call(
        functools.partial(_nsa_kernel, n_sel=n_sel),
        grid=(batch, NSA_GROUPS // NSA_GPS, nq),
        in_specs=[
            pl.BlockSpec((Q_BLOCK, gw), lambda b, g, i: (b * nq + i, OFF_Q // gw + g)),
            pl.BlockSpec((Q_BLOCK, gates_w), lambda b, g, i: (b * nq + i, OFF_GATE // gates_w)),
            rope_spec, rope_spec,
            *kv_specs,
            pl.BlockSpec((LANES, NSA_GPS * 2 * HEAD_DIM), lambda b, g, i: (b, g)),
            pl.BlockSpec((n_sel, LANES), lambda b, g, i: (0, 0)),
        ],
        out_specs=pl.BlockSpec((Q_BLOCK, gw), lambda b, g, i: (b * nq + i, g)),
        out_shape=jax.ShapeDtypeStruct((batch * seq, NSA_HEADS * HEAD_DIM), F32),
        scratch_shapes=[pltpu.VMEM((gates_w, Q_BLOCK), F32), pltpu.VMEM((NSA_GPS, seq, NSA_LANES), F32),
                        pltpu.VMEM((2, NSA_GPS, nq, V_AUG, Q_BLOCK), BF16)],
        compiler_params=pltpu.CompilerParams(
            dimension_semantics=("arbitrary", "arbitrary", "arbitrary"), vmem_limit_bytes=VMEM_LIMIT),
        name="nsa",
    )(z, z, cos_q, sin_q, *([z] * len(kv_specs)), kvc, c2st)


def _rglru_kernel(x_ref, gr_ref, cw_ref, cb_ref, wa_ref, ba_ref, wi_ref, bi_ref, lam_ref, y_ref,
                  xs_scr, a_scr, b_scr, h_scr):
    nb, ts, c = x_ref.shape
    pitch = a_scr.shape[1] // nb
    lane_tiles = c // LANES

    def put(scr, rows, val):
        for m in range(lane_tiles):
            scr[m, rows, :] = val[:, m * LANES:(m + 1) * LANES]

    def get(scr, rows):
        return jnp.concatenate([scr[m, rows, :] for m in range(lane_tiles)], axis=1)

    @pl.when(pl.program_id(0) == 0)
    def _():
        xs_scr[:, 0:SUBLANES, :] = jnp.zeros((nb, SUBLANES, c), F32)
        h_scr[...] = jnp.zeros(h_scr.shape, F32)

    nl = -lam_ref[...]
    softplus = jnp.maximum(nl, 0.0) + jnp.log1p(jnp.exp(-jnp.abs(nl)))

    def gates(bi_, carry):
        x = x_ref[bi_].astype(F32)
        xs_scr[bi_, SUBLANES:SUBLANES + ts, :] = x
        xc = cb_ref[...] + cw_ref[CONV_WIDTH - 1:CONV_WIDTH, :] * x
        for k in range(CONV_WIDTH - 1):
            off = SUBLANES - (CONV_WIDTH - 1) + k
            xc = xc + cw_ref[k:k + 1, :] * xs_scr[bi_, off:off + ts, :]
        xs_scr[bi_, 0:SUBLANES, :] = x[ts - SUBLANES:, :]

        ra, ri = [], []
        for mblk in range(c // LANES):
            xb = xc[:, mblk * LANES:(mblk + 1) * LANES].astype(BF16)
            ra.append(jnp.dot(xb, wa_ref[mblk], preferred_element_type=F32))
            ri.append(jnp.dot(xb, wi_ref[mblk], preferred_element_type=F32))
        r = _sigmoid(jnp.concatenate(ra, axis=1) + ba_ref[...])
        gi = _sigmoid(jnp.concatenate(ri, axis=1) + bi_ref[...])

        log_a = (-RGLRU_C) * r * softplus
        th = jnp.tanh(log_a)
        y2 = -2.0 * th / (1.0 - th)
        rows = pl.ds(pl.multiple_of(bi_ * pitch, SUBLANES), ts)
        put(a_scr, rows, jnp.exp(log_a))
        put(b_scr, rows, jnp.where(y2 > 0.0, y2 * lax.rsqrt(y2), 0.0) * (gi * xc))
        return carry

    lax.fori_loop(0, nb, gates, 0)

    def step(t, h):
        rows = pl.ds(t, nb, stride=pitch)
        h = get(a_scr, rows) * h + get(b_scr, rows)
        put(b_scr, rows, h)
        return h

    h_scr[...] = lax.fori_loop(0, ts, step, h_scr[...], unroll=8)

    def outputs(bi_, carry):
        rows = pl.ds(pl.multiple_of(bi_ * pitch, SUBLANES), ts)
        y_ref[bi_] = _gelu_tanh(gr_ref[bi_].astype(F32)) * get(b_scr, rows)
        return carry

    lax.fori_loop(0, nb, outputs, 0)


def _rglru(z, cw, cb, wa2, ba, wi2, bi, lam, *, batch, seq, ts):
    c = D_MODEL
    pitch = ts + SUBLANES
    z3 = z.reshape(batch, seq, z.shape[1])
    vec = pl.BlockSpec((1, c), lambda t: (0, 0))
    y = pl.pallas_call(
        _rglru_kernel,
        grid=(seq // ts,),
        in_specs=[
            pl.BlockSpec((batch, ts, c), lambda t: (0, t, OFF_XRNN // c)),
            pl.BlockSpec((batch, ts, c), lambda t: (0, t, OFF_GRNN // c)),
            pl.BlockSpec((CONV_WIDTH, c), lambda t: (0, 0)),
            vec,
            pl.BlockSpec((c // LANES, LANES, LANES), lambda t: (0, 0, 0)),
            vec,
            pl.BlockSpec((c // LANES, LANES, LANES), lambda t: (0, 0, 0)),
            vec, vec,
        ],
        out_specs=pl.BlockSpec((batch, ts, c), lambda t: (0, t, 0)),
        out_shape=jax.ShapeDtypeStruct((batch, seq, c), F32),
        scratch_shapes=[pltpu.VMEM((batch, SUBLANES + ts, c), F32), pltpu.VMEM((c // LANES, batch * pitch, LANES), F32),
                        pltpu.VMEM((c // LANES, batch * pitch, LANES), F32), pltpu.VMEM((batch, c), F32)],
        compiler_params=pltpu.CompilerParams(
            dimension_semantics=("arbitrary",), vmem_limit_bytes=VMEM_LIMIT),
        name="rglru",
    )(z3, z3, cw, cb, wa2, ba, wi2, bi, lam)
    return y.reshape(batch * seq, c)


def _merge_kernel(x_ref, m0_ref, m1_ref, m2_ref, ynsa_ref, yrnn_ref, qx_ref, km_ref, vm_ref, wxo_ref, wo_ref, h_ref):
    tq = x_ref.shape[0]
    mem_len = km_ref.shape[0]
    row_chunk = tq // 4

    def cross_attention():
        qt = jnp.transpose(qx_ref[...].astype(F32) * (SCALE * LOG2E))
        row_head = lax.broadcasted_iota(jnp.int32, qt.shape, 0) // HEAD_DIM
        q_bd = jnp.concatenate([jnp.where(row_head == h, qt, 0.0) for h in range(XATTN_HEADS)],
                               axis=1).astype(BF16)
        yield
        s = jnp.dot(km_ref[...], q_bd, preferred_element_type=F32)
        yield
        m = jnp.max(_fold_rows(s, jnp.maximum), axis=0, keepdims=True)
        p = jnp.exp2(s - m).astype(BF16)
        yield
        ones_rows = jnp.where(lax.broadcasted_iota(jnp.int32, (BF16_ROWS, mem_len), 0) == 0, 1.0, 0.0)
        vt = jnp.concatenate([jnp.transpose(vm_ref[...].astype(F32)), ones_rows], axis=0).astype(BF16)
        o_all = jnp.dot(vt, p, preferred_element_type=F32)
        yield
        outs = []
        for h in range(XATTN_HEADS):
            lanes = slice(h * tq, (h + 1) * tq)
            inv_l = 1.0 / o_all[XATTN_WIDTH:XATTN_WIDTH + 1, lanes]
            outs.append(o_all[h * HEAD_DIM:(h + 1) * HEAD_DIM, lanes] * inv_l)
        o = jnp.transpose(jnp.concatenate(outs, axis=0)).astype(BF16)
        yield
        return jnp.dot(o, wxo_ref[...], preferred_element_type=F32)

    def gated(gate_ref, val_ref):
        parts = []
        for r in range(0, tq, row_chunk):
            parts.append(_sigmoid_tanh(gate_ref[r:r + row_chunk, :].astype(F32)) * val_ref[r:r + row_chunk, :])
            yield
        return jnp.concatenate(parts, axis=0)

    def gate_only(gate_ref):
        parts = []
        for r in range(0, tq, row_chunk):
            parts.append(_sigmoid_tanh(gate_ref[r:r + row_chunk, :].astype(F32)))
            yield
        return jnp.concatenate(parts, axis=0)

    yx, y_a, y_b, g_x = _round_robin([cross_attention(), gated(m0_ref, ynsa_ref), gated(m1_ref, yrnn_ref),
                                      gate_only(m2_ref)])
    y = y_a + y_b + g_x * yx
    h_ref[...] = x_ref[...] + jnp.dot(y.astype(BF16), wo_ref[...], preferred_element_type=F32)


def _merge(x, z, ynsa, yrnn, kvm, wxo, wo, *, seq, mem_len, tq):
    t, d = x.shape
    nt = seq // tq
    row = lambda cb: pl.BlockSpec((tq, d), lambda i, cb=cb: (i, cb))
    mem_spec = lambda cb: pl.BlockSpec((mem_len, XATTN_WIDTH), lambda i, cb=cb: (i // nt, cb))
    return pl.pallas_call(
        _merge_kernel,
        grid=(t // tq,),
        in_specs=[
            row(0),
            row(OFF_MERGE // d), row(OFF_MERGE // d + 1), row(OFF_MERGE // d + 2),
            row(0), row(0),
            pl.BlockSpec((tq, XATTN_WIDTH), lambda i: (i, OFF_QX // XATTN_WIDTH)),
            mem_spec(0), mem_spec(1),
            pl.BlockSpec((XATTN_WIDTH, d), lambda i: (0, 0)),
            pl.BlockSpec((d, d), lambda i: (0, 0)),
        ],
        out_specs=row(0),
        out_shape=jax.ShapeDtypeStruct((t, d), F32),
        compiler_params=pltpu.CompilerParams(
            dimension_semantics=("parallel",), vmem_limit_bytes=VMEM_LIMIT),
        name="merge",
    )(x, z, z, z, ynsa, yrnn, z, kvm, kvm, wxo, wo)


def _mlp_kernel(h_ref, g_ref, wup_ref, wdn_ref, gf_ref, o_ref, *, tf):
    h = h_ref[...]
    ms = jnp.mean(h * h, axis=-1, keepdims=True)
    v = (h * lax.rsqrt(ms + RMS_EPS) * g_ref[...]).astype(BF16)
    h2 = h
    for f in range(0, wup_ref.shape[1], tf):
        up = jnp.dot(v, wup_ref[:, f:f + tf], preferred_element_type=F32)
        act = jnp.square(jnp.maximum(up, 0.0)).astype(BF16)
        h2 = h2 + jnp.dot(act, wdn_ref[f:f + tf, :], preferred_element_type=F32)
    ms = jnp.mean(h2 * h2, axis=-1, keepdims=True)
    o_ref[...] = h2 * lax.rsqrt(ms + RMS_EPS) * gf_ref[...]


def _mlp(h, g, wup, wdn, gf, *, tm, tf):
    t, d = h.shape
    dff = wup.shape[1]
    resident = lambda shape: pl.BlockSpec(shape, lambda i: (0, 0), pipeline_mode=pl.Buffered(1))
    return pl.pallas_call(
        functools.partial(_mlp_kernel, tf=tf),
        grid=(t // tm,),
        in_specs=[
            pl.BlockSpec((tm, d), lambda i: (i, 0)),
            pl.BlockSpec((1, d), lambda i: (0, 0)),
            resident((d, dff)), resident((dff, d)),
            pl.BlockSpec((1, d), lambda i: (0, 0)),
        ],
        out_specs=pl.BlockSpec((tm, d), lambda i: (i, 0)),
        out_shape=jax.ShapeDtypeStruct((t, d), F32),
        compiler_params=pltpu.CompilerParams(
            dimension_semantics=("parallel",), vmem_limit_bytes=VMEM_LIMIT),
        name="mlp",
    )(h, g, wup, wdn, gf)


def _rope_angles(pos):
    inv = 1.0 / (ROPE_THETA ** (jnp.arange(0, ROPE_DIM, 2, dtype=F32) / ROPE_DIM))
    ang = pos.astype(F32)[:, None] * inv[None, :]
    return jnp.cos(ang), jnp.sin(ang)


def _rope_tables(pos, width, rope_width=None):
    rope_width = width if rope_width is None else rope_width
    cos, sin = _rope_angles(pos)
    n = pos.shape[0]
    pad = HEAD_DIM - ROPE_DIM
    cos_h = jnp.concatenate([cos, cos, jnp.ones((n, pad), F32)], axis=1)
    sin_h = jnp.concatenate([-sin, sin, jnp.zeros((n, pad), F32)], axis=1)
    reps = rope_width // HEAD_DIM
    rest = width - rope_width
    cos_t = jnp.concatenate([jnp.tile(cos_h, (1, reps)), jnp.ones((n, rest), F32)], axis=1)
    sin_t = jnp.concatenate([jnp.tile(sin_h, (1, reps)), jnp.zeros((n, rest), F32)], axis=1)
    return cos_t, sin_t


def _pack_w_in(w_in):
    d = w_in.shape[0]
    w = w_in.astype(BF16)
    n_qkv = NSA_HEADS * HEAD_DIM + 6 * NSA_KV
    n_gates = NSA_HEADS * NSA_BRANCHES
    rnn_lo = n_qkv + n_gates
    qx_lo = rnn_lo + 2 * D_MODEL
    mg_lo = qx_lo + XATTN_WIDTH
    per_group = n_gates // NSA_GROUPS
    gates = w[:, n_qkv:rnn_lo].reshape(d, NSA_GROUPS, per_group)
    gates = jnp.pad(gates, ((0, 0), (0, 0), (0, GATE_STRIDE - per_group))).reshape(d, NSA_GROUPS * GATE_STRIDE)
    return jnp.concatenate([w[:, :n_qkv], w[:, qx_lo:mg_lo], gates, w[:, rnn_lo:qx_lo], w[:, mg_lo:]], axis=1)


def _block_diag_pairs(w):
    nb, k, _ = w.shape
    w = w.reshape(nb // 2, 2, k, k)
    zero = jnp.zeros((nb // 2, k, k), w.dtype)
    top = jnp.concatenate([w[:, 0], zero], axis=2)
    bot = jnp.concatenate([zero, w[:, 1]], axis=2)
    return jnp.concatenate([top, bot], axis=1).astype(BF16)


def _cmp_to_sel_t(n_cmp, n_cmp_pad, n_sel):
    c0 = np.arange(n_cmp_pad)[None, :] * CMP_STRIDE
    s0 = np.arange(n_sel)[:, None] * SEL_BLOCK
    ov = np.clip(np.minimum(c0 + CMP_LEN, s0 + SEL_BLOCK) - np.maximum(c0, s0), 0, None)
    ov = np.where(np.arange(n_cmp_pad)[None, :] < n_cmp, ov, 0)
    return (ov / CMP_LEN).astype(np.float32)


def kernel(x, mem, g_mix, w_in, cmp_pos_k, cmp_pos_v, w_cmp_k1, w_cmp_k2, w_cmp_v1, w_cmp_v2, conv_w, conv_b, w_rg_a, b_rg_a, w_rg_i, b_rg_i, rg_lambda, g_mem, w_mem_kv, w_xo, w_o, g_mlp, w_up, w_down, g_final):
    batch, seq, d = x.shape
    mem_len = mem.shape[1]
    t = batch * seq
    depth = g_mix.shape[0]
    n_sel = seq // SEL_BLOCK
    n_cmp_pad = seq // CMP_STRIDE
    assert n_cmp_pad == LANES and n_sel <= LANES and d == D_MODEL and depth == 1

    pos = jnp.arange(seq)
    cos_k, sin_k = _rope_tables(pos, NSA_KV)
    cos_q, sin_q = (a.T for a in _rope_angles(pos))
    no_rope = jnp.zeros((mem_len, LANES), F32)
    cmp_pos = jnp.arange(n_cmp_pad) * CMP_STRIDE + (CMP_LEN - 1)
    cos_c, sin_c = _rope_tables(cmp_pos, 2 * HEAD_DIM, HEAD_DIM)
    cos_c = jnp.tile(cos_c, (CMP_TM // n_cmp_pad, 2))
    sin_c = jnp.tile(sin_c, (CMP_TM // n_cmp_pad, 2))
    c2st = jnp.asarray(_cmp_to_sel_t((seq - CMP_LEN) // CMP_STRIDE + 1, n_cmp_pad, n_sel))

    h = x.reshape(t, d)
    for l in range(depth):
        z = _norm_proj(h, g_mix[l][None, :], _pack_w_in(w_in[l]), cos_k, sin_k, tm=INPROJ_TM, tn=INPROJ_TN,
                       rope_cols=ROPE_COLS, out_dtype=BF16, name="inproj")

        def pair_w1(w1):
            w3 = w1.astype(BF16).reshape(CMP_LEN, 1, HEAD_DIM, CMP_HIDDEN)
            zero = jnp.zeros_like(w3)
            both = jnp.concatenate([jnp.concatenate([w3, zero], axis=3), jnp.concatenate([zero, w3], axis=3)], axis=1)
            return both.reshape(CMP_LEN, 2 * HEAD_DIM, 2 * CMP_HIDDEN)

        def pair_pos(p):
            return jnp.concatenate([p, p], axis=1)

        w2 = jnp.zeros((2, 2, CMP_HIDDEN, 2, 2, HEAD_DIM), F32)
        for grp in range(2):
            w2 = w2.at[0, grp, :, grp, 0, :].set(w_cmp_k2[l]).at[1, grp, :, grp, 1, :].set(w_cmp_v2[l])
        w2 = w2.reshape(4 * CMP_HIDDEN, 4 * HEAD_DIM)
        kvc = _compress(z, pair_pos(cmp_pos_k[l]), pair_pos(cmp_pos_v[l]), pair_w1(w_cmp_k1[l]), pair_w1(w_cmp_v1[l]),
                        w2.astype(BF16), cos_c, sin_c, tm=CMP_TM)
        y_nsa = _nsa(z, cos_q, sin_q, kvc, c2st, batch=batch, seq=seq)

        y_rnn = _rglru(z, conv_w[l], conv_b[l][None, :], _block_diag_pairs(w_rg_a[l]), b_rg_a[l][None, :],
                       _block_diag_pairs(w_rg_i[l]), b_rg_i[l][None, :], rg_lambda[l][None, :],
                       batch=batch, seq=seq, ts=RGLRU_TS)

        kvm = _norm_proj(mem.reshape(batch * mem_len, d), g_mem[l][None, :], w_mem_kv[l].astype(BF16),
                         no_rope, no_rope, tm=mem_len, tn=2 * XATTN_WIDTH, rope_cols=(),
                         out_dtype=BF16, name="memkv")

        h1 = _merge(h, z, y_nsa, y_rnn, kvm, w_xo[l].astype(BF16), w_o[l].astype(BF16),
                    seq=seq, mem_len=mem_len, tq=MERGE_TQ)
        h = _mlp(h1, g_mlp[l][None, :], w_up[l].astype(BF16), w_down[l].astype(BF16), g_final[None, :],
                 tm=MLP_TM, tf=MLP_TF)
    return h.reshape(batch, seq, d)
```

```python
import functools
import math

import numpy as np
import jax
import jax.numpy as jnp
from jax import lax
from jax.experimental import pallas as pl
from jax.experimental.pallas import tpu as pltpu

F32 = jnp.float32
BF16 = jnp.bfloat16

D_MODEL = 1024
HEAD_DIM = 64
NSA_HEADS = 16
NSA_GROUPS = 4
NSA_HPG = NSA_HEADS // NSA_GROUPS
NSA_BRANCHES = 3
CMP_LEN = 32
CMP_STRIDE = 16
CMP_HIDDEN = 256
SEL_BLOCK = 64
SEL_TOPK = 8
WINDOW = 512
Q_BLOCK = 128
ROPE_THETA = 500000.0
ROPE_DIM = HEAD_DIM // 4
ROPE_HALF = ROPE_DIM // 2
CONV_WIDTH = 4
RGLRU_C = 8.0
XATTN_HEADS = 4
XATTN_WIDTH = XATTN_HEADS * HEAD_DIM
RMS_EPS = 1e-6
SCALE = HEAD_DIM ** -0.5
LOG2E = math.log2(math.e)
NEG = -1e30

LANES = 128
SUBLANES = 8
BF16_ROWS = 16
VMEM_LIMIT = 48 * 1024 * 1024

NSA_KV = NSA_GROUPS * HEAD_DIM
OFF_Q = 0
OFF_KC = 1024
OFF_VC = 1280
OFF_KS = 1536
OFF_VS = 1792
OFF_KW = 2048
OFF_VW = 2304
OFF_QX = 2560
OFF_GATE = 2816
GATE_STRIDE = 64
OFF_XRNN = 3072
OFF_GRNN = 4096
OFF_MERGE = 5120
INPROJ_TN = 2048
INPROJ_TM = 512
ROPE_COLS = (OFF_KS, OFF_KW)

NSA_PAIRS = 2
NSA_GPS = 2 * NSA_PAIRS
NSA_LANES = NSA_HPG * Q_BLOCK
SEL_CHUNK = 2 * Q_BLOCK
WIN_TILES = WINDOW // Q_BLOCK + 1
V_AUG = HEAD_DIM + BF16_ROWS

MERGE_TQ = 512
RGLRU_TS = 128
MLP_TM = 1024
MLP_TF = 1024
CMP_TM = 512


def _sigmoid(x):
    return 1.0 / (1.0 + jnp.exp(-x))


def _sigmoid_tanh(x):
    return 0.5 * jnp.tanh(0.5 * x) + 0.5


def _gelu_tanh(x):
    return 0.5 * x * (1.0 + jnp.tanh(0.7978845608028654 * (x + 0.044715 * (x * x * x))))


def _rope_swap(z):
    n = z.shape[-1]
    lane = lax.broadcasted_iota(jnp.int32, z.shape, z.ndim - 1)
    first_half = (lane & (HEAD_DIM - 1)) < ROPE_HALF
    return jnp.where(first_half, pltpu.roll(z, n - ROPE_HALF, z.ndim - 1), pltpu.roll(z, ROPE_HALF, z.ndim - 1))


def _round_robin(gens, lead=0):
    results = [None] * len(gens)
    alive = list(range(len(gens)))

    def advance(idx):
        try:
            next(gens[idx])
        except StopIteration as stop:
            results[idx] = stop.value
            alive.remove(idx)

    for _ in range(lead):
        for idx in [n for n in alive if n < len(gens) // 2]:
            advance(idx)
    while alive:
        for idx in list(alive):
            advance(idx)
    return results


def _fold_rows(x, op):
    parts = [x[r * SUBLANES:(r + 1) * SUBLANES] for r in range(x.shape[0] // SUBLANES)]
    return functools.reduce(op, parts)


def _norm_proj_kernel(x_ref, g_ref, w_ref, cos_ref, sin_ref, z_ref, *, tn, rope_cols):
    x = x_ref[...]
    ms = jnp.mean(x * x, axis=-1, keepdims=True)
    u = (x * lax.rsqrt(ms + RMS_EPS) * g_ref[...]).astype(BF16)
    rw = cos_ref.shape[1]
    for n0 in range(0, w_ref.shape[1], tn):
        z = jnp.dot(u, w_ref[:, n0:n0 + tn], preferred_element_type=F32)
        z_ref[:, n0:n0 + tn] = z.astype(z_ref.dtype)
        for col in rope_cols:
            if n0 <= col < n0 + tn:
                zr = z[:, col - n0:col - n0 + rw]
                z_ref[:, col:col + rw] = (zr * cos_ref[...] + _rope_swap(zr) * sin_ref[...]).astype(z_ref.dtype)


def _norm_proj(x, g, w, cos_t, sin_t, *, tm, tn, rope_cols, out_dtype, name):
    t, d = x.shape
    n = w.shape[1]
    s_tiles = cos_t.shape[0] // tm
    rw = cos_t.shape[1]
    table_spec = pl.BlockSpec((tm, rw), lambda i: (i % s_tiles, 0))
    return pl.pallas_call(
        functools.partial(_norm_proj_kernel, tn=tn, rope_cols=tuple(rope_cols)),
        grid=(t // tm,),
        in_specs=[
            pl.BlockSpec((tm, d), lambda i: (i, 0)),
            pl.BlockSpec((1, d), lambda i: (0, 0)),
            pl.BlockSpec((d, n), lambda i: (0, 0), pipeline_mode=pl.Buffered(1)),
            table_spec, table_spec,
        ],
        out_specs=pl.BlockSpec((tm, n), lambda i: (i, 0)),
        out_shape=jax.ShapeDtypeStruct((t, n), out_dtype),
        compiler_params=pltpu.CompilerParams(
            dimension_semantics=("parallel",), vmem_limit_bytes=VMEM_LIMIT),
        name=name,
    )(x, g, w, cos_t, sin_t)


def _compress_kernel(xk_ref, xv_ref, pk_ref, pv_ref, wk1_ref, wv1_ref, w2_ref, cos_ref, sin_ref, o_ref, xs_scr):
    tm = o_ref.shape[0]

    def hidden(x_ref, p_ref, w_ref):
        xs_scr[...] = x_ref[...].astype(F32)
        first = jnp.zeros((tm, w_ref.shape[2]), F32)
        second = jnp.zeros((tm, w_ref.shape[2]), F32)
        for j in range(CMP_STRIDE):
            xj = xs_scr[pl.ds(j, tm, stride=CMP_STRIDE), :]
            first = first + jnp.dot((xj + p_ref[j:j + 1, :]).astype(BF16), w_ref[j],
                                    preferred_element_type=F32)
            second = second + jnp.dot((xj + p_ref[CMP_STRIDE + j:CMP_STRIDE + j + 1, :]).astype(BF16),
                                      w_ref[CMP_STRIDE + j], preferred_element_type=F32)
        return _gelu_tanh(first + pltpu.roll(second, tm - 1, 0)).astype(BF16)

    hcat = jnp.concatenate([hidden(xk_ref, pk_ref, wk1_ref), hidden(xv_ref, pv_ref, wv1_ref)], axis=1)
    kv = jnp.dot(hcat, w2_ref[...], preferred_element_type=F32)
    o_ref[...] = kv * cos_ref[...] + _rope_swap(kv) * sin_ref[...]


def _compress(z, pk, pv, wk1, wv1, w2, cos_t, sin_t, *, tm):
    t = z.shape[0]
    m = t // CMP_STRIDE
    pairs = NSA_GROUPS // 2
    pair_w = 2 * HEAD_DIM
    hid2 = wk1.shape[2]
    out_w = w2.shape[1]
    full = lambda shape: pl.BlockSpec(shape, lambda p, i: (0,) * len(shape))
    tokens = lambda off: pl.BlockSpec((tm * CMP_STRIDE, pair_w), lambda p, i: (i, off // pair_w + p))
    return pl.pallas_call(
        _compress_kernel,
        grid=(pairs, m // tm),
        in_specs=[
            tokens(OFF_KC), tokens(OFF_VC),
            full((CMP_LEN, pair_w)), full((CMP_LEN, pair_w)),
            full((CMP_LEN, pair_w, hid2)), full((CMP_LEN, pair_w, hid2)),
            full((2 * hid2, out_w)),
            full((tm, out_w)), full((tm, out_w)),
        ],
        out_specs=pl.BlockSpec((tm, out_w), lambda p, i: (i, p)),
        out_shape=jax.ShapeDtypeStruct((m, pairs * out_w), F32),
        scratch_shapes=[pltpu.VMEM((tm * CMP_STRIDE, pair_w), F32)],
        compiler_params=pltpu.CompilerParams(
            dimension_semantics=("parallel", "parallel"), vmem_limit_bytes=VMEM_LIMIT),
        name="compress",
    )(z, z, pk, pv, wk1, wv1, w2, cos_t, sin_t)


def _nsa_kernel(q_ref, g_ref, cos_ref, sin_ref, *rest, n_sel):
    kv_refs, rest = rest[:4 * NSA_PAIRS], rest[4 * NSA_PAIRS:]
    kvc_ref, c2st_ref, o_ref, gt_scr, s_scr, vt_scr = rest
    ks_refs, vs_refs, kw_refs, vw_refs = (kv_refs[n::4] for n in range(4))
    g2 = pl.program_id(1)
    i = pl.program_id(2)
    q0 = i * Q_BLOCK
    seq = ks_refs[0].shape[0]

    @pl.when(i == 0)
    def _():
        ones_rows = jnp.where(lax.broadcasted_iota(jnp.int32, (V_AUG - HEAD_DIM, Q_BLOCK), 0) == 0,
                              1.0, 0.0).astype(BF16)

        def fill(t, carry):
            r0 = pl.multiple_of(t * Q_BLOCK, Q_BLOCK)
            for kind, refs in enumerate((vs_refs, vw_refs)):
                for pair, ref in enumerate(refs):
                    vt = jnp.transpose(ref[pl.ds(r0, Q_BLOCK), :].astype(F32)).astype(BF16)
                    for half in range(2):
                        gg = 2 * pair + half
                        vt_scr[kind, gg, t, 0:HEAD_DIM, :] = vt[half * HEAD_DIM:(half + 1) * HEAD_DIM]
                        vt_scr[kind, gg, t, HEAD_DIM:V_AUG, :] = ones_rows
            return carry

        lax.fori_loop(0, seq // Q_BLOCK, fill, 0)

    def q_operand(gg):
        zeros_q = jnp.zeros((HEAD_DIM, NSA_LANES), BF16)
        return [qts[gg], zeros_q] if gg % 2 == 0 else [zeros_q, qts[gg]]

    blocks_per_chunk = SEL_CHUNK // SEL_BLOCK
    tiles_per_chunk = SEL_CHUNK // Q_BLOCK
    q_per_chunk = SEL_CHUNK // Q_BLOCK

    def tile_heads(x):
        return jnp.concatenate([x] * NSA_HPG, axis=1)

    tq = q0 + (lax.broadcasted_iota(jnp.int32, (1, NSA_LANES), 1) & (Q_BLOCK - 1))
    c_idx = lax.broadcasted_iota(jnp.int32, (Q_BLOCK, 1), 0)
    r_idx = lax.broadcasted_iota(jnp.int32, (1, Q_BLOCK), 1)
    tri_diag = tile_heads(jnp.where(c_idx <= r_idx, 0.0, NEG))
    tri_old = tile_heads(jnp.where(c_idx > r_idx, 0.0, NEG))
    cos_q = tile_heads(cos_ref[...])
    sin_q = tile_heads(sin_ref[...])
    blk = lax.broadcasted_iota(jnp.int32, (n_sel, Q_BLOCK), 0)
    cur = (q0 + lax.broadcasted_iota(jnp.int32, (n_sel, Q_BLOCK), 1)) // SEL_BLOCK
    sub = lax.broadcasted_iota(jnp.int32, (SUBLANES, Q_BLOCK), 0)

    gt_scr[...] = jnp.transpose(_sigmoid(g_ref[...].astype(F32)))

    def prepare(gg):
        qx = jnp.transpose(q_ref[:, gg * NSA_HPG * HEAD_DIM:(gg + 1) * NSA_HPG * HEAD_DIM].astype(F32)
                           * (SCALE * LOG2E))
        qf = jnp.concatenate([qx[h * HEAD_DIM:(h + 1) * HEAD_DIM, :] for h in range(NSA_HPG)], axis=1)
        x1, x2 = qf[0:ROPE_HALF], qf[ROPE_HALF:ROPE_DIM]
        qt = jnp.concatenate([x1 * cos_q - x2 * sin_q, x2 * cos_q + x1 * sin_q, qf[ROPE_DIM:]],
                             axis=0).astype(BF16)
        qts[gg] = qt
        yield

        kvc = kvc_ref[:, gg * LANES:(gg + 1) * LANES]
        kc = kvc[:, :HEAD_DIM].astype(BF16)
        vct = jnp.transpose(kvc)[HEAD_DIM:, :].astype(BF16)
        sc = jnp.dot(kc, qt, preferred_element_type=F32)
        yield
        n_idx = lax.broadcasted_iota(jnp.int32, (LANES, 1), 0)
        sc = jnp.where((n_idx * CMP_STRIDE + (CMP_LEN - 1)) <= tq, sc, NEG)
        m_c = jnp.max(sc, axis=0, keepdims=True)
        e_c = jnp.exp2(sc - m_c)
        inv_c = 1.0 / jnp.maximum(jnp.sum(e_c, axis=0, keepdims=True), 1e-30)
        p_c = e_c * jnp.where(m_c > 0.5 * NEG, inv_c, 0.0)
        o_c = jnp.dot(vct, p_c.astype(BF16), preferred_element_type=F32)
        yield

        p_sum = p_c[:, 0:Q_BLOCK]
        for h in range(1, NSA_HPG):
            p_sum = p_sum + p_c[:, h * Q_BLOCK:(h + 1) * Q_BLOCK]
        imp = jnp.dot(c2st_ref[...], p_sum, preferred_element_type=F32,
                      precision=lax.Precision.HIGHEST)
        yield
        forced = (blk == 0) | (blk == cur) | (blk == cur - 1)
        val = jnp.where(forced, jnp.inf, jnp.where(blk > cur, -jnp.inf, imp))
        n_grp = n_sel // SUBLANES
        grp = [val[v * SUBLANES:(v + 1) * SUBLANES] for v in range(n_grp)]
        cnt = [jnp.zeros((SUBLANES, Q_BLOCK), F32) for _ in range(n_grp)]
        for sp in range(n_sel):
            r = val[sp:sp + 1, :]
            for v in range(n_grp):
                if v * SUBLANES > sp:
                    cnt[v] = cnt[v] + jnp.where(r >= grp[v], 1.0, 0.0)
                elif v * SUBLANES + SUBLANES - 1 <= sp:
                    cnt[v] = cnt[v] + jnp.where(r > grp[v], 1.0, 0.0)
                else:
                    later = jnp.where(sub > sp - v * SUBLANES, 1.0, 0.0)
                    cnt[v] = cnt[v] + jnp.where(r > grp[v], 1.0, jnp.where(r == grp[v], later, 0.0))
            if sp % SUBLANES == SUBLANES - 1:
                yield
        chosen = jnp.concatenate([jnp.where(c < float(SEL_TOPK), 0.0, NEG) for c in cnt], axis=0)
        block_bias = tile_heads(jnp.where(blk > cur, NEG, chosen))
        row0 = pl.multiple_of((g2 * NSA_GPS + gg) * GATE_STRIDE, GATE_STRIDE)
        gt = gt_scr[pl.ds(row0, 2 * SUBLANES), :]
        return qt, o_c, block_bias, gt

    def window(gg):
        tiles = []
        for d in range(1 - WIN_TILES, 1):
            td = i + d
            tc = jnp.maximum(td, 0)
            k0 = pl.multiple_of(tc * Q_BLOCK, Q_BLOCK)
            before_start = jnp.where(td >= 0, 0.0, NEG) if d < 0 else 0.0
            tri = tri_diag if d == 0 else (tri_old if d == 1 - WIN_TILES else None)
            tiles.append((tc, k0, before_start, tri))

        ones_rows = jnp.where(lax.broadcasted_iota(jnp.int32, (BF16_ROWS, NSA_LANES), 0) == 0, 1.0, 0.0).astype(BF16)
        q_aug = jnp.concatenate(q_operand(gg) + [ones_rows, jnp.zeros((LANES - BF16_ROWS, NSA_LANES), BF16)],
                                axis=0)
        first_lane = lax.broadcasted_iota(jnp.int32, (Q_BLOCK, LANES), 1) == 0
        k_rows = jnp.concatenate([kw_refs[gg // 2][pl.ds(k0, Q_BLOCK), :] for _, k0, _, _ in tiles], axis=0)
        k_bias = jnp.concatenate([jnp.where(first_lane, before_start, 0.0).astype(BF16)
                                  for _, _, before_start, _ in tiles], axis=0)
        s_all = jnp.dot(jnp.concatenate([k_rows, k_bias], axis=1), q_aug, preferred_element_type=F32)
        yield
        scores = []
        for n, (_, _, _, tri) in enumerate(tiles):
            s = s_all[n * Q_BLOCK:(n + 1) * Q_BLOCK]
            scores.append(s if tri is None else s + tri)
        m_w = jnp.max(functools.reduce(jnp.maximum, [_fold_rows(s, jnp.maximum) for s in scores]),
                      axis=0, keepdims=True)
        n_pad = jnp.maximum(WINDOW - 1 - tq, 0).astype(F32)
        m_w = jnp.where(n_pad > 0.0, jnp.maximum(m_w, 0.0), m_w)
        yield
        acc = jnp.zeros((V_AUG, NSA_LANES), F32)
        for (tc, _, _, _), s in zip(tiles, scores):
            acc = acc + jnp.dot(vt_scr[1, gg, tc], jnp.exp2(s - m_w).astype(BF16), preferred_element_type=F32)
            yield
        l_w = acc[HEAD_DIM:HEAD_DIM + 1] + n_pad * jnp.exp2(-m_w)
        return acc[:HEAD_DIM] * (1.0 / jnp.maximum(l_w, 1e-30))

    def chunk_scores(gg, c, diag_rel):
        rows = slice(c * SEL_CHUNK, (c + 1) * SEL_CHUNK)
        s = jnp.dot(ks_refs[gg // 2][rows, :], jnp.concatenate(q_operand(gg), axis=0),
                    preferred_element_type=F32)
        if diag_rel is not None:
            s = s + jnp.concatenate([jnp.where(diag_rel == u, tri_diag, 0.0) for u in range(tiles_per_chunk)], axis=0)
        s_scr[gg, rows, :] = s
        bmax = [_fold_rows(s[b * SEL_BLOCK:(b + 1) * SEL_BLOCK], jnp.maximum) for b in range(blocks_per_chunk)]
        yield
        return bmax

    def first_chunk(gg):
        yield
        return (yield from chunk_scores(gg, 0, i))

    def selected(gg, k, block_bias, bmax0):
        bmax = list(bmax0)
        for c in range(1, k + 1):
            bmax += yield from chunk_scores(gg, c, i - k * q_per_chunk if c == k else None)
        m8 = functools.reduce(jnp.maximum, [bm + block_bias[n:n + 1, :] for n, bm in enumerate(bmax)])
        m_s = jnp.max(m8, axis=0, keepdims=True)
        shift = block_bias - m_s
        yield
        acc = jnp.zeros((V_AUG, NSA_LANES), F32)
        for c in range(k + 1):
            ps = []
            for b in range(blocks_per_chunk):
                n = c * blocks_per_chunk + b
                ps.append(jnp.exp2(s_scr[gg, n * SEL_BLOCK:(n + 1) * SEL_BLOCK, :] + shift[n:n + 1, :]).astype(BF16))
            vt = jnp.concatenate([vt_scr[0, gg, c * tiles_per_chunk + u] for u in range(tiles_per_chunk)], axis=1)
            acc = acc + jnp.dot(vt, jnp.concatenate(ps, axis=0), preferred_element_type=F32)
            yield
        l_s = acc[HEAD_DIM:HEAD_DIM + 1]
        return acc[:HEAD_DIM] * (1.0 / jnp.maximum(l_s, 1e-30))

    qts = [None] * NSA_GPS
    groups = range(NSA_GPS)
    shared = _round_robin([prepare(gg) for gg in groups] + [window(gg) for gg in groups]
                          + [first_chunk(gg) for gg in groups])
    prepared, windows, bmax0 = (shared[n * NSA_GPS:(n + 1) * NSA_GPS] for n in range(3))

    def finish(k):
        branches = []
        for gg, (qt, o_c, block_bias, gt) in enumerate(prepared):
            branches.append(selected(gg, k, block_bias, bmax0[gg]))
        results = _round_robin(branches, lead=k + 1)
        outs = []
        for gg, (qt, o_c, block_bias, gt) in enumerate(prepared):
            o_w, o_s = windows[gg], results[gg]
            for h in range(NSA_HPG):
                sl = slice(h * Q_BLOCK, (h + 1) * Q_BLOCK)
                row = NSA_BRANCHES * h
                outs.append(gt[row:row + 1, :] * o_c[:, sl] + gt[row + 1:row + 2, :] * o_s[:, sl]
                            + gt[row + 2:row + 3, :] * o_w[:, sl])
        o_ref[...] = jnp.transpose(jnp.concatenate(outs, axis=0))

    for k in range(n_sel * SEL_BLOCK // SEL_CHUNK):
        pl.when(i // q_per_chunk == k)(functools.partial(finish, k))


def _nsa(z, cos_q, sin_q, kvc, c2st, *, batch, seq):
    nq = seq // Q_BLOCK
    n_sel = seq // SEL_BLOCK
    gw = NSA_GPS * NSA_HPG * HEAD_DIM
    gates_w = NSA_GROUPS * GATE_STRIDE
    pair_w = 2 * HEAD_DIM
    assert pair_w == LANES
    kv_spec = lambda off, pair: pl.BlockSpec(
        (seq, pair_w), lambda b, g, i: (b, off // pair_w + g * NSA_PAIRS + pair))
    kv_specs = [kv_spec(off, pair) for pair in range(NSA_PAIRS) for off in (OFF_KS, OFF_VS, OFF_KW, OFF_VW)]
    rope_spec = pl.BlockSpec((ROPE_HALF, Q_BLOCK), lambda b, g, i: (0, i))
    return pl.pallas_call(
        functools.partial(_nsa_kernel, n_sel=n_sel),
        grid=(batch, NSA_GROUPS // NSA_GPS, nq),
        in_specs=[
            pl.BlockSpec((Q_BLOCK, gw), lambda b, g, i: (b * nq + i, OFF_Q // gw + g)),
            pl.BlockSpec((Q_BLOCK, gates_w), lambda b, g, i: (b * nq + i, OFF_GATE // gates_w)),
            rope_spec, rope_spec,
            *kv_specs,
            pl.BlockSpec((LANES, NSA_GPS * 2 * HEAD_DIM), lambda b, g, i: (b, g)),
            pl.BlockSpec((n_sel, LANES), lambda b, g, i: (0, 0)),
        ],
        out_specs=pl.BlockSpec((Q_BLOCK, gw), lambda b, g, i: (b * nq + i, g)),
        out_shape=jax.ShapeDtypeStruct((batch * seq, NSA_HEADS * HEAD_DIM), F32),
        scratch_shapes=[pltpu.VMEM((gates_w, Q_BLOCK), F32), pltpu.VMEM((NSA_GPS, seq, NSA_LANES), F32),
                        pltpu.VMEM((2, NSA_GPS, nq, V_AUG, Q_BLOCK), BF16)],
        compiler_params=pltpu.CompilerParams(
            dimension_semantics=("arbitrary", "arbitrary", "arbitrary"), vmem_limit_bytes=VMEM_LIMIT),
        name="nsa",
    )(z, z, cos_q, sin_q, *([z] * len(kv_specs)), kvc, c2st)


def _rglru_kernel(x_ref, gr_ref, cw_ref, cb_ref, wa_ref, ba_ref, wi_ref, bi_ref, lam_ref, y_ref,
                  xs_scr, a_scr, b_scr, h_scr):
    nb, ts, c = x_ref.shape
    pitch = a_scr.shape[1] // nb
    lane_tiles = c // LANES

    def put(scr, rows, val):
        for m in range(lane_tiles):
            scr[m, rows, :] = val[:, m * LANES:(m + 1) * LANES]

    def get(scr, rows):
        return jnp.concatenate([scr[m, rows, :] for m in range(lane_tiles)], axis=1)

    @pl.when(pl.program_id(0) == 0)
    def _():
        xs_scr[:, 0:SUBLANES, :] = jnp.zeros((nb, SUBLANES, c), F32)
        h_scr[...] = jnp.zeros(h_scr.shape, F32)

    nl = -lam_ref[...]
    softplus = jnp.maximum(nl, 0.0) + jnp.log1p(jnp.exp(-jnp.abs(nl)))

    def gates(bi_, carry):
        x = x_ref[bi_].astype(F32)
        xs_scr[bi_, SUBLANES:SUBLANES + ts, :] = x
        xc = cb_ref[...] + cw_ref[CONV_WIDTH - 1:CONV_WIDTH, :] * x
        for k in range(CONV_WIDTH - 1):
            off = SUBLANES - (CONV_WIDTH - 1) + k
            xc = xc + cw_ref[k:k + 1, :] * xs_scr[bi_, off:off + ts, :]
        xs_scr[bi_, 0:SUBLANES, :] = x[ts - SUBLANES:, :]

        ra, ri = [], []
        for mblk in range(c // LANES):
            xb = xc[:, mblk * LANES:(mblk + 1) * LANES].astype(BF16)
            ra.append(jnp.dot(xb, wa_ref[mblk], preferred_element_type=F32))
            ri.append(jnp.dot(xb, wi_ref[mblk], preferred_element_type=F32))
        r = _sigmoid(jnp.concatenate(ra, axis=1) + ba_ref[...])
        gi = _sigmoid(jnp.concatenate(ri, axis=1) + bi_ref[...])

        log_a = (-RGLRU_C) * r * softplus
        th = jnp.tanh(log_a)
        y2 = -2.0 * th / (1.0 - th)
        rows = pl.ds(pl.multiple_of(bi_ * pitch, SUBLANES), ts)
        put(a_scr, rows, jnp.exp(log_a))
        put(b_scr, rows, jnp.where(y2 > 0.0, y2 * lax.rsqrt(y2), 0.0) * (gi * xc))
        return carry

    lax.fori_loop(0, nb, gates, 0)

    def step(t, h):
        rows = pl.ds(t, nb, stride=pitch)
        h = get(a_scr, rows) * h + get(b_scr, rows)
        put(b_scr, rows, h)
        return h

    h_scr[...] = lax.fori_loop(0, ts, step, h_scr[...], unroll=8)

    def outputs(bi_, carry):
        rows = pl.ds(pl.multiple_of(bi_ * pitch, SUBLANES), ts)
        y_ref[bi_] = _gelu_tanh(gr_ref[bi_].astype(F32)) * get(b_scr, rows)
        return carry

    lax.fori_loop(0, nb, outputs, 0)


def _rglru(z, cw, cb, wa2, ba, wi2, bi, lam, *, batch, seq, ts):
    c = D_MODEL
    pitch = ts + SUBLANES
    z3 = z.reshape(batch, seq, z.shape[1])
    vec = pl.BlockSpec((1, c), lambda t: (0, 0))
    y = pl.pallas_call(
        _rglru_kernel,
        grid=(seq // ts,),
        in_specs=[
            pl.BlockSpec((batch, ts, c), lambda t: (0, t, OFF_XRNN // c)),
            pl.BlockSpec((batch, ts, c), lambda t: (0, t, OFF_GRNN // c)),
            pl.BlockSpec((CONV_WIDTH, c), lambda t: (0, 0)),
            vec,
            pl.BlockSpec((c // LANES, LANES, LANES), lambda t: (0, 0, 0)),
            vec,
            pl.BlockSpec((c // LANES, LANES, LANES), lambda t: (0, 0, 0)),
            vec, vec,
        ],
        out_specs=pl.BlockSpec((batch, ts, c), lambda t: (0, t, 0)),
        out_shape=jax.ShapeDtypeStruct((batch, seq, c), F32),
        scratch_shapes=[pltpu.VMEM((batch, SUBLANES + ts, c), F32), pltpu.VMEM((c // LANES, batch * pitch, LANES), F32),
                        pltpu.VMEM((c // LANES, batch * pitch, LANES), F32), pltpu.VMEM((batch, c), F32)],
        compiler_params=pltpu.CompilerParams(
            dimension_semantics=("arbitrary",), vmem_limit_bytes=VMEM_LIMIT),
        name="rglru",
    )(z3, z3, cw, cb, wa2, ba, wi2, bi, lam)
    return y.reshape(batch * seq, c)


def _merge_kernel(x_ref, m0_ref, m1_ref, m2_ref, ynsa_ref, yrnn_ref, qx_ref, km_ref, vm_ref, wxo_ref, wo_ref, h_ref):
    tq = x_ref.shape[0]
    mem_len = km_ref.shape[0]
    row_chunk = tq // 4

    def cross_attention():
        qt = jnp.transpose(qx_ref[...].astype(F32) * (SCALE * LOG2E))
        row_head = lax.broadcasted_iota(jnp.int32, qt.shape, 0) // HEAD_DIM
        q_bd = jnp.concatenate([jnp.where(row_head == h, qt, 0.0) for h in range(XATTN_HEADS)],
                               axis=1).astype(BF16)
        yield
        s = jnp.dot(km_ref[...], q_bd, preferred_element_type=F32)
        yield
        m = jnp.max(_fold_rows(s, jnp.maximum), axis=0, keepdims=True)
        p = jnp.exp2(s - m).astype(BF16)
        yield
        ones_rows = jnp.where(lax.broadcasted_iota(jnp.int32, (BF16_ROWS, mem_len), 0) == 0, 1.0, 0.0)
        vt = jnp.concatenate([jnp.transpose(vm_ref[...].astype(F32)), ones_rows], axis=0).astype(BF16)
        o_all = jnp.dot(vt, p, preferred_element_type=F32)
        yield
        outs = []
        for h in range(XATTN_HEADS):
            lanes = slice(h * tq, (h + 1) * tq)
            inv_l = 1.0 / o_all[XATTN_WIDTH:XATTN_WIDTH + 1, lanes]
            outs.append(o_all[h * HEAD_DIM:(h + 1) * HEAD_DIM, lanes] * inv_l)
        o = jnp.transpose(jnp.concatenate(outs, axis=0)).astype(BF16)
        yield
        return jnp.dot(o, wxo_ref[...], preferred_element_type=F32)

    def gated(gate_ref, val_ref):
        parts = []
        for r in range(0, tq, row_chunk):
            parts.append(_sigmoid_tanh(gate_ref[r:r + row_chunk, :].astype(F32)) * val_ref[r:r + row_chunk, :])
            yield
        return jnp.concatenate(parts, axis=0)

    def gate_only(gate_ref):
        parts = []
        for r in range(0, tq, row_chunk):
            parts.append(_sigmoid_tanh(gate_ref[r:r + row_chunk, :].astype(F32)))
            yield
        return jnp.concatenate(parts, axis=0)

    yx, y_a, y_b, g_x = _round_robin([cross_attention(), gated(m0_ref, ynsa_ref), gated(m1_ref, yrnn_ref),
                                      gate_only(m2_ref)])
    y = y_a + y_b + g_x * yx
    h_ref[...] = x_ref[...] + jnp.dot(y.astype(BF16), wo_ref[...], preferred_element_type=F32)


def _merge(x, z, ynsa, yrnn, kvm, wxo, wo, *, seq, mem_len, tq):
    t, d = x.shape
    nt = seq // tq
    row = lambda cb: pl.BlockSpec((tq, d), lambda i, cb=cb: (i, cb))
    mem_spec = lambda cb: pl.BlockSpec((mem_len, XATTN_WIDTH), lambda i, cb=cb: (i // nt, cb))
    return pl.pallas_call(
        _merge_kernel,
        grid=(t // tq,),
        in_specs=[
            row(0),
            row(OFF_MERGE // d), row(OFF_MERGE // d + 1), row(OFF_MERGE // d + 2),
            row(0), row(0),
            pl.BlockSpec((tq, XATTN_WIDTH), lambda i: (i, OFF_QX // XATTN_WIDTH)),
            mem_spec(0), mem_spec(1),
            pl.BlockSpec((XATTN_WIDTH, d), lambda i: (0, 0)),
            pl.BlockSpec((d, d), lambda i: (0, 0)),
        ],
        out_specs=row(0),
        out_shape=jax.ShapeDtypeStruct((t, d), F32),
        compiler_params=pltpu.CompilerParams(
            dimension_semantics=("parallel",), vmem_limit_bytes=VMEM_LIMIT),
        name="merge",
    )(x, z, z, z, ynsa, yrnn, z, kvm, kvm, wxo, wo)


def _mlp_kernel(h_ref, g_ref, wup_ref, wdn_ref, gf_ref, o_ref, *, tf):
    h = h_ref[...]
    ms = jnp.mean(h * h, axis=-1, keepdims=True)
    v = (h * lax.rsqrt(ms + RMS_EPS) * g_ref[...]).astype(BF16)
    h2 = h
    for f in range(0, wup_ref.shape[1], tf):
        up = jnp.dot(v, wup_ref[:, f:f + tf], preferred_element_type=F32)
        act = jnp.square(jnp.maximum(up, 0.0)).astype(BF16)
        h2 = h2 + jnp.dot(act, wdn_ref[f:f + tf, :], preferred_element_type=F32)
    ms = jnp.mean(h2 * h2, axis=-1, keepdims=True)
    o_ref[...] = h2 * lax.rsqrt(ms + RMS_EPS) * gf_ref[...]


def _mlp(h, g, wup, wdn, gf, *, tm, tf):
    t, d = h.shape
    dff = wup.shape[1]
    resident = lambda shape: pl.BlockSpec(shape, lambda i: (0, 0), pipeline_mode=pl.Buffered(1))
    return pl.pallas_call(
        functools.partial(_mlp_kernel, tf=tf),
        grid=(t // tm,),
        in_specs=[
            pl.BlockSpec((tm, d), lambda i: (i, 0)),
            pl.BlockSpec((1, d), lambda i: (0, 0)),
            resident((d, dff)), resident((dff, d)),
            pl.BlockSpec((1, d), lambda i: (0, 0)),
        ],
        out_specs=pl.BlockSpec((tm, d), lambda i: (i, 0)),
        out_shape=jax.ShapeDtypeStruct((t, d), F32),
        compiler_params=pltpu.CompilerParams(
            dimension_semantics=("parallel",), vmem_limit_bytes=VMEM_LIMIT),
        name="mlp",
    )(h, g, wup, wdn, gf)


def _rope_angles(pos):
    inv = 1.0 / (ROPE_THETA ** (jnp.arange(0, ROPE_DIM, 2, dtype=F32) / ROPE_DIM))
    ang = pos.astype(F32)[:, None] * inv[None, :]
    return jnp.cos(ang), jnp.sin(ang)


def _rope_tables(pos, width, rope_width=None):
    rope_width = width if rope_width is None else rope_width
    cos, sin = _rope_angles(pos)
    n = pos.shape[0]
    pad = HEAD_DIM - ROPE_DIM
    cos_h = jnp.concatenate([cos, cos, jnp.ones((n, pad), F32)], axis=1)
    sin_h = jnp.concatenate([-sin, sin, jnp.zeros((n, pad), F32)], axis=1)
    reps = rope_width // HEAD_DIM
    rest = width - rope_width
    cos_t = jnp.concatenate([jnp.tile(cos_h, (1, reps)), jnp.ones((n, rest), F32)], axis=1)
    sin_t = jnp.concatenate([jnp.tile(sin_h, (1, reps)), jnp.zeros((n, rest), F32)], axis=1)
    return cos_t, sin_t


def _pack_w_in(w_in):
    d = w_in.shape[0]
    w = w_in.astype(BF16)
    n_qkv = NSA_HEADS * HEAD_DIM + 6 * NSA_KV
    n_gates = NSA_HEADS * NSA_BRANCHES
    rnn_lo = n_qkv + n_gates
    qx_lo = rnn_lo + 2 * D_MODEL
    mg_lo = qx_lo + XATTN_WIDTH
    per_group = n_gates // NSA_GROUPS
    gates = w[:, n_qkv:rnn_lo].reshape(d, NSA_GROUPS, per_group)
    gates = jnp.pad(gates, ((0, 0), (0, 0), (0, GATE_STRIDE - per_group))).reshape(d, NSA_GROUPS * GATE_STRIDE)
    return jnp.concatenate([w[:, :n_qkv], w[:, qx_lo:mg_lo], gates, w[:, rnn_lo:qx_lo], w[:, mg_lo:]], axis=1)


def _block_diag_pairs(w):
    nb, k, _ = w.shape
    w = w.reshape(nb // 2, 2, k, k)
    zero = jnp.zeros((nb // 2, k, k), w.dtype)
    top = jnp.concatenate([w[:, 0], zero], axis=2)
    bot = jnp.concatenate([zero, w[:, 1]], axis=2)
    return jnp.concatenate([top, bot], axis=1).astype(BF16)


def _cmp_to_sel_t(n_cmp, n_cmp_pad, n_sel):
    c0 = np.arange(n_cmp_pad)[None, :] * CMP_STRIDE
    s0 = np.arange(n_sel)[:, None] * SEL_BLOCK
    ov = np.clip(np.minimum(c0 + CMP_LEN, s0 + SEL_BLOCK) - np.maximum(c0, s0), 0, None)
    ov = np.where(np.arange(n_cmp_pad)[None, :] < n_cmp, ov, 0)
    return (ov / CMP_LEN).astype(np.float32)


def kernel(x, mem, g_mix, w_in, cmp_pos_k, cmp_pos_v, w_cmp_k1, w_cmp_k2, w_cmp_v1, w_cmp_v2, conv_w, conv_b, w_rg_a, b_rg_a, w_rg_i, b_rg_i, rg_lambda, g_mem, w_mem_kv, w_xo, w_o, g_mlp, w_up, w_down, g_final):
    batch, seq, d = x.shape
    mem_len = mem.shape[1]
    t = batch * seq
    depth = g_mix.shape[0]
    n_sel = seq // SEL_BLOCK
    n_cmp_pad = seq // CMP_STRIDE
    assert n_cmp_pad == LANES and n_sel <= LANES and d == D_MODEL and depth == 1

    pos = jnp.arange(seq)
    cos_k, sin_k = _rope_tables(pos, NSA_KV)
    cos_q, sin_q = (a.T for a in _rope_angles(pos))
    no_rope = jnp.zeros((mem_len, LANES), F32)
    cmp_pos = jnp.arange(n_cmp_pad) * CMP_STRIDE + (CMP_LEN - 1)
    cos_c, sin_c = _rope_tables(cmp_pos, 2 * HEAD_DIM, HEAD_DIM)
    cos_c = jnp.tile(cos_c, (CMP_TM // n_cmp_pad, 2))
    sin_c = jnp.tile(sin_c, (CMP_TM // n_cmp_pad, 2))
    c2st = jnp.asarray(_cmp_to_sel_t((seq - CMP_LEN) // CMP_STRIDE + 1, n_cmp_pad, n_sel))

    h = x.reshape(t, d)
    for l in range(depth):
        z = _norm_proj(h, g_mix[l][None, :], _pack_w_in(w_in[l]), cos_k, sin_k, tm=INPROJ_TM, tn=INPROJ_TN,
                       rope_cols=ROPE_COLS, out_dtype=BF16, name="inproj")

        def pair_w1(w1):
            w3 = w1.astype(BF16).reshape(CMP_LEN, 1, HEAD_DIM, CMP_HIDDEN)
            zero = jnp.zeros_like(w3)
            both = jnp.concatenate([jnp.concatenate([w3, zero], axis=3), jnp.concatenate([zero, w3], axis=3)], axis=1)
            return both.reshape(CMP_LEN, 2 * HEAD_DIM, 2 * CMP_HIDDEN)

        def pair_pos(p):
            return jnp.concatenate([p, p], axis=1)

        w2 = jnp.zeros((2, 2, CMP_HIDDEN, 2, 2, HEAD_DIM), F32)
        for grp in range(2):
            w2 = w2.at[0, grp, :, grp, 0, :].set(w_cmp_k2[l]).at[1, grp, :, grp, 1, :].set(w_cmp_v2[l])
        w2 = w2.reshape(4 * CMP_HIDDEN, 4 * HEAD_DIM)
        kvc = _compress(z, pair_pos(cmp_pos_k[l]), pair_pos(cmp_pos_v[l]), pair_w1(w_cmp_k1[l]), pair_w1(w_cmp_v1[l]),
                        w2.astype(BF16), cos_c, sin_c, tm=CMP_TM)
        y_nsa = _nsa(z, cos_q, sin_q, kvc, c2st, batch=batch, seq=seq)

        y_rnn = _rglru(z, conv_w[l], conv_b[l][None, :], _block_diag_pairs(w_rg_a[l]), b_rg_a[l][None, :],
                       _block_diag_pairs(w_rg_i[l]), b_rg_i[l][None, :], rg_lambda[l][None, :],
                       batch=batch, seq=seq, ts=RGLRU_TS)

        kvm = _norm_proj(mem.reshape(batch * mem_len, d), g_mem[l][None, :], w_mem_kv[l].astype(BF16),
                         no_rope, no_rope, tm=mem_len, tn=2 * XATTN_WIDTH, rope_cols=(),
                         out_dtype=BF16, name="memkv")

        h1 = _merge(h, z, y_nsa, y_rnn, kvm, w_xo[l].astype(BF16), w_o[l].astype(BF16),
                    seq=seq, mem_len=mem_len, tq=MERGE_TQ)
        h = _mlp(h1, g_mlp[l][None, :], w_up[l].astype(BF16), w_down[l].astype(BF16), g_final[None, :],
                 tm=MLP_TM, tf=MLP_TF)
    return h.reshape(batch, seq, d)
```

```python
import functools
import math

import numpy as np
import jax
import jax.numpy as jnp
from jax import lax
from jax.experimental import pallas as pl
from jax.experimental.pallas import tpu as pltpu

F32 = jnp.float32
BF16 = jnp.bfloat16

D_MODEL = 1024
HEAD_DIM = 64
NSA_HEADS = 16
NSA_GROUPS = 4
NSA_HPG = NSA_HEADS // NSA_GROUPS
NSA_BRANCHES = 3
CMP_LEN = 32
CMP_STRIDE = 16
CMP_HIDDEN = 256
SEL_BLOCK = 64
SEL_TOPK = 8
WINDOW = 512
Q_BLOCK = 128
ROPE_THETA = 500000.0
ROPE_DIM = HEAD_DIM // 4
ROPE_HALF = ROPE_DIM // 2
CONV_WIDTH = 4
RGLRU_C = 8.0
XATTN_HEADS = 4
XATTN_WIDTH = XATTN_HEADS * HEAD_DIM
RMS_EPS = 1e-6
SCALE = HEAD_DIM ** -0.5
LOG2E = math.log2(math.e)
NEG = -1e30

LANES = 128
SUBLANES = 8
BF16_ROWS = 16
VMEM_LIMIT = 48 * 1024 * 1024

NSA_KV = NSA_GROUPS * HEAD_DIM
OFF_Q = 0
OFF_KC = 1024
OFF_VC = 1280
OFF_KS = 1536
OFF_VS = 1792
OFF_KW = 2048
OFF_VW = 2304
OFF_QX = 2560
OFF_GATE = 2816
GATE_STRIDE = 64
OFF_XRNN = 3072
OFF_GRNN = 4096
OFF_MERGE = 5120
INPROJ_TN = 2048
INPROJ_TM = 512
ROPE_COLS = (OFF_KS, OFF_KW)

NSA_PAIRS = 2
NSA_GPS = 2 * NSA_PAIRS
NSA_LANES = NSA_HPG * Q_BLOCK
SEL_CHUNK = 2 * Q_BLOCK
WIN_TILES = WINDOW // Q_BLOCK + 1
V_AUG = HEAD_DIM + BF16_ROWS

MERGE_TQ = 512
RGLRU_TS = 128
MLP_TM = 1024
MLP_TF = 1024
CMP_TM = 512


def _sigmoid(x):
    return 1.0 / (1.0 + jnp.exp(-x))


def _gelu_tanh(x):
    return 0.5 * x * (1.0 + jnp.tanh(0.7978845608028654 * (x + 0.044715 * (x * x * x))))


def _rope_swap(z):
    n = z.shape[-1]
    lane = lax.broadcasted_iota(jnp.int32, z.shape, z.ndim - 1)
    first_half = (lane & (HEAD_DIM - 1)) < ROPE_HALF
    return jnp.where(first_half, pltpu.roll(z, n - ROPE_HALF, z.ndim - 1), pltpu.roll(z, ROPE_HALF, z.ndim - 1))


def _round_robin(gens, lead=0):
    results = [None] * len(gens)
    alive = list(range(len(gens)))

    def advance(idx):
        try:
            next(gens[idx])
        except StopIteration as stop:
            results[idx] = stop.value
            alive.remove(idx)

    for _ in range(lead):
        for idx in [n for n in alive if n < len(gens) // 2]:
            advance(idx)
    while alive:
        for idx in list(alive):
            advance(idx)
    return results


def _fold_rows(x, op):
    parts = [x[r * SUBLANES:(r + 1) * SUBLANES] for r in range(x.shape[0] // SUBLANES)]
    return functools.reduce(op, parts)


def _norm_proj_kernel(x_ref, g_ref, w_ref, cos_ref, sin_ref, z_ref, *, tn, rope_cols):
    x = x_ref[...]
    ms = jnp.mean(x * x, axis=-1, keepdims=True)
    u = (x * lax.rsqrt(ms + RMS_EPS) * g_ref[...]).astype(BF16)
    rw = cos_ref.shape[1]
    for n0 in range(0, w_ref.shape[1], tn):
        z = jnp.dot(u, w_ref[:, n0:n0 + tn], preferred_element_type=F32)
        z_ref[:, n0:n0 + tn] = z.astype(z_ref.dtype)
        for col in rope_cols:
            if n0 <= col < n0 + tn:
                zr = z[:, col - n0:col - n0 + rw]
                z_ref[:, col:col + rw] = (zr * cos_ref[...] + _rope_swap(zr) * sin_ref[...]).astype(z_ref.dtype)


def _norm_proj(x, g, w, cos_t, sin_t, *, tm, tn, rope_cols, out_dtype, name):
    t, d = x.shape
    n = w.shape[1]
    s_tiles = cos_t.shape[0] // tm
    rw = cos_t.shape[1]
    table_spec = pl.BlockSpec((tm, rw), lambda i: (i % s_tiles, 0))
    return pl.pallas_call(
        functools.partial(_norm_proj_kernel, tn=tn, rope_cols=tuple(rope_cols)),
        grid=(t // tm,),
        in_specs=[
            pl.BlockSpec((tm, d), lambda i: (i, 0)),
            pl.BlockSpec((1, d), lambda i: (0, 0)),
            pl.BlockSpec((d, n), lambda i: (0, 0), pipeline_mode=pl.Buffered(1)),
            table_spec, table_spec,
        ],
        out_specs=pl.BlockSpec((tm, n), lambda i: (i, 0)),
        out_shape=jax.ShapeDtypeStruct((t, n), out_dtype),
        compiler_params=pltpu.CompilerParams(
            dimension_semantics=("parallel",), vmem_limit_bytes=VMEM_LIMIT),
        name=name,
    )(x, g, w, cos_t, sin_t)


def _compress_kernel(xk_ref, xv_ref, pk_ref, pv_ref, wk1_ref, wv1_ref, w2_ref, cos_ref, sin_ref, o_ref, xs_scr):
    tm = o_ref.shape[0]

    def hidden(x_ref, p_ref, w_ref):
        xs_scr[...] = x_ref[...].astype(F32)
        first = jnp.zeros((tm, w_ref.shape[2]), F32)
        second = jnp.zeros((tm, w_ref.shape[2]), F32)
        for j in range(CMP_STRIDE):
            xj = xs_scr[pl.ds(j, tm, stride=CMP_STRIDE), :]
            first = first + jnp.dot((xj + p_ref[j:j + 1, :]).astype(BF16), w_ref[j],
                                    preferred_element_type=F32)
            second = second + jnp.dot((xj + p_ref[CMP_STRIDE + j:CMP_STRIDE + j + 1, :]).astype(BF16),
                                      w_ref[CMP_STRIDE + j], preferred_element_type=F32)
        return _gelu_tanh(first + pltpu.roll(second, tm - 1, 0)).astype(BF16)

    hcat = jnp.concatenate([hidden(xk_ref, pk_ref, wk1_ref), hidden(xv_ref, pv_ref, wv1_ref)], axis=1)
    kv = jnp.dot(hcat, w2_ref[...], preferred_element_type=F32)
    o_ref[...] = kv * cos_ref[...] + _rope_swap(kv) * sin_ref[...]


def _compress(z, pk, pv, wk1, wv1, w2, cos_t, sin_t, *, tm):
    t = z.shape[0]
    m = t // CMP_STRIDE
    pairs = NSA_GROUPS // 2
    pair_w = 2 * HEAD_DIM
    hid2 = wk1.shape[2]
    out_w = w2.shape[1]
    full = lambda shape: pl.BlockSpec(shape, lambda p, i: (0,) * len(shape))
    tokens = lambda off: pl.BlockSpec((tm * CMP_STRIDE, pair_w), lambda p, i: (i, off // pair_w + p))
    return pl.pallas_call(
        _compress_kernel,
        grid=(pairs, m // tm),
        in_specs=[
            tokens(OFF_KC), tokens(OFF_VC),
            full((CMP_LEN, pair_w)), full((CMP_LEN, pair_w)),
            full((CMP_LEN, pair_w, hid2)), full((CMP_LEN, pair_w, hid2)),
            full((2 * hid2, out_w)),
            full((tm, out_w)), full((tm, out_w)),
        ],
        out_specs=pl.BlockSpec((tm, out_w), lambda p, i: (i, p)),
        out_shape=jax.ShapeDtypeStruct((m, pairs * out_w), F32),
        scratch_shapes=[pltpu.VMEM((tm * CMP_STRIDE, pair_w), F32)],
        compiler_params=pltpu.CompilerParams(
            dimension_semantics=("parallel", "parallel"), vmem_limit_bytes=VMEM_LIMIT),
        name="compress",
    )(z, z, pk, pv, wk1, wv1, w2, cos_t, sin_t)


def _nsa_kernel(q_ref, g_ref, cos_ref, sin_ref, *rest, n_sel):
    kv_refs, rest = rest[:4 * NSA_PAIRS], rest[4 * NSA_PAIRS:]
    kvc_ref, c2st_ref, o_ref, gt_scr, s_scr, vt_scr = rest
    ks_refs, vs_refs, kw_refs, vw_refs = (kv_refs[n::4] for n in range(4))
    g2 = pl.program_id(1)
    i = pl.program_id(2)
    q0 = i * Q_BLOCK
    seq = ks_refs[0].shape[0]

    @pl.when(i == 0)
    def _():
        ones_rows = jnp.where(lax.broadcasted_iota(jnp.int32, (V_AUG - HEAD_DIM, Q_BLOCK), 0) == 0,
                              1.0, 0.0).astype(BF16)

        def fill(t, carry):
            r0 = pl.multiple_of(t * Q_BLOCK, Q_BLOCK)
            for kind, refs in enumerate((vs_refs, vw_refs)):
                for pair, ref in enumerate(refs):
                    vt = jnp.transpose(ref[pl.ds(r0, Q_BLOCK), :].astype(F32)).astype(BF16)
                    for half in range(2):
                        gg = 2 * pair + half
                        vt_scr[kind, gg, t, 0:HEAD_DIM, :] = vt[half * HEAD_DIM:(half + 1) * HEAD_DIM]
                        vt_scr[kind, gg, t, HEAD_DIM:V_AUG, :] = ones_rows
            return carry

        lax.fori_loop(0, seq // Q_BLOCK, fill, 0)

    def q_operand(gg):
        zeros_q = jnp.zeros((HEAD_DIM, NSA_LANES), BF16)
        return [qts[gg], zeros_q] if gg % 2 == 0 else [zeros_q, qts[gg]]

    blocks_per_chunk = SEL_CHUNK // SEL_BLOCK
    tiles_per_chunk = SEL_CHUNK // Q_BLOCK
    q_per_chunk = SEL_CHUNK // Q_BLOCK

    def tile_heads(x):
        return jnp.concatenate([x] * NSA_HPG, axis=1)

    tq = q0 + (lax.broadcasted_iota(jnp.int32, (1, NSA_LANES), 1) & (Q_BLOCK - 1))
    c_idx = lax.broadcasted_iota(jnp.int32, (Q_BLOCK, 1), 0)
    r_idx = lax.broadcasted_iota(jnp.int32, (1, Q_BLOCK), 1)
    tri_diag = tile_heads(jnp.where(c_idx <= r_idx, 0.0, NEG))
    tri_old = tile_heads(jnp.where(c_idx > r_idx, 0.0, NEG))
    cos_q = tile_heads(cos_ref[...])
    sin_q = tile_heads(sin_ref[...])
    blk = lax.broadcasted_iota(jnp.int32, (n_sel, Q_BLOCK), 0)
    cur = (q0 + lax.broadcasted_iota(jnp.int32, (n_sel, Q_BLOCK), 1)) // SEL_BLOCK
    sub = lax.broadcasted_iota(jnp.int32, (SUBLANES, Q_BLOCK), 0)

    gt_scr[...] = jnp.transpose(_sigmoid(g_ref[...].astype(F32)))

    def prepare(gg):
        qx = jnp.transpose(q_ref[:, gg * NSA_HPG * HEAD_DIM:(gg + 1) * NSA_HPG * HEAD_DIM].astype(F32)
                           * (SCALE * LOG2E))
        qf = jnp.concatenate([qx[h * HEAD_DIM:(h + 1) * HEAD_DIM, :] for h in range(NSA_HPG)], axis=1)
        x1, x2 = qf[0:ROPE_HALF], qf[ROPE_HALF:ROPE_DIM]
        qt = jnp.concatenate([x1 * cos_q - x2 * sin_q, x2 * cos_q + x1 * sin_q, qf[ROPE_DIM:]],
                             axis=0).astype(BF16)
        qts[gg] = qt
        yield

        kvc = kvc_ref[:, gg * LANES:(gg + 1) * LANES]
        kc = kvc[:, :HEAD_DIM].astype(BF16)
        vct = jnp.transpose(kvc)[HEAD_DIM:, :].astype(BF16)
        sc = jnp.dot(kc, qt, preferred_element_type=F32)
        yield
        n_idx = lax.broadcasted_iota(jnp.int32, (LANES, 1), 0)
        sc = jnp.where((n_idx * CMP_STRIDE + (CMP_LEN - 1)) <= tq, sc, NEG)
        m_c = jnp.max(sc, axis=0, keepdims=True)
        e_c = jnp.exp2(sc - m_c)
        inv_c = 1.0 / jnp.maximum(jnp.sum(e_c, axis=0, keepdims=True), 1e-30)
        p_c = e_c * jnp.where(m_c > 0.5 * NEG, inv_c, 0.0)
        o_c = jnp.dot(vct, p_c.astype(BF16), preferred_element_type=F32)
        yield

        p_sum = p_c[:, 0:Q_BLOCK]
        for h in range(1, NSA_HPG):
            p_sum = p_sum + p_c[:, h * Q_BLOCK:(h + 1) * Q_BLOCK]
        imp = jnp.dot(c2st_ref[...], p_sum, preferred_element_type=F32,
                      precision=lax.Precision.HIGHEST)
        yield
        forced = (blk == 0) | (blk == cur) | (blk == cur - 1)
        val = jnp.where(forced, jnp.inf, jnp.where(blk > cur, -jnp.inf, imp))
        n_grp = n_sel // SUBLANES
        grp = [val[v * SUBLANES:(v + 1) * SUBLANES] for v in range(n_grp)]
        cnt = [jnp.zeros((SUBLANES, Q_BLOCK), F32) for _ in range(n_grp)]
        for sp in range(n_sel):
            r = val[sp:sp + 1, :]
            for v in range(n_grp):
                if v * SUBLANES > sp:
                    cnt[v] = cnt[v] + jnp.where(r >= grp[v], 1.0, 0.0)
                elif v * SUBLANES + SUBLANES - 1 <= sp:
                    cnt[v] = cnt[v] + jnp.where(r > grp[v], 1.0, 0.0)
                else:
                    later = jnp.where(sub > sp - v * SUBLANES, 1.0, 0.0)
                    cnt[v] = cnt[v] + jnp.where(r > grp[v], 1.0, jnp.where(r == grp[v], later, 0.0))
            if sp % SUBLANES == SUBLANES - 1:
                yield
        chosen = jnp.concatenate([jnp.where(c < float(SEL_TOPK), 0.0, NEG) for c in cnt], axis=0)
        block_bias = tile_heads(jnp.where(blk > cur, NEG, chosen))
        row0 = pl.multiple_of((g2 * NSA_GPS + gg) * GATE_STRIDE, GATE_STRIDE)
        gt = gt_scr[pl.ds(row0, 2 * SUBLANES), :]
        return qt, o_c, block_bias, gt

    def window(gg):
        tiles = []
        for d in range(1 - WIN_TILES, 1):
            td = i + d
            tc = jnp.maximum(td, 0)
            k0 = pl.multiple_of(tc * Q_BLOCK, Q_BLOCK)
            before_start = jnp.where(td >= 0, 0.0, NEG) if d < 0 else 0.0
            tri = tri_diag if d == 0 else (tri_old if d == 1 - WIN_TILES else None)
            tiles.append((tc, k0, before_start, tri))

        ones_rows = jnp.where(lax.broadcasted_iota(jnp.int32, (BF16_ROWS, NSA_LANES), 0) == 0, 1.0, 0.0).astype(BF16)
        q_aug = jnp.concatenate(q_operand(gg) + [ones_rows, jnp.zeros((LANES - BF16_ROWS, NSA_LANES), BF16)],
                                axis=0)
        first_lane = lax.broadcasted_iota(jnp.int32, (Q_BLOCK, LANES), 1) == 0
        scores = []
        for _, k0, before_start, tri in tiles:
            k_aug = jnp.concatenate([kw_refs[gg // 2][pl.ds(k0, Q_BLOCK), :],
                                     jnp.where(first_lane, before_start, 0.0).astype(BF16)], axis=1)
            s = jnp.dot(k_aug, q_aug, preferred_element_type=F32)
            scores.append(s if tri is None else s + tri)
            yield
        m_w = jnp.max(functools.reduce(jnp.maximum, [_fold_rows(s, jnp.maximum) for s in scores]),
                      axis=0, keepdims=True)
        n_pad = jnp.maximum(WINDOW - 1 - tq, 0).astype(F32)
        m_w = jnp.where(n_pad > 0.0, jnp.maximum(m_w, 0.0), m_w)
        yield
        acc = jnp.zeros((V_AUG, NSA_LANES), F32)
        for (tc, _, _, _), s in zip(tiles, scores):
            acc = acc + jnp.dot(vt_scr[1, gg, tc], jnp.exp2(s - m_w).astype(BF16), preferred_element_type=F32)
            yield
        l_w = acc[HEAD_DIM:HEAD_DIM + 1] + n_pad * jnp.exp2(-m_w)
        return acc[:HEAD_DIM] * (1.0 / jnp.maximum(l_w, 1e-30))

    def chunk_scores(gg, c, diag_rel):
        rows = slice(c * SEL_CHUNK, (c + 1) * SEL_CHUNK)
        s = jnp.dot(ks_refs[gg // 2][rows, :], jnp.concatenate(q_operand(gg), axis=0),
                    preferred_element_type=F32)
        if diag_rel is not None:
            s = s + jnp.concatenate([jnp.where(diag_rel == u, tri_diag, 0.0) for u in range(tiles_per_chunk)], axis=0)
        s_scr[gg, rows, :] = s
        bmax = [_fold_rows(s[b * SEL_BLOCK:(b + 1) * SEL_BLOCK], jnp.maximum) for b in range(blocks_per_chunk)]
        yield
        return bmax

    def first_chunk(gg):
        yield
        return (yield from chunk_scores(gg, 0, i))

    def selected(gg, k, block_bias, bmax0):
        bmax = list(bmax0)
        for c in range(1, k + 1):
            bmax += yield from chunk_scores(gg, c, i - k * q_per_chunk if c == k else None)
        m8 = functools.reduce(jnp.maximum, [bm + block_bias[n:n + 1, :] for n, bm in enumerate(bmax)])
        m_s = jnp.max(m8, axis=0, keepdims=True)
        shift = block_bias - m_s
        yield
        acc = jnp.zeros((V_AUG, NSA_LANES), F32)
        for c in range(k + 1):
            ps = []
            for b in range(blocks_per_chunk):
                n = c * blocks_per_chunk + b
                ps.append(jnp.exp2(s_scr[gg, n * SEL_BLOCK:(n + 1) * SEL_BLOCK, :] + shift[n:n + 1, :]).astype(BF16))
            vt = jnp.concatenate([vt_scr[0, gg, c * tiles_per_chunk + u] for u in range(tiles_per_chunk)], axis=1)
            acc = acc + jnp.dot(vt, jnp.concatenate(ps, axis=0), preferred_element_type=F32)
            yield
        l_s = acc[HEAD_DIM:HEAD_DIM + 1]
        return acc[:HEAD_DIM] * (1.0 / jnp.maximum(l_s, 1e-30))

    qts = [None] * NSA_GPS
    groups = range(NSA_GPS)
    shared = _round_robin([prepare(gg) for gg in groups] + [window(gg) for gg in groups]
                          + [first_chunk(gg) for gg in groups])
    prepared, windows, bmax0 = (shared[n * NSA_GPS:(n + 1) * NSA_GPS] for n in range(3))

    def finish(k):
        branches = []
        for gg, (qt, o_c, block_bias, gt) in enumerate(prepared):
            branches.append(selected(gg, k, block_bias, bmax0[gg]))
        results = _round_robin(branches, lead=k + 1)
        outs = []
        for gg, (qt, o_c, block_bias, gt) in enumerate(prepared):
            o_w, o_s = windows[gg], results[gg]
            for h in range(NSA_HPG):
                sl = slice(h * Q_BLOCK, (h + 1) * Q_BLOCK)
                row = NSA_BRANCHES * h
                outs.append(gt[row:row + 1, :] * o_c[:, sl] + gt[row + 1:row + 2, :] * o_s[:, sl]
                            + gt[row + 2:row + 3, :] * o_w[:, sl])
        o_ref[...] = jnp.transpose(jnp.concatenate(outs, axis=0))

    for k in range(n_sel * SEL_BLOCK // SEL_CHUNK):
        pl.when(i // q_per_chunk == k)(functools.partial(finish, k))


def _nsa(z, cos_q, sin_q, kvc, c2st, *, batch, seq):
    nq = seq // Q_BLOCK
    n_sel = seq // SEL_BLOCK
    gw = NSA_GPS * NSA_HPG * HEAD_DIM
    gates_w = NSA_GROUPS * GATE_STRIDE
    pair_w = 2 * HEAD_DIM
    assert pair_w == LANES
    kv_spec = lambda off, pair: pl.BlockSpec(
        (seq, pair_w), lambda b, g, i: (b, off // pair_w + g * NSA_PAIRS + pair))
    kv_specs = [kv_spec(off, pair) for pair in range(NSA_PAIRS) for off in (OFF_KS, OFF_VS, OFF_KW, OFF_VW)]
    rope_spec = pl.BlockSpec((ROPE_HALF, Q_BLOCK), lambda b, g, i: (0, i))
    return pl.pallas_call(
        functools.partial(_nsa_kernel, n_sel=n_sel),
        grid=(batch, NSA_GROUPS // NSA_GPS, nq),
        in_specs=[
            pl.BlockSpec((Q_BLOCK, gw), lambda b, g, i: (b * nq + i, OFF_Q // gw + g)),
            pl.BlockSpec((Q_BLOCK, gates_w), lambda b, g, i: (b * nq + i, OFF_GATE // gates_w)),
            rope_spec, rope_spec,
            *kv_specs,
            pl.BlockSpec((LANES, NSA_GPS * 2 * HEAD_DIM), lambda b, g, i: (b, g)),
            pl.BlockSpec((n_sel, LANES), lambda b, g, i: (0, 0)),
        ],
        out_specs=pl.BlockSpec((Q_BLOCK, gw), lambda b, g, i: (b * nq + i, g)),
        out_shape=jax.ShapeDtypeStruct((batch * seq, NSA_HEADS * HEAD_DIM), F32),
        scratch_shapes=[pltpu.VMEM((gates_w, Q_BLOCK), F32), pltpu.VMEM((NSA_GPS, seq, NSA_LANES), F32),
                        pltpu.VMEM((2, NSA_GPS, nq, V_AUG, Q_BLOCK), BF16)],
        compiler_params=pltpu.CompilerParams(
            dimension_semantics=("arbitrary", "arbitrary", "arbitrary"), vmem_limit_bytes=VMEM_LIMIT),
        name="nsa",
    )(z, z, cos_q, sin_q, *([z] * len(kv_specs)), kvc, c2st)


def _rglru_kernel(x_ref, gr_ref, cw_ref, cb_ref, wa_ref, ba_ref, wi_ref, bi_ref, lam_ref, y_ref,
                  xs_scr, a_scr, b_scr, h_scr):
    nb, ts, c = x_ref.shape
    pitch = a_scr.shape[1] // nb
    lane_tiles = c // LANES

    def put(scr, rows, val):
        for m in range(lane_tiles):
            scr[m, rows, :] = val[:, m * LANES:(m + 1) * LANES]

    def get(scr, rows):
        return jnp.concatenate([scr[m, rows, :] for m in range(lane_tiles)], axis=1)

    @pl.when(pl.program_id(0) == 0)
    def _():
        xs_scr[:, 0:SUBLANES, :] = jnp.zeros((nb, SUBLANES, c), F32)
        h_scr[...] = jnp.zeros(h_scr.shape, F32)

    nl = -lam_ref[...]
    softplus = jnp.maximum(nl, 0.0) + jnp.log1p(jnp.exp(-jnp.abs(nl)))

    def gates(bi_, carry):
        x = x_ref[bi_].astype(F32)
        xs_scr[bi_, SUBLANES:SUBLANES + ts, :] = x
        xc = cb_ref[...] + cw_ref[CONV_WIDTH - 1:CONV_WIDTH, :] * x
        for k in range(CONV_WIDTH - 1):
            off = SUBLANES - (CONV_WIDTH - 1) + k
            xc = xc + cw_ref[k:k + 1, :] * xs_scr[bi_, off:off + ts, :]
        xs_scr[bi_, 0:SUBLANES, :] = x[ts - SUBLANES:, :]

        ra, ri = [], []
        for mblk in range(c // LANES):
            xb = xc[:, mblk * LANES:(mblk + 1) * LANES].astype(BF16)
            ra.append(jnp.dot(xb, wa_ref[mblk], preferred_element_type=F32))
            ri.append(jnp.dot(xb, wi_ref[mblk], preferred_element_type=F32))
        r = _sigmoid(jnp.concatenate(ra, axis=1) + ba_ref[...])
        gi = _sigmoid(jnp.concatenate(ri, axis=1) + bi_ref[...])

        log_a = (-RGLRU_C) * r * softplus
        th = jnp.tanh(log_a)
        y2 = -2.0 * th / (1.0 - th)
        rows = pl.ds(pl.multiple_of(bi_ * pitch, SUBLANES), ts)
        put(a_scr, rows, jnp.exp(log_a))
        put(b_scr, rows, jnp.where(y2 > 0.0, y2 * lax.rsqrt(y2), 0.0) * (gi * xc))
        return carry

    lax.fori_loop(0, nb, gates, 0)

    def step(t, h):
        rows = pl.ds(t, nb, stride=pitch)
        h = get(a_scr, rows) * h + get(b_scr, rows)
        put(b_scr, rows, h)
        return h

    h_scr[...] = lax.fori_loop(0, ts, step, h_scr[...], unroll=8)

    def outputs(bi_, carry):
        rows = pl.ds(pl.multiple_of(bi_ * pitch, SUBLANES), ts)
        y_ref[bi_] = _gelu_tanh(gr_ref[bi_].astype(F32)) * get(b_scr, rows)
        return carry

    lax.fori_loop(0, nb, outputs, 0)


def _rglru(z, cw, cb, wa2, ba, wi2, bi, lam, *, batch, seq, ts):
    c = D_MODEL
    pitch = ts + SUBLANES
    z3 = z.reshape(batch, seq, z.shape[1])
    vec = pl.BlockSpec((1, c), lambda t: (0, 0))
    y = pl.pallas_call(
        _rglru_kernel,
        grid=(seq // ts,),
        in_specs=[
            pl.BlockSpec((batch, ts, c), lambda t: (0, t, OFF_XRNN // c)),
            pl.BlockSpec((batch, ts, c), lambda t: (0, t, OFF_GRNN // c)),
            pl.BlockSpec((CONV_WIDTH, c), lambda t: (0, 0)),
            vec,
            pl.BlockSpec((c // LANES, LANES, LANES), lambda t: (0, 0, 0)),
            vec,
            pl.BlockSpec((c // LANES, LANES, LANES), lambda t: (0, 0, 0)),
            vec, vec,
        ],
        out_specs=pl.BlockSpec((batch, ts, c), lambda t: (0, t, 0)),
        out_shape=jax.ShapeDtypeStruct((batch, seq, c), F32),
        scratch_shapes=[pltpu.VMEM((batch, SUBLANES + ts, c), F32), pltpu.VMEM((c // LANES, batch * pitch, LANES), F32),
                        pltpu.VMEM((c // LANES, batch * pitch, LANES), F32), pltpu.VMEM((batch, c), F32)],
        compiler_params=pltpu.CompilerParams(
            dimension_semantics=("arbitrary",), vmem_limit_bytes=VMEM_LIMIT),
        name="rglru",
    )(z3, z3, cw, cb, wa2, ba, wi2, bi, lam)
    return y.reshape(batch * seq, c)


def _merge_kernel(x_ref, m0_ref, m1_ref, m2_ref, ynsa_ref, yrnn_ref, qx_ref, km_ref, vm_ref, wxo_ref, wo_ref, h_ref):
    tq = x_ref.shape[0]
    mem_len = km_ref.shape[0]
    row_chunk = tq // 4

    def cross_attention():
        qt = jnp.transpose(qx_ref[...].astype(F32) * (SCALE * LOG2E))
        row_head = lax.broadcasted_iota(jnp.int32, qt.shape, 0) // HEAD_DIM
        q_bd = jnp.concatenate([jnp.where(row_head == h, qt, 0.0) for h in range(XATTN_HEADS)],
                               axis=1).astype(BF16)
        yield
        s = jnp.dot(km_ref[...], q_bd, preferred_element_type=F32)
        yield
        m = jnp.max(_fold_rows(s, jnp.maximum), axis=0, keepdims=True)
        p = jnp.exp2(s - m).astype(BF16)
        yield
        ones_rows = jnp.where(lax.broadcasted_iota(jnp.int32, (BF16_ROWS, mem_len), 0) == 0, 1.0, 0.0)
        vt = jnp.concatenate([jnp.transpose(vm_ref[...].astype(F32)), ones_rows], axis=0).astype(BF16)
        o_all = jnp.dot(vt, p, preferred_element_type=F32)
        yield
        outs = []
        for h in range(XATTN_HEADS):
            lanes = slice(h * tq, (h + 1) * tq)
            inv_l = 1.0 / o_all[XATTN_WIDTH:XATTN_WIDTH + 1, lanes]
            outs.append(o_all[h * HEAD_DIM:(h + 1) * HEAD_DIM, lanes] * inv_l)
        o = jnp.transpose(jnp.concatenate(outs, axis=0)).astype(BF16)
        yield
        return jnp.dot(o, wxo_ref[...], preferred_element_type=F32)

    def gated(gate_ref, val_ref):
        parts = []
        for r in range(0, tq, row_chunk):
            half = 0.5 * val_ref[r:r + row_chunk, :]
            parts.append(half * jnp.tanh(0.5 * gate_ref[r:r + row_chunk, :].astype(F32)) + half)
            yield
        return jnp.concatenate(parts, axis=0)

    def gate_tanh(gate_ref):
        parts = []
        for r in range(0, tq, row_chunk):
            parts.append(jnp.tanh(0.5 * gate_ref[r:r + row_chunk, :].astype(F32)))
            yield
        return jnp.concatenate(parts, axis=0)

    yx, y_a, y_b, t_x = _round_robin([cross_attention(), gated(m0_ref, ynsa_ref), gated(m1_ref, yrnn_ref),
                                      gate_tanh(m2_ref)])
    half_yx = 0.5 * yx
    y = y_a + y_b + (half_yx * t_x + half_yx)
    h_ref[...] = x_ref[...] + jnp.dot(y.astype(BF16), wo_ref[...], preferred_element_type=F32)


def _merge(x, z, ynsa, yrnn, kvm, wxo, wo, *, seq, mem_len, tq):
    t, d = x.shape
    nt = seq // tq
    row = lambda cb: pl.BlockSpec((tq, d), lambda i, cb=cb: (i, cb))
    mem_spec = lambda cb: pl.BlockSpec((mem_len, XATTN_WIDTH), lambda i, cb=cb: (i // nt, cb))
    return pl.pallas_call(
        _merge_kernel,
        grid=(t // tq,),
        in_specs=[
            row(0),
            row(OFF_MERGE // d), row(OFF_MERGE // d + 1), row(OFF_MERGE // d + 2),
            row(0), row(0),
            pl.BlockSpec((tq, XATTN_WIDTH), lambda i: (i, OFF_QX // XATTN_WIDTH)),
            mem_spec(0), mem_spec(1),
            pl.BlockSpec((XATTN_WIDTH, d), lambda i: (0, 0)),
            pl.BlockSpec((d, d), lambda i: (0, 0)),
        ],
        out_specs=row(0),
        out_shape=jax.ShapeDtypeStruct((t, d), F32),
        compiler_params=pltpu.CompilerParams(
            dimension_semantics=("parallel",), vmem_limit_bytes=VMEM_LIMIT),
        name="merge",
    )(x, z, z, z, ynsa, yrnn, z, kvm, kvm, wxo, wo)


def _mlp_kernel(h_ref, g_ref, wup_ref, wdn_ref, gf_ref, o_ref, *, tf):
    h = h_ref[...]
    ms = jnp.mean(h * h, axis=-1, keepdims=True)
    v = (h * lax.rsqrt(ms + RMS_EPS) * g_ref[...]).astype(BF16)
    h2 = h
    for f in range(0, wup_ref.shape[1], tf):
        up = jnp.dot(v, wup_ref[:, f:f + tf], preferred_element_type=F32)
        act = jnp.square(jnp.maximum(up, 0.0)).astype(BF16)
        h2 = h2 + jnp.dot(act, wdn_ref[f:f + tf, :], preferred_element_type=F32)
    ms = jnp.mean(h2 * h2, axis=-1, keepdims=True)
    o_ref[...] = h2 * lax.rsqrt(ms + RMS_EPS) * gf_ref[...]


def _mlp(h, g, wup, wdn, gf, *, tm, tf):
    t, d = h.shape
    dff = wup.shape[1]
    resident = lambda shape: pl.BlockSpec(shape, lambda i: (0, 0), pipeline_mode=pl.Buffered(1))
    return pl.pallas_call(
        functools.partial(_mlp_kernel, tf=tf),
        grid=(t // tm,),
        in_specs=[
            pl.BlockSpec((tm, d), lambda i: (i, 0)),
            pl.BlockSpec((1, d), lambda i: (0, 0)),
            resident((d, dff)), resident((dff, d)),
            pl.BlockSpec((1, d), lambda i: (0, 0)),
        ],
        out_specs=pl.BlockSpec((tm, d), lambda i: (i, 0)),
        out_shape=jax.ShapeDtypeStruct((t, d), F32),
        compiler_params=pltpu.CompilerParams(
            dimension_semantics=("parallel",), vmem_limit_bytes=VMEM_LIMIT),
        name="mlp",
    )(h, g, wup, wdn, gf)


def _rope_angles(pos):
    inv = 1.0 / (ROPE_THETA ** (jnp.arange(0, ROPE_DIM, 2, dtype=F32) / ROPE_DIM))
    ang = pos.astype(F32)[:, None] * inv[None, :]
    return jnp.cos(ang), jnp.sin(ang)


def _rope_tables(pos, width, rope_width=None):
    rope_width = width if rope_width is None else rope_width
    cos, sin = _rope_angles(pos)
    n = pos.shape[0]
    pad = HEAD_DIM - ROPE_DIM
    cos_h = jnp.concatenate([cos, cos, jnp.ones((n, pad), F32)], axis=1)
    sin_h = jnp.concatenate([-sin, sin, jnp.zeros((n, pad), F32)], axis=1)
    reps = rope_width // HEAD_DIM
    rest = width - rope_width
    cos_t = jnp.concatenate([jnp.tile(cos_h, (1, reps)), jnp.ones((n, rest), F32)], axis=1)
    sin_t = jnp.concatenate([jnp.tile(sin_h, (1, reps)), jnp.zeros((n, rest), F32)], axis=1)
    return cos_t, sin_t


def _pack_w_in(w_in):
    d = w_in.shape[0]
    w = w_in.astype(BF16)
    n_qkv = NSA_HEADS * HEAD_DIM + 6 * NSA_KV
    n_gates = NSA_HEADS * NSA_BRANCHES
    rnn_lo = n_qkv + n_gates
    qx_lo = rnn_lo + 2 * D_MODEL
    mg_lo = qx_lo + XATTN_WIDTH
    per_group = n_gates // NSA_GROUPS
    gates = w[:, n_qkv:rnn_lo].reshape(d, NSA_GROUPS, per_group)
    gates = jnp.pad(gates, ((0, 0), (0, 0), (0, GATE_STRIDE - per_group))).reshape(d, NSA_GROUPS * GATE_STRIDE)
    return jnp.concatenate([w[:, :n_qkv], w[:, qx_lo:mg_lo], gates, w[:, rnn_lo:qx_lo], w[:, mg_lo:]], axis=1)


def _block_diag_pairs(w):
    nb, k, _ = w.shape
    w = w.reshape(nb // 2, 2, k, k)
    zero = jnp.zeros((nb // 2, k, k), w.dtype)
    top = jnp.concatenate([w[:, 0], zero], axis=2)
    bot = jnp.concatenate([zero, w[:, 1]], axis=2)
    return jnp.concatenate([top, bot], axis=1).astype(BF16)


def _cmp_to_sel_t(n_cmp, n_cmp_pad, n_sel):
    c0 = np.arange(n_cmp_pad)[None, :] * CMP_STRIDE
    s0 = np.arange(n_sel)[:, None] * SEL_BLOCK
    ov = np.clip(np.minimum(c0 + CMP_LEN, s0 + SEL_BLOCK) - np.maximum(c0, s0), 0, None)
    ov = np.where(np.arange(n_cmp_pad)[None, :] < n_cmp, ov, 0)
    return (ov / CMP_LEN).astype(np.float32)


def kernel(x, mem, g_mix, w_in, cmp_pos_k, cmp_pos_v, w_cmp_k1, w_cmp_k2, w_cmp_v1, w_cmp_v2, conv_w, conv_b, w_rg_a, b_rg_a, w_rg_i, b_rg_i, rg_lambda, g_mem, w_mem_kv, w_xo, w_o, g_mlp, w_up, w_down, g_final):
    batch, seq, d = x.shape
    mem_len = mem.shape[1]
    t = batch * seq
    depth = g_mix.shape[0]
    n_sel = seq // SEL_BLOCK
    n_cmp_pad = seq // CMP_STRIDE
    assert n_cmp_pad == LANES and n_sel <= LANES and d == D_MODEL and depth == 1

    pos = jnp.arange(seq)
    cos_k, sin_k = _rope_tables(pos, NSA_KV)
    cos_q, sin_q = (a.T for a in _rope_angles(pos))
    no_rope = jnp.zeros((mem_len, LANES), F32)
    cmp_pos = jnp.arange(n_cmp_pad) * CMP_STRIDE + (CMP_LEN - 1)
    cos_c, sin_c = _rope_tables(cmp_pos, 2 * HEAD_DIM, HEAD_DIM)
    cos_c = jnp.tile(cos_c, (CMP_TM // n_cmp_pad, 2))
    sin_c = jnp.tile(sin_c, (CMP_TM // n_cmp_pad, 2))
    c2st = jnp.asarray(_cmp_to_sel_t((seq - CMP_LEN) // CMP_STRIDE + 1, n_cmp_pad, n_sel))

    h = x.reshape(t, d)
    for l in range(depth):
        z = _norm_proj(h, g_mix[l][None, :], _pack_w_in(w_in[l]), cos_k, sin_k, tm=INPROJ_TM, tn=INPROJ_TN,
                       rope_cols=ROPE_COLS, out_dtype=BF16, name="inproj")

        def pair_w1(w1):
            w3 = w1.astype(BF16).reshape(CMP_LEN, 1, HEAD_DIM, CMP_HIDDEN)
            zero = jnp.zeros_like(w3)
            both = jnp.concatenate([jnp.concatenate([w3, zero], axis=3), jnp.concatenate([zero, w3], axis=3)], axis=1)
            return both.reshape(CMP_LEN, 2 * HEAD_DIM, 2 * CMP_HIDDEN)

        def pair_pos(p):
            return jnp.concatenate([p, p], axis=1)

        w2 = jnp.zeros((2, 2, CMP_HIDDEN, 2, 2, HEAD_DIM), F32)
        for grp in range(2):
            w2 = w2.at[0, grp, :, grp, 0, :].set(w_cmp_k2[l]).at[1, grp, :, grp, 1, :].set(w_cmp_v2[l])
        w2 = w2.reshape(4 * CMP_HIDDEN, 4 * HEAD_DIM)
        kvc = _compress(z, pair_pos(cmp_pos_k[l]), pair_pos(cmp_pos_v[l]), pair_w1(w_cmp_k1[l]), pair_w1(w_cmp_v1[l]),
                        w2.astype(BF16), cos_c, sin_c, tm=CMP_TM)
        y_nsa = _nsa(z, cos_q, sin_q, kvc, c2st, batch=batch, seq=seq)

        y_rnn = _rglru(z, conv_w[l], conv_b[l][None, :], _block_diag_pairs(w_rg_a[l]), b_rg_a[l][None, :],
                       _block_diag_pairs(w_rg_i[l]), b_rg_i[l][None, :], rg_lambda[l][None, :],
                       batch=batch, seq=seq, ts=RGLRU_TS)

        kvm = _norm_proj(mem.reshape(batch * mem_len, d), g_mem[l][None, :], w_mem_kv[l].astype(BF16),
                         no_rope, no_rope, tm=mem_len, tn=2 * XATTN_WIDTH, rope_cols=(),
                         out_dtype=BF16, name="memkv")

        h1 = _merge(h, z, y_nsa, y_rnn, kvm, w_xo[l].astype(BF16), w_o[l].astype(BF16),
                    seq=seq, mem_len=mem_len, tq=MERGE_TQ)
        h = _mlp(h1, g_mlp[l][None, :], w_up[l].astype(BF16), w_down[l].astype(BF16), g_final[None, :],
                 tm=MLP_TM, tf=MLP_TF)
    return h.reshape(batch, seq, d)
```

```python
import functools
import math

import numpy as np
import jax
import jax.numpy as jnp
from jax import lax
from jax.experimental import pallas as pl
from jax.experimental.pallas import tpu as pltpu

F32 = jnp.float32
BF16 = jnp.bfloat16

D_MODEL = 1024
HEAD_DIM = 64
NSA_HEADS = 16
NSA_GROUPS = 4
NSA_HPG = NSA_HEADS // NSA_GROUPS
NSA_BRANCHES = 3
CMP_LEN = 32
CMP_STRIDE = 16
CMP_HIDDEN = 256
SEL_BLOCK = 64
SEL_TOPK = 8
WINDOW = 512
Q_BLOCK = 128
ROPE_THETA = 500000.0
ROPE_DIM = HEAD_DIM // 4
ROPE_HALF = ROPE_DIM // 2
CONV_WIDTH = 4
RGLRU_C = 8.0
XATTN_HEADS = 4
XATTN_WIDTH = XATTN_HEADS * HEAD_DIM
RMS_EPS = 1e-6
SCALE = HEAD_DIM ** -0.5
LOG2E = math.log2(math.e)
NEG = -1e30

LANES = 128
SUBLANES = 8
BF16_ROWS = 16
VMEM_LIMIT = 48 * 1024 * 1024

NSA_KV = NSA_GROUPS * HEAD_DIM
OFF_Q = 0
OFF_KC = 1024
OFF_VC = 1280
OFF_KS = 1536
OFF_VS = 1792
OFF_KW = 2048
OFF_VW = 2304
OFF_QX = 2560
OFF_GATE = 2816
GATE_STRIDE = 64
OFF_XRNN = 3072
OFF_GRNN = 4096
OFF_MERGE = 5120
INPROJ_TN = 2048
INPROJ_TM = 512
ROPE_COLS = (OFF_KS, OFF_KW)

NSA_PAIRS = 2
NSA_GPS = 2 * NSA_PAIRS
NSA_LANES = NSA_HPG * Q_BLOCK
SEL_CHUNK = 2 * Q_BLOCK
WIN_TILES = WINDOW // Q_BLOCK + 1
V_AUG = HEAD_DIM + BF16_ROWS

MERGE_TQ = 512
RGLRU_TS = 128
MLP_TM = 1024
MLP_TF = 1024
CMP_TM = 512


def _sigmoid(x):
    return 1.0 / (1.0 + jnp.exp(-x))


def _gelu_tanh(x):
    return 0.5 * x * (1.0 + jnp.tanh(0.7978845608028654 * (x + 0.044715 * (x * x * x))))


def _rope_swap(z):
    n = z.shape[-1]
    lane = lax.broadcasted_iota(jnp.int32, z.shape, z.ndim - 1)
    first_half = (lane & (HEAD_DIM - 1)) < ROPE_HALF
    return jnp.where(first_half, pltpu.roll(z, n - ROPE_HALF, z.ndim - 1), pltpu.roll(z, ROPE_HALF, z.ndim - 1))


def _round_robin(gens, lead=0):
    results = [None] * len(gens)
    alive = list(range(len(gens)))

    def advance(idx):
        try:
            next(gens[idx])
        except StopIteration as stop:
            results[idx] = stop.value
            alive.remove(idx)

    for _ in range(lead):
        for idx in [n for n in alive if n < len(gens) // 2]:
            advance(idx)
    while alive:
        for idx in list(alive):
            advance(idx)
    return results


def _fold_rows(x, op):
    parts = [x[r * SUBLANES:(r + 1) * SUBLANES] for r in range(x.shape[0] // SUBLANES)]
    return functools.reduce(op, parts)


def _norm_proj_kernel(x_ref, g_ref, w_ref, cos_ref, sin_ref, z_ref, *, tn, rope_cols):
    x = x_ref[...]
    ms = jnp.mean(x * x, axis=-1, keepdims=True)
    u = (x * lax.rsqrt(ms + RMS_EPS) * g_ref[...]).astype(BF16)
    rw = cos_ref.shape[1]
    for n0 in range(0, w_ref.shape[1], tn):
        z = jnp.dot(u, w_ref[:, n0:n0 + tn], preferred_element_type=F32)
        z_ref[:, n0:n0 + tn] = z.astype(z_ref.dtype)
        for col in rope_cols:
            if n0 <= col < n0 + tn:
                zr = z[:, col - n0:col - n0 + rw]
                z_ref[:, col:col + rw] = (zr * cos_ref[...] + _rope_swap(zr) * sin_ref[...]).astype(z_ref.dtype)


def _norm_proj(x, g, w, cos_t, sin_t, *, tm, tn, rope_cols, out_dtype, name):
    t, d = x.shape
    n = w.shape[1]
    s_tiles = cos_t.shape[0] // tm
    rw = cos_t.shape[1]
    table_spec = pl.BlockSpec((tm, rw), lambda i: (i % s_tiles, 0))
    return pl.pallas_call(
        functools.partial(_norm_proj_kernel, tn=tn, rope_cols=tuple(rope_cols)),
        grid=(t // tm,),
        in_specs=[
            pl.BlockSpec((tm, d), lambda i: (i, 0)),
            pl.BlockSpec((1, d), lambda i: (0, 0)),
            pl.BlockSpec((d, n), lambda i: (0, 0), pipeline_mode=pl.Buffered(1)),
            table_spec, table_spec,
        ],
        out_specs=pl.BlockSpec((tm, n), lambda i: (i, 0)),
        out_shape=jax.ShapeDtypeStruct((t, n), out_dtype),
        compiler_params=pltpu.CompilerParams(
            dimension_semantics=("parallel",), vmem_limit_bytes=VMEM_LIMIT),
        name=name,
    )(x, g, w, cos_t, sin_t)


def _compress_kernel(xk_ref, xv_ref, pk_ref, pv_ref, wk1_ref, wv1_ref, w2_ref, cos_ref, sin_ref, o_ref, xs_scr):
    tm = o_ref.shape[0]

    def hidden(x_ref, p_ref, w_ref):
        xs_scr[...] = x_ref[...].astype(F32)
        first = jnp.zeros((tm, w_ref.shape[2]), F32)
        second = jnp.zeros((tm, w_ref.shape[2]), F32)
        for j in range(CMP_STRIDE):
            xj = xs_scr[pl.ds(j, tm, stride=CMP_STRIDE), :]
            first = first + jnp.dot((xj + p_ref[j:j + 1, :]).astype(BF16), w_ref[j],
                                    preferred_element_type=F32)
            second = second + jnp.dot((xj + p_ref[CMP_STRIDE + j:CMP_STRIDE + j + 1, :]).astype(BF16),
                                      w_ref[CMP_STRIDE + j], preferred_element_type=F32)
        return _gelu_tanh(first + pltpu.roll(second, tm - 1, 0)).astype(BF16)

    hcat = jnp.concatenate([hidden(xk_ref, pk_ref, wk1_ref), hidden(xv_ref, pv_ref, wv1_ref)], axis=1)
    kv = jnp.dot(hcat, w2_ref[...], preferred_element_type=F32)
    o_ref[...] = kv * cos_ref[...] + _rope_swap(kv) * sin_ref[...]


def _compress(z, pk, pv, wk1, wv1, w2, cos_t, sin_t, *, tm):
    t = z.shape[0]
    m = t // CMP_STRIDE
    pairs = NSA_GROUPS // 2
    pair_w = 2 * HEAD_DIM
    hid2 = wk1.shape[2]
    out_w = w2.shape[1]
    full = lambda shape: pl.BlockSpec(shape, lambda p, i: (0,) * len(shape))
    tokens = lambda off: pl.BlockSpec((tm * CMP_STRIDE, pair_w), lambda p, i: (i, off // pair_w + p))
    return pl.pallas_call(
        _compress_kernel,
        grid=(pairs, m // tm),
        in_specs=[
            tokens(OFF_KC), tokens(OFF_VC),
            full((CMP_LEN, pair_w)), full((CMP_LEN, pair_w)),
            full((CMP_LEN, pair_w, hid2)), full((CMP_LEN, pair_w, hid2)),
            full((2 * hid2, out_w)),
            full((tm, out_w)), full((tm, out_w)),
        ],
        out_specs=pl.BlockSpec((tm, out_w), lambda p, i: (i, p)),
        out_shape=jax.ShapeDtypeStruct((m, pairs * out_w), F32),
        scratch_shapes=[pltpu.VMEM((tm * CMP_STRIDE, pair_w), F32)],
        compiler_params=pltpu.CompilerParams(
            dimension_semantics=("parallel", "parallel"), vmem_limit_bytes=VMEM_LIMIT),
        name="compress",
    )(z, z, pk, pv, wk1, wv1, w2, cos_t, sin_t)


def _nsa_kernel(q_ref, g_ref, cos_ref, sin_ref, *rest, n_sel):
    kv_refs, rest = rest[:4 * NSA_PAIRS], rest[4 * NSA_PAIRS:]
    kvc_ref, c2st_ref, o_ref, gt_scr, s_scr, vt_scr = rest
    ks_refs, vs_refs, kw_refs, vw_refs = (kv_refs[n::4] for n in range(4))
    g2 = pl.program_id(1)
    i = pl.program_id(2)
    q0 = i * Q_BLOCK
    seq = ks_refs[0].shape[0]

    @pl.when(i == 0)
    def _():
        ones_rows = jnp.where(lax.broadcasted_iota(jnp.int32, (V_AUG - HEAD_DIM, Q_BLOCK), 0) == 0,
                              1.0, 0.0).astype(BF16)

        def fill(t, carry):
            r0 = pl.multiple_of(t * Q_BLOCK, Q_BLOCK)
            for kind, refs in enumerate((vs_refs, vw_refs)):
                for pair, ref in enumerate(refs):
                    vt = jnp.transpose(ref[pl.ds(r0, Q_BLOCK), :].astype(F32)).astype(BF16)
                    for half in range(2):
                        gg = 2 * pair + half
                        vt_scr[kind, gg, t, 0:HEAD_DIM, :] = vt[half * HEAD_DIM:(half + 1) * HEAD_DIM]
                        vt_scr[kind, gg, t, HEAD_DIM:V_AUG, :] = ones_rows
            return carry

        lax.fori_loop(0, seq // Q_BLOCK, fill, 0)

    def q_operand(gg):
        zeros_q = jnp.zeros((HEAD_DIM, NSA_LANES), BF16)
        return [qts[gg], zeros_q] if gg % 2 == 0 else [zeros_q, qts[gg]]

    blocks_per_chunk = SEL_CHUNK // SEL_BLOCK
    tiles_per_chunk = SEL_CHUNK // Q_BLOCK
    q_per_chunk = SEL_CHUNK // Q_BLOCK

    def tile_heads(x):
        return jnp.concatenate([x] * NSA_HPG, axis=1)

    tq = q0 + (lax.broadcasted_iota(jnp.int32, (1, NSA_LANES), 1) & (Q_BLOCK - 1))
    c_idx = lax.broadcasted_iota(jnp.int32, (Q_BLOCK, 1), 0)
    r_idx = lax.broadcasted_iota(jnp.int32, (1, Q_BLOCK), 1)
    tri_diag = tile_heads(jnp.where(c_idx <= r_idx, 0.0, NEG))
    tri_old = tile_heads(jnp.where(c_idx > r_idx, 0.0, NEG))
    cos_q = tile_heads(cos_ref[...])
    sin_q = tile_heads(sin_ref[...])
    blk = lax.broadcasted_iota(jnp.int32, (n_sel, Q_BLOCK), 0)
    cur = (q0 + lax.broadcasted_iota(jnp.int32, (n_sel, Q_BLOCK), 1)) // SEL_BLOCK
    sub = lax.broadcasted_iota(jnp.int32, (SUBLANES, Q_BLOCK), 0)

    gt_scr[...] = jnp.transpose(_sigmoid(g_ref[...].astype(F32)))

    def prepare(gg):
        qx = jnp.transpose(q_ref[:, gg * NSA_HPG * HEAD_DIM:(gg + 1) * NSA_HPG * HEAD_DIM].astype(F32)
                           * (SCALE * LOG2E))
        qf = jnp.concatenate([qx[h * HEAD_DIM:(h + 1) * HEAD_DIM, :] for h in range(NSA_HPG)], axis=1)
        x1, x2 = qf[0:ROPE_HALF], qf[ROPE_HALF:ROPE_DIM]
        qt = jnp.concatenate([x1 * cos_q - x2 * sin_q, x2 * cos_q + x1 * sin_q, qf[ROPE_DIM:]],
                             axis=0).astype(BF16)
        qts[gg] = qt
        yield

        kvc = kvc_ref[:, gg * LANES:(gg + 1) * LANES]
        kc = kvc[:, :HEAD_DIM].astype(BF16)
        vct = jnp.transpose(kvc)[HEAD_DIM:, :].astype(BF16)
        sc = jnp.dot(kc, qt, preferred_element_type=F32)
        yield
        n_idx = lax.broadcasted_iota(jnp.int32, (LANES, 1), 0)
        sc = jnp.where((n_idx * CMP_STRIDE + (CMP_LEN - 1)) <= tq, sc, NEG)
        m_c = jnp.max(sc, axis=0, keepdims=True)
        e_c = jnp.exp2(sc - m_c)
        inv_c = 1.0 / jnp.maximum(jnp.sum(e_c, axis=0, keepdims=True), 1e-30)
        p_c = e_c * jnp.where(m_c > 0.5 * NEG, inv_c, 0.0)
        o_c = jnp.dot(vct, p_c.astype(BF16), preferred_element_type=F32)
        yield

        p_sum = p_c[:, 0:Q_BLOCK]
        for h in range(1, NSA_HPG):
            p_sum = p_sum + p_c[:, h * Q_BLOCK:(h + 1) * Q_BLOCK]
        imp = jnp.dot(c2st_ref[...], p_sum, preferred_element_type=F32,
                      precision=lax.Precision.HIGHEST)
        yield
        forced = (blk == 0) | (blk == cur) | (blk == cur - 1)
        val = jnp.where(forced, jnp.inf, jnp.where(blk > cur, -jnp.inf, imp))
        n_grp = n_sel // SUBLANES
        grp = [val[v * SUBLANES:(v + 1) * SUBLANES] for v in range(n_grp)]
        cnt = [jnp.zeros((SUBLANES, Q_BLOCK), F32) for _ in range(n_grp)]
        for sp in range(n_sel):
            r = val[sp:sp + 1, :]
            for v in range(n_grp):
                if v * SUBLANES > sp:
                    cnt[v] = cnt[v] + jnp.where(r >= grp[v], 1.0, 0.0)
                elif v * SUBLANES + SUBLANES - 1 <= sp:
                    cnt[v] = cnt[v] + jnp.where(r > grp[v], 1.0, 0.0)
                else:
                    later = jnp.where(sub > sp - v * SUBLANES, 1.0, 0.0)
                    cnt[v] = cnt[v] + jnp.where(r > grp[v], 1.0, jnp.where(r == grp[v], later, 0.0))
            if sp % SUBLANES == SUBLANES - 1:
                yield
        chosen = jnp.concatenate([jnp.where(c < float(SEL_TOPK), 0.0, NEG) for c in cnt], axis=0)
        block_bias = tile_heads(jnp.where(blk > cur, NEG, chosen))
        row0 = pl.multiple_of((g2 * NSA_GPS + gg) * GATE_STRIDE, GATE_STRIDE)
        gt = gt_scr[pl.ds(row0, 2 * SUBLANES), :]
        return qt, o_c, block_bias, gt

    def window(gg):
        tiles = []
        for d in range(1 - WIN_TILES, 1):
            td = i + d
            tc = jnp.maximum(td, 0)
            k0 = pl.multiple_of(tc * Q_BLOCK, Q_BLOCK)
            before_start = jnp.where(td >= 0, 0.0, NEG) if d < 0 else 0.0
            tri = tri_diag if d == 0 else (tri_old if d == 1 - WIN_TILES else None)
            tiles.append((tc, k0, before_start, tri))

        ones_rows = jnp.where(lax.broadcasted_iota(jnp.int32, (BF16_ROWS, NSA_LANES), 0) == 0, 1.0, 0.0).astype(BF16)
        q_aug = jnp.concatenate(q_operand(gg) + [ones_rows, jnp.zeros((LANES - BF16_ROWS, NSA_LANES), BF16)],
                                axis=0)
        first_lane = lax.broadcasted_iota(jnp.int32, (Q_BLOCK, LANES), 1) == 0
        scores = []
        for _, k0, before_start, tri in tiles:
            k_aug = jnp.concatenate([kw_refs[gg // 2][pl.ds(k0, Q_BLOCK), :],
                                     jnp.where(first_lane, before_start, 0.0).astype(BF16)], axis=1)
            s = jnp.dot(k_aug, q_aug, preferred_element_type=F32)
            scores.append(s if tri is None else s + tri)
            yield
        m_w = jnp.max(functools.reduce(jnp.maximum, [_fold_rows(s, jnp.maximum) for s in scores]),
                      axis=0, keepdims=True)
        n_pad = jnp.maximum(WINDOW - 1 - tq, 0).astype(F32)
        m_w = jnp.where(n_pad > 0.0, jnp.maximum(m_w, 0.0), m_w)
        yield
        acc = jnp.zeros((V_AUG, NSA_LANES), F32)
        for (tc, _, _, _), s in zip(tiles, scores):
            acc = acc + jnp.dot(vt_scr[1, gg, tc], jnp.exp2(s - m_w).astype(BF16), preferred_element_type=F32)
            yield
        l_w = acc[HEAD_DIM:HEAD_DIM + 1] + n_pad * jnp.exp2(-m_w)
        return acc[:HEAD_DIM] * (1.0 / jnp.maximum(l_w, 1e-30))

    def chunk_scores(gg, c, diag_rel):
        rows = slice(c * SEL_CHUNK, (c + 1) * SEL_CHUNK)
        s = jnp.dot(ks_refs[gg // 2][rows, :], jnp.concatenate(q_operand(gg), axis=0),
                    preferred_element_type=F32)
        if diag_rel is not None:
            s = s + jnp.concatenate([jnp.where(diag_rel == u, tri_diag, 0.0) for u in range(tiles_per_chunk)], axis=0)
        s_scr[gg, rows, :] = s
        bmax = [_fold_rows(s[b * SEL_BLOCK:(b + 1) * SEL_BLOCK], jnp.maximum) for b in range(blocks_per_chunk)]
        yield
        return bmax

    def first_chunk(gg):
        yield
        return (yield from chunk_scores(gg, 0, i))

    def selected(gg, k, block_bias, bmax0):
        bmax = list(bmax0)
        for c in range(1, k + 1):
            bmax += yield from chunk_scores(gg, c, i - k * q_per_chunk if c == k else None)
        m8 = functools.reduce(jnp.maximum, [bm + block_bias[n:n + 1, :] for n, bm in enumerate(bmax)])
        m_s = jnp.max(m8, axis=0, keepdims=True)
        shift = block_bias - m_s
        yield
        acc = jnp.zeros((V_AUG, NSA_LANES), F32)
        for c in range(k + 1):
            ps = []
            for b in range(blocks_per_chunk):
                n = c * blocks_per_chunk + b
                ps.append(jnp.exp2(s_scr[gg, n * SEL_BLOCK:(n + 1) * SEL_BLOCK, :] + shift[n:n + 1, :]).astype(BF16))
            vt = jnp.concatenate([vt_scr[0, gg, c * tiles_per_chunk + u] for u in range(tiles_per_chunk)], axis=1)
            acc = acc + jnp.dot(vt, jnp.concatenate(ps, axis=0), preferred_element_type=F32)
            yield
        l_s = acc[HEAD_DIM:HEAD_DIM + 1]
        return acc[:HEAD_DIM] * (1.0 / jnp.maximum(l_s, 1e-30))

    qts = [None] * NSA_GPS
    groups = range(NSA_GPS)
    shared = _round_robin([prepare(gg) for gg in groups] + [window(gg) for gg in groups]
                          + [first_chunk(gg) for gg in groups])
    prepared, windows, bmax0 = (shared[n * NSA_GPS:(n + 1) * NSA_GPS] for n in range(3))

    def finish(k):
        branches = []
        for gg, (qt, o_c, block_bias, gt) in enumerate(prepared):
            branches.append(selected(gg, k, block_bias, bmax0[gg]))
        results = _round_robin(branches, lead=k + 1)
        outs = []
        for gg, (qt, o_c, block_bias, gt) in enumerate(prepared):
            o_w, o_s = windows[gg], results[gg]
            for h in range(NSA_HPG):
                sl = slice(h * Q_BLOCK, (h + 1) * Q_BLOCK)
                row = NSA_BRANCHES * h
                outs.append(gt[row:row + 1, :] * o_c[:, sl] + gt[row + 1:row + 2, :] * o_s[:, sl]
                            + gt[row + 2:row + 3, :] * o_w[:, sl])
        o_ref[...] = jnp.transpose(jnp.concatenate(outs, axis=0))

    for k in range(n_sel * SEL_BLOCK // SEL_CHUNK):
        pl.when(i // q_per_chunk == k)(functools.partial(finish, k))


def _nsa(z, cos_q, sin_q, kvc, c2st, *, batch, seq):
    nq = seq // Q_BLOCK
    n_sel = seq // SEL_BLOCK
    gw = NSA_GPS * NSA_HPG * HEAD_DIM
    gates_w = NSA_GROUPS * GATE_STRIDE
    pair_w = 2 * HEAD_DIM
    assert pair_w == LANES
    kv_spec = lambda off, pair: pl.BlockSpec(
        (seq, pair_w), lambda b, g, i: (b, off // pair_w + g * NSA_PAIRS + pair))
    kv_specs = [kv_spec(off, pair) for pair in range(NSA_PAIRS) for off in (OFF_KS, OFF_VS, OFF_KW, OFF_VW)]
    rope_spec = pl.BlockSpec((ROPE_HALF, Q_BLOCK), lambda b, g, i: (0, i))
    return pl.pallas_call(
        functools.partial(_nsa_kernel, n_sel=n_sel),
        grid=(batch, NSA_GROUPS // NSA_GPS, nq),
        in_specs=[
            pl.BlockSpec((Q_BLOCK, gw), lambda b, g, i: (b * nq + i, OFF_Q // gw + g)),
            pl.BlockSpec((Q_BLOCK, gates_w), lambda b, g, i: (b * nq + i, OFF_GATE // gates_w)),
            rope_spec, rope_spec,
            *kv_specs,
            pl.BlockSpec((LANES, NSA_GPS * 2 * HEAD_DIM), lambda b, g, i: (b, g)),
            pl.BlockSpec((n_sel, LANES), lambda b, g, i: (0, 0)),
        ],
        out_specs=pl.BlockSpec((Q_BLOCK, gw), lambda b, g, i: (b * nq + i, g)),
        out_shape=jax.ShapeDtypeStruct((batch * seq, NSA_HEADS * HEAD_DIM), F32),
        scratch_shapes=[pltpu.VMEM((gates_w, Q_BLOCK), F32), pltpu.VMEM((NSA_GPS, seq, NSA_LANES), F32),
                        pltpu.VMEM((2, NSA_GPS, nq, V_AUG, Q_BLOCK), BF16)],
        compiler_params=pltpu.CompilerParams(
            dimension_semantics=("arbitrary", "arbitrary", "arbitrary"), vmem_limit_bytes=VMEM_LIMIT),
        name="nsa",
    )(z, z, cos_q, sin_q, *([z] * len(kv_specs)), kvc, c2st)


def _rglru_kernel(x_ref, gr_ref, cw_ref, cb_ref, wa_ref, ba_ref, wi_ref, bi_ref, lam_ref, y_ref,
                  xs_scr, a_scr, b_scr, h_scr):
    nb, ts, c = x_ref.shape
    pitch = a_scr.shape[1] // nb
    lane_tiles = c // LANES

    def put(scr, rows, val):
        for m in range(lane_tiles):
            scr[m, rows, :] = val[:, m * LANES:(m + 1) * LANES]

    def get(scr, rows):
        return jnp.concatenate([scr[m, rows, :] for m in range(lane_tiles)], axis=1)

    @pl.when(pl.program_id(0) == 0)
    def _():
        xs_scr[:, 0:SUBLANES, :] = jnp.zeros((nb, SUBLANES, c), F32)
        h_scr[...] = jnp.zeros(h_scr.shape, F32)

    nl = -lam_ref[...]
    softplus = jnp.maximum(nl, 0.0) + jnp.log1p(jnp.exp(-jnp.abs(nl)))

    def gates(bi_, carry):
        x = x_ref[bi_].astype(F32)
        xs_scr[bi_, SUBLANES:SUBLANES + ts, :] = x
        xc = cb_ref[...] + cw_ref[CONV_WIDTH - 1:CONV_WIDTH, :] * x
        for k in range(CONV_WIDTH - 1):
            off = SUBLANES - (CONV_WIDTH - 1) + k
            xc = xc + cw_ref[k:k + 1, :] * xs_scr[bi_, off:off + ts, :]
        xs_scr[bi_, 0:SUBLANES, :] = x[ts - SUBLANES:, :]

        ra, ri = [], []
        for mblk in range(c // LANES):
            xb = xc[:, mblk * LANES:(mblk + 1) * LANES].astype(BF16)
            ra.append(jnp.dot(xb, wa_ref[mblk], preferred_element_type=F32))
            ri.append(jnp.dot(xb, wi_ref[mblk], preferred_element_type=F32))
        r = _sigmoid(jnp.concatenate(ra, axis=1) + ba_ref[...])
        gi = _sigmoid(jnp.concatenate(ri, axis=1) + bi_ref[...])

        log_a = (-RGLRU_C) * r * softplus
        th = jnp.tanh(log_a)
        y2 = -2.0 * th / (1.0 - th)
        rows = pl.ds(pl.multiple_of(bi_ * pitch, SUBLANES), ts)
        put(a_scr, rows, jnp.exp(log_a))
        put(b_scr, rows, jnp.where(y2 > 0.0, y2 * lax.rsqrt(y2), 0.0) * (gi * xc))
        return carry

    lax.fori_loop(0, nb, gates, 0)

    def step(t, h):
        rows = pl.ds(t, nb, stride=pitch)
        h = get(a_scr, rows) * h + get(b_scr, rows)
        put(b_scr, rows, h)
        return h

    h_scr[...] = lax.fori_loop(0, ts, step, h_scr[...], unroll=8)

    def outputs(bi_, carry):
        rows = pl.ds(pl.multiple_of(bi_ * pitch, SUBLANES), ts)
        y_ref[bi_] = _gelu_tanh(gr_ref[bi_].astype(F32)) * get(b_scr, rows)
        return carry

    lax.fori_loop(0, nb, outputs, 0)


def _rglru(z, cw, cb, wa2, ba, wi2, bi, lam, *, batch, seq, ts):
    c = D_MODEL
    pitch = ts + SUBLANES
    z3 = z.reshape(batch, seq, z.shape[1])
    vec = pl.BlockSpec((1, c), lambda t: (0, 0))
    y = pl.pallas_call(
        _rglru_kernel,
        grid=(seq // ts,),
        in_specs=[
            pl.BlockSpec((batch, ts, c), lambda t: (0, t, OFF_XRNN // c)),
            pl.BlockSpec((batch, ts, c), lambda t: (0, t, OFF_GRNN // c)),
            pl.BlockSpec((CONV_WIDTH, c), lambda t: (0, 0)),
            vec,
            pl.BlockSpec((c // LANES, LANES, LANES), lambda t: (0, 0, 0)),
            vec,
            pl.BlockSpec((c // LANES, LANES, LANES), lambda t: (0, 0, 0)),
            vec, vec,
        ],
        out_specs=pl.BlockSpec((batch, ts, c), lambda t: (0, t, 0)),
        out_shape=jax.ShapeDtypeStruct((batch, seq, c), F32),
        scratch_shapes=[pltpu.VMEM((batch, SUBLANES + ts, c), F32), pltpu.VMEM((c // LANES, batch * pitch, LANES), F32),
                        pltpu.VMEM((c // LANES, batch * pitch, LANES), F32), pltpu.VMEM((batch, c), F32)],
        compiler_params=pltpu.CompilerParams(
            dimension_semantics=("arbitrary",), vmem_limit_bytes=VMEM_LIMIT),
        name="rglru",
    )(z3, z3, cw, cb, wa2, ba, wi2, bi, lam)
    return y.reshape(batch * seq, c)


def _merge_kernel(x_ref, m0_ref, m1_ref, m2_ref, ynsa_ref, yrnn_ref, qx_ref, km_ref, vm_ref, wxo_ref, wo_ref, h_ref):
    tq = x_ref.shape[0]
    mem_len = km_ref.shape[0]
    row_chunk = tq // 4

    def cross_attention():
        qt = jnp.transpose(qx_ref[...].astype(F32) * (SCALE * LOG2E))
        row_head = lax.broadcasted_iota(jnp.int32, qt.shape, 0) // HEAD_DIM
        q_bd = jnp.concatenate([jnp.where(row_head == h, qt, 0.0) for h in range(XATTN_HEADS)],
                               axis=1).astype(BF16)
        yield
        s = jnp.dot(km_ref[...], q_bd, preferred_element_type=F32)
        yield
        m = jnp.max(_fold_rows(s, jnp.maximum), axis=0, keepdims=True)
        p = jnp.exp2(s - m).astype(BF16)
        yield
        ones_rows = jnp.where(lax.broadcasted_iota(jnp.int32, (BF16_ROWS, mem_len), 0) == 0, 1.0, 0.0)
        vt = jnp.concatenate([jnp.transpose(vm_ref[...].astype(F32)), ones_rows], axis=0).astype(BF16)
        o_all = jnp.dot(vt, p, preferred_element_type=F32)
        yield
        outs = []
        for h in range(XATTN_HEADS):
            lanes = slice(h * tq, (h + 1) * tq)
            inv_l = 1.0 / o_all[XATTN_WIDTH:XATTN_WIDTH + 1, lanes]
            outs.append(o_all[h * HEAD_DIM:(h + 1) * HEAD_DIM, lanes] * inv_l)
        o = jnp.transpose(jnp.concatenate(outs, axis=0)).astype(BF16)
        yield
        return jnp.dot(o, wxo_ref[...], preferred_element_type=F32)

    def gated(gate_ref, val_ref):
        parts = []
        for r in range(0, tq, row_chunk):
            half = 0.5 * val_ref[r:r + row_chunk, :]
            parts.append(half * jnp.tanh(0.5 * gate_ref[r:r + row_chunk, :].astype(F32)) + half)
            yield
        return jnp.concatenate(parts, axis=0)

    def gate_tanh(gate_ref):
        parts = []
        for r in range(0, tq, row_chunk):
            parts.append(jnp.tanh(0.5 * gate_ref[r:r + row_chunk, :].astype(F32)))
            yield
        return jnp.concatenate(parts, axis=0)

    yx, y_a, y_b, t_x = _round_robin([cross_attention(), gated(m0_ref, ynsa_ref), gated(m1_ref, yrnn_ref),
                                      gate_tanh(m2_ref)])
    half_yx = 0.5 * yx
    y = y_a + y_b + (half_yx * t_x + half_yx)
    h_ref[...] = x_ref[...] + jnp.dot(y.astype(BF16), wo_ref[...], preferred_element_type=F32)


def _merge(x, z, ynsa, yrnn, kvm, wxo, wo, *, seq, mem_len, tq):
    t, d = x.shape
    nt = seq // tq
    row = lambda cb: pl.BlockSpec((tq, d), lambda i, cb=cb: (i, cb))
    mem_spec = lambda cb: pl.BlockSpec((mem_len, XATTN_WIDTH), lambda i, cb=cb: (i // nt, cb))
    return pl.pallas_call(
        _merge_kernel,
        grid=(t // tq,),
        in_specs=[
            row(0),
            row(OFF_MERGE // d), row(OFF_MERGE // d + 1), row(OFF_MERGE // d + 2),
            row(0), row(0),
            pl.BlockSpec((tq, XATTN_WIDTH), lambda i: (i, OFF_QX // XATTN_WIDTH)),
            mem_spec(0), mem_spec(1),
            pl.BlockSpec((XATTN_WIDTH, d), lambda i: (0, 0)),
            pl.BlockSpec((d, d), lambda i: (0, 0)),
        ],
        out_specs=row(0),
        out_shape=jax.ShapeDtypeStruct((t, d), F32),
        compiler_params=pltpu.CompilerParams(
            dimension_semantics=("parallel",), vmem_limit_bytes=VMEM_LIMIT),
        name="merge",
    )(x, z, z, z, ynsa, yrnn, z, kvm, kvm, wxo, wo)


def _mlp_kernel(h_ref, g_ref, wup_ref, wdn_ref, gf_ref, o_ref, *, tf):
    h = h_ref[...]
    ms = jnp.mean(h * h, axis=-1, keepdims=True)
    v = (h * lax.rsqrt(ms + RMS_EPS) * g_ref[...]).astype(BF16)
    h2 = h
    for f in range(0, wup_ref.shape[1], tf):
        up = jnp.dot(v, wup_ref[:, f:f + tf], preferred_element_type=F32)
        act = jnp.square(jnp.maximum(up, 0.0)).astype(BF16)
        h2 = h2 + jnp.dot(act, wdn_ref[f:f + tf, :], preferred_element_type=F32)
    ms = jnp.mean(h2 * h2, axis=-1, keepdims=True)
    o_ref[...] = h2 * lax.rsqrt(ms + RMS_EPS) * gf_ref[...]


def _mlp(h, g, wup, wdn, gf, *, tm, tf):
    t, d = h.shape
    dff = wup.shape[1]
    resident = lambda shape: pl.BlockSpec(shape, lambda i: (0, 0), pipeline_mode=pl.Buffered(1))
    return pl.pallas_call(
        functools.partial(_mlp_kernel, tf=tf),
        grid=(t // tm,),
        in_specs=[
            pl.BlockSpec((tm, d), lambda i: (i, 0)),
            pl.BlockSpec((1, d), lambda i: (0, 0)),
            resident((d, dff)), resident((dff, d)),
            pl.BlockSpec((1, d), lambda i: (0, 0)),
        ],
        out_specs=pl.BlockSpec((tm, d), lambda i: (i, 0)),
        out_shape=jax.ShapeDtypeStruct((t, d), F32),
        compiler_params=pltpu.CompilerParams(
            dimension_semantics=("parallel",), vmem_limit_bytes=VMEM_LIMIT),
        name="mlp",
    )(h, g, wup, wdn, gf)


def _rope_angles(pos):
    inv = 1.0 / (ROPE_THETA ** (jnp.arange(0, ROPE_DIM, 2, dtype=F32) / ROPE_DIM))
    ang = pos.astype(F32)[:, None] * inv[None, :]
    return jnp.cos(ang), jnp.sin(ang)


def _rope_tables(pos, width, rope_width=None):
    rope_width = width if rope_width is None else rope_width
    cos, sin = _rope_angles(pos)
    n = pos.shape[0]
    pad = HEAD_DIM - ROPE_DIM
    cos_h = jnp.concatenate([cos, cos, jnp.ones((n, pad), F32)], axis=1)
    sin_h = jnp.concatenate([-sin, sin, jnp.zeros((n, pad), F32)], axis=1)
    reps = rope_width // HEAD_DIM
    rest = width - rope_width
    cos_t = jnp.concatenate([jnp.tile(cos_h, (1, reps)), jnp.ones((n, rest), F32)], axis=1)
    sin_t = jnp.concatenate([jnp.tile(sin_h, (1, reps)), jnp.zeros((n, rest), F32)], axis=1)
    return cos_t, sin_t


def _pack_w_in(w_in):
    d = w_in.shape[0]
    w = w_in.astype(BF16)
    n_qkv = NSA_HEADS * HEAD_DIM + 6 * NSA_KV
    n_gates = NSA_HEADS * NSA_BRANCHES
    rnn_lo = n_qkv + n_gates
    per_group = n_gates // NSA_GROUPS
    gates = w[:, n_qkv:rnn_lo].reshape(d, NSA_GROUPS, per_group)
    gates = jnp.pad(gates, ((0, 0), (0, 0), (0, GATE_STRIDE - per_group))).reshape(d, NSA_GROUPS * GATE_STRIDE)
    rest = lax.optimization_barrier(w[:, rnn_lo:])
    qx_lo = 2 * D_MODEL
    mg_lo = qx_lo + XATTN_WIDTH
    return jnp.concatenate([w[:, :n_qkv], rest[:, qx_lo:mg_lo], gates, rest[:, :qx_lo], rest[:, mg_lo:]], axis=1)


def _block_diag_pairs(w):
    nb, k, _ = w.shape
    w = w.reshape(nb // 2, 2, k, k)
    zero = jnp.zeros((nb // 2, k, k), w.dtype)
    top = jnp.concatenate([w[:, 0], zero], axis=2)
    bot = jnp.concatenate([zero, w[:, 1]], axis=2)
    return jnp.concatenate([top, bot], axis=1).astype(BF16)


def _cmp_to_sel_t(n_cmp, n_cmp_pad, n_sel):
    c0 = np.arange(n_cmp_pad)[None, :] * CMP_STRIDE
    s0 = np.arange(n_sel)[:, None] * SEL_BLOCK
    ov = np.clip(np.minimum(c0 + CMP_LEN, s0 + SEL_BLOCK) - np.maximum(c0, s0), 0, None)
    ov = np.where(np.arange(n_cmp_pad)[None, :] < n_cmp, ov, 0)
    return (ov / CMP_LEN).astype(np.float32)


def kernel(x, mem, g_mix, w_in, cmp_pos_k, cmp_pos_v, w_cmp_k1, w_cmp_k2, w_cmp_v1, w_cmp_v2, conv_w, conv_b, w_rg_a, b_rg_a, w_rg_i, b_rg_i, rg_lambda, g_mem, w_mem_kv, w_xo, w_o, g_mlp, w_up, w_down, g_final):
    batch, seq, d = x.shape
    mem_len = mem.shape[1]
    t = batch * seq
    depth = g_mix.shape[0]
    n_sel = seq // SEL_BLOCK
    n_cmp_pad = seq // CMP_STRIDE
    assert n_cmp_pad == LANES and n_sel <= LANES and d == D_MODEL and depth == 1

    pos = jnp.arange(seq)
    cos_k, sin_k = _rope_tables(pos, NSA_KV)
    cos_q, sin_q = (a.T for a in _rope_angles(pos))
    no_rope = jnp.zeros((mem_len, LANES), F32)
    cmp_pos = jnp.arange(n_cmp_pad) * CMP_STRIDE + (CMP_LEN - 1)
    cos_c, sin_c = _rope_tables(cmp_pos, 2 * HEAD_DIM, HEAD_DIM)
    cos_c = jnp.tile(cos_c, (CMP_TM // n_cmp_pad, 2))
    sin_c = jnp.tile(sin_c, (CMP_TM // n_cmp_pad, 2))
    c2st = jnp.asarray(_cmp_to_sel_t((seq - CMP_LEN) // CMP_STRIDE + 1, n_cmp_pad, n_sel))

    h = x.reshape(t, d)
    for l in range(depth):
        z = _norm_proj(h, g_mix[l][None, :], _pack_w_in(w_in[l]), cos_k, sin_k, tm=INPROJ_TM, tn=INPROJ_TN,
                       rope_cols=ROPE_COLS, out_dtype=BF16, name="inproj")

        def pair_w1(w1):
            w3 = w1.astype(BF16).reshape(CMP_LEN, 1, HEAD_DIM, CMP_HIDDEN)
            zero = jnp.zeros_like(w3)
            both = jnp.concatenate([jnp.concatenate([w3, zero], axis=3), jnp.concatenate([zero, w3], axis=3)], axis=1)
            return both.reshape(CMP_LEN, 2 * HEAD_DIM, 2 * CMP_HIDDEN)

        def pair_pos(p):
            return jnp.concatenate([p, p], axis=1)

        w2 = jnp.zeros((2, 2, CMP_HIDDEN, 2, 2, HEAD_DIM), F32)
        for grp in range(2):
            w2 = w2.at[0, grp, :, grp, 0, :].set(w_cmp_k2[l]).at[1, grp, :, grp, 1, :].set(w_cmp_v2[l])
        w2 = w2.reshape(4 * CMP_HIDDEN, 4 * HEAD_DIM)
        kvc = _compress(z, pair_pos(cmp_pos_k[l]), pair_pos(cmp_pos_v[l]), pair_w1(w_cmp_k1[l]), pair_w1(w_cmp_v1[l]),
                        w2.astype(BF16), cos_c, sin_c, tm=CMP_TM)
        y_nsa = _nsa(z, cos_q, sin_q, kvc, c2st, batch=batch, seq=seq)

        y_rnn = _rglru(z, conv_w[l], conv_b[l][None, :], _block_diag_pairs(w_rg_a[l]), b_rg_a[l][None, :],
                       _block_diag_pairs(w_rg_i[l]), b_rg_i[l][None, :], rg_lambda[l][None, :],
                       batch=batch, seq=seq, ts=RGLRU_TS)

        kvm = _norm_proj(mem.reshape(batch * mem_len, d), g_mem[l][None, :], w_mem_kv[l].astype(BF16),
                         no_rope, no_rope, tm=mem_len, tn=2 * XATTN_WIDTH, rope_cols=(),
                         out_dtype=BF16, name="memkv")

        h1 = _merge(h, z, y_nsa, y_rnn, kvm, w_xo[l].astype(BF16), w_o[l].astype(BF16),
                    seq=seq, mem_len=mem_len, tq=MERGE_TQ)
        h = _mlp(h1, g_mlp[l][None, :], w_up[l].astype(BF16), w_down[l].astype(BF16), g_final[None, :],
                 tm=MLP_TM, tf=MLP_TF)
    return h.reshape(batch, seq, d)
```

```python
import functools
import math

import numpy as np
import jax
import jax.numpy as jnp
from jax import lax
from jax.experimental import pallas as pl
from jax.experimental.pallas import tpu as pltpu

F32 = jnp.float32
BF16 = jnp.bfloat16

D_MODEL = 1024
HEAD_DIM = 64
NSA_HEADS = 16
NSA_GROUPS = 4
NSA_HPG = NSA_HEADS // NSA_GROUPS
NSA_BRANCHES = 3
CMP_LEN = 32
CMP_STRIDE = 16
CMP_HIDDEN = 256
SEL_BLOCK = 64
SEL_TOPK = 8
WINDOW = 512
Q_BLOCK = 128
ROPE_THETA = 500000.0
ROPE_DIM = HEAD_DIM // 4
ROPE_HALF = ROPE_DIM // 2
CONV_WIDTH = 4
RGLRU_C = 8.0
XATTN_HEADS = 4
XATTN_WIDTH = XATTN_HEADS * HEAD_DIM
RMS_EPS = 1e-6
SCALE = HEAD_DIM ** -0.5
LOG2E = math.log2(math.e)
NEG = -1e30

LANES = 128
SUBLANES = 8
BF16_ROWS = 16
VMEM_LIMIT = 48 * 1024 * 1024

NSA_KV = NSA_GROUPS * HEAD_DIM
OFF_Q = 0
OFF_KC = 1024
OFF_VC = 1280
OFF_KS = 1536
OFF_VS = 1792
OFF_KW = 2048
OFF_VW = 2304
OFF_QX = 2560
OFF_GATE = 2816
GATE_STRIDE = 64
OFF_XRNN = 3072
OFF_GRNN = 4096
OFF_MERGE = 5120
INPROJ_TN = 2048
INPROJ_TM = 512
ROPE_COLS = (OFF_KS, OFF_KW)

NSA_PAIRS = 2
NSA_GPS = 2 * NSA_PAIRS
NSA_LANES = NSA_HPG * Q_BLOCK
SEL_CHUNK = 2 * Q_BLOCK
WIN_TILES = WINDOW // Q_BLOCK + 1
V_AUG = HEAD_DIM + BF16_ROWS

MERGE_TQ = 512
RGLRU_TS = 128
MLP_TM = 1024
MLP_TF = 1024
CMP_TM = 512


def _sigmoid(x):
    return 1.0 / (1.0 + jnp.exp(-x))


def _gelu_tanh(x):
    return 0.5 * x * (1.0 + jnp.tanh(0.7978845608028654 * (x + 0.044715 * (x * x * x))))


def _rope_swap(z):
    n = z.shape[-1]
    lane = lax.broadcasted_iota(jnp.int32, z.shape, z.ndim - 1)
    first_half = (lane & (HEAD_DIM - 1)) < ROPE_HALF
    return jnp.where(first_half, pltpu.roll(z, n - ROPE_HALF, z.ndim - 1), pltpu.roll(z, ROPE_HALF, z.ndim - 1))


def _round_robin(gens, lead=0):
    results = [None] * len(gens)
    alive = list(range(len(gens)))

    def advance(idx):
        try:
            next(gens[idx])
        except StopIteration as stop:
            results[idx] = stop.value
            alive.remove(idx)

    for _ in range(lead):
        for idx in [n for n in alive if n < len(gens) // 2]:
            advance(idx)
    while alive:
        for idx in list(alive):
            advance(idx)
    return results


def _fold_rows(x, op):
    parts = [x[r * SUBLANES:(r + 1) * SUBLANES] for r in range(x.shape[0] // SUBLANES)]
    return functools.reduce(op, parts)


def _norm_proj_kernel(x_ref, g_ref, w_ref, cos_ref, sin_ref, z_ref, *, tn, rope_cols):
    x = x_ref[...]
    ms = jnp.mean(x * x, axis=-1, keepdims=True)
    u = (x * lax.rsqrt(ms + RMS_EPS) * g_ref[...]).astype(BF16)
    rw = cos_ref.shape[1]
    for n0 in range(0, w_ref.shape[1], tn):
        z = jnp.dot(u, w_ref[:, n0:n0 + tn], preferred_element_type=F32)
        z_ref[:, n0:n0 + tn] = z.astype(z_ref.dtype)
        for col in rope_cols:
            if n0 <= col < n0 + tn:
                zr = z[:, col - n0:col - n0 + rw]
                z_ref[:, col:col + rw] = (zr * cos_ref[...] + _rope_swap(zr) * sin_ref[...]).astype(z_ref.dtype)


def _norm_proj(x, g, w, cos_t, sin_t, *, tm, tn, rope_cols, out_dtype, name):
    t, d = x.shape
    n = w.shape[1]
    s_tiles = cos_t.shape[0] // tm
    rw = cos_t.shape[1]
    table_spec = pl.BlockSpec((tm, rw), lambda i: (i % s_tiles, 0))
    return pl.pallas_call(
        functools.partial(_norm_proj_kernel, tn=tn, rope_cols=tuple(rope_cols)),
        grid=(t // tm,),
        in_specs=[
            pl.BlockSpec((tm, d), lambda i: (i, 0)),
            pl.BlockSpec((1, d), lambda i: (0, 0)),
            pl.BlockSpec((d, n), lambda i: (0, 0), pipeline_mode=pl.Buffered(1)),
            table_spec, table_spec,
        ],
        out_specs=pl.BlockSpec((tm, n), lambda i: (i, 0)),
        out_shape=jax.ShapeDtypeStruct((t, n), out_dtype),
        compiler_params=pltpu.CompilerParams(
            dimension_semantics=("parallel",), vmem_limit_bytes=VMEM_LIMIT),
        name=name,
    )(x, g, w, cos_t, sin_t)


def _compress_kernel(xk_ref, xv_ref, pk_ref, pv_ref, wk1_ref, wv1_ref, w2_ref, cos_ref, sin_ref, o_ref, xs_scr):
    tm = o_ref.shape[0]

    def hidden(x_ref, p_ref, w_ref):
        xs_scr[...] = x_ref[...].astype(F32)
        first = jnp.zeros((tm, w_ref.shape[2]), F32)
        second = jnp.zeros((tm, w_ref.shape[2]), F32)
        pair_k = 2 * x_ref.shape[1]
        for j in range(0, CMP_STRIDE, 2):
            xa = xs_scr[pl.ds(j, tm, stride=CMP_STRIDE), :]
            xb = xs_scr[pl.ds(j + 1, tm, stride=CMP_STRIDE), :]
            for acc_first, lo in ((True, j), (False, CMP_STRIDE + j)):
                lhs = jnp.concatenate([xa + p_ref[lo:lo + 1, :], xb + p_ref[lo + 1:lo + 2, :]], axis=1).astype(BF16)
                part = jnp.dot(lhs, w_ref[lo:lo + 2].reshape(pair_k, w_ref.shape[2]), preferred_element_type=F32)
                if acc_first:
                    first = first + part
                else:
                    second = second + part
        return _gelu_tanh(first + pltpu.roll(second, tm - 1, 0)).astype(BF16)

    hcat = jnp.concatenate([hidden(xk_ref, pk_ref, wk1_ref), hidden(xv_ref, pv_ref, wv1_ref)], axis=1)
    kv = jnp.dot(hcat, w2_ref[...], preferred_element_type=F32)
    o_ref[...] = kv * cos_ref[...] + _rope_swap(kv) * sin_ref[...]


def _compress(z, pk, pv, wk1, wv1, w2, cos_t, sin_t, *, tm):
    t = z.shape[0]
    m = t // CMP_STRIDE
    pairs = NSA_GROUPS // 2
    pair_w = 2 * HEAD_DIM
    hid2 = wk1.shape[2]
    out_w = w2.shape[1]
    full = lambda shape: pl.BlockSpec(shape, lambda p, i: (0,) * len(shape))
    tokens = lambda off: pl.BlockSpec((tm * CMP_STRIDE, pair_w), lambda p, i: (i, off // pair_w + p))
    return pl.pallas_call(
        _compress_kernel,
        grid=(pairs, m // tm),
        in_specs=[
            tokens(OFF_KC), tokens(OFF_VC),
            full((CMP_LEN, pair_w)), full((CMP_LEN, pair_w)),
            full((CMP_LEN, pair_w, hid2)), full((CMP_LEN, pair_w, hid2)),
            full((2 * hid2, out_w)),
            full((tm, out_w)), full((tm, out_w)),
        ],
        out_specs=pl.BlockSpec((tm, out_w), lambda p, i: (i, p)),
        out_shape=jax.ShapeDtypeStruct((m, pairs * out_w), F32),
        scratch_shapes=[pltpu.VMEM((tm * CMP_STRIDE, pair_w), F32)],
        compiler_params=pltpu.CompilerParams(
            dimension_semantics=("parallel", "parallel"), vmem_limit_bytes=VMEM_LIMIT),
        name="compress",
    )(z, z, pk, pv, wk1, wv1, w2, cos_t, sin_t)


def _nsa_kernel(q_ref, g_ref, cos_ref, sin_ref, *rest, n_sel):
    kv_refs, rest = rest[:4 * NSA_PAIRS], rest[4 * NSA_PAIRS:]
    kvc_ref, c2st_ref, o_ref, gt_scr, s_scr, vt_scr = rest
    ks_refs, vs_refs, kw_refs, vw_refs = (kv_refs[n::4] for n in range(4))
    g2 = pl.program_id(1)
    i = pl.program_id(2)
    q0 = i * Q_BLOCK
    seq = ks_refs[0].shape[0]

    @pl.when(i == 0)
    def _():
        ones_rows = jnp.where(lax.broadcasted_iota(jnp.int32, (V_AUG - HEAD_DIM, Q_BLOCK), 0) == 0,
                              1.0, 0.0).astype(BF16)

        def fill(t, carry):
            r0 = pl.multiple_of(t * Q_BLOCK, Q_BLOCK)
            for kind, refs in enumerate((vs_refs, vw_refs)):
                for pair, ref in enumerate(refs):
                    vt = jnp.transpose(ref[pl.ds(r0, Q_BLOCK), :].astype(F32)).astype(BF16)
                    for half in range(2):
                        gg = 2 * pair + half
                        vt_scr[kind, gg, t, 0:HEAD_DIM, :] = vt[half * HEAD_DIM:(half + 1) * HEAD_DIM]
                        vt_scr[kind, gg, t, HEAD_DIM:V_AUG, :] = ones_rows
            return carry

        lax.fori_loop(0, seq // Q_BLOCK, fill, 0)

    def q_operand(gg):
        zeros_q = jnp.zeros((HEAD_DIM, NSA_LANES), BF16)
        return [qts[gg], zeros_q] if gg % 2 == 0 else [zeros_q, qts[gg]]

    blocks_per_chunk = SEL_CHUNK // SEL_BLOCK
    tiles_per_chunk = SEL_CHUNK // Q_BLOCK
    q_per_chunk = SEL_CHUNK // Q_BLOCK

    def tile_heads(x):
        return jnp.concatenate([x] * NSA_HPG, axis=1)

    tq = q0 + (lax.broadcasted_iota(jnp.int32, (1, NSA_LANES), 1) & (Q_BLOCK - 1))
    c_idx = lax.broadcasted_iota(jnp.int32, (Q_BLOCK, 1), 0)
    r_idx = lax.broadcasted_iota(jnp.int32, (1, Q_BLOCK), 1)
    tri_diag = tile_heads(jnp.where(c_idx <= r_idx, 0.0, NEG))
    tri_old = tile_heads(jnp.where(c_idx > r_idx, 0.0, NEG))
    cos_q = tile_heads(cos_ref[...])
    sin_q = tile_heads(sin_ref[...])
    blk = lax.broadcasted_iota(jnp.int32, (n_sel, Q_BLOCK), 0)
    cur = (q0 + lax.broadcasted_iota(jnp.int32, (n_sel, Q_BLOCK), 1)) // SEL_BLOCK
    sub = lax.broadcasted_iota(jnp.int32, (SUBLANES, Q_BLOCK), 0)

    gt_scr[...] = jnp.transpose(_sigmoid(g_ref[...].astype(F32)))

    def prepare(gg):
        qx = jnp.transpose(q_ref[:, gg * NSA_HPG * HEAD_DIM:(gg + 1) * NSA_HPG * HEAD_DIM].astype(F32)
                           * (SCALE * LOG2E))
        qf = jnp.concatenate([qx[h * HEAD_DIM:(h + 1) * HEAD_DIM, :] for h in range(NSA_HPG)], axis=1)
        x1, x2 = qf[0:ROPE_HALF], qf[ROPE_HALF:ROPE_DIM]
        qt = jnp.concatenate([x1 * cos_q - x2 * sin_q, x2 * cos_q + x1 * sin_q, qf[ROPE_DIM:]],
                             axis=0).astype(BF16)
        qts[gg] = qt
        yield

        kvc = kvc_ref[:, gg * LANES:(gg + 1) * LANES]
        kc = kvc[:, :HEAD_DIM].astype(BF16)
        vct = jnp.transpose(kvc)[HEAD_DIM:, :].astype(BF16)
        sc = jnp.dot(kc, qt, preferred_element_type=F32)
        yield
        n_idx = lax.broadcasted_iota(jnp.int32, (LANES, 1), 0)
        sc = jnp.where((n_idx * CMP_STRIDE + (CMP_LEN - 1)) <= tq, sc, NEG)
        m_c = jnp.max(sc, axis=0, keepdims=True)
        e_c = jnp.exp2(sc - m_c)
        inv_c = 1.0 / jnp.maximum(jnp.sum(e_c, axis=0, keepdims=True), 1e-30)
        p_c = e_c * jnp.where(m_c > 0.5 * NEG, inv_c, 0.0)
        o_c = jnp.dot(vct, p_c.astype(BF16), preferred_element_type=F32)
        yield

        p_sum = p_c[:, 0:Q_BLOCK]
        for h in range(1, NSA_HPG):
            p_sum = p_sum + p_c[:, h * Q_BLOCK:(h + 1) * Q_BLOCK]
        imp = jnp.dot(c2st_ref[...], p_sum, preferred_element_type=F32,
                      precision=lax.Precision.HIGHEST)
        yield
        forced = (blk == 0) | (blk == cur) | (blk == cur - 1)
        val = jnp.where(forced, jnp.inf, jnp.where(blk > cur, -jnp.inf, imp))
        n_grp = n_sel // SUBLANES
        grp = [val[v * SUBLANES:(v + 1) * SUBLANES] for v in range(n_grp)]
        cnt = [jnp.zeros((SUBLANES, Q_BLOCK), F32) for _ in range(n_grp)]
        for sp in range(n_sel):
            r = val[sp:sp + 1, :]
            for v in range(n_grp):
                if v * SUBLANES > sp:
                    cnt[v] = cnt[v] + jnp.where(r >= grp[v], 1.0, 0.0)
                elif v * SUBLANES + SUBLANES - 1 <= sp:
                    cnt[v] = cnt[v] + jnp.where(r > grp[v], 1.0, 0.0)
                else:
                    later = jnp.where(sub > sp - v * SUBLANES, 1.0, 0.0)
                    cnt[v] = cnt[v] + jnp.where(r > grp[v], 1.0, jnp.where(r == grp[v], later, 0.0))
            if sp % SUBLANES == SUBLANES - 1:
                yield
        chosen = jnp.concatenate([jnp.where(c < float(SEL_TOPK), 0.0, NEG) for c in cnt], axis=0)
        block_bias = tile_heads(jnp.where(blk > cur, NEG, chosen))
        row0 = pl.multiple_of((g2 * NSA_GPS + gg) * GATE_STRIDE, GATE_STRIDE)
        gt = gt_scr[pl.ds(row0, 2 * SUBLANES), :]
        return qt, o_c, block_bias, gt

    def window(gg):
        tiles = []
        for d in range(1 - WIN_TILES, 1):
            td = i + d
            tc = jnp.maximum(td, 0)
            k0 = pl.multiple_of(tc * Q_BLOCK, Q_BLOCK)
            before_start = jnp.where(td >= 0, 0.0, NEG) if d < 0 else 0.0
            tri = tri_diag if d == 0 else (tri_old if d == 1 - WIN_TILES else None)
            tiles.append((tc, k0, before_start, tri))

        ones_rows = jnp.where(lax.broadcasted_iota(jnp.int32, (BF16_ROWS, NSA_LANES), 0) == 0, 1.0, 0.0).astype(BF16)
        q_aug = jnp.concatenate(q_operand(gg) + [ones_rows, jnp.zeros((LANES - BF16_ROWS, NSA_LANES), BF16)],
                                axis=0)
        first_lane = lax.broadcasted_iota(jnp.int32, (Q_BLOCK, LANES), 1) == 0
        scores = []
        for _, k0, before_start, tri in tiles:
            k_aug = jnp.concatenate([kw_refs[gg // 2][pl.ds(k0, Q_BLOCK), :],
                                     jnp.where(first_lane, before_start, 0.0).astype(BF16)], axis=1)
            s = jnp.dot(k_aug, q_aug, preferred_element_type=F32)
            scores.append(s if tri is None else s + tri)
            yield
        m_w = jnp.max(functools.reduce(jnp.maximum, [_fold_rows(s, jnp.maximum) for s in scores]),
                      axis=0, keepdims=True)
        n_pad = jnp.maximum(WINDOW - 1 - tq, 0).astype(F32)
        m_w = jnp.where(n_pad > 0.0, jnp.maximum(m_w, 0.0), m_w)
        yield
        acc = jnp.zeros((V_AUG, NSA_LANES), F32)
        for (tc, _, _, _), s in zip(tiles, scores):
            acc = acc + jnp.dot(vt_scr[1, gg, tc], jnp.exp2(s - m_w).astype(BF16), preferred_element_type=F32)
            yield
        l_w = acc[HEAD_DIM:HEAD_DIM + 1] + n_pad * jnp.exp2(-m_w)
        return acc[:HEAD_DIM] * (1.0 / jnp.maximum(l_w, 1e-30))

    def chunk_scores(gg, c, diag_rel):
        rows = slice(c * SEL_CHUNK, (c + 1) * SEL_CHUNK)
        s = jnp.dot(ks_refs[gg // 2][rows, :], jnp.concatenate(q_operand(gg), axis=0),
                    preferred_element_type=F32)
        if diag_rel is not None:
            s = s + jnp.concatenate([jnp.where(diag_rel == u, tri_diag, 0.0) for u in range(tiles_per_chunk)], axis=0)
        s_scr[gg, rows, :] = s
        bmax = [_fold_rows(s[b * SEL_BLOCK:(b + 1) * SEL_BLOCK], jnp.maximum) for b in range(blocks_per_chunk)]
        yield
        return bmax

    def first_chunk(gg):
        yield
        return (yield from chunk_scores(gg, 0, i))

    def selected(gg, k, block_bias, bmax0):
        bmax = list(bmax0)
        for c in range(1, k + 1):
            bmax += yield from chunk_scores(gg, c, i - k * q_per_chunk if c == k else None)
        m8 = functools.reduce(jnp.maximum, [bm + block_bias[n:n + 1, :] for n, bm in enumerate(bmax)])
        m_s = jnp.max(m8, axis=0, keepdims=True)
        shift = block_bias - m_s
        yield
        acc = jnp.zeros((V_AUG, NSA_LANES), F32)
        for c in range(k + 1):
            ps = []
            for b in range(blocks_per_chunk):
                n = c * blocks_per_chunk + b
                ps.append(jnp.exp2(s_scr[gg, n * SEL_BLOCK:(n + 1) * SEL_BLOCK, :] + shift[n:n + 1, :]).astype(BF16))
            vt = jnp.concatenate([vt_scr[0, gg, c * tiles_per_chunk + u] for u in range(tiles_per_chunk)], axis=1)
            acc = acc + jnp.dot(vt, jnp.concatenate(ps, axis=0), preferred_element_type=F32)
            yield
        l_s = acc[HEAD_DIM:HEAD_DIM + 1]
        return acc[:HEAD_DIM] * (1.0 / jnp.maximum(l_s, 1e-30))

    qts = [None] * NSA_GPS
    groups = range(NSA_GPS)
    shared = _round_robin([prepare(gg) for gg in groups] + [window(gg) for gg in groups]
                          + [first_chunk(gg) for gg in groups])
    prepared, windows, bmax0 = (shared[n * NSA_GPS:(n + 1) * NSA_GPS] for n in range(3))

    def finish(k):
        branches = []
        for gg, (qt, o_c, block_bias, gt) in enumerate(prepared):
            branches.append(selected(gg, k, block_bias, bmax0[gg]))
        results = _round_robin(branches, lead=k + 1)
        outs = []
        for gg, (qt, o_c, block_bias, gt) in enumerate(prepared):
            o_w, o_s = windows[gg], results[gg]
            for h in range(NSA_HPG):
                sl = slice(h * Q_BLOCK, (h + 1) * Q_BLOCK)
                row = NSA_BRANCHES * h
                outs.append(gt[row:row + 1, :] * o_c[:, sl] + gt[row + 1:row + 2, :] * o_s[:, sl]
                            + gt[row + 2:row + 3, :] * o_w[:, sl])
        o_ref[...] = jnp.transpose(jnp.concatenate(outs, axis=0))

    for k in range(n_sel * SEL_BLOCK // SEL_CHUNK):
        pl.when(i // q_per_chunk == k)(functools.partial(finish, k))


def _nsa(z, cos_q, sin_q, kvc, c2st, *, batch, seq):
    nq = seq // Q_BLOCK
    n_sel = seq // SEL_BLOCK
    gw = NSA_GPS * NSA_HPG * HEAD_DIM
    gates_w = NSA_GROUPS * GATE_STRIDE
    pair_w = 2 * HEAD_DIM
    assert pair_w == LANES
    kv_spec = lambda off, pair: pl.BlockSpec(
        (seq, pair_w), lambda b, g, i: (b, off // pair_w + g * NSA_PAIRS + pair))
    kv_specs = [kv_spec(off, pair) for pair in range(NSA_PAIRS) for off in (OFF_KS, OFF_VS, OFF_KW, OFF_VW)]
    rope_spec = pl.BlockSpec((ROPE_HALF, Q_BLOCK), lambda b, g, i: (0, i))
    return pl.pallas_call(
        functools.partial(_nsa_kernel, n_sel=n_sel),
        grid=(batch, NSA_GROUPS // NSA_GPS, nq),
        in_specs=[
            pl.BlockSpec((Q_BLOCK, gw), lambda b, g, i: (b * nq + i, OFF_Q // gw + g)),
            pl.BlockSpec((Q_BLOCK, gates_w), lambda b, g, i: (b * nq + i, OFF_GATE // gates_w)),
            rope_spec, rope_spec,
            *kv_specs,
            pl.BlockSpec((LANES, NSA_GPS * 2 * HEAD_DIM), lambda b, g, i: (b, g)),
            pl.BlockSpec((n_sel, LANES), lambda b, g, i: (0, 0)),
        ],
        out_specs=pl.BlockSpec((Q_BLOCK, gw), lambda b, g, i: (b * nq + i, g)),
        out_shape=jax.ShapeDtypeStruct((batch * seq, NSA_HEADS * HEAD_DIM), F32),
        scratch_shapes=[pltpu.VMEM((gates_w, Q_BLOCK), F32), pltpu.VMEM((NSA_GPS, seq, NSA_LANES), F32),
                        pltpu.VMEM((2, NSA_GPS, nq, V_AUG, Q_BLOCK), BF16)],
        compiler_params=pltpu.CompilerParams(
            dimension_semantics=("arbitrary", "arbitrary", "arbitrary"), vmem_limit_bytes=VMEM_LIMIT),
        name="nsa",
    )(z, z, cos_q, sin_q, *([z] * len(kv_specs)), kvc, c2st)


def _rglru_kernel(x_ref, gr_ref, cw_ref, cb_ref, wa_ref, ba_ref, wi_ref, bi_ref, lam_ref, y_ref,
                  xs_scr, a_scr, b_scr, h_scr):
    nb, ts, c = x_ref.shape
    pitch = a_scr.shape[1] // nb
    lane_tiles = c // LANES

    def put(scr, rows, val):
        for m in range(lane_tiles):
            scr[m, rows, :] = val[:, m * LANES:(m + 1) * LANES]

    def get(scr, rows):
        return jnp.concatenate([scr[m, rows, :] for m in range(lane_tiles)], axis=1)

    @pl.when(pl.program_id(0) == 0)
    def _():
        xs_scr[:, 0:SUBLANES, :] = jnp.zeros((nb, SUBLANES, c), F32)
        h_scr[...] = jnp.zeros(h_scr.shape, F32)

    nl = -lam_ref[...]
    softplus = jnp.maximum(nl, 0.0) + jnp.log1p(jnp.exp(-jnp.abs(nl)))

    def gates(bi_, carry):
        x = x_ref[bi_].astype(F32)
        xs_scr[bi_, SUBLANES:SUBLANES + ts, :] = x
        xc = cb_ref[...] + cw_ref[CONV_WIDTH - 1:CONV_WIDTH, :] * x
        for k in range(CONV_WIDTH - 1):
            off = SUBLANES - (CONV_WIDTH - 1) + k
            xc = xc + cw_ref[k:k + 1, :] * xs_scr[bi_, off:off + ts, :]
        xs_scr[bi_, 0:SUBLANES, :] = x[ts - SUBLANES:, :]

        ra, ri = [], []
        for mblk in range(c // LANES):
            xb = xc[:, mblk * LANES:(mblk + 1) * LANES].astype(BF16)
            ra.append(jnp.dot(xb, wa_ref[mblk], preferred_element_type=F32))
            ri.append(jnp.dot(xb, wi_ref[mblk], preferred_element_type=F32))
        r = _sigmoid(jnp.concatenate(ra, axis=1) + ba_ref[...])
        gi = _sigmoid(jnp.concatenate(ri, axis=1) + bi_ref[...])

        log_a = (-RGLRU_C) * r * softplus
        th = jnp.tanh(log_a)
        y2 = -2.0 * th / (1.0 - th)
        rows = pl.ds(pl.multiple_of(bi_ * pitch, SUBLANES), ts)
        put(a_scr, rows, jnp.exp(log_a))
        put(b_scr, rows, jnp.where(y2 > 0.0, y2 * lax.rsqrt(y2), 0.0) * (gi * xc))
        return carry

    lax.fori_loop(0, nb, gates, 0)

    def step(t, h):
        rows = pl.ds(t, nb, stride=pitch)
        h = get(a_scr, rows) * h + get(b_scr, rows)
        put(b_scr, rows, h)
        return h

    h_scr[...] = lax.fori_loop(0, ts, step, h_scr[...], unroll=8)

    def outputs(bi_, carry):
        rows = pl.ds(pl.multiple_of(bi_ * pitch, SUBLANES), ts)
        y_ref[bi_] = _gelu_tanh(gr_ref[bi_].astype(F32)) * get(b_scr, rows)
        return carry

    lax.fori_loop(0, nb, outputs, 0)


def _rglru(z, cw, cb, wa2, ba, wi2, bi, lam, *, batch, seq, ts):
    c = D_MODEL
    pitch = ts + SUBLANES
    z3 = z.reshape(batch, seq, z.shape[1])
    vec = pl.BlockSpec((1, c), lambda t: (0, 0))
    y = pl.pallas_call(
        _rglru_kernel,
        grid=(seq // ts,),
        in_specs=[
            pl.BlockSpec((batch, ts, c), lambda t: (0, t, OFF_XRNN // c)),
            pl.BlockSpec((batch, ts, c), lambda t: (0, t, OFF_GRNN // c)),
            pl.BlockSpec((CONV_WIDTH, c), lambda t: (0, 0)),
            vec,
            pl.BlockSpec((c // LANES, LANES, LANES), lambda t: (0, 0, 0)),
            vec,
            pl.BlockSpec((c // LANES, LANES, LANES), lambda t: (0, 0, 0)),
            vec, vec,
        ],
        out_specs=pl.BlockSpec((batch, ts, c), lambda t: (0, t, 0)),
        out_shape=jax.ShapeDtypeStruct((batch, seq, c), F32),
        scratch_shapes=[pltpu.VMEM((batch, SUBLANES + ts, c), F32), pltpu.VMEM((c // LANES, batch * pitch, LANES), F32),
                        pltpu.VMEM((c // LANES, batch * pitch, LANES), F32), pltpu.VMEM((batch, c), F32)],
        compiler_params=pltpu.CompilerParams(
            dimension_semantics=("arbitrary",), vmem_limit_bytes=VMEM_LIMIT),
        name="rglru",
    )(z3, z3, cw, cb, wa2, ba, wi2, bi, lam)
    return y.reshape(batch * seq, c)


def _merge_kernel(x_ref, m0_ref, m1_ref, m2_ref, ynsa_ref, yrnn_ref, qx_ref, km_ref, vm_ref, wxo_ref, wo_ref, h_ref):
    tq = x_ref.shape[0]
    mem_len = km_ref.shape[0]
    row_chunk = tq // 4

    def cross_attention():
        qt = jnp.transpose(qx_ref[...].astype(F32) * (SCALE * LOG2E))
        row_head = lax.broadcasted_iota(jnp.int32, qt.shape, 0) // HEAD_DIM
        q_bd = jnp.concatenate([jnp.where(row_head == h, qt, 0.0) for h in range(XATTN_HEADS)],
                               axis=1).astype(BF16)
        yield
        s = jnp.dot(km_ref[...], q_bd, preferred_element_type=F32)
        yield
        m = jnp.max(_fold_rows(s, jnp.maximum), axis=0, keepdims=True)
        p = jnp.exp2(s - m).astype(BF16)
        yield
        ones_rows = jnp.where(lax.broadcasted_iota(jnp.int32, (BF16_ROWS, mem_len), 0) == 0, 1.0, 0.0)
        vt = jnp.concatenate([jnp.transpose(vm_ref[...].astype(F32)), ones_rows], axis=0).astype(BF16)
        o_all = jnp.dot(vt, p, preferred_element_type=F32)
        yield
        outs = []
        for h in range(XATTN_HEADS):
            lanes = slice(h * tq, (h + 1) * tq)
            inv_l = 1.0 / o_all[XATTN_WIDTH:XATTN_WIDTH + 1, lanes]
            outs.append(o_all[h * HEAD_DIM:(h + 1) * HEAD_DIM, lanes] * inv_l)
        o = jnp.transpose(jnp.concatenate(outs, axis=0)).astype(BF16)
        yield
        return jnp.dot(o, wxo_ref[...], preferred_element_type=F32)

    def gated(gate_ref, val_ref):
        parts = []
        for r in range(0, tq, row_chunk):
            half = 0.5 * val_ref[r:r + row_chunk, :]
            parts.append(half * jnp.tanh(0.5 * gate_ref[r:r + row_chunk, :].astype(F32)) + half)
            yield
        return jnp.concatenate(parts, axis=0)

    def gate_tanh(gate_ref):
        parts = []
        for r in range(0, tq, row_chunk):
            parts.append(jnp.tanh(0.5 * gate_ref[r:r + row_chunk, :].astype(F32)))
            yield
        return jnp.concatenate(parts, axis=0)

    yx, y_a, y_b, t_x = _round_robin([cross_attention(), gated(m0_ref, ynsa_ref), gated(m1_ref, yrnn_ref),
                                      gate_tanh(m2_ref)])
    half_yx = 0.5 * yx
    y = y_a + y_b + (half_yx * t_x + half_yx)
    h_ref[...] = x_ref[...] + jnp.dot(y.astype(BF16), wo_ref[...], preferred_element_type=F32)


def _merge(x, z, ynsa, yrnn, kvm, wxo, wo, *, seq, mem_len, tq):
    t, d = x.shape
    nt = seq // tq
    row = lambda cb: pl.BlockSpec((tq, d), lambda i, cb=cb: (i, cb))
    mem_spec = lambda cb: pl.BlockSpec((mem_len, XATTN_WIDTH), lambda i, cb=cb: (i // nt, cb))
    return pl.pallas_call(
        _merge_kernel,
        grid=(t // tq,),
        in_specs=[
            row(0),
            row(OFF_MERGE // d), row(OFF_MERGE // d + 1), row(OFF_MERGE // d + 2),
            row(0), row(0),
            pl.BlockSpec((tq, XATTN_WIDTH), lambda i: (i, OFF_QX // XATTN_WIDTH)),
            mem_spec(0), mem_spec(1),
            pl.BlockSpec((XATTN_WIDTH, d), lambda i: (0, 0)),
            pl.BlockSpec((d, d), lambda i: (0, 0)),
        ],
        out_specs=row(0),
        out_shape=jax.ShapeDtypeStruct((t, d), F32),
        compiler_params=pltpu.CompilerParams(
            dimension_semantics=("parallel",), vmem_limit_bytes=VMEM_LIMIT),
        name="merge",
    )(x, z, z, z, ynsa, yrnn, z, kvm, kvm, wxo, wo)


def _mlp_kernel(h_ref, g_ref, wup_ref, wdn_ref, gf_ref, o_ref, *, tf):
    h = h_ref[...]
    ms = jnp.mean(h * h, axis=-1, keepdims=True)
    v = (h * lax.rsqrt(ms + RMS_EPS) * g_ref[...]).astype(BF16)
    h2 = h
    for f in range(0, wup_ref.shape[1], tf):
        up = jnp.dot(v, wup_ref[:, f:f + tf], preferred_element_type=F32)
        act = jnp.square(jnp.maximum(up, 0.0)).astype(BF16)
        h2 = h2 + jnp.dot(act, wdn_ref[f:f + tf, :], preferred_element_type=F32)
    ms = jnp.mean(h2 * h2, axis=-1, keepdims=True)
    o_ref[...] = h2 * lax.rsqrt(ms + RMS_EPS) * gf_ref[...]


def _mlp(h, g, wup, wdn, gf, *, tm, tf):
    t, d = h.shape
    dff = wup.shape[1]
    resident = lambda shape: pl.BlockSpec(shape, lambda i: (0, 0), pipeline_mode=pl.Buffered(1))
    return pl.pallas_call(
        functools.partial(_mlp_kernel, tf=tf),
        grid=(t // tm,),
        in_specs=[
            pl.BlockSpec((tm, d), lambda i: (i, 0)),
            pl.BlockSpec((1, d), lambda i: (0, 0)),
            resident((d, dff)), resident((dff, d)),
            pl.BlockSpec((1, d), lambda i: (0, 0)),
        ],
        out_specs=pl.BlockSpec((tm, d), lambda i: (i, 0)),
        out_shape=jax.ShapeDtypeStruct((t, d), F32),
        compiler_params=pltpu.CompilerParams(
            dimension_semantics=("parallel",), vmem_limit_bytes=VMEM_LIMIT),
        name="mlp",
    )(h, g, wup, wdn, gf)


def _rope_angles(pos):
    inv = 1.0 / (ROPE_THETA ** (jnp.arange(0, ROPE_DIM, 2, dtype=F32) / ROPE_DIM))
    ang = pos.astype(F32)[:, None] * inv[None, :]
    return jnp.cos(ang), jnp.sin(ang)


def _rope_tables(pos, width, rope_width=None):
    rope_width = width if rope_width is None else rope_width
    cos, sin = _rope_angles(pos)
    n = pos.shape[0]
    pad = HEAD_DIM - ROPE_DIM
    cos_h = jnp.concatenate([cos, cos, jnp.ones((n, pad), F32)], axis=1)
    sin_h = jnp.concatenate([-sin, sin, jnp.zeros((n, pad), F32)], axis=1)
    reps = rope_width // HEAD_DIM
    rest = width - rope_width
    cos_t = jnp.concatenate([jnp.tile(cos_h, (1, reps)), jnp.ones((n, rest), F32)], axis=1)
    sin_t = jnp.concatenate([jnp.tile(sin_h, (1, reps)), jnp.zeros((n, rest), F32)], axis=1)
    return cos_t, sin_t


def _pack_w_in(w_in):
    d = w_in.shape[0]
    w = w_in.astype(BF16)
    n_qkv = NSA_HEADS * HEAD_DIM + 6 * NSA_KV
    n_gates = NSA_HEADS * NSA_BRANCHES
    rnn_lo = n_qkv + n_gates
    qx_lo = rnn_lo + 2 * D_MODEL
    mg_lo = qx_lo + XATTN_WIDTH
    per_group = n_gates // NSA_GROUPS
    gates = w[:, n_qkv:rnn_lo].reshape(d, NSA_GROUPS, per_group)
    gates = jnp.pad(gates, ((0, 0), (0, 0), (0, GATE_STRIDE - per_group))).reshape(d, NSA_GROUPS * GATE_STRIDE)
    return jnp.concatenate([w[:, :n_qkv], w[:, qx_lo:mg_lo], gates, w[:, rnn_lo:qx_lo], w[:, mg_lo:]], axis=1)


def _block_diag_pairs(w):
    nb, k, _ = w.shape
    w = w.reshape(nb // 2, 2, k, k)
    zero = jnp.zeros((nb // 2, k, k), w.dtype)
    top = jnp.concatenate([w[:, 0], zero], axis=2)
    bot = jnp.concatenate([zero, w[:, 1]], axis=2)
    return jnp.concatenate([top, bot], axis=1).astype(BF16)


def _cmp_to_sel_t(n_cmp, n_cmp_pad, n_sel):
    c0 = np.arange(n_cmp_pad)[None, :] * CMP_STRIDE
    s0 = np.arange(n_sel)[:, None] * SEL_BLOCK
    ov = np.clip(np.minimum(c0 + CMP_LEN, s0 + SEL_BLOCK) - np.maximum(c0, s0), 0, None)
    ov = np.where(np.arange(n_cmp_pad)[None, :] < n_cmp, ov, 0)
    return (ov / CMP_LEN).astype(np.float32)


def kernel(x, mem, g_mix, w_in, cmp_pos_k, cmp_pos_v, w_cmp_k1, w_cmp_k2, w_cmp_v1, w_cmp_v2, conv_w, conv_b, w_rg_a, b_rg_a, w_rg_i, b_rg_i, rg_lambda, g_mem, w_mem_kv, w_xo, w_o, g_mlp, w_up, w_down, g_final):
    batch, seq, d = x.shape
    mem_len = mem.shape[1]
    t = batch * seq
    depth = g_mix.shape[0]
    n_sel = seq // SEL_BLOCK
    n_cmp_pad = seq // CMP_STRIDE
    assert n_cmp_pad == LANES and n_sel <= LANES and d == D_MODEL and depth == 1

    pos = jnp.arange(seq)
    cos_k, sin_k = _rope_tables(pos, NSA_KV)
    cos_q, sin_q = (a.T for a in _rope_angles(pos))
    no_rope = jnp.zeros((mem_len, LANES), F32)
    cmp_pos = jnp.arange(n_cmp_pad) * CMP_STRIDE + (CMP_LEN - 1)
    cos_c, sin_c = _rope_tables(cmp_pos, 2 * HEAD_DIM, HEAD_DIM)
    cos_c = jnp.tile(cos_c, (CMP_TM // n_cmp_pad, 2))
    sin_c = jnp.tile(sin_c, (CMP_TM // n_cmp_pad, 2))
    c2st = jnp.asarray(_cmp_to_sel_t((seq - CMP_LEN) // CMP_STRIDE + 1, n_cmp_pad, n_sel))

    h = x.reshape(t, d)
    for l in range(depth):
        z = _norm_proj(h, g_mix[l][None, :], _pack_w_in(w_in[l]), cos_k, sin_k, tm=INPROJ_TM, tn=INPROJ_TN,
                       rope_cols=ROPE_COLS, out_dtype=BF16, name="inproj")

        def pair_w1(w1):
            w3 = w1.astype(BF16).reshape(CMP_LEN, 1, HEAD_DIM, CMP_HIDDEN)
            zero = jnp.zeros_like(w3)
            both = jnp.concatenate([jnp.concatenate([w3, zero], axis=3), jnp.concatenate([zero, w3], axis=3)], axis=1)
            return both.reshape(CMP_LEN, 2 * HEAD_DIM, 2 * CMP_HIDDEN)

        def pair_pos(p):
            return jnp.concatenate([p, p], axis=1)

        w2 = jnp.zeros((2, 2, CMP_HIDDEN, 2, 2, HEAD_DIM), F32)
        for grp in range(2):
            w2 = w2.at[0, grp, :, grp, 0, :].set(w_cmp_k2[l]).at[1, grp, :, grp, 1, :].set(w_cmp_v2[l])
        w2 = w2.reshape(4 * CMP_HIDDEN, 4 * HEAD_DIM)
        kvc = _compress(z, pair_pos(cmp_pos_k[l]), pair_pos(cmp_pos_v[l]), pair_w1(w_cmp_k1[l]), pair_w1(w_cmp_v1[l]),
                        w2.astype(BF16), cos_c, sin_c, tm=CMP_TM)
        y_nsa = _nsa(z, cos_q, sin_q, kvc, c2st, batch=batch, seq=seq)

        y_rnn = _rglru(z, conv_w[l], conv_b[l][None, :], _block_diag_pairs(w_rg_a[l]), b_rg_a[l][None, :],
                       _block_diag_pairs(w_rg_i[l]), b_rg_i[l][None, :], rg_lambda[l][None, :],
                       batch=batch, seq=seq, ts=RGLRU_TS)

        kvm = _norm_proj(mem.reshape(batch * mem_len, d), g_mem[l][None, :], w_mem_kv[l].astype(BF16),
                         no_rope, no_rope, tm=mem_len, tn=2 * XATTN_WIDTH, rope_cols=(),
                         out_dtype=BF16, name="memkv")

        h1 = _merge(h, z, y_nsa, y_rnn, kvm, w_xo[l].astype(BF16), w_o[l].astype(BF16),
                    seq=seq, mem_len=mem_len, tq=MERGE_TQ)
        h = _mlp(h1, g_mlp[l][None, :], w_up[l].astype(BF16), w_down[l].astype(BF16), g_final[None, :],
                 tm=MLP_TM, tf=MLP_TF)
    return h.reshape(batch, seq, d)
```
